```python
import math
import jax, jax.numpy as jnp
from jax import lax
import numpy as np

D_MODEL = 1024
BATCH = 8
SEQ = 4096
DEPTH = 1

PLE_DIM = 256
MIX_WIDTH = D_MODEL
ATTN_WIDTH = MIX_WIDTH // 2
HYENA_WIDTH = MIX_WIDTH - ATTN_WIDTH
DIFF_HEADS = 4
DIFF_HEAD_DIM = ATTN_WIDTH // (2 * DIFF_HEADS)
DIFF_V_DIM = 2 * DIFF_HEAD_DIM
HYENA_GROUPS = 8
SHORT_CONV = 3
FILTER_EMB = 33
FILTER_ORDER = 64
FILTER_INNER = 2
FAST_DECAY_PCT = 0.3
SLOW_DECAY_PCT = 1.5
DECAY_TARGET = 1e-2
ROPE_THETA = 10000.0
N_EXPERTS = 32
TOP_K = 4
D_FF = D_MODEL
SWIGLU_ALPHA = 1.702
SWIGLU_LIMIT = 7.0
EXPERT_BLOCK = 256
Q_BLOCK = 128
EPS = 1e-6
SUBLN_EPS = 1e-5
IN_PROJ_WIDTH = 3 * ATTN_WIDTH + 3 * HYENA_WIDTH

kernel_name = 'hymba_diffattn_hyena_moe_block'


def rms_norm(x, g, eps=EPS):
    xf = x.astype(jnp.float32)
    y = xf * lax.rsqrt(jnp.mean(xf * xf, axis=-1, keepdims=True) + eps)
    return (y * g.astype(jnp.float32)).astype(x.dtype)


def apply_rope(x):
    S, d = x.shape[1], x.shape[-1]
    pos = jnp.arange(S, dtype=jnp.float32)
    inv = ROPE_THETA ** (-jnp.arange(0, d, 2, dtype=jnp.float32) / d)
    ang = pos[:, None] * inv[None, :]
    cos = jnp.cos(ang)[None, :, None, None, :]
    sin = jnp.sin(ang)[None, :, None, None, :]
    xf = x.astype(jnp.float32)
    x1, x2 = xf[..., : d // 2], xf[..., d // 2:]
    return jnp.concatenate([x1 * cos - x2 * sin, x2 * cos + x1 * sin], axis=-1).astype(x.dtype)


def diff_attention(q, k, v, lam, lambda_init, g_subln):
    B, S, H, _, dh = q.shape
    dv = v.shape[-1]
    scale = dh ** -0.5
    nq = S // Q_BLOCK
    qb = jnp.moveaxis(q.reshape(B, nq, Q_BLOCK, H, 2, dh), 1, 0)

    def block(qi):
        s = jnp.einsum('bqhcd,bkhcd->bhcqk', qi, k, preferred_element_type=jnp.float32) * scale
        pr = jax.nn.softmax(s, axis=-1)
        a = pr[:, :, 0] - lam * pr[:, :, 1]
        return jnp.einsum('bhqk,bkhe->bqhe', a.astype(v.dtype), v)

    o = lax.map(block, qb)
    o = jnp.moveaxis(o, 0, 1).reshape(B, S, H, dv)
    o = rms_norm(o, g_subln, SUBLN_EPS) * (1.0 - lambda_init)
    return o.reshape(B, S, H * dv)


def short_conv(x, w, b):
    half = SHORT_CONV // 2
    S = x.shape[1]
    xp = jnp.pad(x, ((0, 0), (half, SHORT_CONV - 1 - half), (0, 0)))
    y = b
    for j in range(SHORT_CONV):
        y = y + xp[:, j:j + S] * w[j]
    return y


def hyena_filters(seq, w1, b1, w_inner, b_inner, freq, w_out):
    f32 = jnp.float32
    pos = jnp.arange(seq, dtype=f32)
    t = pos / (seq - 1)
    bands = (FILTER_EMB - 1) // 2
    f = jnp.linspace(1e-4, bands - 1, bands, dtype=f32)
    fw = ((2.0 * math.pi / seq) * pos)[:, None] * f[None, :]
    z = jnp.concatenate([t[:, None], jnp.cos(fw), -jnp.sin(fw)], axis=-1)
    h = jnp.sin(freq[0].astype(f32) * (z @ w1.astype(f32) + b1.astype(f32)))
    for j in range(FILTER_INNER):
        h = jnp.sin(freq[j + 1].astype(f32) * (h @ w_inner[j].astype(f32) + b_inner[j].astype(f32)))
    h = h @ w_out.astype(f32)
    max_decay = math.log(DECAY_TARGET) / FAST_DECAY_PCT
    min_decay = math.log(DECAY_TARGET) / SLOW_DECAY_PCT
    deltas = jnp.abs(jnp.linspace(min_decay, max_decay, HYENA_WIDTH, dtype=f32))
    decay = jnp.exp(-t[:, None] * deltas[None, :])
    return h[:, :HYENA_WIDTH] * decay, h[:, HYENA_WIDTH:] * decay


def bidir_long_conv(u, h_fwd, h_bwd, d_skip):
    B, L, C = u.shape
    n = 2 * L
    lag0 = h_fwd[:1] + h_bwd[:1]
    kern = jnp.concatenate([lag0, h_fwd[1:], jnp.zeros((1, C), jnp.float32), h_bwd[:0:-1]], axis=0)
    u_f = jnp.fft.rfft(u.astype(jnp.float32), n=n, axis=1)
    k_f = jnp.fft.rfft(kern, n=n, axis=0)
    y = jnp.fft.irfft(u_f * k_f[None], n=n, axis=1)[:, :L]
    return (y + u.astype(jnp.float32) * d_skip.astype(jnp.float32)).astype(u.dtype)


def moe_ffn(h, router_w, router_b, w_gate, b_gate, w_up, b_up, w_down, b_down):
    B, S, D = h.shape
    T = B * S
    tok = h.reshape(T, D)
    logits = (tok @ router_w + router_b).astype(jnp.float32)
    top_logit, top_e = lax.top_k(logits, TOP_K)
    top_w = jax.nn.softmax(top_logit, axis=-1)
    A = T * TOP_K
    flat_e = top_e.reshape(A)
    order = jnp.argsort(flat_e)
    sorted_e = flat_e[order]
    sorted_tok = (order // TOP_K).astype(jnp.int32)
    sorted_w = top_w.reshape(A)[order]
    counts = jnp.bincount(flat_e, length=N_EXPERTS)
    padded = (counts + EXPERT_BLOCK - 1) // EXPERT_BLOCK * EXPERT_BLOCK
    pad_end = jnp.cumsum(padded)
    pad_start = pad_end - padded
    start = jnp.cumsum(counts) - counts
    dest = pad_start[sorted_e] + jnp.arange(A) - start[sorted_e]
    n_blocks = -(-A // EXPERT_BLOCK) + N_EXPERTS
    P = n_blocks * EXPERT_BLOCK
    slot_tok = jnp.full((P,), T, jnp.int32).at[dest].set(sorted_tok)
    slot_w = jnp.zeros((P,), jnp.float32).at[dest].set(sorted_w)
    block_e = jnp.minimum(
        jnp.searchsorted(pad_end, jnp.arange(n_blocks) * EXPERT_BLOCK, side='right'), N_EXPERTS - 1)
    tok_pad = jnp.concatenate([tok, jnp.zeros((1, D), tok.dtype)], axis=0)

    def expert_block(args):
        e, idx = args
        xb = tok_pad[idx]
        g = jnp.minimum(xb @ w_gate[e] + b_gate[e], SWIGLU_LIMIT)
        u = jnp.clip(xb @ w_up[e] + b_up[e], -SWIGLU_LIMIT, SWIGLU_LIMIT)
        a = (u + 1.0) * (g * jax.nn.sigmoid(SWIGLU_ALPHA * g))
        return a @ w_down[e] + b_down[e]

    y = lax.map(expert_block, (block_e, slot_tok.reshape(n_blocks, EXPERT_BLOCK)))
    y = y.reshape(P, D) * slot_w[:, None].astype(y.dtype)
    out = jax.ops.segment_sum(y, slot_tok, num_segments=T + 1)[:T]
    return out.reshape(B, S, D)


def setup_inputs(seed: int = 0) -> dict:
    key = jax.random.key(seed)
    ks = iter(jax.random.split(key, 48))

    def nrm(shape, scale):
        return jax.random.normal(next(ks), shape, jnp.float32) * scale

    def gain(shape):
        return 1.0 + nrm(shape, 0.02)

    Dh = HYENA_WIDTH
    return {
        'x': nrm((BATCH, SEQ, D_MODEL), 1.0),
        'p': nrm((DEPTH, BATCH, SEQ, PLE_DIM), 1.0),
        'g_mix': gain((DEPTH, D_MODEL)),
        'w_in': nrm((DEPTH, D_MODEL, IN_PROJ_WIDTH), D_MODEL ** -0.5),
        'hyena_conv_w': nrm((DEPTH, SHORT_CONV, 3 * Dh), SHORT_CONV ** -0.5),
        'hyena_conv_b': nrm((DEPTH, 3 * Dh), 0.02),
        'flt_w1': nrm((DEPTH, FILTER_EMB, FILTER_ORDER), FILTER_EMB ** -0.5),
        'flt_b1': nrm((DEPTH, FILTER_ORDER), 0.02),
        'flt_w_inner': nrm((DEPTH, FILTER_INNER, FILTER_ORDER, FILTER_ORDER), FILTER_ORDER ** -0.5),
        'flt_b_inner': nrm((DEPTH, FILTER_INNER, FILTER_ORDER), 0.02),
        'flt_freq': gain((DEPTH, FILTER_INNER + 1, FILTER_ORDER)),
        'flt_w_out': nrm((DEPTH, FILTER_ORDER, 2 * Dh), FILTER_ORDER ** -0.5),
        'hyena_d': nrm((DEPTH, Dh), 1.0),
        'hyena_gn': gain((DEPTH, Dh)),
        'lambda_q1': nrm((DEPTH, DIFF_HEAD_DIM), 0.1),
        'lambda_k1': nrm((DEPTH, DIFF_HEAD_DIM), 0.1),
        'lambda_q2': nrm((DEPTH, DIFF_HEAD_DIM), 0.1),
        'lambda_k2': nrm((DEPTH, DIFF_HEAD_DIM), 0.1),
        'attn_subln': gain((DEPTH, DIFF_V_DIM)),
        'w_out': nrm((DEPTH, MIX_WIDTH, D_MODEL), MIX_WIDTH ** -0.5),
        'g_ffn': gain((DEPTH, D_MODEL)),
        'router_w': nrm((DEPTH, D_MODEL, N_EXPERTS), D_MODEL ** -0.5),
        'router_b': nrm((DEPTH, N_EXPERTS), 0.01),
        'w_gate': nrm((DEPTH, N_EXPERTS, D_MODEL, D_FF), D_MODEL ** -0.5),
        'b_gate': nrm((DEPTH, N_EXPERTS, D_FF), 0.02),
        'w_up': nrm((DEPTH, N_EXPERTS, D_MODEL, D_FF), D_MODEL ** -0.5),
        'b_up': nrm((DEPTH, N_EXPERTS, D_FF), 0.02),
        'w_down': nrm((DEPTH, N_EXPERTS, D_FF, D_MODEL), D_FF ** -0.5),
        'b_down': nrm((DEPTH, N_EXPERTS, D_MODEL), 0.02),
        'w_ple': nrm((DEPTH, PLE_DIM, D_MODEL), PLE_DIM ** -0.5),
        'g_ple': gain((DEPTH, D_MODEL)),
        'w_ple_gate': nrm((DEPTH, D_MODEL, D_MODEL), D_MODEL ** -0.5),
        'b_ple_gate': nrm((DEPTH, D_MODEL), 0.02),
        'g_final': gain((D_MODEL,)),
    }


def reference(x, p, g_mix, w_in, hyena_conv_w, hyena_conv_b, flt_w1, flt_b1, flt_w_inner,
              flt_b_inner, flt_freq, flt_w_out, hyena_d, hyena_gn, lambda_q1, lambda_k1,
              lambda_q2, lambda_k2, attn_subln, w_out, g_ffn, router_w, router_b, w_gate,
              b_gate, w_up, b_up, w_down, b_down, w_ple, g_ple, w_ple_gate, b_ple_gate, g_final):
    B, S, _ = x.shape
    H, dh, dv = DIFF_HEADS, DIFF_HEAD_DIM, DIFF_V_DIM
    h = x
    for i in range(DEPTH):
        a = rms_norm(h, g_mix[i])
        proj = a @ w_in[i]
        q = proj[..., :ATTN_WIDTH].reshape(B, S, H, 2, dh)
        k = proj[..., ATTN_WIDTH:2 * ATTN_WIDTH].reshape(B, S, H, 2, dh)
        v = proj[..., 2 * ATTN_WIDTH:3 * ATTN_WIDTH].reshape(B, S, H, dv)
        hy_in = proj[..., 3 * ATTN_WIDTH:]

        lambda_init = 0.8 - 0.6 * math.exp(-0.3 * i)
        f32 = jnp.float32
        lam = (jnp.exp(jnp.sum(lambda_q1[i].astype(f32) * lambda_k1[i].astype(f32)))
               - jnp.exp(jnp.sum(lambda_q2[i].astype(f32) * lambda_k2[i].astype(f32)))
               + lambda_init)
        attn_out = diff_attention(apply_rope(q), apply_rope(k), v, lam, lambda_init, attn_subln[i])

        hy = short_conv(hy_in, hyena_conv_w[i], hyena_conv_b[i])
        hv = hy[..., :HYENA_WIDTH]
        hx0 = hy[..., HYENA_WIDTH:2 * HYENA_WIDTH]
        hx1 = hy[..., 2 * HYENA_WIDTH:]
        h_fwd, h_bwd = hyena_filters(S, flt_w1[i], flt_b1[i], flt_w_inner[i], flt_b_inner[i],
                                     flt_freq[i], flt_w_out[i])
        z = bidir_long_conv(hv * hx1, h_fwd, h_bwd, hyena_d[i]) * hx0
        z = rms_norm(z.reshape(B, S, HYENA_GROUPS, HYENA_WIDTH // HYENA_GROUPS),
                     hyena_gn[i].reshape(HYENA_GROUPS, HYENA_WIDTH // HYENA_GROUPS))
        hyena_out = z.reshape(B, S, HYENA_WIDTH)

        mix = jnp.concatenate([attn_out, hyena_out], axis=-1) @ w_out[i]
        h = h + mix

        h = h + moe_ffn(rms_norm(h, g_ffn[i]), router_w[i], router_b[i], w_gate[i], b_gate[i],
                        w_up[i], b_up[i], w_down[i], b_down[i])

        e = rms_norm(p[i] @ w_ple[i], g_ple[i])
        gate = jax.nn.sigmoid(h @ w_ple_gate[i] + b_ple_gate[i])
        h = h + gate * e
    return rms_norm(h, g_final)
```

```python
import functools
import math

import jax
import jax.numpy as jnp
from jax import lax
from jax.experimental import pallas as pl
from jax.experimental.pallas import tpu as pltpu

F32 = jnp.float32
BF16 = jnp.bfloat16

D_MODEL = 1024
ATTN_WIDTH = 512
HYENA_WIDTH = 512
DIFF_HEADS = 4
DIFF_HEAD_DIM = 64
DIFF_V_DIM = 128
HYENA_GROUPS = 8
FILTER_EMB = 33
FILTER_INNER = 2
FAST_DECAY_PCT = 0.3
SLOW_DECAY_PCT = 1.5
DECAY_TARGET = 1e-2
ROPE_THETA = 10000.0
N_EXPERTS = 32
TOP_K = 4
SWIGLU_ALPHA = 1.702
SWIGLU_LIMIT = 7.0
EPS = 1e-6
SUBLN_EPS = 1e-5
LAMBDA_INIT = 0.8 - 0.6 * math.exp(-0.3 * 0)

V7X_VMEM_LIMIT_BYTES = 56 * 1024 * 1024
LANES = 128
SUBLANES = 8

TOKEN_TILE = 512
ATTN_Q_TILE = 256
ATTN_KV_TILE = 512
EXPERT_TILE = 512
ROUTER_PAD = LANES


def _params(*sem):
    return pltpu.CompilerParams(dimension_semantics=sem, vmem_limit_bytes=V7X_VMEM_LIMIT_BYTES)


def _rms(x, g, eps):
    return x * lax.rsqrt(jnp.mean(x * x, axis=-1, keepdims=True) + eps) * g


def _inproj_kernel(x_ref, g_ref, w_ref, c_ref, s1_ref, s2_ref, q_ref, k_ref, v_ref, hy_ref):
    a = _rms(x_ref[...], g_ref[...], EPS).astype(BF16)
    cos, s1, s2 = c_ref[...], s1_ref[...], s2_ref[...]

    def rope(t):
        return t * cos + pltpu.roll(t, LANES - 32, axis=1) * s1 + pltpu.roll(t, 32, axis=1) * s2

    qk = jnp.dot(a, w_ref[:, : 2 * ATTN_WIDTH], preferred_element_type=F32)
    for j in range(ATTN_WIDTH // LANES):
        sl = slice(j * LANES, (j + 1) * LANES)
        q_ref[:, sl] = (rope(qk[:, sl]) * (DIFF_HEAD_DIM ** -0.5)).astype(BF16)
        k_ref[:, sl] = rope(qk[:, ATTN_WIDTH + j * LANES: ATTN_WIDTH + (j + 1) * LANES]).astype(BF16)
    v_ref[...] = jnp.dot(a, w_ref[:, 2 * ATTN_WIDTH: 3 * ATTN_WIDTH],
                         preferred_element_type=F32).astype(BF16)
    hy_ref[...] = jnp.dot(a, w_ref[:, 3 * ATTN_WIDTH:], preferred_element_type=F32)


def _in_proj(x2, g_mix, w_in_bf, cos_t, s1_t, s2_t, seq):
    T, D = x2.shape
    tm = TOKEN_TILE
    nseq = seq // tm
    row = lambda i: (i, 0)
    const = lambda i: (0, 0)
    pos = lambda i: (i % nseq, 0)
    return pl.pallas_call(
        _inproj_kernel,
        grid=(T // tm,),
        in_specs=[
            pl.BlockSpec((tm, D), row),
            pl.BlockSpec((1, D), const),
            pl.BlockSpec(w_in_bf.shape, const),
            pl.BlockSpec((tm, LANES), pos),
            pl.BlockSpec((tm, LANES), pos),
            pl.BlockSpec((tm, LANES), pos),
        ],
        out_specs=[
            pl.BlockSpec((tm, ATTN_WIDTH), row),
            pl.BlockSpec((tm, ATTN_WIDTH), row),
            pl.BlockSpec((tm, ATTN_WIDTH), row),
            pl.BlockSpec((tm, 3 * HYENA_WIDTH), row),
        ],
        out_shape=[
            jax.ShapeDtypeStruct((T, ATTN_WIDTH), BF16),
            jax.ShapeDtypeStruct((T, ATTN_WIDTH), BF16),
            jax.ShapeDtypeStruct((T, ATTN_WIDTH), BF16),
            jax.ShapeDtypeStruct((T, 3 * HYENA_WIDTH), F32),
        ],
        compiler_params=_params("parallel"),
        name="in_proj",
    )(x2, g_mix.reshape(1, D), w_in_bf, cos_t, s1_t, s2_t)


def _attn_kernel(lam_ref, q_ref, k_ref, v_ref, g_ref, o_ref, *, tk):
    q = q_ref[0]
    tq = q.shape[0]
    nk = k_ref.shape[1] // tk
    lane = lax.broadcasted_iota(jnp.int32, q.shape, 1)
    zero = jnp.zeros_like(q)
    q1 = jnp.where(lane < DIFF_HEAD_DIM, q, zero)
    q2 = jnp.where(lane >= DIFF_HEAD_DIM, q, zero)

    def update(qc, kb, vb, m, l, acc):
        s = lax.dot_general(qc, kb, (((1,), (1,)), ((), ())), preferred_element_type=F32)
        m_new = jnp.maximum(m, jnp.max(s, axis=-1, keepdims=True))
        p = jnp.exp(s - m_new)
        alpha = jnp.exp(m - m_new)
        l_new = alpha * l + jnp.sum(p, axis=-1, keepdims=True)
        acc_new = alpha * acc + jnp.dot(p.astype(BF16), vb, preferred_element_type=F32)
        return m_new, l_new, acc_new

    def body(j, carry):
        m1, l1, a1, m2, l2, a2 = carry
        start = pl.multiple_of(j * tk, tk)
        kb = k_ref[0, pl.ds(start, tk), :]
        vb = v_ref[0, pl.ds(start, tk), :]
        m1, l1, a1 = update(q1, kb, vb, m1, l1, a1)
        m2, l2, a2 = update(q2, kb, vb, m2, l2, a2)
        return m1, l1, a1, m2, l2, a2

    neg = jnp.full((tq, 1), -jnp.inf, F32)
    zl = jnp.zeros((tq, 1), F32)
    za = jnp.zeros((tq, DIFF_V_DIM), F32)
    m1, l1, a1, m2, l2, a2 = lax.fori_loop(0, nk, body, (neg, zl, za, neg, zl, za))
    o = a1 / l1 - lam_ref[0] * (a2 / l2)
    o = _rms(o, g_ref[...], SUBLN_EPS) * (1.0 - LAMBDA_INIT)
    o_ref[0] = o.astype(o_ref.dtype)


def _diff_attention(lam, q, k, v, g_subln):
    B, S, _ = q.shape
    tq, tk = ATTN_Q_TILE, ATTN_KV_TILE
    return pl.pallas_call(
        functools.partial(_attn_kernel, tk=tk),
        grid=(B, DIFF_HEADS, S // tq),
        in_specs=[
            pl.BlockSpec(memory_space=pltpu.SMEM),
            pl.BlockSpec((1, tq, LANES), lambda b, h, i: (b, i, h)),
            pl.BlockSpec((1, S, LANES), lambda b, h, i: (b, 0, h)),
            pl.BlockSpec((1, S, LANES), lambda b, h, i: (b, 0, h)),
            pl.BlockSpec((1, DIFF_V_DIM), lambda b, h, i: (0, 0)),
        ],
        out_specs=pl.BlockSpec((1, tq, LANES), lambda b, h, i: (b, i, h)),
        out_shape=jax.ShapeDtypeStruct((B, S, ATTN_WIDTH), BF16),
        compiler_params=_params("parallel", "parallel", "parallel"),
        name="diff_attn",
    )(lam, q, k, v, g_subln.reshape(1, DIFF_V_DIM))


def _shortconv_kernel(hy_ref, prev_ref, next_ref, w_ref, b_ref, u_ref, x0_ref):
    i = pl.program_id(1)
    last = pl.num_programs(1) - 1
    x = hy_ref[0]
    ts = x.shape[0]
    prev_row = jnp.where(i == 0, 0.0, prev_ref[0, SUBLANES - 1:SUBLANES, :])
    next_row = jnp.where(i == last, 0.0, next_ref[0, 0:1, :])
    row = lax.broadcasted_iota(jnp.int32, (ts, 1), 0)
    xm = jnp.where(row == 0, prev_row, pltpu.roll(x, 1, axis=0))
    xp = jnp.where(row == ts - 1, next_row, pltpu.roll(x, ts - 1, axis=0))
    y = b_ref[...] + xm * w_ref[0:1, :] + x * w_ref[1:2, :] + xp * w_ref[2:3, :]
    C = HYENA_WIDTH
    u_ref[0] = y[:, :C] * y[:, 2 * C:]
    x0_ref[0] = y[:, C:2 * C]


def _short_conv(hy, conv_w, conv_b):
    B, S, C3 = hy.shape
    ts = TOKEN_TILE
    nb = ts // SUBLANES
    return pl.pallas_call(
        _shortconv_kernel,
        grid=(B, S // ts),
        in_specs=[
            pl.BlockSpec((1, ts, C3), lambda b, i: (b, i, 0)),
            pl.BlockSpec((1, SUBLANES, C3), lambda b, i: (b, jnp.maximum(i * nb - 1, 0), 0)),
            pl.BlockSpec((1, SUBLANES, C3), lambda b, i: (b, jnp.minimum((i + 1) * nb, S // SUBLANES - 1), 0)),
            pl.BlockSpec((3, C3), lambda b, i: (0, 0)),
            pl.BlockSpec((1, C3), lambda b, i: (0, 0)),
        ],
        out_specs=[
            pl.BlockSpec((1, ts, HYENA_WIDTH), lambda b, i: (b, i, 0)),
            pl.BlockSpec((1, ts, HYENA_WIDTH), lambda b, i: (b, i, 0)),
        ],
        out_shape=[
            jax.ShapeDtypeStruct((B, S, HYENA_WIDTH), F32),
            jax.ShapeDtypeStruct((B, S, HYENA_WIDTH), F32),
        ],
        compiler_params=_params("parallel", "parallel"),
        name="short_conv",
    )(hy, hy, hy, conv_w, conv_b.reshape(1, C3))


def _split_bf16(x):
    hi = x.astype(BF16)
    lo = (x - hi.astype(F32)).astype(BF16)
    return hi, lo


def _outproj_kernel(attn_ref, yc_ref, u_ref, x0_ref, d_ref, gn_ref, grp_ref, wo_ref, x_ref,
                    gf_ref, rwh_ref, rwl_ref, rb_ref, h_ref, hn_ref, lg_ref):
    z = (yc_ref[...] + u_ref[...] * d_ref[...]) * x0_ref[...]
    zh, zl = _split_bf16(z * z)
    grp = grp_ref[...]
    ssq = (jnp.dot(zh, grp, preferred_element_type=F32) + jnp.dot(zl, grp, preferred_element_type=F32))
    gsz = HYENA_WIDTH // HYENA_GROUPS
    hy_out = (z * lax.rsqrt(ssq * (1.0 / gsz) + EPS) * gn_ref[...]).astype(BF16)
    mix = (jnp.dot(attn_ref[...], wo_ref[:ATTN_WIDTH, :], preferred_element_type=F32)
           + jnp.dot(hy_out, wo_ref[ATTN_WIDTH:, :], preferred_element_type=F32))
    h = x_ref[...] + mix
    h_ref[...] = h
    hn = _rms(h, gf_ref[...], EPS)
    hn_ref[...] = hn.astype(BF16)
    nh, nl = _split_bf16(hn)
    lg_ref[...] = (jnp.dot(nh, rwh_ref[...], preferred_element_type=F32)
                   + jnp.dot(nl, rwh_ref[...], preferred_element_type=F32)
                   + jnp.dot(nh, rwl_ref[...], preferred_element_type=F32)) + rb_ref[...]


def _out_proj(attn2, yc2, u2, x02, hyena_d, hyena_gn, w_out_bf, x2, g_ffn, router_w, router_b):
    T, D = x2.shape
    tm = TOKEN_TILE
    C = HYENA_WIDTH
    gid = jnp.arange(C) // (C // HYENA_GROUPS)
    grp = (gid[:, None] == gid[None, :]).astype(BF16)
    rw = jnp.zeros((D, ROUTER_PAD), F32).at[:, :N_EXPERTS].set(router_w)
    rwh, rwl = _split_bf16(rw)
    rb = jnp.zeros((1, ROUTER_PAD), F32).at[0, :N_EXPERTS].set(router_b)
    row = lambda i: (i, 0)
    const = lambda i: (0, 0)
    return pl.pallas_call(
        _outproj_kernel,
        grid=(T // tm,),
        in_specs=[
            pl.BlockSpec((tm, ATTN_WIDTH), row),
            pl.BlockSpec((tm, C), row),
            pl.BlockSpec((tm, C), row),
            pl.BlockSpec((tm, C), row),
            pl.BlockSpec((1, C), const),
            pl.BlockSpec((1, C), const),
            pl.BlockSpec((C, C), const),
            pl.BlockSpec((D, D), const),
            pl.BlockSpec((tm, D), row),
            pl.BlockSpec((1, D), const),
            pl.BlockSpec((D, ROUTER_PAD), const),
            pl.BlockSpec((D, ROUTER_PAD), const),
            pl.BlockSpec((1, ROUTER_PAD), const),
        ],
        out_specs=[
            pl.BlockSpec((tm, D), row),
            pl.BlockSpec((tm, D), row),
            pl.BlockSpec((tm, ROUTER_PAD), row),
        ],
        out_shape=[
            jax.ShapeDtypeStruct((T, D), F32),
            jax.ShapeDtypeStruct((T, D), BF16),
            jax.ShapeDtypeStruct((T, ROUTER_PAD), F32),
        ],
        compiler_params=_params("parallel"),
        name="out_proj",
    )(attn2, yc2, u2, x02, hyena_d.reshape(1, C), hyena_gn.reshape(1, C), grp, w_out_bf, x2,
      g_ffn.reshape(1, D), rwh, rwl, rb)


def _expert_kernel(be_ref, nused_ref, x_ref, wg_ref, bg_ref, wu_ref, bu_ref, wd_ref, bd_ref, y_ref):
    i = pl.program_id(0)

    @pl.when(i < nused_ref[0])
    def _():
        x = x_ref[...]
        g = jnp.minimum(jnp.dot(x, wg_ref[0], preferred_element_type=F32) + bg_ref[0], SWIGLU_LIMIT)
        u = jnp.clip(jnp.dot(x, wu_ref[0], preferred_element_type=F32) + bu_ref[0],
                     -SWIGLU_LIMIT, SWIGLU_LIMIT)
        a = (u + 1.0) * (g * jax.nn.sigmoid(SWIGLU_ALPHA * g))
        y_ref[...] = jnp.dot(a.astype(BF16), wd_ref[0], preferred_element_type=F32) + bd_ref[0]

    @pl.when(i >= nused_ref[0])
    def _():
        y_ref[...] = jnp.zeros_like(y_ref)


def _experts(block_e, n_used, x_sorted, wg, bg, wu, bu, wd, bd):
    P, D = x_sorted.shape
    tm = EXPERT_TILE
    E, _, FF = wg.shape
    wmap = lambda i, be, nu: (be[i], 0, 0)
    grid_spec = pltpu.PrefetchScalarGridSpec(
        num_scalar_prefetch=2,
        grid=(P // tm,),
        in_specs=[
            pl.BlockSpec((tm, D), lambda i, be, nu: (i, 0)),
            pl.BlockSpec((1, D, FF), wmap),
            pl.BlockSpec((1, 1, FF), wmap),
            pl.BlockSpec((1, D, FF), wmap),
            pl.BlockSpec((1, 1, FF), wmap),
            pl.BlockSpec((1, FF, D), wmap),
            pl.BlockSpec((1, 1, D), wmap),
        ],
        out_specs=pl.BlockSpec((tm, D), lambda i, be, nu: (i, 0)),
    )
    return pl.pallas_call(
        _expert_kernel,
        grid_spec=grid_spec,
        out_shape=jax.ShapeDtypeStruct((P, D), F32),
        compiler_params=_params("arbitrary"),
        name="moe_experts",
    )(block_e, n_used, x_sorted, wg, bg.reshape(E, 1, FF), wu, bu.reshape(E, 1, FF),
      wd, bd.reshape(E, 1, D))


def _final_kernel(h_ref, moe_ref, p_ref, wp_ref, gp_ref, wg_ref, bg_ref, gfin_ref, o_ref):
    h = h_ref[...] + moe_ref[...]
    e = _rms(jnp.dot(p_ref[...].astype(BF16), wp_ref[...], preferred_element_type=F32),
             gp_ref[...], EPS)
    gate = jax.nn.sigmoid(jnp.dot(h.astype(BF16), wg_ref[...], preferred_element_type=F32)
                          + bg_ref[...])
    h = h + gate * e
    o_ref[...] = _rms(h, gfin_ref[...], EPS)


def _final(h1, moe, p2, w_ple_bf, g_ple, w_gate_bf, b_gate, g_final):
    T, D = h1.shape
    tm = TOKEN_TILE
    PD = p2.shape[1]
    row = lambda i: (i, 0)
    const = lambda i: (0, 0)
    return pl.pallas_call(
        _final_kernel,
        grid=(T // tm,),
        in_specs=[
            pl.BlockSpec((tm, D), row),
            pl.BlockSpec((tm, D), row),
            pl.BlockSpec((tm, PD), row),
            pl.BlockSpec((PD, D), const),
            pl.BlockSpec((1, D), const),
            pl.BlockSpec((D, D), const),
            pl.BlockSpec((1, D), const),
            pl.BlockSpec((1, D), const),
        ],
        out_specs=pl.BlockSpec((tm, D), row),
        out_shape=jax.ShapeDtypeStruct((T, D), F32),
        compiler_params=_params("parallel"),
        name="final",
    )(h1, moe, p2, w_ple_bf, g_ple.reshape(1, D), w_gate_bf, b_gate.reshape(1, D),
      g_final.reshape(1, D))


def _rope_tables(seq):
    d = DIFF_HEAD_DIM
    pos = jnp.arange(seq, dtype=F32)
    inv = ROPE_THETA ** (-jnp.arange(0, d, 2, dtype=F32) / d)
    ang = pos[:, None] * inv[None, :]
    cos, sin = jnp.cos(ang), jnp.sin(ang)
    z = jnp.zeros_like(sin)
    cos_t = jnp.tile(jnp.concatenate([cos, cos], -1), (1, LANES // d))
    s1_t = jnp.tile(jnp.concatenate([-sin, z], -1), (1, LANES // d))
    s2_t = jnp.tile(jnp.concatenate([z, sin], -1), (1, LANES // d))
    return cos_t, s1_t, s2_t


def _hyena_filters(seq, w1, b1, w_inner, b_inner, freq, w_out):
    pos = jnp.arange(seq, dtype=F32)
    t = pos / (seq - 1)
    bands = (FILTER_EMB - 1) // 2
    f = jnp.linspace(1e-4, bands - 1, bands, dtype=F32)
    fw = ((2.0 * math.pi / seq) * pos)[:, None] * f[None, :]
    z = jnp.concatenate([t[:, None], jnp.cos(fw), -jnp.sin(fw)], axis=-1)
    hp = lax.Precision.HIGHEST
    h = jnp.sin(freq[0] * (jnp.dot(z, w1, precision=hp) + b1))
    for j in range(FILTER_INNER):
        h = jnp.sin(freq[j + 1] * (jnp.dot(h, w_inner[j], precision=hp) + b_inner[j]))
    h = jnp.dot(h, w_out, precision=hp)
    max_decay = math.log(DECAY_TARGET) / FAST_DECAY_PCT
    min_decay = math.log(DECAY_TARGET) / SLOW_DECAY_PCT
    deltas = jnp.abs(jnp.linspace(min_decay, max_decay, HYENA_WIDTH, dtype=F32))
    decay = jnp.exp(-t[:, None] * deltas[None, :])
    return h[:, :HYENA_WIDTH] * decay, h[:, HYENA_WIDTH:] * decay


def _long_conv(u, h_fwd, h_bwd):
    B, L, C = u.shape
    n = 2 * L
    lag0 = h_fwd[:1] + h_bwd[:1]
    kern = jnp.concatenate([lag0, h_fwd[1:], jnp.zeros((1, C), F32), h_bwd[:0:-1]], axis=0)
    u_f = jnp.fft.rfft(u, n=n, axis=1)
    k_f = jnp.fft.rfft(kern, n=n, axis=0)
    return jnp.fft.irfft(u_f * k_f[None], n=n, axis=1)[:, :L]


def _route(logits, T):
    tm = EXPERT_TILE
    top_logit, top_e = lax.top_k(logits, TOP_K)
    top_w = jax.nn.softmax(top_logit, axis=-1)
    A = T * TOP_K
    flat_e = top_e.reshape(A)
    order = jnp.argsort(flat_e)
    sorted_e = flat_e[order]
    sorted_tok = (order // TOP_K).astype(jnp.int32)
    counts = jnp.bincount(flat_e, length=N_EXPERTS)
    padded = (counts + tm - 1) // tm * tm
    pad_end = jnp.cumsum(padded)
    pad_start = pad_end - padded
    start = jnp.cumsum(counts) - counts
    dest = (pad_start[sorted_e] + jnp.arange(A) - start[sorted_e]).astype(jnp.int32)
    n_blocks = -(-A // tm) + N_EXPERTS
    P = n_blocks * tm
    slot_tok = jnp.full((P,), T, jnp.int32).at[dest].set(sorted_tok)
    block_e = jnp.minimum(
        jnp.searchsorted(pad_end, jnp.arange(n_blocks) * tm, side='right'), N_EXPERTS - 1
    ).astype(jnp.int32)
    n_used = (pad_end[-1] // tm).astype(jnp.int32).reshape(1)
    pos = jnp.zeros((A,), jnp.int32).at[order].set(dest)
    return top_w, slot_tok, block_e, n_used, pos


def kernel(x, p, g_mix, w_in, hyena_conv_w, hyena_conv_b, flt_w1, flt_b1, flt_w_inner, flt_b_inner, flt_freq, flt_w_out, hyena_d, hyena_gn, lambda_q1, lambda_k1, lambda_q2, lambda_k2, attn_subln, w_out, g_ffn, router_w, router_b, w_gate, b_gate, w_up, b_up, w_down, b_down, w_ple, g_ple, w_ple_gate, b_ple_gate, g_final):
    B, S, D = x.shape
    T = B * S
    i = 0
    x2 = x.reshape(T, D)

    cos_t, s1_t, s2_t = _rope_tables(S)
    q, k, v, hy = _in_proj(x2, g_mix[i], w_in[i].astype(BF16), cos_t, s1_t, s2_t, S)

    lam = (jnp.exp(jnp.sum(lambda_q1[i] * lambda_k1[i])) - jnp.exp(jnp.sum(lambda_q2[i] * lambda_k2[i]))
           + LAMBDA_INIT).reshape(1).astype(F32)
    attn = _diff_attention(lam, q.reshape(B, S, -1), k.reshape(B, S, -1), v.reshape(B, S, -1),
                           attn_subln[i])

    u, hx0 = _short_conv(hy.reshape(B, S, -1), hyena_conv_w[i], hyena_conv_b[i])
    h_fwd, h_bwd = _hyena_filters(S, flt_w1[i], flt_b1[i], flt_w_inner[i], flt_b_inner[i],
                                  flt_freq[i], flt_w_out[i])
    yc = _long_conv(u, h_fwd, h_bwd)

    h1, hn, logits = _out_proj(attn.reshape(T, -1), yc.reshape(T, -1), u.reshape(T, -1),
                               hx0.reshape(T, -1), hyena_d[i], hyena_gn[i], w_out[i].astype(BF16),
                               x2, g_ffn[i], router_w[i], router_b[i])

    top_w, slot_tok, block_e, n_used, pos = _route(logits[:, :N_EXPERTS], T)
    hn_pad = jnp.concatenate([hn, jnp.zeros((1, D), hn.dtype)], axis=0)
    x_sorted = hn_pad[slot_tok]
    y = _experts(block_e, n_used, x_sorted, w_gate[i].astype(BF16), b_gate[i], w_up[i].astype(BF16),
                 b_up[i], w_down[i].astype(BF16), b_down[i])
    moe = jnp.sum(y[pos].reshape(T, TOP_K, D) * top_w[..., None], axis=1)

    out = _final(h1, moe, p[i].reshape(T, -1), w_ple[i].astype(BF16), g_ple[i],
                 w_ple_gate[i].astype(BF16), b_ple_gate[i], g_final)
    return out.reshape(B, S, D)
```

```python
import functools
import math

import jax
import jax.numpy as jnp
from jax import lax
from jax.experimental import pallas as pl
from jax.experimental.pallas import tpu as pltpu

F32 = jnp.float32
BF16 = jnp.bfloat16

D_MODEL = 1024
ATTN_WIDTH = 512
HYENA_WIDTH = 512
DIFF_HEADS = 4
DIFF_HEAD_DIM = 64
DIFF_V_DIM = 128
HYENA_GROUPS = 8
FILTER_EMB = 33
FILTER_INNER = 2
FAST_DECAY_PCT = 0.3
SLOW_DECAY_PCT = 1.5
DECAY_TARGET = 1e-2
ROPE_THETA = 10000.0
N_EXPERTS = 32
TOP_K = 4
SWIGLU_ALPHA = 1.702
SWIGLU_LIMIT = 7.0
EPS = 1e-6
SUBLN_EPS = 1e-5
LAMBDA_INIT = 0.8 - 0.6 * math.exp(-0.3 * 0)

V7X_VMEM_LIMIT_BYTES = 56 * 1024 * 1024
LANES = 128
SUBLANES = 8

TOKEN_TILE = 512
ATTN_Q_TILE = 256
ATTN_KV_TILE = 512
EXPERT_TILE = 512
ROUTER_PAD = LANES


def _params(*sem):
    return pltpu.CompilerParams(dimension_semantics=sem, vmem_limit_bytes=V7X_VMEM_LIMIT_BYTES)


def _rms(x, g, eps):
    return x * lax.rsqrt(jnp.mean(x * x, axis=-1, keepdims=True) + eps) * g


def _inproj_kernel(x_ref, g_ref, w_ref, wvt_ref, c_ref, s1_ref, s2_ref, q_ref, k_ref, vt_ref, hy_ref):
    a = _rms(x_ref[...], g_ref[...], EPS).astype(BF16)
    cos, s1, s2 = c_ref[...], s1_ref[...], s2_ref[...]

    def rope(t):
        return t * cos + pltpu.roll(t, LANES - 32, axis=1) * s1 + pltpu.roll(t, 32, axis=1) * s2

    q_scale = (DIFF_HEAD_DIM ** -0.5) * math.log2(math.e)
    qk = jnp.dot(a, w_ref[:, : 2 * ATTN_WIDTH], preferred_element_type=F32)
    for j in range(ATTN_WIDTH // LANES):
        sl = slice(j * LANES, (j + 1) * LANES)
        q_ref[:, sl] = (rope(qk[:, sl]) * q_scale).astype(BF16)
        k_ref[:, sl] = rope(qk[:, ATTN_WIDTH + j * LANES: ATTN_WIDTH + (j + 1) * LANES]).astype(BF16)
    vt_ref[...] = lax.dot_general(wvt_ref[...], a, (((1,), (1,)), ((), ())),
                                  preferred_element_type=F32).astype(BF16)
    hy_ref[...] = jnp.dot(a, w_ref[:, 3 * ATTN_WIDTH:], preferred_element_type=F32)


def _in_proj(x2, g_mix, w_in_bf, cos_t, s1_t, s2_t, seq):
    T, D = x2.shape
    tm = TOKEN_TILE
    nseq = seq // tm
    wvt = w_in_bf[:, 2 * ATTN_WIDTH: 3 * ATTN_WIDTH].T
    row = lambda i: (i, 0)
    const = lambda i: (0, 0)
    pos = lambda i: (i % nseq, 0)
    return pl.pallas_call(
        _inproj_kernel,
        grid=(T // tm,),
        in_specs=[
            pl.BlockSpec((tm, D), row),
            pl.BlockSpec((1, D), const),
            pl.BlockSpec(w_in_bf.shape, const),
            pl.BlockSpec(wvt.shape, const),
            pl.BlockSpec((tm, LANES), pos),
            pl.BlockSpec((tm, LANES), pos),
            pl.BlockSpec((tm, LANES), pos),
        ],
        out_specs=[
            pl.BlockSpec((tm, ATTN_WIDTH), row),
            pl.BlockSpec((tm, ATTN_WIDTH), row),
            pl.BlockSpec((ATTN_WIDTH, tm), lambda i: (0, i)),
            pl.BlockSpec((tm, 3 * HYENA_WIDTH), row),
        ],
        out_shape=[
            jax.ShapeDtypeStruct((T, ATTN_WIDTH), BF16),
            jax.ShapeDtypeStruct((T, ATTN_WIDTH), BF16),
            jax.ShapeDtypeStruct((ATTN_WIDTH, T), BF16),
            jax.ShapeDtypeStruct((T, 3 * HYENA_WIDTH), F32),
        ],
        compiler_params=_params("parallel"),
        name="in_proj",
    )(x2, g_mix.reshape(1, D), w_in_bf, wvt, cos_t, s1_t, s2_t)


def _attn_kernel(lam_ref, q_ref, k_ref, vt_ref, g_ref, o_ref, s_ref, *, kc):
    q = q_ref[0]
    tq = q.shape[0]
    nc = k_ref.shape[1] // kc
    lane = lax.broadcasted_iota(jnp.int32, q.shape, 1)
    zero = jnp.zeros_like(q)
    qs = (jnp.where(lane < DIFF_HEAD_DIM, q, zero), jnp.where(lane >= DIFF_HEAD_DIM, q, zero))

    def fold8(t, op):
        r = t[0:SUBLANES]
        for j in range(1, kc // SUBLANES):
            r = op(r, t[j * SUBLANES:(j + 1) * SUBLANES])
        return r

    outs = []
    for c in range(2):
        m8 = jnp.full((SUBLANES, tq), -jnp.inf, F32)
        for j in range(nc):
            s = lax.dot_general(k_ref[0, j * kc:(j + 1) * kc, :], qs[c], (((1,), (1,)), ((), ())),
                                preferred_element_type=F32)
            s_ref[c, j * kc:(j + 1) * kc, :] = s
            m8 = jnp.maximum(m8, fold8(s, jnp.maximum))
        m = jnp.max(m8, axis=0, keepdims=True)
        l8 = jnp.zeros((SUBLANES, tq), F32)
        acc = jnp.zeros((DIFF_V_DIM, tq), F32)
        for j in range(nc):
            p = jnp.exp2(s_ref[c, j * kc:(j + 1) * kc, :] - m)
            l8 = l8 + fold8(p, jnp.add)
            acc = acc + jnp.dot(vt_ref[:, j * kc:(j + 1) * kc], p.astype(BF16),
                                preferred_element_type=F32)
        outs.append(acc * (1.0 / jnp.sum(l8, axis=0, keepdims=True)))
    o = outs[0] - lam_ref[0] * outs[1]
    o = o * lax.rsqrt(jnp.mean(o * o, axis=0, keepdims=True) + SUBLN_EPS)
    o = o * (g_ref[...] * (1.0 - LAMBDA_INIT))
    o_ref[0] = o.T.astype(o_ref.dtype)


def _diff_attention(lam, q, k, vt, g_subln):
    B, S, _ = q.shape
    tq, kc = ATTN_Q_TILE, ATTN_KV_TILE
    return pl.pallas_call(
        functools.partial(_attn_kernel, kc=kc),
        grid=(B, DIFF_HEADS, S // tq),
        in_specs=[
            pl.BlockSpec(memory_space=pltpu.SMEM),
            pl.BlockSpec((1, tq, LANES), lambda b, h, i: (b, i, h)),
            pl.BlockSpec((1, S, LANES), lambda b, h, i: (b, 0, h)),
            pl.BlockSpec((DIFF_V_DIM, S), lambda b, h, i: (h, b)),
            pl.BlockSpec((DIFF_V_DIM, 1), lambda b, h, i: (0, 0)),
        ],
        out_specs=pl.BlockSpec((1, tq, LANES), lambda b, h, i: (b, i, h)),
        out_shape=jax.ShapeDtypeStruct((B, S, ATTN_WIDTH), BF16),
        scratch_shapes=[pltpu.VMEM((2, S, tq), F32)],
        compiler_params=_params("parallel", "parallel", "parallel"),
        name="diff_attn",
    )(lam, q, k, vt, g_subln.reshape(DIFF_V_DIM, 1))


def _shortconv_kernel(hy_ref, prev_ref, next_ref, w_ref, b_ref, u_ref, x0_ref):
    i = pl.program_id(1)
    last = pl.num_programs(1) - 1
    x = hy_ref[0]
    ts = x.shape[0]
    prev_row = jnp.where(i == 0, 0.0, prev_ref[0, SUBLANES - 1:SUBLANES, :])
    next_row = jnp.where(i == last, 0.0, next_ref[0, 0:1, :])
    row = lax.broadcasted_iota(jnp.int32, (ts, 1), 0)
    xm = jnp.where(row == 0, prev_row, pltpu.roll(x, 1, axis=0))
    xp = jnp.where(row == ts - 1, next_row, pltpu.roll(x, ts - 1, axis=0))
    y = b_ref[...] + xm * w_ref[0:1, :] + x * w_ref[1:2, :] + xp * w_ref[2:3, :]
    C = HYENA_WIDTH
    u_ref[0] = y[:, :C] * y[:, 2 * C:]
    x0_ref[0] = y[:, C:2 * C]


def _short_conv(hy, conv_w, conv_b):
    B, S, C3 = hy.shape
    ts = TOKEN_TILE
    nb = ts // SUBLANES
    return pl.pallas_call(
        _shortconv_kernel,
        grid=(B, S // ts),
        in_specs=[
            pl.BlockSpec((1, ts, C3), lambda b, i: (b, i, 0)),
            pl.BlockSpec((1, SUBLANES, C3), lambda b, i: (b, jnp.maximum(i * nb - 1, 0), 0)),
            pl.BlockSpec((1, SUBLANES, C3), lambda b, i: (b, jnp.minimum((i + 1) * nb, S // SUBLANES - 1), 0)),
            pl.BlockSpec((3, C3), lambda b, i: (0, 0)),
            pl.BlockSpec((1, C3), lambda b, i: (0, 0)),
        ],
        out_specs=[
            pl.BlockSpec((1, ts, HYENA_WIDTH), lambda b, i: (b, i, 0)),
            pl.BlockSpec((1, ts, HYENA_WIDTH), lambda b, i: (b, i, 0)),
        ],
        out_shape=[
            jax.ShapeDtypeStruct((B, S, HYENA_WIDTH), F32),
            jax.ShapeDtypeStruct((B, S, HYENA_WIDTH), F32),
        ],
        compiler_params=_params("parallel", "parallel"),
        name="short_conv",
    )(hy, hy, hy, conv_w, conv_b.reshape(1, C3))


def _split_bf16(x):
    hi = x.astype(BF16)
    lo = (x - hi.astype(F32)).astype(BF16)
    return hi, lo


def _outproj_kernel(attn_ref, yc_ref, u_ref, x0_ref, d_ref, gn_ref, grp_ref, wo_ref, x_ref,
                    gf_ref, rwh_ref, rwl_ref, rb_ref, h_ref, hn_ref, lg_ref):
    z = (yc_ref[...] + u_ref[...] * d_ref[...]) * x0_ref[...]
    zh, zl = _split_bf16(z * z)
    grp = grp_ref[...]
    ssq = (jnp.dot(zh, grp, preferred_element_type=F32) + jnp.dot(zl, grp, preferred_element_type=F32))
    gsz = HYENA_WIDTH // HYENA_GROUPS
    hy_out = (z * lax.rsqrt(ssq * (1.0 / gsz) + EPS) * gn_ref[...]).astype(BF16)
    mix = (jnp.dot(attn_ref[...], wo_ref[:ATTN_WIDTH, :], preferred_element_type=F32)
           + jnp.dot(hy_out, wo_ref[ATTN_WIDTH:, :], preferred_element_type=F32))
    h = x_ref[...] + mix
    h_ref[...] = h
    hn = _rms(h, gf_ref[...], EPS)
    hn_ref[...] = hn.astype(BF16)
    nh, nl = _split_bf16(hn)
    lg_ref[...] = (jnp.dot(nh, rwh_ref[...], preferred_element_type=F32)
                   + jnp.dot(nl, rwh_ref[...], preferred_element_type=F32)
                   + jnp.dot(nh, rwl_ref[...], preferred_element_type=F32)) + rb_ref[...]


def _out_proj(attn2, yc2, u2, x02, hyena_d, hyena_gn, w_out_bf, x2, g_ffn, router_w, router_b):
    T, D = x2.shape
    tm = TOKEN_TILE
    C = HYENA_WIDTH
    gid = jnp.arange(C) // (C // HYENA_GROUPS)
    grp = (gid[:, None] == gid[None, :]).astype(BF16)
    rw = jnp.zeros((D, ROUTER_PAD), F32).at[:, :N_EXPERTS].set(router_w)
    rwh, rwl = _split_bf16(rw)
    rb = jnp.zeros((1, ROUTER_PAD), F32).at[0, :N_EXPERTS].set(router_b)
    row = lambda i: (i, 0)
    const = lambda i: (0, 0)
    return pl.pallas_call(
        _outproj_kernel,
        grid=(T // tm,),
        in_specs=[
            pl.BlockSpec((tm, ATTN_WIDTH), row),
            pl.BlockSpec((tm, C), row),
            pl.BlockSpec((tm, C), row),
            pl.BlockSpec((tm, C), row),
            pl.BlockSpec((1, C), const),
            pl.BlockSpec((1, C), const),
            pl.BlockSpec((C, C), const),
            pl.BlockSpec((D, D), const),
            pl.BlockSpec((tm, D), row),
            pl.BlockSpec((1, D), const),
            pl.BlockSpec((D, ROUTER_PAD), const),
            pl.BlockSpec((D, ROUTER_PAD), const),
            pl.BlockSpec((1, ROUTER_PAD), const),
        ],
        out_specs=[
            pl.BlockSpec((tm, D), row),
            pl.BlockSpec((tm, D), row),
            pl.BlockSpec((tm, ROUTER_PAD), row),
        ],
        out_shape=[
            jax.ShapeDtypeStruct((T, D), F32),
            jax.ShapeDtypeStruct((T, D), BF16),
            jax.ShapeDtypeStruct((T, ROUTER_PAD), F32),
        ],
        compiler_params=_params("parallel"),
        name="out_proj",
    )(attn2, yc2, u2, x02, hyena_d.reshape(1, C), hyena_gn.reshape(1, C), grp, w_out_bf, x2,
      g_ffn.reshape(1, D), rwh, rwl, rb)


def _expert_kernel(be_ref, nused_ref, x_ref, wg_ref, bg_ref, wu_ref, bu_ref, wd_ref, bd_ref, y_ref):
    i = pl.program_id(0)

    @pl.when(i < nused_ref[0])
    def _():
        x = x_ref[...]
        g = jnp.minimum(jnp.dot(x, wg_ref[0], preferred_element_type=F32) + bg_ref[0], SWIGLU_LIMIT)
        u = jnp.clip(jnp.dot(x, wu_ref[0], preferred_element_type=F32) + bu_ref[0],
                     -SWIGLU_LIMIT, SWIGLU_LIMIT)
        a = (u + 1.0) * (g * jax.nn.sigmoid(SWIGLU_ALPHA * g))
        y_ref[...] = jnp.dot(a.astype(BF16), wd_ref[0], preferred_element_type=F32) + bd_ref[0]

    @pl.when(i >= nused_ref[0])
    def _():
        y_ref[...] = jnp.zeros_like(y_ref)


def _experts(block_e, n_used, x_sorted, wg, bg, wu, bu, wd, bd):
    P, D = x_sorted.shape
    tm = EXPERT_TILE
    E, _, FF = wg.shape
    wmap = lambda i, be, nu: (be[i], 0, 0)
    grid_spec = pltpu.PrefetchScalarGridSpec(
        num_scalar_prefetch=2,
        grid=(P // tm,),
        in_specs=[
            pl.BlockSpec((tm, D), lambda i, be, nu: (i, 0)),
            pl.BlockSpec((1, D, FF), wmap),
            pl.BlockSpec((1, 1, FF), wmap),
            pl.BlockSpec((1, D, FF), wmap),
            pl.BlockSpec((1, 1, FF), wmap),
            pl.BlockSpec((1, FF, D), wmap),
            pl.BlockSpec((1, 1, D), wmap),
        ],
        out_specs=pl.BlockSpec((tm, D), lambda i, be, nu: (i, 0)),
    )
    return pl.pallas_call(
        _expert_kernel,
        grid_spec=grid_spec,
        out_shape=jax.ShapeDtypeStruct((P, D), F32),
        compiler_params=_params("arbitrary"),
        name="moe_experts",
    )(block_e, n_used, x_sorted, wg, bg.reshape(E, 1, FF), wu, bu.reshape(E, 1, FF),
      wd, bd.reshape(E, 1, D))


def _final_kernel(h_ref, moe_ref, p_ref, wp_ref, gp_ref, wg_ref, bg_ref, gfin_ref, o_ref):
    h = h_ref[...] + moe_ref[...]
    e = _rms(jnp.dot(p_ref[...].astype(BF16), wp_ref[...], preferred_element_type=F32),
             gp_ref[...], EPS)
    gate = jax.nn.sigmoid(jnp.dot(h.astype(BF16), wg_ref[...], preferred_element_type=F32)
                          + bg_ref[...])
    h = h + gate * e
    o_ref[...] = _rms(h, gfin_ref[...], EPS)


def _final(h1, moe, p2, w_ple_bf, g_ple, w_gate_bf, b_gate, g_final):
    T, D = h1.shape
    tm = TOKEN_TILE
    PD = p2.shape[1]
    row = lambda i: (i, 0)
    const = lambda i: (0, 0)
    return pl.pallas_call(
        _final_kernel,
        grid=(T // tm,),
        in_specs=[
            pl.BlockSpec((tm, D), row),
            pl.BlockSpec((tm, D), row),
            pl.BlockSpec((tm, PD), row),
            pl.BlockSpec((PD, D), const),
            pl.BlockSpec((1, D), const),
            pl.BlockSpec((D, D), const),
            pl.BlockSpec((1, D), const),
            pl.BlockSpec((1, D), const),
        ],
        out_specs=pl.BlockSpec((tm, D), row),
        out_shape=jax.ShapeDtypeStruct((T, D), F32),
        compiler_params=_params("parallel"),
        name="final",
    )(h1, moe, p2, w_ple_bf, g_ple.reshape(1, D), w_gate_bf, b_gate.reshape(1, D),
      g_final.reshape(1, D))


def _rope_tables(seq):
    d = DIFF_HEAD_DIM
    pos = jnp.arange(seq, dtype=F32)
    inv = ROPE_THETA ** (-jnp.arange(0, d, 2, dtype=F32) / d)
    ang = pos[:, None] * inv[None, :]
    cos, sin = jnp.cos(ang), jnp.sin(ang)
    z = jnp.zeros_like(sin)
    cos_t = jnp.tile(jnp.concatenate([cos, cos], -1), (1, LANES // d))
    s1_t = jnp.tile(jnp.concatenate([-sin, z], -1), (1, LANES // d))
    s2_t = jnp.tile(jnp.concatenate([z, sin], -1), (1, LANES // d))
    return cos_t, s1_t, s2_t


def _hyena_filters(seq, w1, b1, w_inner, b_inner, freq, w_out):
    pos = jnp.arange(seq, dtype=F32)
    t = pos / (seq - 1)
    bands = (FILTER_EMB - 1) // 2
    f = jnp.linspace(1e-4, bands - 1, bands, dtype=F32)
    fw = ((2.0 * math.pi / seq) * pos)[:, None] * f[None, :]
    z = jnp.concatenate([t[:, None], jnp.cos(fw), -jnp.sin(fw)], axis=-1)
    hp = lax.Precision.HIGHEST
    h = jnp.sin(freq[0] * (jnp.dot(z, w1, precision=hp) + b1))
    for j in range(FILTER_INNER):
        h = jnp.sin(freq[j + 1] * (jnp.dot(h, w_inner[j], precision=hp) + b_inner[j]))
    h = jnp.dot(h, w_out, precision=hp)
    max_decay = math.log(DECAY_TARGET) / FAST_DECAY_PCT
    min_decay = math.log(DECAY_TARGET) / SLOW_DECAY_PCT
    deltas = jnp.abs(jnp.linspace(min_decay, max_decay, HYENA_WIDTH, dtype=F32))
    decay = jnp.exp(-t[:, None] * deltas[None, :])
    return h[:, :HYENA_WIDTH] * decay, h[:, HYENA_WIDTH:] * decay


def _long_conv(u, h_fwd, h_bwd):
    B, L, C = u.shape
    n = 2 * L
    lag0 = h_fwd[:1] + h_bwd[:1]
    kern = jnp.concatenate([lag0, h_fwd[1:], jnp.zeros((1, C), F32), h_bwd[:0:-1]], axis=0)
    u_f = jnp.fft.rfft(u, n=n, axis=1)
    k_f = jnp.fft.rfft(kern, n=n, axis=0)
    return jnp.fft.irfft(u_f * k_f[None], n=n, axis=1)[:, :L]


def _route(logits, T):
    tm = EXPERT_TILE
    top_logit, top_e = lax.top_k(logits, TOP_K)
    top_w = jax.nn.softmax(top_logit, axis=-1)
    A = T * TOP_K
    flat_e = top_e.reshape(A)
    order = jnp.argsort(flat_e)
    sorted_e = flat_e[order]
    sorted_tok = (order // TOP_K).astype(jnp.int32)
    counts = jnp.bincount(flat_e, length=N_EXPERTS)
    padded = (counts + tm - 1) // tm * tm
    pad_end = jnp.cumsum(padded)
    pad_start = pad_end - padded
    start = jnp.cumsum(counts) - counts
    dest = (pad_start[sorted_e] + jnp.arange(A) - start[sorted_e]).astype(jnp.int32)
    n_blocks = -(-A // tm) + N_EXPERTS
    P = n_blocks * tm
    slot_tok = jnp.full((P,), T, jnp.int32).at[dest].set(sorted_tok)
    block_e = jnp.minimum(
        jnp.searchsorted(pad_end, jnp.arange(n_blocks) * tm, side='right'), N_EXPERTS - 1
    ).astype(jnp.int32)
    n_used = (pad_end[-1] // tm).astype(jnp.int32).reshape(1)
    pos = jnp.zeros((A,), jnp.int32).at[order].set(dest)
    return top_w, slot_tok, block_e, n_used, pos


def kernel(x, p, g_mix, w_in, hyena_conv_w, hyena_conv_b, flt_w1, flt_b1, flt_w_inner, flt_b_inner, flt_freq, flt_w_out, hyena_d, hyena_gn, lambda_q1, lambda_k1, lambda_q2, lambda_k2, attn_subln, w_out, g_ffn, router_w, router_b, w_gate, b_gate, w_up, b_up, w_down, b_down, w_ple, g_ple, w_ple_gate, b_ple_gate, g_final):
    B, S, D = x.shape
    T = B * S
    i = 0
    x2 = x.reshape(T, D)

    cos_t, s1_t, s2_t = _rope_tables(S)
    q, k, vt, hy = _in_proj(x2, g_mix[i], w_in[i].astype(BF16), cos_t, s1_t, s2_t, S)

    lam = (jnp.exp(jnp.sum(lambda_q1[i] * lambda_k1[i])) - jnp.exp(jnp.sum(lambda_q2[i] * lambda_k2[i]))
           + LAMBDA_INIT).reshape(1).astype(F32)
    attn = _diff_attention(lam, q.reshape(B, S, -1), k.reshape(B, S, -1), vt, attn_subln[i])

    u, hx0 = _short_conv(hy.reshape(B, S, -1), hyena_conv_w[i], hyena_conv_b[i])
    h_fwd, h_bwd = _hyena_filters(S, flt_w1[i], flt_b1[i], flt_w_inner[i], flt_b_inner[i],
                                  flt_freq[i], flt_w_out[i])
    yc = _long_conv(u, h_fwd, h_bwd)

    h1, hn, logits = _out_proj(attn.reshape(T, -1), yc.reshape(T, -1), u.reshape(T, -1),
                               hx0.reshape(T, -1), hyena_d[i], hyena_gn[i], w_out[i].astype(BF16),
                               x2, g_ffn[i], router_w[i], router_b[i])

    top_w, slot_tok, block_e, n_used, pos = _route(logits[:, :N_EXPERTS], T)
    hn_pad = jnp.concatenate([hn, jnp.zeros((1, D), hn.dtype)], axis=0)
    x_sorted = hn_pad[slot_tok]
    y = _experts(block_e, n_used, x_sorted, w_gate[i].astype(BF16), b_gate[i], w_up[i].astype(BF16),
                 b_up[i], w_down[i].astype(BF16), b_down[i])
    moe = jnp.sum(y[pos].reshape(T, TOP_K, D) * top_w[..., None], axis=1)

    out = _final(h1, moe, p[i].reshape(T, -1), w_ple[i].astype(BF16), g_ple[i],
                 w_ple_gate[i].astype(BF16), b_ple_gate[i], g_final)
    return out.reshape(B, S, D)
```

```python
import functools
import math

import jax
import jax.numpy as jnp
from jax import lax
from jax.experimental import pallas as pl
from jax.experimental.pallas import tpu as pltpu

F32 = jnp.float32
BF16 = jnp.bfloat16

D_MODEL = 1024
ATTN_WIDTH = 512
HYENA_WIDTH = 512
DIFF_HEADS = 4
DIFF_HEAD_DIM = 64
DIFF_V_DIM = 128
HYENA_GROUPS = 8
FILTER_EMB = 33
FILTER_INNER = 2
FAST_DECAY_PCT = 0.3
SLOW_DECAY_PCT = 1.5
DECAY_TARGET = 1e-2
ROPE_THETA = 10000.0
N_EXPERTS = 32
TOP_K = 4
SWIGLU_ALPHA = 1.702
SWIGLU_LIMIT = 7.0
EPS = 1e-6
SUBLN_EPS = 1e-5
LAMBDA_INIT = 0.8 - 0.6 * math.exp(-0.3 * 0)

V7X_VMEM_LIMIT_BYTES = 56 * 1024 * 1024
LANES = 128
SUBLANES = 8

TOKEN_TILE = 512
ATTN_Q_TILE = 256
ATTN_KV_TILE = 512
EXPERT_TILE = 512
ROUTER_PAD = LANES


def _params(*sem):
    return pltpu.CompilerParams(dimension_semantics=sem, vmem_limit_bytes=V7X_VMEM_LIMIT_BYTES)


def _rms(x, g, eps):
    return x * lax.rsqrt(jnp.mean(x * x, axis=-1, keepdims=True) + eps) * g


def _inproj_kernel(x_ref, g_ref, w_ref, wvt_ref, c_ref, s1_ref, s2_ref, q_ref, k_ref, vt_ref, hy_ref):
    a = _rms(x_ref[...], g_ref[...], EPS).astype(BF16)
    cos, s1, s2 = c_ref[...], s1_ref[...], s2_ref[...]

    def rope(t):
        return t * cos + pltpu.roll(t, LANES - 32, axis=1) * s1 + pltpu.roll(t, 32, axis=1) * s2

    q_scale = (DIFF_HEAD_DIM ** -0.5) * math.log2(math.e)
    qk = jnp.dot(a, w_ref[:, : 2 * ATTN_WIDTH], preferred_element_type=F32)
    for j in range(ATTN_WIDTH // LANES):
        sl = slice(j * LANES, (j + 1) * LANES)
        q_ref[:, sl] = (rope(qk[:, sl]) * q_scale).astype(BF16)
        k_ref[:, sl] = rope(qk[:, ATTN_WIDTH + j * LANES: ATTN_WIDTH + (j + 1) * LANES]).astype(BF16)
    vt_ref[...] = lax.dot_general(wvt_ref[...], a, (((1,), (1,)), ((), ())),
                                  preferred_element_type=F32).astype(BF16)
    hy_ref[...] = jnp.dot(a, w_ref[:, 3 * ATTN_WIDTH:], preferred_element_type=F32)


def _in_proj(x2, g_mix, w_in_bf, cos_t, s1_t, s2_t, seq):
    T, D = x2.shape
    tm = TOKEN_TILE
    nseq = seq // tm
    wvt = w_in_bf[:, 2 * ATTN_WIDTH: 3 * ATTN_WIDTH].T
    row = lambda i: (i, 0)
    const = lambda i: (0, 0)
    pos = lambda i: (i % nseq, 0)
    return pl.pallas_call(
        _inproj_kernel,
        grid=(T // tm,),
        in_specs=[
            pl.BlockSpec((tm, D), row),
            pl.BlockSpec((1, D), const),
            pl.BlockSpec(w_in_bf.shape, const),
            pl.BlockSpec(wvt.shape, const),
            pl.BlockSpec((tm, LANES), pos),
            pl.BlockSpec((tm, LANES), pos),
            pl.BlockSpec((tm, LANES), pos),
        ],
        out_specs=[
            pl.BlockSpec((tm, ATTN_WIDTH), row),
            pl.BlockSpec((tm, ATTN_WIDTH), row),
            pl.BlockSpec((ATTN_WIDTH, tm), lambda i: (0, i)),
            pl.BlockSpec((tm, 3 * HYENA_WIDTH), row),
        ],
        out_shape=[
            jax.ShapeDtypeStruct((T, ATTN_WIDTH), BF16),
            jax.ShapeDtypeStruct((T, ATTN_WIDTH), BF16),
            jax.ShapeDtypeStruct((ATTN_WIDTH, T), BF16),
            jax.ShapeDtypeStruct((T, 3 * HYENA_WIDTH), F32),
        ],
        compiler_params=_params("parallel"),
        name="in_proj",
    )(x2, g_mix.reshape(1, D), w_in_bf, wvt, cos_t, s1_t, s2_t)


def _attn_kernel(lam_ref, q_ref, k_ref, vt_ref, g_ref, o_ref, s_ref, *, kc):
    q = q_ref[0]
    tq = q.shape[0]
    nc = k_ref.shape[1] // kc
    lane = lax.broadcasted_iota(jnp.int32, q.shape, 1)
    zero = jnp.zeros_like(q)
    qs = (jnp.where(lane < DIFF_HEAD_DIM, q, zero), jnp.where(lane >= DIFF_HEAD_DIM, q, zero))

    def fold8(t, op):
        r = t[0:SUBLANES]
        for j in range(1, kc // SUBLANES):
            r = op(r, t[j * SUBLANES:(j + 1) * SUBLANES])
        return r

    outs = []
    for c in range(2):
        m8 = jnp.full((SUBLANES, tq), -jnp.inf, F32)
        for j in range(nc):
            s = lax.dot_general(k_ref[0, j * kc:(j + 1) * kc, :], qs[c], (((1,), (1,)), ((), ())),
                                preferred_element_type=F32)
            s_ref[c, j * kc:(j + 1) * kc, :] = s
            m8 = jnp.maximum(m8, fold8(s, jnp.maximum))
        m = jnp.max(m8, axis=0, keepdims=True)
        l8 = jnp.zeros((SUBLANES, tq), F32)
        acc = jnp.zeros((DIFF_V_DIM, tq), F32)
        for j in range(nc):
            p = jnp.exp2(s_ref[c, j * kc:(j + 1) * kc, :] - m)
            l8 = l8 + fold8(p, jnp.add)
            acc = acc + jnp.dot(vt_ref[:, j * kc:(j + 1) * kc], p.astype(BF16),
                                preferred_element_type=F32)
        outs.append(acc * (1.0 / jnp.sum(l8, axis=0, keepdims=True)))
    o = outs[0] - lam_ref[0] * outs[1]
    o = o * lax.rsqrt(jnp.mean(o * o, axis=0, keepdims=True) + SUBLN_EPS)
    o = o * (g_ref[...] * (1.0 - LAMBDA_INIT))
    o_ref[0] = o.T.astype(o_ref.dtype)


def _diff_attention(lam, q, k, vt, g_subln):
    B, S, _ = q.shape
    tq, kc = ATTN_Q_TILE, ATTN_KV_TILE
    return pl.pallas_call(
        functools.partial(_attn_kernel, kc=kc),
        grid=(B, DIFF_HEADS, S // tq),
        in_specs=[
            pl.BlockSpec(memory_space=pltpu.SMEM),
            pl.BlockSpec((1, tq, LANES), lambda b, h, i: (b, i, h)),
            pl.BlockSpec((1, S, LANES), lambda b, h, i: (b, 0, h)),
            pl.BlockSpec((DIFF_V_DIM, S), lambda b, h, i: (h, b)),
            pl.BlockSpec((DIFF_V_DIM, 1), lambda b, h, i: (0, 0)),
        ],
        out_specs=pl.BlockSpec((1, tq, LANES), lambda b, h, i: (b, i, h)),
        out_shape=jax.ShapeDtypeStruct((B, S, ATTN_WIDTH), BF16),
        scratch_shapes=[pltpu.VMEM((2, S, tq), F32)],
        compiler_params=_params("parallel", "parallel", "parallel"),
        name="diff_attn",
    )(lam, q, k, vt, g_subln.reshape(DIFF_V_DIM, 1))


def _shortconv_kernel(hy_ref, prev_ref, next_ref, w_ref, b_ref, u_ref, x0_ref):
    i = pl.program_id(1)
    last = pl.num_programs(1) - 1
    x = hy_ref[0]
    ts = x.shape[0]
    prev_row = jnp.where(i == 0, 0.0, prev_ref[0, SUBLANES - 1:SUBLANES, :])
    next_row = jnp.where(i == last, 0.0, next_ref[0, 0:1, :])
    row = lax.broadcasted_iota(jnp.int32, (ts, 1), 0)
    xm = jnp.where(row == 0, prev_row, pltpu.roll(x, 1, axis=0))
    xp = jnp.where(row == ts - 1, next_row, pltpu.roll(x, ts - 1, axis=0))
    y = b_ref[...] + xm * w_ref[0:1, :] + x * w_ref[1:2, :] + xp * w_ref[2:3, :]
    C = HYENA_WIDTH
    u_ref[0] = y[:, :C] * y[:, 2 * C:]
    x0_ref[0] = y[:, C:2 * C]


def _short_conv(hy, conv_w, conv_b):
    B, S, C3 = hy.shape
    ts = TOKEN_TILE
    nb = ts // SUBLANES
    return pl.pallas_call(
        _shortconv_kernel,
        grid=(B, S // ts),
        in_specs=[
            pl.BlockSpec((1, ts, C3), lambda b, i: (b, i, 0)),
            pl.BlockSpec((1, SUBLANES, C3), lambda b, i: (b, jnp.maximum(i * nb - 1, 0), 0)),
            pl.BlockSpec((1, SUBLANES, C3), lambda b, i: (b, jnp.minimum((i + 1) * nb, S // SUBLANES - 1), 0)),
            pl.BlockSpec((3, C3), lambda b, i: (0, 0)),
            pl.BlockSpec((1, C3), lambda b, i: (0, 0)),
        ],
        out_specs=[
            pl.BlockSpec((1, ts, HYENA_WIDTH), lambda b, i: (b, i, 0)),
            pl.BlockSpec((1, ts, HYENA_WIDTH), lambda b, i: (b, i, 0)),
        ],
        out_shape=[
            jax.ShapeDtypeStruct((B, S, HYENA_WIDTH), F32),
            jax.ShapeDtypeStruct((B, S, HYENA_WIDTH), F32),
        ],
        compiler_params=_params("parallel", "parallel"),
        name="short_conv",
    )(hy, hy, hy, conv_w, conv_b.reshape(1, C3))


FFT_N = 8192
FFT_N1 = 64
FFT_N2 = 128
FFT_CH_BLOCK = 32
FFT_UNROLL = 4


def _dft_constants():
    import numpy as np
    n1, n2, n = FFT_N1, FFT_N2, FFT_N
    k1 = np.arange(n1)[:, None]
    t1 = np.arange(n1)[None, :]
    f1 = np.exp(-2j * np.pi * k1 * t1 / n1)
    f1h = f1[:, : n1 // 2]
    w1c = np.block([[f1h.real, -f1h.imag], [f1h.imag, f1h.real]])
    w1r = np.concatenate([f1.real, f1.imag], axis=0)
    t2 = np.arange(n2)[:, None]
    k2 = np.arange(n2)[None, :]
    f2 = np.exp(-2j * np.pi * t2 * k2 / n2)
    w2a = np.concatenate([f2.real, f2.imag], axis=1)
    w2b = np.concatenate([-f2.imag, f2.real], axis=1)
    g2 = np.conj(f2)
    w3a = np.concatenate([g2.real, g2.imag], axis=1)
    w3b = np.concatenate([-g2.imag, g2.real], axis=1)
    h = np.conj(f1).T[: n1 // 2] / n
    w4 = np.block([[h.real, -h.imag], [h.imag, h.real]])
    tw = np.exp(-2j * np.pi * np.arange(n1)[:, None] * np.arange(n2)[None, :] / n)
    bf = lambda a: jnp.asarray(a, dtype=F32).astype(BF16)
    return dict(w1c=bf(w1c), w1r=bf(w1r), w2a=bf(w2a), w2b=bf(w2b), w3a=bf(w3a), w3b=bf(w3b),
                w4=bf(w4), tc=jnp.asarray(tw.real, F32), ts=jnp.asarray(tw.imag, F32))


def _dft_forward(x_ref, w1_ref, tc_ref, ts_ref, w2a_ref, w2b_ref, ar_ref, ai_ref):
    cb = x_ref.shape[0]
    n1 = FFT_N1
    w1 = w1_ref[...]
    tc, ts = tc_ref[...], ts_ref[...]

    def body(c, carry):
        a = jnp.dot(w1, x_ref[c], preferred_element_type=F32)
        ar, ai = a[:n1], a[n1:]
        rows = pl.ds(pl.multiple_of(c * n1, n1), n1)
        ar_ref[rows, :] = (ar * tc - ai * ts).astype(BF16)
        ai_ref[rows, :] = (ar * ts + ai * tc).astype(BF16)
        return carry

    lax.fori_loop(0, cb, body, 0, unroll=FFT_UNROLL)
    return (jnp.dot(ar_ref[...], w2a_ref[...], preferred_element_type=F32)
            + jnp.dot(ai_ref[...], w2b_ref[...], preferred_element_type=F32))


def _spectrum_kernel(x_ref, w1_ref, tc_ref, ts_ref, w2a_ref, w2b_ref, o_ref, ar_ref, ai_ref):
    b = _dft_forward(x_ref, w1_ref, tc_ref, ts_ref, w2a_ref, w2b_ref, ar_ref, ai_ref)
    o_ref[...] = b.reshape(o_ref.shape)


def _fftconv_kernel(x_ref, kf_ref, w1_ref, tc_ref, ts_ref, w2a_ref, w2b_ref, w3a_ref, w3b_ref,
                    w4_ref, y_ref, ar_ref, ai_ref, c_ref):
    cb = x_ref.shape[0]
    n1, n2 = FFT_N1, FFT_N2
    b = _dft_forward(x_ref, w1_ref, tc_ref, ts_ref, w2a_ref, w2b_ref, ar_ref, ai_ref)
    kf = kf_ref[...].reshape(cb * n1, 2 * n2)
    br, bi = b[:, :n2], b[:, n2:]
    kr, ki = kf[:, :n2], kf[:, n2:]
    ar_ref[...] = (br * kr - bi * ki).astype(BF16)
    ai_ref[...] = (br * ki + bi * kr).astype(BF16)
    c_ref[...] = (jnp.dot(ar_ref[...], w3a_ref[...], preferred_element_type=F32)
                  + jnp.dot(ai_ref[...], w3b_ref[...], preferred_element_type=F32))
    w4 = w4_ref[...]
    tc, ts = tc_ref[...], ts_ref[...]

    def body(c, carry):
        rows = pl.ds(pl.multiple_of(c * n1, n1), n1)
        cr, ci = c_ref[rows, :n2], c_ref[rows, n2:]
        dr = (cr * tc + ci * ts).astype(BF16)
        di = (ci * tc - cr * ts).astype(BF16)
        y_ref[c] = (jnp.dot(w4[:, :n1], dr, preferred_element_type=F32)
                    + jnp.dot(w4[:, n1:], di, preferred_element_type=F32))
        return carry

    lax.fori_loop(0, cb, body, 0, unroll=FFT_UNROLL)


def _const_spec(a):
    nd = a.ndim
    return pl.BlockSpec(a.shape, lambda *_: (0,) * nd)


def _filter_spectrum(kern_slabs, cst):
    C = kern_slabs.shape[0]
    cb = FFT_CH_BLOCK
    consts = [cst["w1r"], cst["tc"], cst["ts"], cst["w2a"], cst["w2b"]]
    return pl.pallas_call(
        _spectrum_kernel,
        grid=(C // cb,),
        in_specs=[pl.BlockSpec((cb, FFT_N1, FFT_N2), lambda i: (i, 0, 0))] + [_const_spec(a) for a in consts],
        out_specs=pl.BlockSpec((cb, FFT_N1, 2 * FFT_N2), lambda i: (i, 0, 0)),
        out_shape=jax.ShapeDtypeStruct((C, FFT_N1, 2 * FFT_N2), F32),
        scratch_shapes=[pltpu.VMEM((cb * FFT_N1, FFT_N2), BF16), pltpu.VMEM((cb * FFT_N1, FFT_N2), BF16)],
        compiler_params=_params("parallel"),
        name="filter_spectrum",
    )(kern_slabs, *consts)


def _fft_conv(x_slabs, kf, cst):
    P, C = x_slabs.shape[:2]
    cb = FFT_CH_BLOCK
    consts = [cst["w1c"], cst["tc"], cst["ts"], cst["w2a"], cst["w2b"], cst["w3a"], cst["w3b"], cst["w4"]]
    return pl.pallas_call(
        _fftconv_kernel,
        grid=(C // cb, P),
        in_specs=[pl.BlockSpec((None, cb, FFT_N1, FFT_N2), lambda i, p: (p, i, 0, 0)),
                  pl.BlockSpec((cb, FFT_N1, 2 * FFT_N2), lambda i, p: (i, 0, 0))]
                 + [_const_spec(a) for a in consts],
        out_specs=pl.BlockSpec((None, cb, FFT_N1, FFT_N2), lambda i, p: (p, i, 0, 0)),
        out_shape=jax.ShapeDtypeStruct((P, C, FFT_N1, FFT_N2), F32),
        scratch_shapes=[pltpu.VMEM((cb * FFT_N1, FFT_N2), BF16), pltpu.VMEM((cb * FFT_N1, FFT_N2), BF16),
                        pltpu.VMEM((cb * FFT_N1, 2 * FFT_N2), F32)],
        compiler_params=_params("parallel", "arbitrary"),
        name="fft_conv",
    )(x_slabs, kf, *consts)


def _split_bf16(x):
    hi = x.astype(BF16)
    lo = (x - hi.astype(F32)).astype(BF16)
    return hi, lo


def _outproj_kernel(attn_ref, yc_ref, u_ref, x0_ref, d_ref, gn_ref, grp_ref, wo_ref, x_ref,
                    gf_ref, rwh_ref, rwl_ref, rb_ref, h_ref, hn_ref, lg_ref):
    z = (yc_ref[...] + u_ref[...] * d_ref[...]) * x0_ref[...]
    zh, zl = _split_bf16(z * z)
    grp = grp_ref[...]
    ssq = (jnp.dot(zh, grp, preferred_element_type=F32) + jnp.dot(zl, grp, preferred_element_type=F32))
    gsz = HYENA_WIDTH // HYENA_GROUPS
    hy_out = (z * lax.rsqrt(ssq * (1.0 / gsz) + EPS) * gn_ref[...]).astype(BF16)
    mix = (jnp.dot(attn_ref[...], wo_ref[:ATTN_WIDTH, :], preferred_element_type=F32)
           + jnp.dot(hy_out, wo_ref[ATTN_WIDTH:, :], preferred_element_type=F32))
    h = x_ref[...] + mix
    h_ref[...] = h
    hn = _rms(h, gf_ref[...], EPS)
    hn_ref[...] = hn.astype(BF16)
    nh, nl = _split_bf16(hn)
    lg_ref[...] = (jnp.dot(nh, rwh_ref[...], preferred_element_type=F32)
                   + jnp.dot(nl, rwh_ref[...], preferred_element_type=F32)
                   + jnp.dot(nh, rwl_ref[...], preferred_element_type=F32)) + rb_ref[...]


def _out_proj(attn2, yc2, u2, x02, hyena_d, hyena_gn, w_out_bf, x2, g_ffn, router_w, router_b):
    T, D = x2.shape
    tm = TOKEN_TILE
    C = HYENA_WIDTH
    gid = jnp.arange(C) // (C // HYENA_GROUPS)
    grp = (gid[:, None] == gid[None, :]).astype(BF16)
    rw = jnp.zeros((D, ROUTER_PAD), F32).at[:, :N_EXPERTS].set(router_w)
    rwh, rwl = _split_bf16(rw)
    rb = jnp.zeros((1, ROUTER_PAD), F32).at[0, :N_EXPERTS].set(router_b)
    row = lambda i: (i, 0)
    const = lambda i: (0, 0)
    return pl.pallas_call(
        _outproj_kernel,
        grid=(T // tm,),
        in_specs=[
            pl.BlockSpec((tm, ATTN_WIDTH), row),
            pl.BlockSpec((tm, C), row),
            pl.BlockSpec((tm, C), row),
            pl.BlockSpec((tm, C), row),
            pl.BlockSpec((1, C), const),
            pl.BlockSpec((1, C), const),
            pl.BlockSpec((C, C), const),
            pl.BlockSpec((D, D), const),
            pl.BlockSpec((tm, D), row),
            pl.BlockSpec((1, D), const),
            pl.BlockSpec((D, ROUTER_PAD), const),
            pl.BlockSpec((D, ROUTER_PAD), const),
            pl.BlockSpec((1, ROUTER_PAD), const),
        ],
        out_specs=[
            pl.BlockSpec((tm, D), row),
            pl.BlockSpec((tm, D), row),
            pl.BlockSpec((tm, ROUTER_PAD), row),
        ],
        out_shape=[
            jax.ShapeDtypeStruct((T, D), F32),
            jax.ShapeDtypeStruct((T, D), BF16),
            jax.ShapeDtypeStruct((T, ROUTER_PAD), F32),
        ],
        compiler_params=_params("parallel"),
        name="out_proj",
    )(attn2, yc2, u2, x02, hyena_d.reshape(1, C), hyena_gn.reshape(1, C), grp, w_out_bf, x2,
      g_ffn.reshape(1, D), rwh, rwl, rb)


def _expert_kernel(be_ref, nused_ref, x_ref, wg_ref, bg_ref, wu_ref, bu_ref, wd_ref, bd_ref, y_ref):
    i = pl.program_id(0)

    @pl.when(i < nused_ref[0])
    def _():
        x = x_ref[...]
        g = jnp.minimum(jnp.dot(x, wg_ref[0], preferred_element_type=F32) + bg_ref[0], SWIGLU_LIMIT)
        u = jnp.clip(jnp.dot(x, wu_ref[0], preferred_element_type=F32) + bu_ref[0],
                     -SWIGLU_LIMIT, SWIGLU_LIMIT)
        a = (u + 1.0) * (g * jax.nn.sigmoid(SWIGLU_ALPHA * g))
        y_ref[...] = jnp.dot(a.astype(BF16), wd_ref[0], preferred_element_type=F32) + bd_ref[0]

    @pl.when(i >= nused_ref[0])
    def _():
        y_ref[...] = jnp.zeros_like(y_ref)


def _experts(block_e, n_used, x_sorted, wg, bg, wu, bu, wd, bd):
    P, D = x_sorted.shape
    tm = EXPERT_TILE
    E, _, FF = wg.shape
    wmap = lambda i, be, nu: (be[i], 0, 0)
    grid_spec = pltpu.PrefetchScalarGridSpec(
        num_scalar_prefetch=2,
        grid=(P // tm,),
        in_specs=[
            pl.BlockSpec((tm, D), lambda i, be, nu: (i, 0)),
            pl.BlockSpec((1, D, FF), wmap),
            pl.BlockSpec((1, 1, FF), wmap),
            pl.BlockSpec((1, D, FF), wmap),
            pl.BlockSpec((1, 1, FF), wmap),
            pl.BlockSpec((1, FF, D), wmap),
            pl.BlockSpec((1, 1, D), wmap),
        ],
        out_specs=pl.BlockSpec((tm, D), lambda i, be, nu: (i, 0)),
    )
    return pl.pallas_call(
        _expert_kernel,
        grid_spec=grid_spec,
        out_shape=jax.ShapeDtypeStruct((P, D), F32),
        compiler_params=_params("arbitrary"),
        name="moe_experts",
    )(block_e, n_used, x_sorted, wg, bg.reshape(E, 1, FF), wu, bu.reshape(E, 1, FF),
      wd, bd.reshape(E, 1, D))


def _final_kernel(h_ref, moe_ref, p_ref, wp_ref, gp_ref, wg_ref, bg_ref, gfin_ref, o_ref):
    h = h_ref[...] + moe_ref[...]
    e = _rms(jnp.dot(p_ref[...].astype(BF16), wp_ref[...], preferred_element_type=F32),
             gp_ref[...], EPS)
    gate = jax.nn.sigmoid(jnp.dot(h.astype(BF16), wg_ref[...], preferred_element_type=F32)
                          + bg_ref[...])
    h = h + gate * e
    o_ref[...] = _rms(h, gfin_ref[...], EPS)


def _final(h1, moe, p2, w_ple_bf, g_ple, w_gate_bf, b_gate, g_final):
    T, D = h1.shape
    tm = TOKEN_TILE
    PD = p2.shape[1]
    row = lambda i: (i, 0)
    const = lambda i: (0, 0)
    return pl.pallas_call(
        _final_kernel,
        grid=(T // tm,),
        in_specs=[
            pl.BlockSpec((tm, D), row),
            pl.BlockSpec((tm, D), row),
            pl.BlockSpec((tm, PD), row),
            pl.BlockSpec((PD, D), const),
            pl.BlockSpec((1, D), const),
            pl.BlockSpec((D, D), const),
            pl.BlockSpec((1, D), const),
            pl.BlockSpec((1, D), const),
        ],
        out_specs=pl.BlockSpec((tm, D), row),
        out_shape=jax.ShapeDtypeStruct((T, D), F32),
        compiler_params=_params("parallel"),
        name="final",
    )(h1, moe, p2, w_ple_bf, g_ple.reshape(1, D), w_gate_bf, b_gate.reshape(1, D),
      g_final.reshape(1, D))


def _rope_tables(seq):
    d = DIFF_HEAD_DIM
    pos = jnp.arange(seq, dtype=F32)
    inv = ROPE_THETA ** (-jnp.arange(0, d, 2, dtype=F32) / d)
    ang = pos[:, None] * inv[None, :]
    cos, sin = jnp.cos(ang), jnp.sin(ang)
    z = jnp.zeros_like(sin)
    cos_t = jnp.tile(jnp.concatenate([cos, cos], -1), (1, LANES // d))
    s1_t = jnp.tile(jnp.concatenate([-sin, z], -1), (1, LANES // d))
    s2_t = jnp.tile(jnp.concatenate([z, sin], -1), (1, LANES // d))
    return cos_t, s1_t, s2_t


def _hyena_filters(seq, w1, b1, w_inner, b_inner, freq, w_out):
    pos = jnp.arange(seq, dtype=F32)
    t = pos / (seq - 1)
    bands = (FILTER_EMB - 1) // 2
    f = jnp.linspace(1e-4, bands - 1, bands, dtype=F32)
    fw = ((2.0 * math.pi / seq) * pos)[:, None] * f[None, :]
    z = jnp.concatenate([t[:, None], jnp.cos(fw), -jnp.sin(fw)], axis=-1)
    hp = lax.Precision.HIGHEST
    h = jnp.sin(freq[0] * (jnp.dot(z, w1, precision=hp) + b1))
    for j in range(FILTER_INNER):
        h = jnp.sin(freq[j + 1] * (jnp.dot(h, w_inner[j], precision=hp) + b_inner[j]))
    h = jnp.dot(h, w_out, precision=hp)
    max_decay = math.log(DECAY_TARGET) / FAST_DECAY_PCT
    min_decay = math.log(DECAY_TARGET) / SLOW_DECAY_PCT
    deltas = jnp.abs(jnp.linspace(min_decay, max_decay, HYENA_WIDTH, dtype=F32))
    decay = jnp.exp(-t[:, None] * deltas[None, :])
    return h[:, :HYENA_WIDTH] * decay, h[:, HYENA_WIDTH:] * decay


def _long_conv(u, h_fwd, h_bwd):
    B, L, C = u.shape
    assert 2 * L == FFT_N and B % 2 == 0
    P, R = B // 2, FFT_N1 // 2
    cst = _dft_constants()
    lag0 = h_fwd[:1] + h_bwd[:1]
    kern = jnp.concatenate([lag0, h_fwd[1:], jnp.zeros((1, C), F32), h_bwd[:0:-1]], axis=0)
    kern_slabs = kern.T.reshape(C, FFT_N1, FFT_N2).astype(BF16)
    kf = _filter_spectrum(kern_slabs, cst)
    x_slabs = (u.reshape(2, P, R, FFT_N2, C).transpose(1, 4, 0, 2, 3)
               .reshape(P, C, FFT_N1, FFT_N2).astype(BF16))
    y = _fft_conv(x_slabs, kf, cst)
    return y.reshape(P, C, 2, R, FFT_N2).transpose(2, 0, 3, 4, 1).reshape(B, L, C)


def _route(logits, T):
    tm = EXPERT_TILE
    top_logit, top_e = lax.top_k(logits, TOP_K)
    top_w = jax.nn.softmax(top_logit, axis=-1)
    A = T * TOP_K
    flat_e = top_e.reshape(A)
    order = jnp.argsort(flat_e)
    sorted_e = flat_e[order]
    sorted_tok = (order // TOP_K).astype(jnp.int32)
    counts = jnp.bincount(flat_e, length=N_EXPERTS)
    padded = (counts + tm - 1) // tm * tm
    pad_end = jnp.cumsum(padded)
    pad_start = pad_end - padded
    start = jnp.cumsum(counts) - counts
    dest = (pad_start[sorted_e] + jnp.arange(A) - start[sorted_e]).astype(jnp.int32)
    n_blocks = -(-A // tm) + N_EXPERTS
    P = n_blocks * tm
    slot_tok = jnp.full((P,), T, jnp.int32).at[dest].set(sorted_tok)
    block_e = jnp.minimum(
        jnp.searchsorted(pad_end, jnp.arange(n_blocks) * tm, side='right'), N_EXPERTS - 1
    ).astype(jnp.int32)
    n_used = (pad_end[-1] // tm).astype(jnp.int32).reshape(1)
    pos = jnp.zeros((A,), jnp.int32).at[order].set(dest)
    return top_w, slot_tok, block_e, n_used, pos


def kernel(x, p, g_mix, w_in, hyena_conv_w, hyena_conv_b, flt_w1, flt_b1, flt_w_inner, flt_b_inner, flt_freq, flt_w_out, hyena_d, hyena_gn, lambda_q1, lambda_k1, lambda_q2, lambda_k2, attn_subln, w_out, g_ffn, router_w, router_b, w_gate, b_gate, w_up, b_up, w_down, b_down, w_ple, g_ple, w_ple_gate, b_ple_gate, g_final):
    B, S, D = x.shape
    T = B * S
    i = 0
    x2 = x.reshape(T, D)

    cos_t, s1_t, s2_t = _rope_tables(S)
    q, k, vt, hy = _in_proj(x2, g_mix[i], w_in[i].astype(BF16), cos_t, s1_t, s2_t, S)

    lam = (jnp.exp(jnp.sum(lambda_q1[i] * lambda_k1[i])) - jnp.exp(jnp.sum(lambda_q2[i] * lambda_k2[i]))
           + LAMBDA_INIT).reshape(1).astype(F32)
    attn = _diff_attention(lam, q.reshape(B, S, -1), k.reshape(B, S, -1), vt, attn_subln[i])

    u, hx0 = _short_conv(hy.reshape(B, S, -1), hyena_conv_w[i], hyena_conv_b[i])
    h_fwd, h_bwd = _hyena_filters(S, flt_w1[i], flt_b1[i], flt_w_inner[i], flt_b_inner[i],
                                  flt_freq[i], flt_w_out[i])
    yc = _long_conv(u, h_fwd, h_bwd)

    h1, hn, logits = _out_proj(attn.reshape(T, -1), yc.reshape(T, -1), u.reshape(T, -1),
                               hx0.reshape(T, -1), hyena_d[i], hyena_gn[i], w_out[i].astype(BF16),
                               x2, g_ffn[i], router_w[i], router_b[i])

    top_w, slot_tok, block_e, n_used, pos = _route(logits[:, :N_EXPERTS], T)
    hn_pad = jnp.concatenate([hn, jnp.zeros((1, D), hn.dtype)], axis=0)
    x_sorted = hn_pad[slot_tok]
    y = _experts(block_e, n_used, x_sorted, w_gate[i].astype(BF16), b_gate[i], w_up[i].astype(BF16),
                 b_up[i], w_down[i].astype(BF16), b_down[i])
    moe = jnp.sum(y[pos].reshape(T, TOP_K, D) * top_w[..., None], axis=1)

    out = _final(h1, moe, p[i].reshape(T, -1), w_ple[i].astype(BF16), g_ple[i],
                 w_ple_gate[i].astype(BF16), b_ple_gate[i], g_final)
    return out.reshape(B, S, D)
```

```python
import functools
import math

import jax
import jax.numpy as jnp
from jax import lax
from jax.experimental import pallas as pl
from jax.experimental.pallas import tpu as pltpu

F32 = jnp.float32
BF16 = jnp.bfloat16

D_MODEL = 1024
ATTN_WIDTH = 512
HYENA_WIDTH = 512
DIFF_HEADS = 4
DIFF_HEAD_DIM = 64
DIFF_V_DIM = 128
HYENA_GROUPS = 8
FILTER_EMB = 33
FILTER_INNER = 2
FAST_DECAY_PCT = 0.3
SLOW_DECAY_PCT = 1.5
DECAY_TARGET = 1e-2
ROPE_THETA = 10000.0
N_EXPERTS = 32
TOP_K = 4
SWIGLU_ALPHA = 1.702
SWIGLU_LIMIT = 7.0
EPS = 1e-6
SUBLN_EPS = 1e-5
LAMBDA_INIT = 0.8 - 0.6 * math.exp(-0.3 * 0)

V7X_VMEM_LIMIT_BYTES = 56 * 1024 * 1024
LANES = 128
SUBLANES = 8

TOKEN_TILE = 512
ATTN_Q_TILE = 256
ATTN_KV_TILE = 512
EXPERT_TILE = 512
ROUTER_PAD = LANES


def _params(*sem):
    return pltpu.CompilerParams(dimension_semantics=sem, vmem_limit_bytes=V7X_VMEM_LIMIT_BYTES)


def _rms(x, g, eps):
    return x * lax.rsqrt(jnp.mean(x * x, axis=-1, keepdims=True) + eps) * g


def _inproj_kernel(x_ref, g_ref, w_ref, wvt_ref, c_ref, s1_ref, s2_ref, q_ref, k_ref, vt_ref, hy_ref):
    a = _rms(x_ref[...], g_ref[...], EPS).astype(BF16)
    cos, s1, s2 = c_ref[...], s1_ref[...], s2_ref[...]

    def rope(t):
        return t * cos + pltpu.roll(t, LANES - 32, axis=1) * s1 + pltpu.roll(t, 32, axis=1) * s2

    q_scale = (DIFF_HEAD_DIM ** -0.5) * math.log2(math.e)
    qk = jnp.dot(a, w_ref[:, : 2 * ATTN_WIDTH], preferred_element_type=F32)
    for j in range(ATTN_WIDTH // LANES):
        sl = slice(j * LANES, (j + 1) * LANES)
        q_ref[:, sl] = (rope(qk[:, sl]) * q_scale).astype(BF16)
        k_ref[:, sl] = rope(qk[:, ATTN_WIDTH + j * LANES: ATTN_WIDTH + (j + 1) * LANES]).astype(BF16)
    vt_ref[...] = lax.dot_general(wvt_ref[...], a, (((1,), (1,)), ((), ())),
                                  preferred_element_type=F32).astype(BF16)
    hy_ref[...] = jnp.dot(a, w_ref[:, 3 * ATTN_WIDTH:], preferred_element_type=F32)


def _in_proj(x2, g_mix, w_in_bf, cos_t, s1_t, s2_t, seq):
    T, D = x2.shape
    tm = TOKEN_TILE
    nseq = seq // tm
    wvt = w_in_bf[:, 2 * ATTN_WIDTH: 3 * ATTN_WIDTH].T
    row = lambda i: (i, 0)
    const = lambda i: (0, 0)
    pos = lambda i: (i % nseq, 0)
    return pl.pallas_call(
        _inproj_kernel,
        grid=(T // tm,),
        in_specs=[
            pl.BlockSpec((tm, D), row),
            pl.BlockSpec((1, D), const),
            pl.BlockSpec(w_in_bf.shape, const),
            pl.BlockSpec(wvt.shape, const),
            pl.BlockSpec((tm, LANES), pos),
            pl.BlockSpec((tm, LANES), pos),
            pl.BlockSpec((tm, LANES), pos),
        ],
        out_specs=[
            pl.BlockSpec((tm, ATTN_WIDTH), row),
            pl.BlockSpec((tm, ATTN_WIDTH), row),
            pl.BlockSpec((ATTN_WIDTH, tm), lambda i: (0, i)),
            pl.BlockSpec((tm, 3 * HYENA_WIDTH), row),
        ],
        out_shape=[
            jax.ShapeDtypeStruct((T, ATTN_WIDTH), BF16),
            jax.ShapeDtypeStruct((T, ATTN_WIDTH), BF16),
            jax.ShapeDtypeStruct((ATTN_WIDTH, T), BF16),
            jax.ShapeDtypeStruct((T, 3 * HYENA_WIDTH), F32),
        ],
        compiler_params=_params("parallel"),
        name="in_proj",
    )(x2, g_mix.reshape(1, D), w_in_bf, wvt, cos_t, s1_t, s2_t)


def _attn_kernel(lam_ref, q_ref, k_ref, vt_ref, g_ref, o_ref, s_ref, *, kc):
    q = q_ref[0]
    tq = q.shape[0]
    nc = k_ref.shape[1] // kc
    lane = lax.broadcasted_iota(jnp.int32, q.shape, 1)
    zero = jnp.zeros_like(q)
    qs = (jnp.where(lane < DIFF_HEAD_DIM, q, zero), jnp.where(lane >= DIFF_HEAD_DIM, q, zero))

    def fold8(t, op):
        r = t[0:SUBLANES]
        for j in range(1, kc // SUBLANES):
            r = op(r, t[j * SUBLANES:(j + 1) * SUBLANES])
        return r

    outs = []
    for c in range(2):
        m8 = jnp.full((SUBLANES, tq), -jnp.inf, F32)
        for j in range(nc):
            s = lax.dot_general(k_ref[0, j * kc:(j + 1) * kc, :], qs[c], (((1,), (1,)), ((), ())),
                                preferred_element_type=F32)
            s_ref[c, j * kc:(j + 1) * kc, :] = s
            m8 = jnp.maximum(m8, fold8(s, jnp.maximum))
        m = jnp.max(m8, axis=0, keepdims=True)
        l8 = jnp.zeros((SUBLANES, tq), F32)
        acc = jnp.zeros((DIFF_V_DIM, tq), F32)
        for j in range(nc):
            p = jnp.exp2(s_ref[c, j * kc:(j + 1) * kc, :] - m)
            l8 = l8 + fold8(p, jnp.add)
            acc = acc + jnp.dot(vt_ref[:, j * kc:(j + 1) * kc], p.astype(BF16),
                                preferred_element_type=F32)
        outs.append(acc * (1.0 / jnp.sum(l8, axis=0, keepdims=True)))
    o = outs[0] - lam_ref[0] * outs[1]
    o = o * lax.rsqrt(jnp.mean(o * o, axis=0, keepdims=True) + SUBLN_EPS)
    o = o * (g_ref[...] * (1.0 - LAMBDA_INIT))
    o_ref[0] = o.T.astype(o_ref.dtype)


def _diff_attention(lam, q, k, vt, g_subln):
    B, S, _ = q.shape
    tq, kc = ATTN_Q_TILE, ATTN_KV_TILE
    return pl.pallas_call(
        functools.partial(_attn_kernel, kc=kc),
        grid=(B, DIFF_HEADS, S // tq),
        in_specs=[
            pl.BlockSpec(memory_space=pltpu.SMEM),
            pl.BlockSpec((1, tq, LANES), lambda b, h, i: (b, i, h)),
            pl.BlockSpec((1, S, LANES), lambda b, h, i: (b, 0, h)),
            pl.BlockSpec((DIFF_V_DIM, S), lambda b, h, i: (h, b)),
            pl.BlockSpec((DIFF_V_DIM, 1), lambda b, h, i: (0, 0)),
        ],
        out_specs=pl.BlockSpec((1, tq, LANES), lambda b, h, i: (b, i, h)),
        out_shape=jax.ShapeDtypeStruct((B, S, ATTN_WIDTH), BF16),
        scratch_shapes=[pltpu.VMEM((2, S, tq), F32)],
        compiler_params=_params("parallel", "parallel", "parallel"),
        name="diff_attn",
    )(lam, q, k, vt, g_subln.reshape(DIFF_V_DIM, 1))


def _shortconv_kernel(hy_ref, prev_ref, next_ref, w_ref, b_ref, u_ref, x0_ref):
    i = pl.program_id(1)
    last = pl.num_programs(1) - 1
    x = hy_ref[0]
    ts = x.shape[0]
    prev_row = jnp.where(i == 0, 0.0, prev_ref[0, SUBLANES - 1:SUBLANES, :])
    next_row = jnp.where(i == last, 0.0, next_ref[0, 0:1, :])
    row = lax.broadcasted_iota(jnp.int32, (ts, 1), 0)
    xm = jnp.where(row == 0, prev_row, pltpu.roll(x, 1, axis=0))
    xp = jnp.where(row == ts - 1, next_row, pltpu.roll(x, ts - 1, axis=0))
    y = b_ref[...] + xm * w_ref[0:1, :] + x * w_ref[1:2, :] + xp * w_ref[2:3, :]
    C = HYENA_WIDTH
    u_ref[0] = y[:, :C] * y[:, 2 * C:]
    x0_ref[0] = y[:, C:2 * C]


def _short_conv(hy, conv_w, conv_b):
    B, S, C3 = hy.shape
    ts = TOKEN_TILE
    nb = ts // SUBLANES
    return pl.pallas_call(
        _shortconv_kernel,
        grid=(B, S // ts),
        in_specs=[
            pl.BlockSpec((1, ts, C3), lambda b, i: (b, i, 0)),
            pl.BlockSpec((1, SUBLANES, C3), lambda b, i: (b, jnp.maximum(i * nb - 1, 0), 0)),
            pl.BlockSpec((1, SUBLANES, C3), lambda b, i: (b, jnp.minimum((i + 1) * nb, S // SUBLANES - 1), 0)),
            pl.BlockSpec((3, C3), lambda b, i: (0, 0)),
            pl.BlockSpec((1, C3), lambda b, i: (0, 0)),
        ],
        out_specs=[
            pl.BlockSpec((1, ts, HYENA_WIDTH), lambda b, i: (b, i, 0)),
            pl.BlockSpec((1, ts, HYENA_WIDTH), lambda b, i: (b, i, 0)),
        ],
        out_shape=[
            jax.ShapeDtypeStruct((B, S, HYENA_WIDTH), F32),
            jax.ShapeDtypeStruct((B, S, HYENA_WIDTH), F32),
        ],
        compiler_params=_params("parallel", "parallel"),
        name="short_conv",
    )(hy, hy, hy, conv_w, conv_b.reshape(1, C3))


FFT_N = 8192
FFT_N1 = 64
FFT_N2 = 128
FFT_CH_BLOCK = 32
FFT_UNROLL = 4


def _dft_constants():
    import numpy as np
    n1, n2, n = FFT_N1, FFT_N2, FFT_N
    k1 = np.arange(n1)[:, None]
    t1 = np.arange(n1)[None, :]
    f1 = np.exp(-2j * np.pi * k1 * t1 / n1)
    f1h = f1[:, : n1 // 2]
    w1c = np.block([[f1h.real, -f1h.imag], [f1h.imag, f1h.real]])
    w1r = np.concatenate([f1.real, f1.imag], axis=0)
    t2 = np.arange(n2)[:, None]
    k2 = np.arange(n2)[None, :]
    f2 = np.exp(-2j * np.pi * t2 * k2 / n2)
    w2a = np.concatenate([f2.real, f2.imag], axis=1)
    w2b = np.concatenate([-f2.imag, f2.real], axis=1)
    g2 = np.conj(f2)
    w3a = np.concatenate([g2.real, g2.imag], axis=1)
    w3b = np.concatenate([-g2.imag, g2.real], axis=1)
    h = np.conj(f1).T[: n1 // 2] / n
    w4 = np.block([[h.real, -h.imag], [h.imag, h.real]])
    tw = np.exp(-2j * np.pi * np.arange(n1)[:, None] * np.arange(n2)[None, :] / n)
    bf = lambda a: jnp.asarray(a, dtype=F32).astype(BF16)
    return dict(w1c=bf(w1c), w1r=bf(w1r), w2a=bf(w2a), w2b=bf(w2b), w3a=bf(w3a), w3b=bf(w3b),
                w4=bf(w4), tc=jnp.asarray(tw.real, F32), ts=jnp.asarray(tw.imag, F32))


def _dft_forward(x_ref, w1_ref, tc_ref, ts_ref, w2a_ref, w2b_ref, ar_ref, ai_ref):
    cb = x_ref.shape[0]
    n1 = FFT_N1
    w1 = w1_ref[...]
    tc, ts = tc_ref[...], ts_ref[...]

    def body(c, carry):
        a = jnp.dot(w1, x_ref[c], preferred_element_type=F32)
        ar, ai = a[:n1], a[n1:]
        rows = pl.ds(pl.multiple_of(c * n1, n1), n1)
        ar_ref[rows, :] = (ar * tc - ai * ts).astype(BF16)
        ai_ref[rows, :] = (ar * ts + ai * tc).astype(BF16)
        return carry

    lax.fori_loop(0, cb, body, 0, unroll=FFT_UNROLL)
    return (jnp.dot(ar_ref[...], w2a_ref[...], preferred_element_type=F32)
            + jnp.dot(ai_ref[...], w2b_ref[...], preferred_element_type=F32))


def _spectrum_kernel(x_ref, w1_ref, tc_ref, ts_ref, w2a_ref, w2b_ref, o_ref, ar_ref, ai_ref):
    b = _dft_forward(x_ref, w1_ref, tc_ref, ts_ref, w2a_ref, w2b_ref, ar_ref, ai_ref)
    o_ref[...] = b.reshape(o_ref.shape)


def _fftconv_kernel(x_ref, kf_ref, w1_ref, tc_ref, ts_ref, w2a_ref, w2b_ref, w3a_ref, w3b_ref,
                    w4_ref, y_ref, ar_ref, ai_ref, c_ref):
    cb = x_ref.shape[0]
    n1, n2 = FFT_N1, FFT_N2
    b = _dft_forward(x_ref, w1_ref, tc_ref, ts_ref, w2a_ref, w2b_ref, ar_ref, ai_ref)
    kf = kf_ref[...].reshape(cb * n1, 2 * n2)
    br, bi = b[:, :n2], b[:, n2:]
    kr, ki = kf[:, :n2], kf[:, n2:]
    ar_ref[...] = (br * kr - bi * ki).astype(BF16)
    ai_ref[...] = (br * ki + bi * kr).astype(BF16)
    c_ref[...] = (jnp.dot(ar_ref[...], w3a_ref[...], preferred_element_type=F32)
                  + jnp.dot(ai_ref[...], w3b_ref[...], preferred_element_type=F32))
    w4 = w4_ref[...]
    tc, ts = tc_ref[...], ts_ref[...]

    def body(c, carry):
        rows = pl.ds(pl.multiple_of(c * n1, n1), n1)
        cr, ci = c_ref[rows, :n2], c_ref[rows, n2:]
        dr = (cr * tc + ci * ts).astype(BF16)
        di = (ci * tc - cr * ts).astype(BF16)
        y_ref[c] = (jnp.dot(w4[:, :n1], dr, preferred_element_type=F32)
                    + jnp.dot(w4[:, n1:], di, preferred_element_type=F32))
        return carry

    lax.fori_loop(0, cb, body, 0, unroll=FFT_UNROLL)


def _const_spec(a):
    nd = a.ndim
    return pl.BlockSpec(a.shape, lambda *_: (0,) * nd)


def _filter_spectrum(kern_slabs, cst):
    C = kern_slabs.shape[0]
    cb = FFT_CH_BLOCK
    consts = [cst["w1r"], cst["tc"], cst["ts"], cst["w2a"], cst["w2b"]]
    return pl.pallas_call(
        _spectrum_kernel,
        grid=(C // cb,),
        in_specs=[pl.BlockSpec((cb, FFT_N1, FFT_N2), lambda i: (i, 0, 0))] + [_const_spec(a) for a in consts],
        out_specs=pl.BlockSpec((cb, FFT_N1, 2 * FFT_N2), lambda i: (i, 0, 0)),
        out_shape=jax.ShapeDtypeStruct((C, FFT_N1, 2 * FFT_N2), F32),
        scratch_shapes=[pltpu.VMEM((cb * FFT_N1, FFT_N2), BF16), pltpu.VMEM((cb * FFT_N1, FFT_N2), BF16)],
        compiler_params=_params("parallel"),
        name="filter_spectrum",
    )(kern_slabs, *consts)


def _fft_conv(x_slabs, kf, cst):
    P, C = x_slabs.shape[:2]
    cb = FFT_CH_BLOCK
    consts = [cst["w1c"], cst["tc"], cst["ts"], cst["w2a"], cst["w2b"], cst["w3a"], cst["w3b"], cst["w4"]]
    return pl.pallas_call(
        _fftconv_kernel,
        grid=(C // cb, P),
        in_specs=[pl.BlockSpec((None, cb, FFT_N1, FFT_N2), lambda i, p: (p, i, 0, 0)),
                  pl.BlockSpec((cb, FFT_N1, 2 * FFT_N2), lambda i, p: (i, 0, 0))]
                 + [_const_spec(a) for a in consts],
        out_specs=pl.BlockSpec((None, cb, FFT_N1, FFT_N2), lambda i, p: (p, i, 0, 0)),
        out_shape=jax.ShapeDtypeStruct((P, C, FFT_N1, FFT_N2), F32),
        scratch_shapes=[pltpu.VMEM((cb * FFT_N1, FFT_N2), BF16), pltpu.VMEM((cb * FFT_N1, FFT_N2), BF16),
                        pltpu.VMEM((cb * FFT_N1, 2 * FFT_N2), F32)],
        compiler_params=_params("parallel", "arbitrary"),
        name="fft_conv",
    )(x_slabs, kf, *consts)


def _split_bf16(x):
    hi = x.astype(BF16)
    lo = (x - hi.astype(F32)).astype(BF16)
    return hi, lo


def _outproj_kernel(attn_ref, yc_ref, u_ref, x0_ref, d_ref, gn_ref, grp_ref, wo_ref, x_ref,
                    gf_ref, rwh_ref, rwl_ref, rb_ref, tri_ref, h_ref, hn_ref, rt_ref, cnt_out_ref,
                    cnt_ref):
    z = (yc_ref[...] + u_ref[...] * d_ref[...]) * x0_ref[...]
    zh, zl = _split_bf16(z * z)
    grp = grp_ref[...]
    ssq = (jnp.dot(zh, grp, preferred_element_type=F32) + jnp.dot(zl, grp, preferred_element_type=F32))
    gsz = HYENA_WIDTH // HYENA_GROUPS
    hy_out = (z * lax.rsqrt(ssq * (1.0 / gsz) + EPS) * gn_ref[...]).astype(BF16)
    mix = (jnp.dot(attn_ref[...], wo_ref[:ATTN_WIDTH, :], preferred_element_type=F32)
           + jnp.dot(hy_out, wo_ref[ATTN_WIDTH:, :], preferred_element_type=F32))
    h = x_ref[...] + mix
    h_ref[...] = h
    hn = _rms(h, gf_ref[...], EPS)
    hn_ref[...] = hn.astype(BF16)
    nh, nl = _split_bf16(hn)
    logits = (jnp.dot(nh, rwh_ref[...], preferred_element_type=F32)
              + jnp.dot(nl, rwh_ref[...], preferred_element_type=F32)
              + jnp.dot(nh, rwl_ref[...], preferred_element_type=F32)) + rb_ref[...]

    @pl.when(pl.program_id(0) == 0)
    def _():
        cnt_ref[...] = jnp.zeros_like(cnt_ref)

    lane = lax.broadcasted_iota(jnp.int32, logits.shape, 1)
    work = logits
    top_val, top_hot = [], []
    for _ in range(TOP_K):
        m = jnp.max(work, axis=-1, keepdims=True)
        idx = jnp.min(jnp.where(work == m, lane, ROUTER_PAD), axis=-1, keepdims=True)
        hot = lane == idx
        top_val.append(m)
        top_hot.append(hot)
        work = jnp.where(hot, -jnp.inf, work)
    ex = [jnp.exp(v - top_val[0]) for v in top_val]
    inv_den = 1.0 / (ex[0] + ex[1] + ex[2] + ex[3])
    sel = jnp.zeros(logits.shape, F32)
    for hot in top_hot:
        sel = sel + hot.astype(F32)
    before = jnp.dot(tri_ref[...], sel.astype(BF16), preferred_element_type=F32) + cnt_ref[0:1, :]
    lane_f = lane.astype(F32)
    packed = jnp.zeros(logits.shape, F32)
    for r, hot in enumerate(top_hot):
        e_r = jnp.sum(jnp.where(hot, lane_f, 0.0), axis=-1, keepdims=True)
        rank_r = jnp.sum(jnp.where(hot, before, 0.0), axis=-1, keepdims=True)
        packed = jnp.where(lane == r, e_r, packed)
        packed = jnp.where(lane == TOP_K + r, rank_r, packed)
        packed = jnp.where(lane == 2 * TOP_K + r, ex[r] * inv_den, packed)
    rt_ref[...] = packed
    cnt_ref[...] = cnt_ref[...] + jnp.sum(sel, axis=0, keepdims=True)
    cnt_out_ref[...] = cnt_ref[...]


def _out_proj(attn2, yc2, u2, x02, hyena_d, hyena_gn, w_out_bf, x2, g_ffn, router_w, router_b):
    T, D = x2.shape
    tm = TOKEN_TILE
    C = HYENA_WIDTH
    gid = jnp.arange(C) // (C // HYENA_GROUPS)
    grp = (gid[:, None] == gid[None, :]).astype(BF16)
    rw = jnp.zeros((D, ROUTER_PAD), F32).at[:, :N_EXPERTS].set(router_w)
    rwh, rwl = _split_bf16(rw)
    rb = jnp.full((1, ROUTER_PAD), -jnp.inf, F32).at[0, :N_EXPERTS].set(router_b)
    tri = (jnp.arange(tm)[:, None] > jnp.arange(tm)[None, :]).astype(BF16)
    row = lambda i: (i, 0)
    const = lambda i: (0, 0)
    return pl.pallas_call(
        _outproj_kernel,
        grid=(T // tm,),
        in_specs=[
            pl.BlockSpec((tm, ATTN_WIDTH), row),
            pl.BlockSpec((tm, C), row),
            pl.BlockSpec((tm, C), row),
            pl.BlockSpec((tm, C), row),
            pl.BlockSpec((1, C), const),
            pl.BlockSpec((1, C), const),
            pl.BlockSpec((C, C), const),
            pl.BlockSpec((D, D), const),
            pl.BlockSpec((tm, D), row),
            pl.BlockSpec((1, D), const),
            pl.BlockSpec((D, ROUTER_PAD), const),
            pl.BlockSpec((D, ROUTER_PAD), const),
            pl.BlockSpec((1, ROUTER_PAD), const),
            pl.BlockSpec((tm, tm), const),
        ],
        out_specs=[
            pl.BlockSpec((tm, D), row),
            pl.BlockSpec((tm, D), row),
            pl.BlockSpec((tm, ROUTER_PAD), row),
            pl.BlockSpec((SUBLANES, ROUTER_PAD), const),
        ],
        out_shape=[
            jax.ShapeDtypeStruct((T, D), F32),
            jax.ShapeDtypeStruct((T, D), BF16),
            jax.ShapeDtypeStruct((T, ROUTER_PAD), F32),
            jax.ShapeDtypeStruct((SUBLANES, ROUTER_PAD), F32),
        ],
        scratch_shapes=[pltpu.VMEM((SUBLANES, ROUTER_PAD), F32)],
        compiler_params=_params("arbitrary"),
        name="out_proj",
    )(attn2, yc2, u2, x02, hyena_d.reshape(1, C), hyena_gn.reshape(1, C), grp, w_out_bf, x2,
      g_ffn.reshape(1, D), rwh, rwl, rb, tri)


def _expert_kernel(be_ref, nused_ref, x_ref, wg_ref, bg_ref, wu_ref, bu_ref, wd_ref, bd_ref, y_ref,
                   wg_bf, wu_bf, wd_bf):
    i = pl.program_id(0)
    used = i < nused_ref[0]
    new_expert = jnp.logical_or(i == 0, be_ref[i] != be_ref[jnp.maximum(i - 1, 0)])

    @pl.when(jnp.logical_and(used, new_expert))
    def _():
        wg_bf[...] = wg_ref[0].astype(BF16)
        wu_bf[...] = wu_ref[0].astype(BF16)
        wd_bf[...] = wd_ref[0].astype(BF16)

    @pl.when(used)
    def _():
        x = x_ref[...]
        g = jnp.minimum(jnp.dot(x, wg_bf[...], preferred_element_type=F32) + bg_ref[0], SWIGLU_LIMIT)
        u = jnp.clip(jnp.dot(x, wu_bf[...], preferred_element_type=F32) + bu_ref[0],
                     -SWIGLU_LIMIT, SWIGLU_LIMIT)
        a = (u + 1.0) * (g * jax.nn.sigmoid(SWIGLU_ALPHA * g))
        y = jnp.dot(a.astype(BF16), wd_bf[...], preferred_element_type=F32) + bd_ref[0]
        y_ref[...] = y.astype(y_ref.dtype)

    @pl.when(jnp.logical_not(used))
    def _():
        y_ref[...] = jnp.zeros_like(y_ref)


def _experts(block_e, n_used, x_sorted, wg, bg, wu, bu, wd, bd):
    P, D = x_sorted.shape
    tm = EXPERT_TILE
    E, _, FF = wg.shape
    wmap = lambda i, be, nu: (be[i], 0, 0)
    grid_spec = pltpu.PrefetchScalarGridSpec(
        num_scalar_prefetch=2,
        grid=(P // tm,),
        in_specs=[
            pl.BlockSpec((tm, D), lambda i, be, nu: (i, 0)),
            pl.BlockSpec((1, D, FF), wmap),
            pl.BlockSpec((1, 1, FF), wmap),
            pl.BlockSpec((1, D, FF), wmap),
            pl.BlockSpec((1, 1, FF), wmap),
            pl.BlockSpec((1, FF, D), wmap),
            pl.BlockSpec((1, 1, D), wmap),
        ],
        out_specs=pl.BlockSpec((tm, D), lambda i, be, nu: (i, 0)),
        scratch_shapes=[pltpu.VMEM((D, FF), BF16), pltpu.VMEM((D, FF), BF16), pltpu.VMEM((FF, D), BF16)],
    )
    return pl.pallas_call(
        _expert_kernel,
        grid_spec=grid_spec,
        out_shape=jax.ShapeDtypeStruct((P, D), BF16),
        compiler_params=_params("arbitrary"),
        name="moe_experts",
    )(block_e, n_used, x_sorted, wg, bg.reshape(E, 1, FF), wu, bu.reshape(E, 1, FF),
      wd, bd.reshape(E, 1, D))


def _final_kernel(h_ref, yg_ref, rt_ref, p_ref, wp_ref, gp_ref, wg_ref, bg_ref, gfin_ref, o_ref):
    h = h_ref[...]
    for r in range(TOP_K):
        h = h + yg_ref[r].astype(F32) * rt_ref[:, 2 * TOP_K + r: 2 * TOP_K + r + 1]
    e = _rms(jnp.dot(p_ref[...].astype(BF16), wp_ref[...], preferred_element_type=F32),
             gp_ref[...], EPS)
    gate = jax.nn.sigmoid(jnp.dot(h.astype(BF16), wg_ref[...], preferred_element_type=F32)
                          + bg_ref[...])
    h = h + gate * e
    o_ref[...] = _rms(h, gfin_ref[...], EPS)


def _final(h1, yg, route, p2, w_ple_bf, g_ple, w_gate_bf, b_gate, g_final):
    T, D = h1.shape
    tm = TOKEN_TILE
    PD = p2.shape[1]
    row = lambda i: (i, 0)
    const = lambda i: (0, 0)
    return pl.pallas_call(
        _final_kernel,
        grid=(T // tm,),
        in_specs=[
            pl.BlockSpec((tm, D), row),
            pl.BlockSpec((TOP_K, tm, D), lambda i: (0, i, 0)),
            pl.BlockSpec((tm, ROUTER_PAD), row),
            pl.BlockSpec((tm, PD), row),
            pl.BlockSpec((PD, D), const),
            pl.BlockSpec((1, D), const),
            pl.BlockSpec((D, D), const),
            pl.BlockSpec((1, D), const),
            pl.BlockSpec((1, D), const),
        ],
        out_specs=pl.BlockSpec((tm, D), row),
        out_shape=jax.ShapeDtypeStruct((T, D), F32),
        compiler_params=_params("parallel"),
        name="final",
    )(h1, yg, route, p2, w_ple_bf, g_ple.reshape(1, D), w_gate_bf, b_gate.reshape(1, D),
      g_final.reshape(1, D))


def _rope_tables(seq):
    d = DIFF_HEAD_DIM
    pos = jnp.arange(seq, dtype=F32)
    inv = ROPE_THETA ** (-jnp.arange(0, d, 2, dtype=F32) / d)
    ang = pos[:, None] * inv[None, :]
    cos, sin = jnp.cos(ang), jnp.sin(ang)
    z = jnp.zeros_like(sin)
    cos_t = jnp.tile(jnp.concatenate([cos, cos], -1), (1, LANES // d))
    s1_t = jnp.tile(jnp.concatenate([-sin, z], -1), (1, LANES // d))
    s2_t = jnp.tile(jnp.concatenate([z, sin], -1), (1, LANES // d))
    return cos_t, s1_t, s2_t


def _hyena_filters(seq, w1, b1, w_inner, b_inner, freq, w_out):
    pos = jnp.arange(seq, dtype=F32)
    t = pos / (seq - 1)
    bands = (FILTER_EMB - 1) // 2
    f = jnp.linspace(1e-4, bands - 1, bands, dtype=F32)
    fw = ((2.0 * math.pi / seq) * pos)[:, None] * f[None, :]
    z = jnp.concatenate([t[:, None], jnp.cos(fw), -jnp.sin(fw)], axis=-1)
    hp = lax.Precision.HIGHEST
    h = jnp.sin(freq[0] * (jnp.dot(z, w1, precision=hp) + b1))
    for j in range(FILTER_INNER):
        h = jnp.sin(freq[j + 1] * (jnp.dot(h, w_inner[j], precision=hp) + b_inner[j]))
    h = jnp.dot(h, w_out, precision=hp)
    max_decay = math.log(DECAY_TARGET) / FAST_DECAY_PCT
    min_decay = math.log(DECAY_TARGET) / SLOW_DECAY_PCT
    deltas = jnp.abs(jnp.linspace(min_decay, max_decay, HYENA_WIDTH, dtype=F32))
    decay = jnp.exp(-t[:, None] * deltas[None, :])
    return h[:, :HYENA_WIDTH] * decay, h[:, HYENA_WIDTH:] * decay


def _long_conv(u, h_fwd, h_bwd):
    B, L, C = u.shape
    assert 2 * L == FFT_N and B % 2 == 0
    P, R = B // 2, FFT_N1 // 2
    cst = _dft_constants()
    lag0 = h_fwd[:1] + h_bwd[:1]
    kern = jnp.concatenate([lag0, h_fwd[1:], jnp.zeros((1, C), F32), h_bwd[:0:-1]], axis=0)
    kern_slabs = kern.T.reshape(C, FFT_N1, FFT_N2).astype(BF16)
    kf = _filter_spectrum(kern_slabs, cst)
    x_slabs = (u.reshape(2, P, R, FFT_N2, C).transpose(1, 4, 0, 2, 3)
               .reshape(P, C, FFT_N1, FFT_N2).astype(BF16))
    y = _fft_conv(x_slabs, kf, cst)
    return y.reshape(P, C, 2, R, FFT_N2).transpose(2, 0, 3, 4, 1).reshape(B, L, C)


def _dispatch_indices(route, cnt, T):
    tm = EXPERT_TILE
    A = T * TOP_K
    top_e = route[:, :TOP_K].astype(jnp.int32)
    rank = route[:, TOP_K:2 * TOP_K].astype(jnp.int32)
    counts = cnt[0, :N_EXPERTS].astype(jnp.int32)
    padded = (counts + tm - 1) // tm * tm
    pad_end = jnp.cumsum(padded)
    pad_start = pad_end - padded
    start = jnp.cumsum(counts) - counts
    pos = pad_start[top_e] + rank
    n_blocks = -(-A // tm) + N_EXPERTS
    P = n_blocks * tm
    block_e = jnp.minimum(
        jnp.searchsorted(pad_end, jnp.arange(n_blocks) * tm, side='right'), N_EXPERTS - 1
    ).astype(jnp.int32)
    n_used = (pad_end[-1] // tm).astype(jnp.int32).reshape(1)
    tok = jnp.broadcast_to(jnp.arange(T, dtype=jnp.int32)[:, None], (T, TOP_K))
    _, sorted_tok = lax.sort_key_val(pos.reshape(A), tok.reshape(A))
    slot = jnp.arange(P, dtype=jnp.int32)
    slot_e = jnp.repeat(block_e, tm)
    r = slot - pad_start[slot_e]
    compact = jnp.clip(start[slot_e] + r, 0, A - 1)
    slot_tok = jnp.where(r < counts[slot_e], sorted_tok[compact], 0)
    return pos.T, slot_tok, block_e, n_used


def kernel(x, p, g_mix, w_in, hyena_conv_w, hyena_conv_b, flt_w1, flt_b1, flt_w_inner, flt_b_inner, flt_freq, flt_w_out, hyena_d, hyena_gn, lambda_q1, lambda_k1, lambda_q2, lambda_k2, attn_subln, w_out, g_ffn, router_w, router_b, w_gate, b_gate, w_up, b_up, w_down, b_down, w_ple, g_ple, w_ple_gate, b_ple_gate, g_final):
    B, S, D = x.shape
    T = B * S
    i = 0
    x2 = x.reshape(T, D)

    cos_t, s1_t, s2_t = _rope_tables(S)
    q, k, vt, hy = _in_proj(x2, g_mix[i], w_in[i].astype(BF16), cos_t, s1_t, s2_t, S)

    lam = (jnp.exp(jnp.sum(lambda_q1[i] * lambda_k1[i])) - jnp.exp(jnp.sum(lambda_q2[i] * lambda_k2[i]))
           + LAMBDA_INIT).reshape(1).astype(F32)
    attn = _diff_attention(lam, q.reshape(B, S, -1), k.reshape(B, S, -1), vt, attn_subln[i])

    u, hx0 = _short_conv(hy.reshape(B, S, -1), hyena_conv_w[i], hyena_conv_b[i])
    h_fwd, h_bwd = _hyena_filters(S, flt_w1[i], flt_b1[i], flt_w_inner[i], flt_b_inner[i],
                                  flt_freq[i], flt_w_out[i])
    yc = _long_conv(u, h_fwd, h_bwd)

    h1, hn, route, cnt = _out_proj(attn.reshape(T, -1), yc.reshape(T, -1), u.reshape(T, -1),
                                   hx0.reshape(T, -1), hyena_d[i], hyena_gn[i],
                                   w_out[i].astype(BF16), x2, g_ffn[i], router_w[i], router_b[i])

    pos, slot_tok, block_e, n_used = _dispatch_indices(route, cnt, T)
    x_sorted = hn[slot_tok]
    y = _experts(block_e, n_used, x_sorted, w_gate[i], b_gate[i], w_up[i], b_up[i],
                 w_down[i], b_down[i])
    yg = y[pos.reshape(-1)].reshape(TOP_K, T, D)

    out = _final(h1, yg, route, p[i].reshape(T, -1), w_ple[i].astype(BF16), g_ple[i],
                 w_ple_gate[i].astype(BF16), b_ple_gate[i], g_final)
    return out.reshape(B, S, D)
```

```python
import functools
import math

import jax
import jax.numpy as jnp
from jax import lax
from jax.experimental import pallas as pl
from jax.experimental.pallas import tpu as pltpu

F32 = jnp.float32
BF16 = jnp.bfloat16

D_MODEL = 1024
ATTN_WIDTH = 512
HYENA_WIDTH = 512
DIFF_HEADS = 4
DIFF_HEAD_DIM = 64
DIFF_V_DIM = 128
HYENA_GROUPS = 8
FILTER_EMB = 33
FILTER_INNER = 2
FAST_DECAY_PCT = 0.3
SLOW_DECAY_PCT = 1.5
DECAY_TARGET = 1e-2
ROPE_THETA = 10000.0
N_EXPERTS = 32
TOP_K = 4
SWIGLU_ALPHA = 1.702
SWIGLU_LIMIT = 7.0
EPS = 1e-6
SUBLN_EPS = 1e-5
LAMBDA_INIT = 0.8 - 0.6 * math.exp(-0.3 * 0)

V7X_VMEM_LIMIT_BYTES = 56 * 1024 * 1024
LANES = 128
SUBLANES = 8

TOKEN_TILE = 512
ATTN_Q_TILE = 256
ATTN_KV_TILE = 512
EXPERT_TILE = 512
ROUTER_PAD = LANES


def _params(*sem):
    return pltpu.CompilerParams(dimension_semantics=sem, vmem_limit_bytes=V7X_VMEM_LIMIT_BYTES)


def _rms(x, g, eps):
    return x * lax.rsqrt(jnp.mean(x * x, axis=-1, keepdims=True) + eps) * g


def _inproj_kernel(x_ref, g_ref, w_ref, wvt_ref, c_ref, s1_ref, s2_ref, q_ref, k_ref, vt_ref, hy_ref):
    a = _rms(x_ref[...], g_ref[...], EPS).astype(BF16)
    cos, s1, s2 = c_ref[...], s1_ref[...], s2_ref[...]

    def rope(t):
        return t * cos + pltpu.roll(t, LANES - 32, axis=1) * s1 + pltpu.roll(t, 32, axis=1) * s2

    q_scale = (DIFF_HEAD_DIM ** -0.5) * math.log2(math.e)
    qk = jnp.dot(a, w_ref[:, : 2 * ATTN_WIDTH], preferred_element_type=F32)
    for j in range(ATTN_WIDTH // LANES):
        sl = slice(j * LANES, (j + 1) * LANES)
        q_ref[:, sl] = (rope(qk[:, sl]) * q_scale).astype(BF16)
        k_ref[:, sl] = rope(qk[:, ATTN_WIDTH + j * LANES: ATTN_WIDTH + (j + 1) * LANES]).astype(BF16)
    vt_ref[...] = lax.dot_general(wvt_ref[...], a, (((1,), (1,)), ((), ())),
                                  preferred_element_type=F32).astype(BF16)
    hy_ref[...] = jnp.dot(a, w_ref[:, 3 * ATTN_WIDTH:], preferred_element_type=F32)


def _in_proj(x2, g_mix, w_in_bf, cos_t, s1_t, s2_t, seq):
    T, D = x2.shape
    tm = TOKEN_TILE
    nseq = seq // tm
    wvt = w_in_bf[:, 2 * ATTN_WIDTH: 3 * ATTN_WIDTH].T
    row = lambda i: (i, 0)
    const = lambda i: (0, 0)
    pos = lambda i: (i % nseq, 0)
    return pl.pallas_call(
        _inproj_kernel,
        grid=(T // tm,),
        in_specs=[
            pl.BlockSpec((tm, D), row),
            pl.BlockSpec((1, D), const),
            pl.BlockSpec(w_in_bf.shape, const),
            pl.BlockSpec(wvt.shape, const),
            pl.BlockSpec((tm, LANES), pos),
            pl.BlockSpec((tm, LANES), pos),
            pl.BlockSpec((tm, LANES), pos),
        ],
        out_specs=[
            pl.BlockSpec((tm, ATTN_WIDTH), row),
            pl.BlockSpec((tm, ATTN_WIDTH), row),
            pl.BlockSpec((ATTN_WIDTH, tm), lambda i: (0, i)),
            pl.BlockSpec((tm, 3 * HYENA_WIDTH), row),
        ],
        out_shape=[
            jax.ShapeDtypeStruct((T, ATTN_WIDTH), BF16),
            jax.ShapeDtypeStruct((T, ATTN_WIDTH), BF16),
            jax.ShapeDtypeStruct((ATTN_WIDTH, T), BF16),
            jax.ShapeDtypeStruct((T, 3 * HYENA_WIDTH), F32),
        ],
        compiler_params=_params("parallel"),
        name="in_proj",
    )(x2, g_mix.reshape(1, D), w_in_bf, wvt, cos_t, s1_t, s2_t)


def _attn_kernel(lam_ref, q_ref, k_ref, vt_ref, g_ref, o_ref, s_ref, *, kc):
    q = q_ref[0]
    tq = q.shape[0]
    nc = k_ref.shape[1] // kc
    lane = lax.broadcasted_iota(jnp.int32, q.shape, 1)
    zero = jnp.zeros_like(q)
    qs = (jnp.where(lane < DIFF_HEAD_DIM, q, zero), jnp.where(lane >= DIFF_HEAD_DIM, q, zero))

    def fold8(t, op):
        r = t[0:SUBLANES]
        for j in range(1, kc // SUBLANES):
            r = op(r, t[j * SUBLANES:(j + 1) * SUBLANES])
        return r

    outs = []
    for c in range(2):
        m8 = jnp.full((SUBLANES, tq), -jnp.inf, F32)
        for j in range(nc):
            s = lax.dot_general(k_ref[0, j * kc:(j + 1) * kc, :], qs[c], (((1,), (1,)), ((), ())),
                                preferred_element_type=F32)
            s_ref[c, j * kc:(j + 1) * kc, :] = s
            m8 = jnp.maximum(m8, fold8(s, jnp.maximum))
        m = jnp.max(m8, axis=0, keepdims=True)
        l8 = jnp.zeros((SUBLANES, tq), F32)
        acc = jnp.zeros((DIFF_V_DIM, tq), F32)
        for j in range(nc):
            p = jnp.exp2(s_ref[c, j * kc:(j + 1) * kc, :] - m)
            l8 = l8 + fold8(p, jnp.add)
            acc = acc + jnp.dot(vt_ref[:, j * kc:(j + 1) * kc], p.astype(BF16),
                                preferred_element_type=F32)
        outs.append(acc * (1.0 / jnp.sum(l8, axis=0, keepdims=True)))
    o = outs[0] - lam_ref[0] * outs[1]
    o = o * lax.rsqrt(jnp.mean(o * o, axis=0, keepdims=True) + SUBLN_EPS)
    o = o * (g_ref[...] * (1.0 - LAMBDA_INIT))
    o_ref[0] = o.T.astype(o_ref.dtype)


def _diff_attention(lam, q, k, vt, g_subln):
    B, S, _ = q.shape
    tq, kc = ATTN_Q_TILE, ATTN_KV_TILE
    return pl.pallas_call(
        functools.partial(_attn_kernel, kc=kc),
        grid=(B, DIFF_HEADS, S // tq),
        in_specs=[
            pl.BlockSpec(memory_space=pltpu.SMEM),
            pl.BlockSpec((1, tq, LANES), lambda b, h, i: (b, i, h)),
            pl.BlockSpec((1, S, LANES), lambda b, h, i: (b, 0, h)),
            pl.BlockSpec((DIFF_V_DIM, S), lambda b, h, i: (h, b)),
            pl.BlockSpec((DIFF_V_DIM, 1), lambda b, h, i: (0, 0)),
        ],
        out_specs=pl.BlockSpec((1, tq, LANES), lambda b, h, i: (b, i, h)),
        out_shape=jax.ShapeDtypeStruct((B, S, ATTN_WIDTH), BF16),
        scratch_shapes=[pltpu.VMEM((2, S, tq), F32)],
        compiler_params=_params("parallel", "parallel", "parallel"),
        name="diff_attn",
    )(lam, q, k, vt, g_subln.reshape(DIFF_V_DIM, 1))


def _shortconv_kernel(hy_ref, prev_ref, next_ref, w_ref, b_ref, u_ref, x0_ref):
    i = pl.program_id(1)
    last = pl.num_programs(1) - 1
    x = hy_ref[0]
    ts = x.shape[0]
    prev_row = jnp.where(i == 0, 0.0, prev_ref[0, SUBLANES - 1:SUBLANES, :])
    next_row = jnp.where(i == last, 0.0, next_ref[0, 0:1, :])
    row = lax.broadcasted_iota(jnp.int32, (ts, 1), 0)
    xm = jnp.where(row == 0, prev_row, pltpu.roll(x, 1, axis=0))
    xp = jnp.where(row == ts - 1, next_row, pltpu.roll(x, ts - 1, axis=0))
    y = b_ref[...] + xm * w_ref[0:1, :] + x * w_ref[1:2, :] + xp * w_ref[2:3, :]
    C = HYENA_WIDTH
    u_ref[0] = y[:, :C] * y[:, 2 * C:]
    x0_ref[0] = y[:, C:2 * C]


def _short_conv(hy, conv_w, conv_b):
    B, S, C3 = hy.shape
    ts = TOKEN_TILE
    nb = ts // SUBLANES
    return pl.pallas_call(
        _shortconv_kernel,
        grid=(B, S // ts),
        in_specs=[
            pl.BlockSpec((1, ts, C3), lambda b, i: (b, i, 0)),
            pl.BlockSpec((1, SUBLANES, C3), lambda b, i: (b, jnp.maximum(i * nb - 1, 0), 0)),
            pl.BlockSpec((1, SUBLANES, C3), lambda b, i: (b, jnp.minimum((i + 1) * nb, S // SUBLANES - 1), 0)),
            pl.BlockSpec((3, C3), lambda b, i: (0, 0)),
            pl.BlockSpec((1, C3), lambda b, i: (0, 0)),
        ],
        out_specs=[
            pl.BlockSpec((1, ts, HYENA_WIDTH), lambda b, i: (b, i, 0)),
            pl.BlockSpec((1, ts, HYENA_WIDTH), lambda b, i: (b, i, 0)),
        ],
        out_shape=[
            jax.ShapeDtypeStruct((B, S, HYENA_WIDTH), F32),
            jax.ShapeDtypeStruct((B, S, HYENA_WIDTH), F32),
        ],
        compiler_params=_params("parallel", "parallel"),
        name="short_conv",
    )(hy, hy, hy, conv_w, conv_b.reshape(1, C3))


FFT_N = 8192
FFT_N1 = 64
FFT_N2 = 128
FFT_CH_BLOCK = 32
FFT_UNROLL = 4


def _dft_constants():
    import numpy as np
    n1, n2, n = FFT_N1, FFT_N2, FFT_N
    k1 = np.arange(n1)[:, None]
    t1 = np.arange(n1)[None, :]
    f1 = np.exp(-2j * np.pi * k1 * t1 / n1)
    f1h = f1[:, : n1 // 2]
    w1c = np.block([[f1h.real, -f1h.imag], [f1h.imag, f1h.real]])
    w1r = np.concatenate([f1.real, f1.imag], axis=0)
    t2 = np.arange(n2)[:, None]
    k2 = np.arange(n2)[None, :]
    f2 = np.exp(-2j * np.pi * t2 * k2 / n2)
    w2a = np.concatenate([f2.real, f2.imag], axis=1)
    w2b = np.concatenate([-f2.imag, f2.real], axis=1)
    g2 = np.conj(f2)
    w3a = np.concatenate([g2.real, g2.imag], axis=1)
    w3b = np.concatenate([-g2.imag, g2.real], axis=1)
    h = np.conj(f1).T[: n1 // 2] / n
    w4 = np.block([[h.real, -h.imag], [h.imag, h.real]])
    tw = np.exp(-2j * np.pi * np.arange(n1)[:, None] * np.arange(n2)[None, :] / n)
    bf = lambda a: jnp.asarray(a, dtype=F32).astype(BF16)
    return dict(w1c=bf(w1c), w1r=bf(w1r), w2a=bf(w2a), w2b=bf(w2b), w3a=bf(w3a), w3b=bf(w3b),
                w4=bf(w4), tc=jnp.asarray(tw.real, F32), ts=jnp.asarray(tw.imag, F32))


def _dft_forward(x_ref, w1_ref, tc_ref, ts_ref, w2a_ref, w2b_ref, ar_ref, ai_ref):
    cb = x_ref.shape[0]
    n1 = FFT_N1
    w1 = w1_ref[...]
    tc, ts = tc_ref[...], ts_ref[...]

    def body(c, carry):
        a = jnp.dot(w1, x_ref[c], preferred_element_type=F32)
        ar, ai = a[:n1], a[n1:]
        rows = pl.ds(pl.multiple_of(c * n1, n1), n1)
        ar_ref[rows, :] = (ar * tc - ai * ts).astype(BF16)
        ai_ref[rows, :] = (ar * ts + ai * tc).astype(BF16)
        return carry

    lax.fori_loop(0, cb, body, 0, unroll=FFT_UNROLL)
    return (jnp.dot(ar_ref[...], w2a_ref[...], preferred_element_type=F32)
            + jnp.dot(ai_ref[...], w2b_ref[...], preferred_element_type=F32))


def _spectrum_kernel(x_ref, w1_ref, tc_ref, ts_ref, w2a_ref, w2b_ref, o_ref, ar_ref, ai_ref):
    b = _dft_forward(x_ref, w1_ref, tc_ref, ts_ref, w2a_ref, w2b_ref, ar_ref, ai_ref)
    o_ref[...] = b.reshape(o_ref.shape)


def _fftconv_kernel(x_ref, kf_ref, w1_ref, tc_ref, ts_ref, w2a_ref, w2b_ref, w3a_ref, w3b_ref,
                    w4_ref, y_ref, ar_ref, ai_ref, c_ref):
    cb = x_ref.shape[0]
    n1, n2 = FFT_N1, FFT_N2
    b = _dft_forward(x_ref, w1_ref, tc_ref, ts_ref, w2a_ref, w2b_ref, ar_ref, ai_ref)
    kf = kf_ref[...].reshape(cb * n1, 2 * n2)
    br, bi = b[:, :n2], b[:, n2:]
    kr, ki = kf[:, :n2], kf[:, n2:]
    ar_ref[...] = (br * kr - bi * ki).astype(BF16)
    ai_ref[...] = (br * ki + bi * kr).astype(BF16)
    c_ref[...] = (jnp.dot(ar_ref[...], w3a_ref[...], preferred_element_type=F32)
                  + jnp.dot(ai_ref[...], w3b_ref[...], preferred_element_type=F32))
    w4 = w4_ref[...]
    tc, ts = tc_ref[...], ts_ref[...]

    def body(c, carry):
        rows = pl.ds(pl.multiple_of(c * n1, n1), n1)
        cr, ci = c_ref[rows, :n2], c_ref[rows, n2:]
        dr = (cr * tc + ci * ts).astype(BF16)
        di = (ci * tc - cr * ts).astype(BF16)
        y_ref[c] = (jnp.dot(w4[:, :n1], dr, preferred_element_type=F32)
                    + jnp.dot(w4[:, n1:], di, preferred_element_type=F32))
        return carry

    lax.fori_loop(0, cb, body, 0, unroll=FFT_UNROLL)


def _const_spec(a):
    nd = a.ndim
    return pl.BlockSpec(a.shape, lambda *_: (0,) * nd)


def _filter_spectrum(kern_slabs, cst):
    C = kern_slabs.shape[0]
    cb = FFT_CH_BLOCK
    consts = [cst["w1r"], cst["tc"], cst["ts"], cst["w2a"], cst["w2b"]]
    return pl.pallas_call(
        _spectrum_kernel,
        grid=(C // cb,),
        in_specs=[pl.BlockSpec((cb, FFT_N1, FFT_N2), lambda i: (i, 0, 0))] + [_const_spec(a) for a in consts],
        out_specs=pl.BlockSpec((cb, FFT_N1, 2 * FFT_N2), lambda i: (i, 0, 0)),
        out_shape=jax.ShapeDtypeStruct((C, FFT_N1, 2 * FFT_N2), F32),
        scratch_shapes=[pltpu.VMEM((cb * FFT_N1, FFT_N2), BF16), pltpu.VMEM((cb * FFT_N1, FFT_N2), BF16)],
        compiler_params=_params("parallel"),
        name="filter_spectrum",
    )(kern_slabs, *consts)


def _fft_conv(x_slabs, kf, cst):
    P, C = x_slabs.shape[:2]
    cb = FFT_CH_BLOCK
    consts = [cst["w1c"], cst["tc"], cst["ts"], cst["w2a"], cst["w2b"], cst["w3a"], cst["w3b"], cst["w4"]]
    return pl.pallas_call(
        _fftconv_kernel,
        grid=(C // cb, P),
        in_specs=[pl.BlockSpec((None, cb, FFT_N1, FFT_N2), lambda i, p: (p, i, 0, 0)),
                  pl.BlockSpec((cb, FFT_N1, 2 * FFT_N2), lambda i, p: (i, 0, 0))]
                 + [_const_spec(a) for a in consts],
        out_specs=pl.BlockSpec((None, cb, FFT_N1, FFT_N2), lambda i, p: (p, i, 0, 0)),
        out_shape=jax.ShapeDtypeStruct((P, C, FFT_N1, FFT_N2), F32),
        scratch_shapes=[pltpu.VMEM((cb * FFT_N1, FFT_N2), BF16), pltpu.VMEM((cb * FFT_N1, FFT_N2), BF16),
                        pltpu.VMEM((cb * FFT_N1, 2 * FFT_N2), F32)],
        compiler_params=_params("parallel", "arbitrary"),
        name="fft_conv",
    )(x_slabs, kf, *consts)


def _split_bf16(x):
    hi = x.astype(BF16)
    lo = (x - hi.astype(F32)).astype(BF16)
    return hi, lo


def _outproj_kernel(attn_ref, yc_ref, u_ref, x0_ref, d_ref, gn_ref, grp_ref, wo_ref, x_ref,
                    gf_ref, rwh_ref, rwl_ref, rb_ref, tri_ref, h_ref, hn_ref, rt_ref, cnt_out_ref,
                    cnt_ref):
    z = (yc_ref[...] + u_ref[...] * d_ref[...]) * x0_ref[...]
    zh, zl = _split_bf16(z * z)
    grp = grp_ref[...]
    ssq = (jnp.dot(zh, grp, preferred_element_type=F32) + jnp.dot(zl, grp, preferred_element_type=F32))
    gsz = HYENA_WIDTH // HYENA_GROUPS
    hy_out = (z * lax.rsqrt(ssq * (1.0 / gsz) + EPS) * gn_ref[...]).astype(BF16)
    mix = (jnp.dot(attn_ref[...], wo_ref[:ATTN_WIDTH, :], preferred_element_type=F32)
           + jnp.dot(hy_out, wo_ref[ATTN_WIDTH:, :], preferred_element_type=F32))
    h = x_ref[...] + mix
    h_ref[...] = h
    hn = _rms(h, gf_ref[...], EPS)
    hn_ref[...] = hn.astype(BF16)
    nh, nl = _split_bf16(hn)
    logits = (jnp.dot(nh, rwh_ref[...], preferred_element_type=F32)
              + jnp.dot(nl, rwh_ref[...], preferred_element_type=F32)
              + jnp.dot(nh, rwl_ref[...], preferred_element_type=F32)) + rb_ref[...]

    @pl.when(pl.program_id(0) == 0)
    def _():
        cnt_ref[...] = jnp.zeros_like(cnt_ref)

    lane = lax.broadcasted_iota(jnp.int32, logits.shape, 1)
    work = logits
    top_val, top_hot = [], []
    for _ in range(TOP_K):
        m = jnp.max(work, axis=-1, keepdims=True)
        idx = jnp.min(jnp.where(work == m, lane, ROUTER_PAD), axis=-1, keepdims=True)
        hot = lane == idx
        top_val.append(m)
        top_hot.append(hot)
        work = jnp.where(hot, -jnp.inf, work)
    ex = [jnp.exp(v - top_val[0]) for v in top_val]
    inv_den = 1.0 / (ex[0] + ex[1] + ex[2] + ex[3])
    sel = jnp.zeros(logits.shape, F32)
    for hot in top_hot:
        sel = sel + hot.astype(F32)
    before = jnp.dot(tri_ref[...], sel.astype(BF16), preferred_element_type=F32) + cnt_ref[0:1, :]
    lane_f = lane.astype(F32)
    packed = jnp.zeros(logits.shape, F32)
    for r, hot in enumerate(top_hot):
        e_r = jnp.sum(jnp.where(hot, lane_f, 0.0), axis=-1, keepdims=True)
        rank_r = jnp.sum(jnp.where(hot, before, 0.0), axis=-1, keepdims=True)
        packed = jnp.where(lane == r, e_r, packed)
        packed = jnp.where(lane == TOP_K + r, rank_r, packed)
        packed = jnp.where(lane == 2 * TOP_K + r, ex[r] * inv_den, packed)
    rt_ref[...] = packed
    cnt_ref[...] = cnt_ref[...] + jnp.sum(sel, axis=0, keepdims=True)
    cnt_out_ref[...] = cnt_ref[...]


def _out_proj(attn2, yc2, u2, x02, hyena_d, hyena_gn, w_out_bf, x2, g_ffn, router_w, router_b):
    T, D = x2.shape
    tm = TOKEN_TILE
    C = HYENA_WIDTH
    gid = jnp.arange(C) // (C // HYENA_GROUPS)
    grp = (gid[:, None] == gid[None, :]).astype(BF16)
    rw = jnp.zeros((D, ROUTER_PAD), F32).at[:, :N_EXPERTS].set(router_w)
    rwh, rwl = _split_bf16(rw)
    rb = jnp.full((1, ROUTER_PAD), -jnp.inf, F32).at[0, :N_EXPERTS].set(router_b)
    tri = (jnp.arange(tm)[:, None] > jnp.arange(tm)[None, :]).astype(BF16)
    row = lambda i: (i, 0)
    const = lambda i: (0, 0)
    return pl.pallas_call(
        _outproj_kernel,
        grid=(T // tm,),
        in_specs=[
            pl.BlockSpec((tm, ATTN_WIDTH), row),
            pl.BlockSpec((tm, C), row),
            pl.BlockSpec((tm, C), row),
            pl.BlockSpec((tm, C), row),
            pl.BlockSpec((1, C), const),
            pl.BlockSpec((1, C), const),
            pl.BlockSpec((C, C), const),
            pl.BlockSpec((D, D), const),
            pl.BlockSpec((tm, D), row),
            pl.BlockSpec((1, D), const),
            pl.BlockSpec((D, ROUTER_PAD), const),
            pl.BlockSpec((D, ROUTER_PAD), const),
            pl.BlockSpec((1, ROUTER_PAD), const),
            pl.BlockSpec((tm, tm), const),
        ],
        out_specs=[
            pl.BlockSpec((tm, D), row),
            pl.BlockSpec((tm, D), row),
            pl.BlockSpec((tm, ROUTER_PAD), row),
            pl.BlockSpec((SUBLANES, ROUTER_PAD), const),
        ],
        out_shape=[
            jax.ShapeDtypeStruct((T, D), F32),
            jax.ShapeDtypeStruct((T, D), BF16),
            jax.ShapeDtypeStruct((T, ROUTER_PAD), F32),
            jax.ShapeDtypeStruct((SUBLANES, ROUTER_PAD), F32),
        ],
        scratch_shapes=[pltpu.VMEM((SUBLANES, ROUTER_PAD), F32)],
        compiler_params=_params("arbitrary"),
        name="out_proj",
    )(attn2, yc2, u2, x02, hyena_d.reshape(1, C), hyena_gn.reshape(1, C), grp, w_out_bf, x2,
      g_ffn.reshape(1, D), rwh, rwl, rb, tri)


def _expert_kernel(be_ref, nused_ref, x_ref, wg_ref, bg_ref, wu_ref, bu_ref, wd_ref, bd_ref, y_ref,
                   wg_bf, wu_bf, wd_bf):
    i = pl.program_id(0)
    used = i < nused_ref[0]
    new_expert = jnp.logical_or(i == 0, be_ref[i] != be_ref[jnp.maximum(i - 1, 0)])

    @pl.when(jnp.logical_and(used, new_expert))
    def _():
        wg_bf[...] = wg_ref[0].astype(BF16)
        wu_bf[...] = wu_ref[0].astype(BF16)
        wd_bf[...] = wd_ref[0].astype(BF16)

    @pl.when(used)
    def _():
        x = x_ref[...]
        g = jnp.minimum(jnp.dot(x, wg_bf[...], preferred_element_type=F32) + bg_ref[0], SWIGLU_LIMIT)
        u = jnp.clip(jnp.dot(x, wu_bf[...], preferred_element_type=F32) + bu_ref[0],
                     -SWIGLU_LIMIT, SWIGLU_LIMIT)
        a = (u + 1.0) * (g * jax.nn.sigmoid(SWIGLU_ALPHA * g))
        y = jnp.dot(a.astype(BF16), wd_bf[...], preferred_element_type=F32) + bd_ref[0]
        y_ref[...] = y.astype(y_ref.dtype)

    @pl.when(jnp.logical_not(used))
    def _():
        y_ref[...] = jnp.zeros_like(y_ref)


def _experts(block_e, n_used, x_sorted, wg, bg, wu, bu, wd, bd):
    P, D = x_sorted.shape
    tm = EXPERT_TILE
    E, _, FF = wg.shape
    wmap = lambda i, be, nu: (be[i], 0, 0)
    grid_spec = pltpu.PrefetchScalarGridSpec(
        num_scalar_prefetch=2,
        grid=(P // tm,),
        in_specs=[
            pl.BlockSpec((tm, D), lambda i, be, nu: (i, 0)),
            pl.BlockSpec((1, D, FF), wmap),
            pl.BlockSpec((1, 1, FF), wmap),
            pl.BlockSpec((1, D, FF), wmap),
            pl.BlockSpec((1, 1, FF), wmap),
            pl.BlockSpec((1, FF, D), wmap),
            pl.BlockSpec((1, 1, D), wmap),
        ],
        out_specs=pl.BlockSpec((tm, D), lambda i, be, nu: (i, 0)),
        scratch_shapes=[pltpu.VMEM((D, FF), BF16), pltpu.VMEM((D, FF), BF16), pltpu.VMEM((FF, D), BF16)],
    )
    return pl.pallas_call(
        _expert_kernel,
        grid_spec=grid_spec,
        out_shape=jax.ShapeDtypeStruct((P, D), BF16),
        compiler_params=_params("arbitrary"),
        name="moe_experts",
    )(block_e, n_used, x_sorted, wg, bg.reshape(E, 1, FF), wu, bu.reshape(E, 1, FF),
      wd, bd.reshape(E, 1, D))


def _final_kernel(h_ref, yg_ref, rt_ref, p_ref, wp_ref, gp_ref, wg_ref, bg_ref, gfin_ref, o_ref):
    h = h_ref[...]
    for r in range(TOP_K):
        h = h + yg_ref[r].astype(F32) * rt_ref[:, 2 * TOP_K + r: 2 * TOP_K + r + 1]
    e = _rms(jnp.dot(p_ref[...].astype(BF16), wp_ref[...], preferred_element_type=F32),
             gp_ref[...], EPS)
    gate = jax.nn.sigmoid(jnp.dot(h.astype(BF16), wg_ref[...], preferred_element_type=F32)
                          + bg_ref[...])
    h = h + gate * e
    o_ref[...] = _rms(h, gfin_ref[...], EPS)


def _final(h1, yg, route, p2, w_ple_bf, g_ple, w_gate_bf, b_gate, g_final):
    T, D = h1.shape
    tm = TOKEN_TILE
    PD = p2.shape[1]
    row = lambda i: (i, 0)
    const = lambda i: (0, 0)
    return pl.pallas_call(
        _final_kernel,
        grid=(T // tm,),
        in_specs=[
            pl.BlockSpec((tm, D), row),
            pl.BlockSpec((TOP_K, tm, D), lambda i: (0, i, 0)),
            pl.BlockSpec((tm, ROUTER_PAD), row),
            pl.BlockSpec((tm, PD), row),
            pl.BlockSpec((PD, D), const),
            pl.BlockSpec((1, D), const),
            pl.BlockSpec((D, D), const),
            pl.BlockSpec((1, D), const),
            pl.BlockSpec((1, D), const),
        ],
        out_specs=pl.BlockSpec((tm, D), row),
        out_shape=jax.ShapeDtypeStruct((T, D), F32),
        compiler_params=_params("parallel"),
        name="final",
    )(h1, yg, route, p2, w_ple_bf, g_ple.reshape(1, D), w_gate_bf, b_gate.reshape(1, D),
      g_final.reshape(1, D))


def _rope_tables(seq):
    d = DIFF_HEAD_DIM
    pos = jnp.arange(seq, dtype=F32)
    inv = ROPE_THETA ** (-jnp.arange(0, d, 2, dtype=F32) / d)
    ang = pos[:, None] * inv[None, :]
    cos, sin = jnp.cos(ang), jnp.sin(ang)
    z = jnp.zeros_like(sin)
    cos_t = jnp.tile(jnp.concatenate([cos, cos], -1), (1, LANES // d))
    s1_t = jnp.tile(jnp.concatenate([-sin, z], -1), (1, LANES // d))
    s2_t = jnp.tile(jnp.concatenate([z, sin], -1), (1, LANES // d))
    return cos_t, s1_t, s2_t


def _hyena_filters(seq, w1, b1, w_inner, b_inner, freq, w_out):
    pos = jnp.arange(seq, dtype=F32)
    t = pos / (seq - 1)
    bands = (FILTER_EMB - 1) // 2
    f = jnp.linspace(1e-4, bands - 1, bands, dtype=F32)
    fw = ((2.0 * math.pi / seq) * pos)[:, None] * f[None, :]
    z = jnp.concatenate([t[:, None], jnp.cos(fw), -jnp.sin(fw)], axis=-1)
    hp = lax.Precision.HIGHEST
    h = jnp.sin(freq[0] * (jnp.dot(z, w1, precision=hp) + b1))
    for j in range(FILTER_INNER):
        h = jnp.sin(freq[j + 1] * (jnp.dot(h, w_inner[j], precision=hp) + b_inner[j]))
    h = jnp.dot(h, w_out, precision=hp)
    max_decay = math.log(DECAY_TARGET) / FAST_DECAY_PCT
    min_decay = math.log(DECAY_TARGET) / SLOW_DECAY_PCT
    deltas = jnp.abs(jnp.linspace(min_decay, max_decay, HYENA_WIDTH, dtype=F32))
    decay = jnp.exp(-t[:, None] * deltas[None, :])
    return h[:, :HYENA_WIDTH] * decay, h[:, HYENA_WIDTH:] * decay


def _long_conv(u, h_fwd, h_bwd):
    B, L, C = u.shape
    assert 2 * L == FFT_N and B % 2 == 0
    P, R = B // 2, FFT_N1 // 2
    cst = _dft_constants()
    lag0 = h_fwd[:1] + h_bwd[:1]
    kern = jnp.concatenate([lag0, h_fwd[1:], jnp.zeros((1, C), F32), h_bwd[:0:-1]], axis=0)
    kern_slabs = kern.T.reshape(C, FFT_N1, FFT_N2).astype(BF16)
    kf = _filter_spectrum(kern_slabs, cst)
    x_slabs = (u.reshape(2, P, R, FFT_N2, C).transpose(1, 4, 0, 2, 3)
               .reshape(P, C, FFT_N1, FFT_N2).astype(BF16))
    y = _fft_conv(x_slabs, kf, cst)
    return y.reshape(P, C, 2, R, FFT_N2).transpose(2, 0, 3, 4, 1).reshape(B, L, C)


def _dispatch_indices(route, cnt, T):
    tm = EXPERT_TILE
    A = T * TOP_K
    top_e = route[:, :TOP_K].astype(jnp.int32)
    rank = route[:, TOP_K:2 * TOP_K].astype(jnp.int32)
    counts = cnt[0, :N_EXPERTS].astype(jnp.int32)
    padded = (counts + tm - 1) // tm * tm
    pad_end = jnp.cumsum(padded)
    pad_start = pad_end - padded
    start = jnp.cumsum(counts) - counts
    pos = pad_start[top_e] + rank
    n_blocks = -(-A // tm) + N_EXPERTS
    P = n_blocks * tm
    block_first = jnp.arange(n_blocks, dtype=jnp.int32) * tm
    block_e = jnp.minimum(jnp.sum(pad_end[None, :] <= block_first[:, None], axis=1),
                          N_EXPERTS - 1).astype(jnp.int32)
    n_used = (pad_end[-1] // tm).astype(jnp.int32).reshape(1)
    tok = jnp.broadcast_to(jnp.arange(T, dtype=jnp.int32)[:, None], (T, TOP_K))
    _, sorted_tok = lax.sort_key_val(pos.reshape(A), tok.reshape(A))
    slot = jnp.arange(P, dtype=jnp.int32)
    slot_e = jnp.repeat(block_e, tm)
    r = slot - pad_start[slot_e]
    compact = jnp.clip(start[slot_e] + r, 0, A - 1)
    slot_tok = jnp.where(r < counts[slot_e], sorted_tok[compact], slot % T)
    return pos.T, slot_tok, block_e, n_used


def kernel(x, p, g_mix, w_in, hyena_conv_w, hyena_conv_b, flt_w1, flt_b1, flt_w_inner, flt_b_inner, flt_freq, flt_w_out, hyena_d, hyena_gn, lambda_q1, lambda_k1, lambda_q2, lambda_k2, attn_subln, w_out, g_ffn, router_w, router_b, w_gate, b_gate, w_up, b_up, w_down, b_down, w_ple, g_ple, w_ple_gate, b_ple_gate, g_final):
    B, S, D = x.shape
    T = B * S
    i = 0
    x2 = x.reshape(T, D)

    cos_t, s1_t, s2_t = _rope_tables(S)
    q, k, vt, hy = _in_proj(x2, g_mix[i], w_in[i].astype(BF16), cos_t, s1_t, s2_t, S)

    lam = (jnp.exp(jnp.sum(lambda_q1[i] * lambda_k1[i])) - jnp.exp(jnp.sum(lambda_q2[i] * lambda_k2[i]))
           + LAMBDA_INIT).reshape(1).astype(F32)
    attn = _diff_attention(lam, q.reshape(B, S, -1), k.reshape(B, S, -1), vt, attn_subln[i])

    u, hx0 = _short_conv(hy.reshape(B, S, -1), hyena_conv_w[i], hyena_conv_b[i])
    h_fwd, h_bwd = _hyena_filters(S, flt_w1[i], flt_b1[i], flt_w_inner[i], flt_b_inner[i],
                                  flt_freq[i], flt_w_out[i])
    yc = _long_conv(u, h_fwd, h_bwd)

    h1, hn, route, cnt = _out_proj(attn.reshape(T, -1), yc.reshape(T, -1), u.reshape(T, -1),
                                   hx0.reshape(T, -1), hyena_d[i], hyena_gn[i],
                                   w_out[i].astype(BF16), x2, g_ffn[i], router_w[i], router_b[i])

    pos, slot_tok, block_e, n_used = _dispatch_indices(route, cnt, T)
    x_sorted = hn[slot_tok]
    y = _experts(block_e, n_used, x_sorted, w_gate[i], b_gate[i], w_up[i], b_up[i],
                 w_down[i], b_down[i])
    yg = y[pos.reshape(-1)].reshape(TOP_K, T, D)

    out = _final(h1, yg, route, p[i].reshape(T, -1), w_ple[i].astype(BF16), g_ple[i],
                 w_ple_gate[i].astype(BF16), b_ple_gate[i], g_final)
    return out.reshape(B, S, D)
```

```python
import functools
import math

import jax
import jax.numpy as jnp
from jax import lax
from jax.experimental import pallas as pl
from jax.experimental.pallas import tpu as pltpu

F32 = jnp.float32
BF16 = jnp.bfloat16

D_MODEL = 1024
ATTN_WIDTH = 512
HYENA_WIDTH = 512
DIFF_HEADS = 4
DIFF_HEAD_DIM = 64
DIFF_V_DIM = 128
HYENA_GROUPS = 8
FILTER_EMB = 33
FILTER_INNER = 2
FAST_DECAY_PCT = 0.3
SLOW_DECAY_PCT = 1.5
DECAY_TARGET = 1e-2
ROPE_THETA = 10000.0
N_EXPERTS = 32
TOP_K = 4
SWIGLU_ALPHA = 1.702
SWIGLU_LIMIT = 7.0
EPS = 1e-6
SUBLN_EPS = 1e-5
LAMBDA_INIT = 0.8 - 0.6 * math.exp(-0.3 * 0)

V7X_VMEM_LIMIT_BYTES = 56 * 1024 * 1024
LANES = 128
SUBLANES = 8

TOKEN_TILE = 512
ATTN_Q_TILE = 512
ATTN_KV_TILE = 512
EXPERT_TILE = 512
ROUTER_PAD = LANES


def _params(*sem):
    return pltpu.CompilerParams(dimension_semantics=sem, vmem_limit_bytes=V7X_VMEM_LIMIT_BYTES)


def _rms(x, g, eps):
    return x * lax.rsqrt(jnp.mean(x * x, axis=-1, keepdims=True) + eps) * g


def _inproj_kernel(x_ref, g_ref, w_ref, wvt_ref, c_ref, s1_ref, s2_ref, q_ref, k_ref, vt_ref, hy_ref):
    a = _rms(x_ref[...], g_ref[...], EPS).astype(BF16)
    cos, s1, s2 = c_ref[...], s1_ref[...], s2_ref[...]

    def rope(t):
        return t * cos + pltpu.roll(t, LANES - 32, axis=1) * s1 + pltpu.roll(t, 32, axis=1) * s2

    q_scale = (DIFF_HEAD_DIM ** -0.5) * math.log2(math.e)
    qk = jnp.dot(a, w_ref[:, : 2 * ATTN_WIDTH], preferred_element_type=F32)
    for j in range(ATTN_WIDTH // LANES):
        sl = slice(j * LANES, (j + 1) * LANES)
        q_ref[:, sl] = (rope(qk[:, sl]) * q_scale).astype(BF16)
        k_ref[:, sl] = rope(qk[:, ATTN_WIDTH + j * LANES: ATTN_WIDTH + (j + 1) * LANES]).astype(BF16)
    vt_ref[...] = lax.dot_general(wvt_ref[...], a, (((1,), (1,)), ((), ())),
                                  preferred_element_type=F32).astype(BF16)
    hy_ref[...] = jnp.dot(a, w_ref[:, 3 * ATTN_WIDTH:], preferred_element_type=F32)


def _in_proj(x2, g_mix, w_in_bf, cos_t, s1_t, s2_t, seq):
    T, D = x2.shape
    tm = TOKEN_TILE
    nseq = seq // tm
    wvt = w_in_bf[:, 2 * ATTN_WIDTH: 3 * ATTN_WIDTH].T
    row = lambda i: (i, 0)
    const = lambda i: (0, 0)
    pos = lambda i: (i % nseq, 0)
    return pl.pallas_call(
        _inproj_kernel,
        grid=(T // tm,),
        in_specs=[
            pl.BlockSpec((tm, D), row),
            pl.BlockSpec((1, D), const),
            pl.BlockSpec(w_in_bf.shape, const),
            pl.BlockSpec(wvt.shape, const),
            pl.BlockSpec((tm, LANES), pos),
            pl.BlockSpec((tm, LANES), pos),
            pl.BlockSpec((tm, LANES), pos),
        ],
        out_specs=[
            pl.BlockSpec((tm, ATTN_WIDTH), row),
            pl.BlockSpec((tm, ATTN_WIDTH), row),
            pl.BlockSpec((ATTN_WIDTH, tm), lambda i: (0, i)),
            pl.BlockSpec((tm, 3 * HYENA_WIDTH), row),
        ],
        out_shape=[
            jax.ShapeDtypeStruct((T, ATTN_WIDTH), BF16),
            jax.ShapeDtypeStruct((T, ATTN_WIDTH), BF16),
            jax.ShapeDtypeStruct((ATTN_WIDTH, T), BF16),
            jax.ShapeDtypeStruct((T, 3 * HYENA_WIDTH), F32),
        ],
        compiler_params=_params("parallel"),
        name="in_proj",
    )(x2, g_mix.reshape(1, D), w_in_bf, wvt, cos_t, s1_t, s2_t)


def _attn_kernel(lam_ref, q_ref, k_ref, vt_ref, g_ref, o_ref, s_ref, *, kc):
    q = q_ref[0]
    tq = q.shape[0]
    nc = k_ref.shape[1] // kc
    lane = lax.broadcasted_iota(jnp.int32, q.shape, 1)
    zero = jnp.zeros_like(q)
    qs = (jnp.where(lane < DIFF_HEAD_DIM, q, zero), jnp.where(lane >= DIFF_HEAD_DIM, q, zero))

    def fold8(t, op):
        r = t[0:SUBLANES]
        for j in range(1, kc // SUBLANES):
            r = op(r, t[j * SUBLANES:(j + 1) * SUBLANES])
        return r

    outs = []
    for c in range(2):
        m8 = jnp.full((SUBLANES, tq), -jnp.inf, F32)
        for j in range(nc):
            s = lax.dot_general(k_ref[0, j * kc:(j + 1) * kc, :], qs[c], (((1,), (1,)), ((), ())),
                                preferred_element_type=F32)
            s_ref[c, j * kc:(j + 1) * kc, :] = s
            m8 = jnp.maximum(m8, fold8(s, jnp.maximum))
        m = jnp.max(m8, axis=0, keepdims=True)
        l8 = jnp.zeros((SUBLANES, tq), F32)
        acc = jnp.zeros((DIFF_V_DIM, tq), F32)
        for j in range(nc):
            p = jnp.exp2(s_ref[c, j * kc:(j + 1) * kc, :] - m)
            l8 = l8 + fold8(p, jnp.add)
            acc = acc + jnp.dot(vt_ref[:, j * kc:(j + 1) * kc], p.astype(BF16),
                                preferred_element_type=F32)
        outs.append(acc * (1.0 / jnp.sum(l8, axis=0, keepdims=True)))
    o = outs[0] - lam_ref[0] * outs[1]
    o = o * lax.rsqrt(jnp.mean(o * o, axis=0, keepdims=True) + SUBLN_EPS)
    o = o * (g_ref[...] * (1.0 - LAMBDA_INIT))
    o_ref[0] = o.T.astype(o_ref.dtype)


def _diff_attention(lam, q, k, vt, g_subln):
    B, S, _ = q.shape
    tq, kc = ATTN_Q_TILE, ATTN_KV_TILE
    return pl.pallas_call(
        functools.partial(_attn_kernel, kc=kc),
        grid=(B, DIFF_HEADS, S // tq),
        in_specs=[
            pl.BlockSpec(memory_space=pltpu.SMEM),
            pl.BlockSpec((1, tq, LANES), lambda b, h, i: (b, i, h)),
            pl.BlockSpec((1, S, LANES), lambda b, h, i: (b, 0, h)),
            pl.BlockSpec((DIFF_V_DIM, S), lambda b, h, i: (h, b)),
            pl.BlockSpec((DIFF_V_DIM, 1), lambda b, h, i: (0, 0)),
        ],
        out_specs=pl.BlockSpec((1, tq, LANES), lambda b, h, i: (b, i, h)),
        out_shape=jax.ShapeDtypeStruct((B, S, ATTN_WIDTH), BF16),
        scratch_shapes=[pltpu.VMEM((2, S, tq), F32)],
        compiler_params=_params("parallel", "parallel", "parallel"),
        name="diff_attn",
    )(lam, q, k, vt, g_subln.reshape(DIFF_V_DIM, 1))


def _shortconv_kernel(hy_ref, prev_ref, next_ref, w_ref, b_ref, u_ref, x0_ref):
    i = pl.program_id(1)
    last = pl.num_programs(1) - 1
    x = hy_ref[0]
    ts = x.shape[0]
    prev_row = jnp.where(i == 0, 0.0, prev_ref[0, SUBLANES - 1:SUBLANES, :])
    next_row = jnp.where(i == last, 0.0, next_ref[0, 0:1, :])
    row = lax.broadcasted_iota(jnp.int32, (ts, 1), 0)
    xm = jnp.where(row == 0, prev_row, pltpu.roll(x, 1, axis=0))
    xp = jnp.where(row == ts - 1, next_row, pltpu.roll(x, ts - 1, axis=0))
    y = b_ref[...] + xm * w_ref[0:1, :] + x * w_ref[1:2, :] + xp * w_ref[2:3, :]
    C = HYENA_WIDTH
    u_ref[0] = y[:, :C] * y[:, 2 * C:]
    x0_ref[0] = y[:, C:2 * C]


def _short_conv(hy, conv_w, conv_b):
    B, S, C3 = hy.shape
    ts = TOKEN_TILE
    nb = ts // SUBLANES
    return pl.pallas_call(
        _shortconv_kernel,
        grid=(B, S // ts),
        in_specs=[
            pl.BlockSpec((1, ts, C3), lambda b, i: (b, i, 0)),
            pl.BlockSpec((1, SUBLANES, C3), lambda b, i: (b, jnp.maximum(i * nb - 1, 0), 0)),
            pl.BlockSpec((1, SUBLANES, C3), lambda b, i: (b, jnp.minimum((i + 1) * nb, S // SUBLANES - 1), 0)),
            pl.BlockSpec((3, C3), lambda b, i: (0, 0)),
            pl.BlockSpec((1, C3), lambda b, i: (0, 0)),
        ],
        out_specs=[
            pl.BlockSpec((1, ts, HYENA_WIDTH), lambda b, i: (b, i, 0)),
            pl.BlockSpec((1, ts, HYENA_WIDTH), lambda b, i: (b, i, 0)),
        ],
        out_shape=[
            jax.ShapeDtypeStruct((B, S, HYENA_WIDTH), F32),
            jax.ShapeDtypeStruct((B, S, HYENA_WIDTH), F32),
        ],
        compiler_params=_params("parallel", "parallel"),
        name="short_conv",
    )(hy, hy, hy, conv_w, conv_b.reshape(1, C3))


FFT_N = 8192
FFT_N1 = 64
FFT_N2 = 128
FFT_CH_BLOCK = 32
FFT_UNROLL = 4


def _dft_constants():
    import numpy as np
    n1, n2, n = FFT_N1, FFT_N2, FFT_N
    k1 = np.arange(n1)[:, None]
    t1 = np.arange(n1)[None, :]
    f1 = np.exp(-2j * np.pi * k1 * t1 / n1)
    f1h = f1[:, : n1 // 2]
    w1c = np.block([[f1h.real, -f1h.imag], [f1h.imag, f1h.real]])
    w1r = np.concatenate([f1h.real, f1h.imag], axis=0)
    t2 = np.arange(n2)[:, None]
    k2 = np.arange(n2)[None, :]
    f2 = np.exp(-2j * np.pi * t2 * k2 / n2)
    w2a = np.concatenate([f2.real, f2.imag], axis=1)
    w2b = np.concatenate([-f2.imag, f2.real], axis=1)
    g2 = np.conj(f2)
    w3a = np.concatenate([g2.real, g2.imag], axis=1)
    w3b = np.concatenate([-g2.imag, g2.real], axis=1)
    h = np.conj(f1).T[: n1 // 2] / n
    w4 = np.block([[h.real, -h.imag], [h.imag, h.real]])
    tw = np.exp(-2j * np.pi * np.arange(n1)[:, None] * np.arange(n2)[None, :] / n)
    bf = lambda a: jnp.asarray(a, dtype=F32).astype(BF16)
    return dict(w1c=bf(w1c), w1r=bf(w1r), w2a=bf(w2a), w2b=bf(w2b), w3a=bf(w3a), w3b=bf(w3b),
                w4=bf(w4), tc=jnp.asarray(tw.real, F32), ts=jnp.asarray(tw.imag, F32))


def _dft_forward(x_ref, w1_ref, tc_ref, ts_ref, w2a_ref, w2b_ref, ar_ref, ai_ref):
    cb = x_ref.shape[0]
    n1 = FFT_N1
    w1 = w1_ref[...]
    tc, ts = tc_ref[...], ts_ref[...]

    def body(c, carry):
        a = jnp.dot(w1, x_ref[c], preferred_element_type=F32)
        ar, ai = a[:n1], a[n1:]
        rows = pl.ds(pl.multiple_of(c * n1, n1), n1)
        ar_ref[rows, :] = (ar * tc - ai * ts).astype(BF16)
        ai_ref[rows, :] = (ar * ts + ai * tc).astype(BF16)
        return carry

    lax.fori_loop(0, cb, body, 0, unroll=FFT_UNROLL)
    return (jnp.dot(ar_ref[...], w2a_ref[...], preferred_element_type=F32)
            + jnp.dot(ai_ref[...], w2b_ref[...], preferred_element_type=F32))


def _spectrum_kernel(hf_ref, hb_ref, w1_ref, tc_ref, ts_ref, w2a_ref, w2b_ref, o_ref, ar_ref, ai_ref):
    n2 = FFT_N2
    f = _dft_forward(hf_ref, w1_ref, tc_ref, ts_ref, w2a_ref, w2b_ref, ar_ref, ai_ref)
    o_ref[...] = f.reshape(o_ref.shape)
    b = _dft_forward(hb_ref, w1_ref, tc_ref, ts_ref, w2a_ref, w2b_ref, ar_ref, ai_ref)
    b = b.reshape(o_ref.shape)
    o_ref[:, :, :n2] = o_ref[:, :, :n2] + b[:, :, :n2]
    o_ref[:, :, n2:] = o_ref[:, :, n2:] - b[:, :, n2:]


def _fftconv_kernel(x_ref, kf_ref, w1_ref, tc_ref, ts_ref, w2a_ref, w2b_ref, w3a_ref, w3b_ref,
                    w4_ref, y_ref, ar_ref, ai_ref, c_ref):
    cb = x_ref.shape[0]
    n1, n2 = FFT_N1, FFT_N2
    b = _dft_forward(x_ref, w1_ref, tc_ref, ts_ref, w2a_ref, w2b_ref, ar_ref, ai_ref)
    kf = kf_ref[...].reshape(cb * n1, 2 * n2)
    br, bi = b[:, :n2], b[:, n2:]
    kr, ki = kf[:, :n2], kf[:, n2:]
    ar_ref[...] = (br * kr - bi * ki).astype(BF16)
    ai_ref[...] = (br * ki + bi * kr).astype(BF16)
    c_ref[...] = (jnp.dot(ar_ref[...], w3a_ref[...], preferred_element_type=F32)
                  + jnp.dot(ai_ref[...], w3b_ref[...], preferred_element_type=F32))
    w4 = w4_ref[...]
    tc, ts = tc_ref[...], ts_ref[...]

    def body(c, carry):
        rows = pl.ds(pl.multiple_of(c * n1, n1), n1)
        cr, ci = c_ref[rows, :n2], c_ref[rows, n2:]
        dr = (cr * tc + ci * ts).astype(BF16)
        di = (ci * tc - cr * ts).astype(BF16)
        y_ref[c] = (jnp.dot(w4[:, :n1], dr, preferred_element_type=F32)
                    + jnp.dot(w4[:, n1:], di, preferred_element_type=F32))
        return carry

    lax.fori_loop(0, cb, body, 0, unroll=FFT_UNROLL)


def _const_spec(a):
    nd = a.ndim
    return pl.BlockSpec(a.shape, lambda *_: (0,) * nd)


def _filter_spectrum(hf_slabs, hb_slabs, cst):
    C = hf_slabs.shape[0]
    cb = FFT_CH_BLOCK
    consts = [cst["w1r"], cst["tc"], cst["ts"], cst["w2a"], cst["w2b"]]
    slab = pl.BlockSpec((cb, FFT_N1 // 2, FFT_N2), lambda i: (i, 0, 0))
    return pl.pallas_call(
        _spectrum_kernel,
        grid=(C // cb,),
        in_specs=[slab, slab] + [_const_spec(a) for a in consts],
        out_specs=pl.BlockSpec((cb, FFT_N1, 2 * FFT_N2), lambda i: (i, 0, 0)),
        out_shape=jax.ShapeDtypeStruct((C, FFT_N1, 2 * FFT_N2), F32),
        scratch_shapes=[pltpu.VMEM((cb * FFT_N1, FFT_N2), BF16), pltpu.VMEM((cb * FFT_N1, FFT_N2), BF16)],
        compiler_params=_params("parallel"),
        name="filter_spectrum",
    )(hf_slabs, hb_slabs, *consts)


FILTER_TIME_TILE = 512


def _filter_kernel(zt_ref, w1t_ref, b1_ref, wit_ref, bi_ref, fr_ref, wot_ref, dec_ref, hf_ref, hb_ref):
    hp = lax.Precision.HIGHEST
    h = jnp.sin(fr_ref[0] * (jnp.dot(w1t_ref[...], zt_ref[...], precision=hp,
                                     preferred_element_type=F32) + b1_ref[...]))
    for j in range(FILTER_INNER):
        h = jnp.sin(fr_ref[j + 1] * (jnp.dot(wit_ref[j], h, precision=hp,
                                             preferred_element_type=F32) + bi_ref[j]))
    o = jnp.dot(wot_ref[...], h, precision=hp, preferred_element_type=F32)
    dec = dec_ref[...]
    hf_ref[...] = o[:HYENA_WIDTH] * dec
    hb_ref[...] = o[HYENA_WIDTH:] * dec


def _hyena_filters(seq, w1, b1, w_inner, b_inner, freq, w_out):
    C = HYENA_WIDTH
    order = w1.shape[1]
    pos = jnp.arange(seq, dtype=F32)
    t = pos / (seq - 1)
    bands = (FILTER_EMB - 1) // 2
    f = jnp.linspace(1e-4, bands - 1, bands, dtype=F32)
    fw = ((2.0 * math.pi / seq) * pos)[:, None] * f[None, :]
    z = jnp.concatenate([t[:, None], jnp.cos(fw), -jnp.sin(fw)], axis=-1)
    zt = jnp.zeros((LANES, seq), F32).at[:FILTER_EMB].set(z.T)
    w1t = jnp.zeros((order, LANES), F32).at[:, :FILTER_EMB].set(w1.T)
    max_decay = math.log(DECAY_TARGET) / FAST_DECAY_PCT
    min_decay = math.log(DECAY_TARGET) / SLOW_DECAY_PCT
    deltas = jnp.abs(jnp.linspace(min_decay, max_decay, C, dtype=F32))
    dec_t = jnp.exp(-deltas[:, None] * t[None, :])
    tt = FILTER_TIME_TILE
    lane_blk = lambda r: pl.BlockSpec((r, tt), lambda i: (0, i))
    args = [zt, w1t, b1.reshape(order, 1), jnp.swapaxes(w_inner, 1, 2),
            b_inner.reshape(FILTER_INNER, order, 1), freq.reshape(FILTER_INNER + 1, order, 1),
            w_out.T, dec_t]
    return pl.pallas_call(
        _filter_kernel,
        grid=(seq // tt,),
        in_specs=[lane_blk(LANES)] + [_const_spec(a) for a in args[1:7]] + [lane_blk(C)],
        out_specs=[lane_blk(C), lane_blk(C)],
        out_shape=[jax.ShapeDtypeStruct((C, seq), F32), jax.ShapeDtypeStruct((C, seq), F32)],
        compiler_params=_params("parallel"),
        name="hyena_filters",
    )(*args)


def _fft_conv(x_slabs, kf, cst):
    P, C = x_slabs.shape[:2]
    cb = FFT_CH_BLOCK
    consts = [cst["w1c"], cst["tc"], cst["ts"], cst["w2a"], cst["w2b"], cst["w3a"], cst["w3b"], cst["w4"]]
    return pl.pallas_call(
        _fftconv_kernel,
        grid=(C // cb, P),
        in_specs=[pl.BlockSpec((None, cb, FFT_N1, FFT_N2), lambda i, p: (p, i, 0, 0)),
                  pl.BlockSpec((cb, FFT_N1, 2 * FFT_N2), lambda i, p: (i, 0, 0))]
                 + [_const_spec(a) for a in consts],
        out_specs=pl.BlockSpec((None, cb, FFT_N1, FFT_N2), lambda i, p: (p, i, 0, 0)),
        out_shape=jax.ShapeDtypeStruct((P, C, FFT_N1, FFT_N2), F32),
        scratch_shapes=[pltpu.VMEM((cb * FFT_N1, FFT_N2), BF16), pltpu.VMEM((cb * FFT_N1, FFT_N2), BF16),
                        pltpu.VMEM((cb * FFT_N1, 2 * FFT_N2), F32)],
        compiler_params=_params("parallel", "arbitrary"),
        name="fft_conv",
    )(x_slabs, kf, *consts)


def _split_bf16(x):
    hi = x.astype(BF16)
    lo = (x - hi.astype(F32)).astype(BF16)
    return hi, lo


def _outproj_kernel(attn_ref, yc_ref, u_ref, x0_ref, d_ref, gn_ref, grp_ref, wo_ref, x_ref,
                    gf_ref, rwh_ref, rwl_ref, rb_ref, tri_ref, h_ref, hn_ref, rt_ref, cnt_out_ref,
                    cnt_ref):
    z = (yc_ref[...] + u_ref[...] * d_ref[...]) * x0_ref[...]
    zh, zl = _split_bf16(z * z)
    grp = grp_ref[...]
    ssq = (jnp.dot(zh, grp, preferred_element_type=F32) + jnp.dot(zl, grp, preferred_element_type=F32))
    gsz = HYENA_WIDTH // HYENA_GROUPS
    hy_out = (z * lax.rsqrt(ssq * (1.0 / gsz) + EPS) * gn_ref[...]).astype(BF16)
    mix = (jnp.dot(attn_ref[...], wo_ref[:ATTN_WIDTH, :], preferred_element_type=F32)
           + jnp.dot(hy_out, wo_ref[ATTN_WIDTH:, :], preferred_element_type=F32))
    h = x_ref[...] + mix
    h_ref[...] = h
    hn = _rms(h, gf_ref[...], EPS)
    hn_ref[...] = hn.astype(BF16)
    nh, nl = _split_bf16(hn)
    logits = (jnp.dot(nh, rwh_ref[...], preferred_element_type=F32)
              + jnp.dot(nl, rwh_ref[...], preferred_element_type=F32)
              + jnp.dot(nh, rwl_ref[...], preferred_element_type=F32)) + rb_ref[...]

    @pl.when(pl.program_id(0) == 0)
    def _():
        cnt_ref[...] = jnp.zeros_like(cnt_ref)

    lane = lax.broadcasted_iota(jnp.int32, logits.shape, 1)
    work = logits
    top_val, top_hot = [], []
    for _ in range(TOP_K):
        m = jnp.max(work, axis=-1, keepdims=True)
        idx = jnp.min(jnp.where(work == m, lane, ROUTER_PAD), axis=-1, keepdims=True)
        hot = lane == idx
        top_val.append(m)
        top_hot.append(hot)
        work = jnp.where(hot, -jnp.inf, work)
    ex = [jnp.exp(v - top_val[0]) for v in top_val]
    inv_den = 1.0 / (ex[0] + ex[1] + ex[2] + ex[3])
    sel = jnp.zeros(logits.shape, F32)
    for hot in top_hot:
        sel = sel + hot.astype(F32)
    before = jnp.dot(tri_ref[...], sel.astype(BF16), preferred_element_type=F32) + cnt_ref[0:1, :]
    lane_f = lane.astype(F32)
    packed = jnp.zeros(logits.shape, F32)
    for r, hot in enumerate(top_hot):
        e_r = jnp.sum(jnp.where(hot, lane_f, 0.0), axis=-1, keepdims=True)
        rank_r = jnp.sum(jnp.where(hot, before, 0.0), axis=-1, keepdims=True)
        packed = jnp.where(lane == r, e_r, packed)
        packed = jnp.where(lane == TOP_K + r, rank_r, packed)
        packed = jnp.where(lane == 2 * TOP_K + r, ex[r] * inv_den, packed)
    rt_ref[...] = packed
    cnt_ref[...] = cnt_ref[...] + jnp.sum(sel, axis=0, keepdims=True)
    cnt_out_ref[...] = cnt_ref[...]


def _out_proj(attn2, yc2, u2, x02, hyena_d, hyena_gn, w_out_bf, x2, g_ffn, router_w, router_b):
    T, D = x2.shape
    tm = TOKEN_TILE
    C = HYENA_WIDTH
    gid = jnp.arange(C) // (C // HYENA_GROUPS)
    grp = (gid[:, None] == gid[None, :]).astype(BF16)
    rw = jnp.zeros((D, ROUTER_PAD), F32).at[:, :N_EXPERTS].set(router_w)
    rwh, rwl = _split_bf16(rw)
    rb = jnp.full((1, ROUTER_PAD), -jnp.inf, F32).at[0, :N_EXPERTS].set(router_b)
    tri = (jnp.arange(tm)[:, None] > jnp.arange(tm)[None, :]).astype(BF16)
    row = lambda i: (i, 0)
    const = lambda i: (0, 0)
    return pl.pallas_call(
        _outproj_kernel,
        grid=(T // tm,),
        in_specs=[
            pl.BlockSpec((tm, ATTN_WIDTH), row),
            pl.BlockSpec((tm, C), row),
            pl.BlockSpec((tm, C), row),
            pl.BlockSpec((tm, C), row),
            pl.BlockSpec((1, C), const),
            pl.BlockSpec((1, C), const),
            pl.BlockSpec((C, C), const),
            pl.BlockSpec((D, D), const),
            pl.BlockSpec((tm, D), row),
            pl.BlockSpec((1, D), const),
            pl.BlockSpec((D, ROUTER_PAD), const),
            pl.BlockSpec((D, ROUTER_PAD), const),
            pl.BlockSpec((1, ROUTER_PAD), const),
            pl.BlockSpec((tm, tm), const),
        ],
        out_specs=[
            pl.BlockSpec((tm, D), row),
            pl.BlockSpec((tm, D), row),
            pl.BlockSpec((tm, ROUTER_PAD), row),
            pl.BlockSpec((SUBLANES, ROUTER_PAD), const),
        ],
        out_shape=[
            jax.ShapeDtypeStruct((T, D), F32),
            jax.ShapeDtypeStruct((T, D), BF16),
            jax.ShapeDtypeStruct((T, ROUTER_PAD), F32),
            jax.ShapeDtypeStruct((SUBLANES, ROUTER_PAD), F32),
        ],
        scratch_shapes=[pltpu.VMEM((SUBLANES, ROUTER_PAD), F32)],
        compiler_params=_params("arbitrary"),
        name="out_proj",
    )(attn2, yc2, u2, x02, hyena_d.reshape(1, C), hyena_gn.reshape(1, C), grp, w_out_bf, x2,
      g_ffn.reshape(1, D), rwh, rwl, rb, tri)


def _expert_kernel(be_ref, nused_ref, x_ref, wg_ref, bg_ref, wu_ref, bu_ref, wd_ref, bd_ref, y_ref,
                   wg_bf, wu_bf, wd_bf):
    i = pl.program_id(0)
    used = i < nused_ref[0]
    new_expert = jnp.logical_or(i == 0, be_ref[i] != be_ref[jnp.maximum(i - 1, 0)])

    @pl.when(jnp.logical_and(used, new_expert))
    def _():
        wg_bf[...] = wg_ref[0].astype(BF16)
        wu_bf[...] = wu_ref[0].astype(BF16)
        wd_bf[...] = wd_ref[0].astype(BF16)

    @pl.when(used)
    def _():
        x = x_ref[...]
        g = jnp.minimum(jnp.dot(x, wg_bf[...], preferred_element_type=F32) + bg_ref[0], SWIGLU_LIMIT)
        u = jnp.clip(jnp.dot(x, wu_bf[...], preferred_element_type=F32) + bu_ref[0],
                     -SWIGLU_LIMIT, SWIGLU_LIMIT)
        a = (u + 1.0) * (g * jax.nn.sigmoid(SWIGLU_ALPHA * g))
        y = jnp.dot(a.astype(BF16), wd_bf[...], preferred_element_type=F32) + bd_ref[0]
        y_ref[...] = y.astype(y_ref.dtype)

    @pl.when(jnp.logical_not(used))
    def _():
        y_ref[...] = jnp.zeros_like(y_ref)


def _experts(block_e, n_used, x_sorted, wg, bg, wu, bu, wd, bd):
    P, D = x_sorted.shape
    tm = EXPERT_TILE
    E, _, FF = wg.shape
    wmap = lambda i, be, nu: (be[i], 0, 0)
    grid_spec = pltpu.PrefetchScalarGridSpec(
        num_scalar_prefetch=2,
        grid=(P // tm,),
        in_specs=[
            pl.BlockSpec((tm, D), lambda i, be, nu: (i, 0)),
            pl.BlockSpec((1, D, FF), wmap),
            pl.BlockSpec((1, 1, FF), wmap),
            pl.BlockSpec((1, D, FF), wmap),
            pl.BlockSpec((1, 1, FF), wmap),
            pl.BlockSpec((1, FF, D), wmap),
            pl.BlockSpec((1, 1, D), wmap),
        ],
        out_specs=pl.BlockSpec((tm, D), lambda i, be, nu: (i, 0)),
        scratch_shapes=[pltpu.VMEM((D, FF), BF16), pltpu.VMEM((D, FF), BF16), pltpu.VMEM((FF, D), BF16)],
    )
    return pl.pallas_call(
        _expert_kernel,
        grid_spec=grid_spec,
        out_shape=jax.ShapeDtypeStruct((P, D), BF16),
        compiler_params=_params("arbitrary"),
        name="moe_experts",
    )(block_e, n_used, x_sorted, wg, bg.reshape(E, 1, FF), wu, bu.reshape(E, 1, FF),
      wd, bd.reshape(E, 1, D))


def _final_kernel(h_ref, yg_ref, rt_ref, p_ref, wp_ref, gp_ref, wg_ref, bg_ref, gfin_ref, o_ref):
    h = h_ref[...]
    for r in range(TOP_K):
        h = h + yg_ref[r].astype(F32) * rt_ref[:, 2 * TOP_K + r: 2 * TOP_K + r + 1]
    e = _rms(jnp.dot(p_ref[...].astype(BF16), wp_ref[...], preferred_element_type=F32),
             gp_ref[...], EPS)
    gate = jax.nn.sigmoid(jnp.dot(h.astype(BF16), wg_ref[...], preferred_element_type=F32)
                          + bg_ref[...])
    h = h + gate * e
    o_ref[...] = _rms(h, gfin_ref[...], EPS)


def _final(h1, yg, route, p2, w_ple_bf, g_ple, w_gate_bf, b_gate, g_final):
    T, D = h1.shape
    tm = TOKEN_TILE
    PD = p2.shape[1]
    row = lambda i: (i, 0)
    const = lambda i: (0, 0)
    return pl.pallas_call(
        _final_kernel,
        grid=(T // tm,),
        in_specs=[
            pl.BlockSpec((tm, D), row),
            pl.BlockSpec((TOP_K, tm, D), lambda i: (0, i, 0)),
            pl.BlockSpec((tm, ROUTER_PAD), row),
            pl.BlockSpec((tm, PD), row),
            pl.BlockSpec((PD, D), const),
            pl.BlockSpec((1, D), const),
            pl.BlockSpec((D, D), const),
            pl.BlockSpec((1, D), const),
            pl.BlockSpec((1, D), const),
        ],
        out_specs=pl.BlockSpec((tm, D), row),
        out_shape=jax.ShapeDtypeStruct((T, D), F32),
        compiler_params=_params("parallel"),
        name="final",
    )(h1, yg, route, p2, w_ple_bf, g_ple.reshape(1, D), w_gate_bf, b_gate.reshape(1, D),
      g_final.reshape(1, D))


def _rope_tables(seq):
    d = DIFF_HEAD_DIM
    pos = jnp.arange(seq, dtype=F32)
    inv = ROPE_THETA ** (-jnp.arange(0, d, 2, dtype=F32) / d)
    ang = pos[:, None] * inv[None, :]
    cos, sin = jnp.cos(ang), jnp.sin(ang)
    z = jnp.zeros_like(sin)
    cos_t = jnp.tile(jnp.concatenate([cos, cos], -1), (1, LANES // d))
    s1_t = jnp.tile(jnp.concatenate([-sin, z], -1), (1, LANES // d))
    s2_t = jnp.tile(jnp.concatenate([z, sin], -1), (1, LANES // d))
    return cos_t, s1_t, s2_t


def _long_conv(u, hf_t, hb_t):
    B, L, C = u.shape
    assert 2 * L == FFT_N and B % 2 == 0
    P, R = B // 2, FFT_N1 // 2
    cst = _dft_constants()
    kf = _filter_spectrum(hf_t.reshape(C, R, FFT_N2).astype(BF16),
                          hb_t.reshape(C, R, FFT_N2).astype(BF16), cst)
    x_slabs = (u.reshape(2, P, R, FFT_N2, C).transpose(1, 4, 0, 2, 3)
               .reshape(P, C, FFT_N1, FFT_N2).astype(BF16))
    y = _fft_conv(x_slabs, kf, cst)
    return y.reshape(P, C, 2, R, FFT_N2).transpose(2, 0, 3, 4, 1).reshape(B, L, C)


def _dispatch_indices(route, cnt, T):
    tm = EXPERT_TILE
    A = T * TOP_K
    top_e = route[:, :TOP_K].astype(jnp.int32)
    rank = route[:, TOP_K:2 * TOP_K].astype(jnp.int32)
    counts = cnt[0, :N_EXPERTS].astype(jnp.int32)
    padded = (counts + tm - 1) // tm * tm
    pad_end = jnp.cumsum(padded)
    pad_start = pad_end - padded
    start = jnp.cumsum(counts) - counts
    pos = pad_start[top_e] + rank
    n_blocks = -(-A // tm) + N_EXPERTS
    P = n_blocks * tm
    block_first = jnp.arange(n_blocks, dtype=jnp.int32) * tm
    block_e = jnp.minimum(jnp.sum(pad_end[None, :] <= block_first[:, None], axis=1),
                          N_EXPERTS - 1).astype(jnp.int32)
    n_used = (pad_end[-1] // tm).astype(jnp.int32).reshape(1)
    tok = jnp.broadcast_to(jnp.arange(T, dtype=jnp.int32)[:, None], (T, TOP_K))
    _, sorted_tok = lax.sort_key_val(pos.reshape(A), tok.reshape(A))
    slot = jnp.arange(P, dtype=jnp.int32)
    slot_e = jnp.repeat(block_e, tm)
    r = slot - pad_start[slot_e]
    compact = jnp.clip(start[slot_e] + r, 0, A - 1)
    slot_tok = jnp.where(r < counts[slot_e], sorted_tok[compact], slot % T)
    return pos.T, slot_tok, block_e, n_used


def kernel(x, p, g_mix, w_in, hyena_conv_w, hyena_conv_b, flt_w1, flt_b1, flt_w_inner, flt_b_inner, flt_freq, flt_w_out, hyena_d, hyena_gn, lambda_q1, lambda_k1, lambda_q2, lambda_k2, attn_subln, w_out, g_ffn, router_w, router_b, w_gate, b_gate, w_up, b_up, w_down, b_down, w_ple, g_ple, w_ple_gate, b_ple_gate, g_final):
    B, S, D = x.shape
    T = B * S
    i = 0
    x2 = x.reshape(T, D)

    cos_t, s1_t, s2_t = _rope_tables(S)
    q, k, vt, hy = _in_proj(x2, g_mix[i], w_in[i].astype(BF16), cos_t, s1_t, s2_t, S)

    lam = (jnp.exp(jnp.sum(lambda_q1[i] * lambda_k1[i])) - jnp.exp(jnp.sum(lambda_q2[i] * lambda_k2[i]))
           + LAMBDA_INIT).reshape(1).astype(F32)
    attn = _diff_attention(lam, q.reshape(B, S, -1), k.reshape(B, S, -1), vt, attn_subln[i])

    u, hx0 = _short_conv(hy.reshape(B, S, -1), hyena_conv_w[i], hyena_conv_b[i])
    h_fwd, h_bwd = _hyena_filters(S, flt_w1[i], flt_b1[i], flt_w_inner[i], flt_b_inner[i],
                                  flt_freq[i], flt_w_out[i])
    yc = _long_conv(u, h_fwd, h_bwd)

    h1, hn, route, cnt = _out_proj(attn.reshape(T, -1), yc.reshape(T, -1), u.reshape(T, -1),
                                   hx0.reshape(T, -1), hyena_d[i], hyena_gn[i],
                                   w_out[i].astype(BF16), x2, g_ffn[i], router_w[i], router_b[i])

    pos, slot_tok, block_e, n_used = _dispatch_indices(route, cnt, T)
    x_sorted = hn[slot_tok]
    y = _experts(block_e, n_used, x_sorted, w_gate[i], b_gate[i], w_up[i], b_up[i],
                 w_down[i], b_down[i])
    yg = y[pos.reshape(-1)].reshape(TOP_K, T, D)

    out = _final(h1, yg, route, p[i].reshape(T, -1), w_ple[i].astype(BF16), g_ple[i],
                 w_ple_gate[i].astype(BF16), b_ple_gate[i], g_final)
    return out.reshape(B, S, D)
```

```python
import functools
import math

import jax
import jax.numpy as jnp
from jax import lax
from jax.experimental import pallas as pl
from jax.experimental.pallas import tpu as pltpu

F32 = jnp.float32
BF16 = jnp.bfloat16

D_MODEL = 1024
ATTN_WIDTH = 512
HYENA_WIDTH = 512
DIFF_HEADS = 4
DIFF_HEAD_DIM = 64
DIFF_V_DIM = 128
HYENA_GROUPS = 8
FILTER_EMB = 33
FILTER_INNER = 2
FAST_DECAY_PCT = 0.3
SLOW_DECAY_PCT = 1.5
DECAY_TARGET = 1e-2
ROPE_THETA = 10000.0
N_EXPERTS = 32
TOP_K = 4
SWIGLU_ALPHA = 1.702
SWIGLU_LIMIT = 7.0
EPS = 1e-6
SUBLN_EPS = 1e-5
LAMBDA_INIT = 0.8 - 0.6 * math.exp(-0.3 * 0)

V7X_VMEM_LIMIT_BYTES = 56 * 1024 * 1024
LANES = 128
SUBLANES = 8

TOKEN_TILE = 512
ATTN_STREAM_W = 512
ATTN_Q_TILE = 1024
ATTN_ONES_ROWS = 16
ATTN_UNROLL = 4
ATTN_KV_TILE = 512
EXPERT_TILE = 512
ROUTER_PAD = LANES


def _params(*sem):
    return pltpu.CompilerParams(dimension_semantics=sem, vmem_limit_bytes=V7X_VMEM_LIMIT_BYTES)


def _rms(x, g, eps):
    return x * lax.rsqrt(jnp.mean(x * x, axis=-1, keepdims=True) + eps) * g


def _inproj_kernel(x_ref, g_ref, w_ref, wvt_ref, c_ref, s1_ref, s2_ref, q_ref, k_ref, vt_ref, hy_ref):
    a = _rms(x_ref[...], g_ref[...], EPS).astype(BF16)
    cos, s1, s2 = c_ref[...], s1_ref[...], s2_ref[...]

    def rope(t):
        return t * cos + pltpu.roll(t, LANES - 32, axis=1) * s1 + pltpu.roll(t, 32, axis=1) * s2

    q_scale = (DIFF_HEAD_DIM ** -0.5) * math.log2(math.e)
    qk = jnp.dot(a, w_ref[:, : 2 * ATTN_WIDTH], preferred_element_type=F32)
    for j in range(ATTN_WIDTH // LANES):
        sl = slice(j * LANES, (j + 1) * LANES)
        q_ref[:, sl] = (rope(qk[:, sl]) * q_scale).astype(BF16)
        k_ref[:, sl] = rope(qk[:, ATTN_WIDTH + j * LANES: ATTN_WIDTH + (j + 1) * LANES]).astype(BF16)
    vt_ref[0] = lax.dot_general(wvt_ref[...], a, (((1,), (1,)), ((), ())),
                                preferred_element_type=F32).astype(BF16)
    hy_ref[...] = jnp.dot(a, w_ref[:, 3 * ATTN_WIDTH:], preferred_element_type=F32)


def _in_proj(x2, g_mix, w_in_bf, cos_t, s1_t, s2_t, seq):
    T, D = x2.shape
    tm = TOKEN_TILE
    nseq = seq // tm
    wvt = w_in_bf[:, 2 * ATTN_WIDTH: 3 * ATTN_WIDTH].T
    row = lambda i: (i, 0)
    const = lambda i: (0, 0)
    pos = lambda i: (i % nseq, 0)
    return pl.pallas_call(
        _inproj_kernel,
        grid=(T // tm,),
        in_specs=[
            pl.BlockSpec((tm, D), row),
            pl.BlockSpec((1, D), const),
            pl.BlockSpec(w_in_bf.shape, const),
            pl.BlockSpec(wvt.shape, const),
            pl.BlockSpec((tm, LANES), pos),
            pl.BlockSpec((tm, LANES), pos),
            pl.BlockSpec((tm, LANES), pos),
        ],
        out_specs=[
            pl.BlockSpec((tm, ATTN_WIDTH), row),
            pl.BlockSpec((tm, ATTN_WIDTH), row),
            pl.BlockSpec((1, ATTN_WIDTH, tm), lambda i: (i, 0, 0)),
            pl.BlockSpec((tm, 3 * HYENA_WIDTH), row),
        ],
        out_shape=[
            jax.ShapeDtypeStruct((T, ATTN_WIDTH), BF16),
            jax.ShapeDtypeStruct((T, ATTN_WIDTH), BF16),
            jax.ShapeDtypeStruct((T // tm, ATTN_WIDTH, tm), BF16),
            jax.ShapeDtypeStruct((T, 3 * HYENA_WIDTH), F32),
        ],
        compiler_params=_params("parallel"),
        name="in_proj",
    )(x2, g_mix.reshape(1, D), w_in_bf, wvt, cos_t, s1_t, s2_t)


def _attn_kernel(lam_ref, q_ref, k_ref, vt_ref, g_ref, o_ref, s_ref, m_ref, acc_ref, *, kc):
    w = ATTN_STREAM_W
    nc = k_ref.shape[1] // kc
    n_groups = q_ref.shape[1] // w
    streams = [(grp, c) for grp in range(n_groups) for c in range(2)]

    def stream_q(grp, c):
        q = q_ref[0, grp * w:(grp + 1) * w, :]
        lane = lax.broadcasted_iota(jnp.int32, q.shape, 1)
        keep = (lane < DIFF_HEAD_DIM) if c == 0 else (lane >= DIFF_HEAD_DIM)
        return jnp.where(keep, q, jnp.zeros_like(q))

    def fold8(t, op):
        r = t[0:SUBLANES]
        for j in range(1, kc // SUBLANES):
            r = op(r, t[j * SUBLANES:(j + 1) * SUBLANES])
        return r

    def score_chunk(i, qc, j):
        rows = pl.ds(pl.multiple_of(j * kc, kc), kc)
        s = lax.dot_general(k_ref[0, rows, :], qc, (((1,), (1,)), ((), ())),
                            preferred_element_type=F32)
        s_ref[i, rows, :] = s
        m_ref[i] = jnp.maximum(m_ref[i], fold8(s, jnp.maximum))

    ones_rows = jnp.ones((ATTN_ONES_ROWS, kc), BF16)

    def prob_chunk(i, j, m):
        rows = pl.ds(pl.multiple_of(j * kc, kc), kc)
        p = jnp.exp2(s_ref[i, rows, :] - m)
        v_aug = jnp.concatenate([vt_ref[j], ones_rows], axis=0)
        acc_ref[...] += jnp.dot(v_aug, p.astype(BF16), preferred_element_type=F32)

    m_ref[...] = jnp.full(m_ref.shape, -jnp.inf, F32)
    q0 = stream_q(*streams[0])

    def first_body(j, carry):
        score_chunk(0, q0, j)
        return carry

    lax.fori_loop(0, nc, first_body, 0, unroll=ATTN_UNROLL)
    outs = []
    for i, (grp, c) in enumerate(streams):
        m = jnp.max(m_ref[i], axis=0, keepdims=True)
        acc_ref[...] = jnp.zeros_like(acc_ref)
        if i + 1 < len(streams):
            qn = stream_q(*streams[i + 1])

            def body(j, carry, i=i, m=m, qn=qn):
                score_chunk(i + 1, qn, j)
                prob_chunk(i, j, m)
                return carry
        else:
            def body(j, carry, i=i, m=m):
                prob_chunk(i, j, m)
                return carry

        lax.fori_loop(0, nc, body, 0, unroll=ATTN_UNROLL)
        outs.append(acc_ref[:DIFF_V_DIM, :] * (1.0 / acc_ref[DIFF_V_DIM:DIFF_V_DIM + 1, :]))
        if c == 1:
            o = outs[-2] - lam_ref[0] * outs[-1]
            o = o * lax.rsqrt(jnp.mean(o * o, axis=0, keepdims=True) + SUBLN_EPS)
            o = o * (g_ref[...] * (1.0 - LAMBDA_INIT))
            o_ref[0, grp * w:(grp + 1) * w, :] = o.T.astype(o_ref.dtype)


def _diff_attention(lam, q, k, vt, g_subln):
    B, S, _ = q.shape
    tq, kc = ATTN_Q_TILE, ATTN_KV_TILE
    nc = S // kc
    n_streams = 2 * (tq // ATTN_STREAM_W)
    return pl.pallas_call(
        functools.partial(_attn_kernel, kc=kc),
        grid=(B, DIFF_HEADS, S // tq),
        in_specs=[
            pl.BlockSpec(memory_space=pltpu.SMEM),
            pl.BlockSpec((1, tq, LANES), lambda b, h, i: (b, i, h)),
            pl.BlockSpec((1, S, LANES), lambda b, h, i: (b, 0, h)),
            pl.BlockSpec((nc, DIFF_V_DIM, kc), lambda b, h, i: (b, h, 0)),
            pl.BlockSpec((DIFF_V_DIM, 1), lambda b, h, i: (0, 0)),
        ],
        out_specs=pl.BlockSpec((1, tq, LANES), lambda b, h, i: (b, i, h)),
        out_shape=jax.ShapeDtypeStruct((B, S, ATTN_WIDTH), BF16),
        scratch_shapes=[pltpu.VMEM((n_streams, S, ATTN_STREAM_W), F32),
                        pltpu.VMEM((n_streams, SUBLANES, ATTN_STREAM_W), F32),
                        pltpu.VMEM((DIFF_V_DIM + ATTN_ONES_ROWS, ATTN_STREAM_W), F32)],
        compiler_params=_params("parallel", "parallel", "parallel"),
        name="diff_attn",
    )(lam, q, k, vt, g_subln.reshape(DIFF_V_DIM, 1))


def _shortconv_kernel(hy_ref, prev_ref, next_ref, w_ref, b_ref, u_ref, x0_ref):
    i = pl.program_id(1)
    last = pl.num_programs(1) - 1
    x = hy_ref[0]
    ts = x.shape[0]
    prev_row = jnp.where(i == 0, 0.0, prev_ref[0, SUBLANES - 1:SUBLANES, :])
    next_row = jnp.where(i == last, 0.0, next_ref[0, 0:1, :])
    row = lax.broadcasted_iota(jnp.int32, (ts, 1), 0)
    xm = jnp.where(row == 0, prev_row, pltpu.roll(x, 1, axis=0))
    xp = jnp.where(row == ts - 1, next_row, pltpu.roll(x, ts - 1, axis=0))
    y = b_ref[...] + xm * w_ref[0:1, :] + x * w_ref[1:2, :] + xp * w_ref[2:3, :]
    C = HYENA_WIDTH
    u_ref[0] = y[:, :C] * y[:, 2 * C:]
    x0_ref[0] = y[:, C:2 * C]


def _short_conv(hy, conv_w, conv_b):
    B, S, C3 = hy.shape
    ts = TOKEN_TILE
    nb = ts // SUBLANES
    return pl.pallas_call(
        _shortconv_kernel,
        grid=(B, S // ts),
        in_specs=[
            pl.BlockSpec((1, ts, C3), lambda b, i: (b, i, 0)),
            pl.BlockSpec((1, SUBLANES, C3), lambda b, i: (b, jnp.maximum(i * nb - 1, 0), 0)),
            pl.BlockSpec((1, SUBLANES, C3), lambda b, i: (b, jnp.minimum((i + 1) * nb, S // SUBLANES - 1), 0)),
            pl.BlockSpec((3, C3), lambda b, i: (0, 0)),
            pl.BlockSpec((1, C3), lambda b, i: (0, 0)),
        ],
        out_specs=[
            pl.BlockSpec((1, ts, HYENA_WIDTH), lambda b, i: (b, i, 0)),
            pl.BlockSpec((1, ts, HYENA_WIDTH), lambda b, i: (b, i, 0)),
        ],
        out_shape=[
            jax.ShapeDtypeStruct((B, S, HYENA_WIDTH), F32),
            jax.ShapeDtypeStruct((B, S, HYENA_WIDTH), F32),
        ],
        compiler_params=_params("parallel", "parallel"),
        name="short_conv",
    )(hy, hy, hy, conv_w, conv_b.reshape(1, C3))


FFT_N = 8192
FFT_N1 = 64
FFT_N2 = 128
FFT_CH_BLOCK = 32
FFT_UNROLL = 4


def _dft_constants():
    import numpy as np
    n1, n2, n = FFT_N1, FFT_N2, FFT_N
    k1 = np.arange(n1)[:, None]
    t1 = np.arange(n1)[None, :]
    f1 = np.exp(-2j * np.pi * k1 * t1 / n1)
    f1h = f1[:, : n1 // 2]
    w1c = np.block([[f1h.real, -f1h.imag], [f1h.imag, f1h.real]])
    w1r = np.concatenate([f1h.real, f1h.imag], axis=0)
    t2 = np.arange(n2)[:, None]
    k2 = np.arange(n2)[None, :]
    f2 = np.exp(-2j * np.pi * t2 * k2 / n2)
    w2a = np.concatenate([f2.real, f2.imag], axis=1)
    w2b = np.concatenate([-f2.imag, f2.real], axis=1)
    g2 = np.conj(f2)
    w3a = np.concatenate([g2.real, g2.imag], axis=1)
    w3b = np.concatenate([-g2.imag, g2.real], axis=1)
    h = np.conj(f1).T[: n1 // 2] / n
    w4 = np.block([[h.real, -h.imag], [h.imag, h.real]])
    tw = np.exp(-2j * np.pi * np.arange(n1)[:, None] * np.arange(n2)[None, :] / n)
    bf = lambda a: jnp.asarray(a, dtype=F32).astype(BF16)
    return dict(w1c=bf(w1c), w1r=bf(w1r), w2a=bf(w2a), w2b=bf(w2b), w3a=bf(w3a), w3b=bf(w3b),
                w4=bf(w4), tc=jnp.asarray(tw.real, F32), ts=jnp.asarray(tw.imag, F32))


def _dft_forward(x_ref, w1_ref, tc_ref, ts_ref, w2a_ref, w2b_ref, ar_ref, ai_ref):
    cb = x_ref.shape[0]
    n1 = FFT_N1
    w1 = w1_ref[...]
    tc, ts = tc_ref[...], ts_ref[...]

    def body(c, carry):
        a = jnp.dot(w1, x_ref[c], preferred_element_type=F32)
        ar, ai = a[:n1], a[n1:]
        rows = pl.ds(pl.multiple_of(c * n1, n1), n1)
        ar_ref[rows, :] = (ar * tc - ai * ts).astype(BF16)
        ai_ref[rows, :] = (ar * ts + ai * tc).astype(BF16)
        return carry

    lax.fori_loop(0, cb, body, 0, unroll=FFT_UNROLL)
    return (jnp.dot(ar_ref[...], w2a_ref[...], preferred_element_type=F32)
            + jnp.dot(ai_ref[...], w2b_ref[...], preferred_element_type=F32))


def _spectrum_kernel(hf_ref, hb_ref, w1_ref, tc_ref, ts_ref, w2a_ref, w2b_ref, o_ref, ar_ref, ai_ref):
    n2 = FFT_N2
    f = _dft_forward(hf_ref, w1_ref, tc_ref, ts_ref, w2a_ref, w2b_ref, ar_ref, ai_ref)
    o_ref[...] = f.reshape(o_ref.shape)
    b = _dft_forward(hb_ref, w1_ref, tc_ref, ts_ref, w2a_ref, w2b_ref, ar_ref, ai_ref)
    b = b.reshape(o_ref.shape)
    o_ref[:, :, :n2] = o_ref[:, :, :n2] + b[:, :, :n2]
    o_ref[:, :, n2:] = o_ref[:, :, n2:] - b[:, :, n2:]


def _fftconv_kernel(x_ref, kf_ref, w1_ref, tc_ref, ts_ref, w2a_ref, w2b_ref, w3a_ref, w3b_ref,
                    w4_ref, y_ref, ar_ref, ai_ref, c_ref):
    cb = x_ref.shape[0]
    n1, n2 = FFT_N1, FFT_N2
    b = _dft_forward(x_ref, w1_ref, tc_ref, ts_ref, w2a_ref, w2b_ref, ar_ref, ai_ref)
    kf = kf_ref[...].reshape(cb * n1, 2 * n2)
    br, bi = b[:, :n2], b[:, n2:]
    kr, ki = kf[:, :n2], kf[:, n2:]
    ar_ref[...] = (br * kr - bi * ki).astype(BF16)
    ai_ref[...] = (br * ki + bi * kr).astype(BF16)
    c_ref[...] = (jnp.dot(ar_ref[...], w3a_ref[...], preferred_element_type=F32)
                  + jnp.dot(ai_ref[...], w3b_ref[...], preferred_element_type=F32))
    w4 = w4_ref[...]
    tc, ts = tc_ref[...], ts_ref[...]

    def body(c, carry):
        rows = pl.ds(pl.multiple_of(c * n1, n1), n1)
        cr, ci = c_ref[rows, :n2], c_ref[rows, n2:]
        dr = (cr * tc + ci * ts).astype(BF16)
        di = (ci * tc - cr * ts).astype(BF16)
        y_ref[c] = (jnp.dot(w4[:, :n1], dr, preferred_element_type=F32)
                    + jnp.dot(w4[:, n1:], di, preferred_element_type=F32))
        return carry

    lax.fori_loop(0, cb, body, 0, unroll=FFT_UNROLL)


def _const_spec(a):
    nd = a.ndim
    return pl.BlockSpec(a.shape, lambda *_: (0,) * nd)


def _filter_spectrum(hf_slabs, hb_slabs, cst):
    C = hf_slabs.shape[0]
    cb = FFT_CH_BLOCK
    consts = [cst["w1r"], cst["tc"], cst["ts"], cst["w2a"], cst["w2b"]]
    slab = pl.BlockSpec((cb, FFT_N1 // 2, FFT_N2), lambda i: (i, 0, 0))
    return pl.pallas_call(
        _spectrum_kernel,
        grid=(C // cb,),
        in_specs=[slab, slab] + [_const_spec(a) for a in consts],
        out_specs=pl.BlockSpec((cb, FFT_N1, 2 * FFT_N2), lambda i: (i, 0, 0)),
        out_shape=jax.ShapeDtypeStruct((C, FFT_N1, 2 * FFT_N2), F32),
        scratch_shapes=[pltpu.VMEM((cb * FFT_N1, FFT_N2), BF16), pltpu.VMEM((cb * FFT_N1, FFT_N2), BF16)],
        compiler_params=_params("parallel"),
        name="filter_spectrum",
    )(hf_slabs, hb_slabs, *consts)


FILTER_TIME_TILE = 512


def _filter_kernel(zt_ref, w1t_ref, b1_ref, wit_ref, bi_ref, fr_ref, wot_ref, dec_ref, hf_ref, hb_ref):
    hp = lax.Precision.HIGHEST
    h = jnp.sin(fr_ref[0] * (jnp.dot(w1t_ref[...], zt_ref[...], precision=hp,
                                     preferred_element_type=F32) + b1_ref[...]))
    for j in range(FILTER_INNER):
        h = jnp.sin(fr_ref[j + 1] * (jnp.dot(wit_ref[j], h, precision=hp,
                                             preferred_element_type=F32) + bi_ref[j]))
    o = jnp.dot(wot_ref[...], h, precision=hp, preferred_element_type=F32)
    dec = dec_ref[...]
    hf_ref[...] = o[:HYENA_WIDTH] * dec
    hb_ref[...] = o[HYENA_WIDTH:] * dec


def _hyena_filters(seq, w1, b1, w_inner, b_inner, freq, w_out):
    C = HYENA_WIDTH
    order = w1.shape[1]
    pos = jnp.arange(seq, dtype=F32)
    t = pos / (seq - 1)
    bands = (FILTER_EMB - 1) // 2
    f = jnp.linspace(1e-4, bands - 1, bands, dtype=F32)
    fw = ((2.0 * math.pi / seq) * pos)[:, None] * f[None, :]
    z = jnp.concatenate([t[:, None], jnp.cos(fw), -jnp.sin(fw)], axis=-1)
    zt = jnp.zeros((LANES, seq), F32).at[:FILTER_EMB].set(z.T)
    w1t = jnp.zeros((order, LANES), F32).at[:, :FILTER_EMB].set(w1.T)
    max_decay = math.log(DECAY_TARGET) / FAST_DECAY_PCT
    min_decay = math.log(DECAY_TARGET) / SLOW_DECAY_PCT
    deltas = jnp.abs(jnp.linspace(min_decay, max_decay, C, dtype=F32))
    dec_t = jnp.exp(-deltas[:, None] * t[None, :])
    tt = FILTER_TIME_TILE
    lane_blk = lambda r: pl.BlockSpec((r, tt), lambda i: (0, i))
    args = [zt, w1t, b1.reshape(order, 1), jnp.swapaxes(w_inner, 1, 2),
            b_inner.reshape(FILTER_INNER, order, 1), freq.reshape(FILTER_INNER + 1, order, 1),
            w_out.T, dec_t]
    return pl.pallas_call(
        _filter_kernel,
        grid=(seq // tt,),
        in_specs=[lane_blk(LANES)] + [_const_spec(a) for a in args[1:7]] + [lane_blk(C)],
        out_specs=[lane_blk(C), lane_blk(C)],
        out_shape=[jax.ShapeDtypeStruct((C, seq), F32), jax.ShapeDtypeStruct((C, seq), F32)],
        compiler_params=_params("parallel"),
        name="hyena_filters",
    )(*args)


def _fft_conv(x_slabs, kf, cst):
    P, C = x_slabs.shape[:2]
    cb = FFT_CH_BLOCK
    consts = [cst["w1c"], cst["tc"], cst["ts"], cst["w2a"], cst["w2b"], cst["w3a"], cst["w3b"], cst["w4"]]
    return pl.pallas_call(
        _fftconv_kernel,
        grid=(C // cb, P),
        in_specs=[pl.BlockSpec((None, cb, FFT_N1, FFT_N2), lambda i, p: (p, i, 0, 0)),
                  pl.BlockSpec((cb, FFT_N1, 2 * FFT_N2), lambda i, p: (i, 0, 0))]
                 + [_const_spec(a) for a in consts],
        out_specs=pl.BlockSpec((None, cb, FFT_N1, FFT_N2), lambda i, p: (p, i, 0, 0)),
        out_shape=jax.ShapeDtypeStruct((P, C, FFT_N1, FFT_N2), F32),
        scratch_shapes=[pltpu.VMEM((cb * FFT_N1, FFT_N2), BF16), pltpu.VMEM((cb * FFT_N1, FFT_N2), BF16),
                        pltpu.VMEM((cb * FFT_N1, 2 * FFT_N2), F32)],
        compiler_params=_params("parallel", "arbitrary"),
        name="fft_conv",
    )(x_slabs, kf, *consts)


def _split_bf16(x):
    hi = x.astype(BF16)
    lo = (x - hi.astype(F32)).astype(BF16)
    return hi, lo


def _outproj_kernel(attn_ref, yc_ref, u_ref, x0_ref, d_ref, gn_ref, grp_ref, wo_ref, x_ref,
                    gf_ref, rwh_ref, rwl_ref, rb_ref, tri_ref, h_ref, hn_ref, rt_ref, cnt_out_ref,
                    cnt_ref):
    z = (yc_ref[...] + u_ref[...] * d_ref[...]) * x0_ref[...]
    zh, zl = _split_bf16(z * z)
    grp = grp_ref[...]
    ssq = (jnp.dot(zh, grp, preferred_element_type=F32) + jnp.dot(zl, grp, preferred_element_type=F32))
    gsz = HYENA_WIDTH // HYENA_GROUPS
    hy_out = (z * lax.rsqrt(ssq * (1.0 / gsz) + EPS) * gn_ref[...]).astype(BF16)
    mix = (jnp.dot(attn_ref[...], wo_ref[:ATTN_WIDTH, :], preferred_element_type=F32)
           + jnp.dot(hy_out, wo_ref[ATTN_WIDTH:, :], preferred_element_type=F32))
    h = x_ref[...] + mix
    h_ref[...] = h
    hn = _rms(h, gf_ref[...], EPS)
    hn_ref[...] = hn.astype(BF16)
    nh, nl = _split_bf16(hn)
    logits = (jnp.dot(nh, rwh_ref[...], preferred_element_type=F32)
              + jnp.dot(nl, rwh_ref[...], preferred_element_type=F32)
              + jnp.dot(nh, rwl_ref[...], preferred_element_type=F32)) + rb_ref[...]

    @pl.when(pl.program_id(0) == 0)
    def _():
        cnt_ref[...] = jnp.zeros_like(cnt_ref)

    lane = lax.broadcasted_iota(jnp.int32, logits.shape, 1)
    work = logits
    top_val, top_hot = [], []
    for _ in range(TOP_K):
        m = jnp.max(work, axis=-1, keepdims=True)
        idx = jnp.min(jnp.where(work == m, lane, ROUTER_PAD), axis=-1, keepdims=True)
        hot = lane == idx
        top_val.append(m)
        top_hot.append(hot)
        work = jnp.where(hot, -jnp.inf, work)
    ex = [jnp.exp(v - top_val[0]) for v in top_val]
    inv_den = 1.0 / (ex[0] + ex[1] + ex[2] + ex[3])
    sel = jnp.zeros(logits.shape, F32)
    for hot in top_hot:
        sel = sel + hot.astype(F32)
    before = jnp.dot(tri_ref[...], sel.astype(BF16), preferred_element_type=F32) + cnt_ref[0:1, :]
    lane_f = lane.astype(F32)
    packed = jnp.zeros(logits.shape, F32)
    for r, hot in enumerate(top_hot):
        e_r = jnp.sum(jnp.where(hot, lane_f, 0.0), axis=-1, keepdims=True)
        rank_r = jnp.sum(jnp.where(hot, before, 0.0), axis=-1, keepdims=True)
        packed = jnp.where(lane == r, e_r, packed)
        packed = jnp.where(lane == TOP_K + r, rank_r, packed)
        packed = jnp.where(lane == 2 * TOP_K + r, ex[r] * inv_den, packed)
    rt_ref[...] = packed
    cnt_ref[...] = cnt_ref[...] + jnp.sum(sel, axis=0, keepdims=True)
    cnt_out_ref[...] = cnt_ref[...]


def _out_proj(attn2, yc2, u2, x02, hyena_d, hyena_gn, w_out_bf, x2, g_ffn, router_w, router_b):
    T, D = x2.shape
    tm = TOKEN_TILE
    C = HYENA_WIDTH
    gid = jnp.arange(C) // (C // HYENA_GROUPS)
    grp = (gid[:, None] == gid[None, :]).astype(BF16)
    rw = jnp.zeros((D, ROUTER_PAD), F32).at[:, :N_EXPERTS].set(router_w)
    rwh, rwl = _split_bf16(rw)
    rb = jnp.full((1, ROUTER_PAD), -jnp.inf, F32).at[0, :N_EXPERTS].set(router_b)
    tri = (jnp.arange(tm)[:, None] > jnp.arange(tm)[None, :]).astype(BF16)
    row = lambda i: (i, 0)
    const = lambda i: (0, 0)
    return pl.pallas_call(
        _outproj_kernel,
        grid=(T // tm,),
        in_specs=[
            pl.BlockSpec((tm, ATTN_WIDTH), row),
            pl.BlockSpec((tm, C), row),
            pl.BlockSpec((tm, C), row),
            pl.BlockSpec((tm, C), row),
            pl.BlockSpec((1, C), const),
            pl.BlockSpec((1, C), const),
            pl.BlockSpec((C, C), const),
            pl.BlockSpec((D, D), const),
            pl.BlockSpec((tm, D), row),
            pl.BlockSpec((1, D), const),
            pl.BlockSpec((D, ROUTER_PAD), const),
            pl.BlockSpec((D, ROUTER_PAD), const),
            pl.BlockSpec((1, ROUTER_PAD), const),
            pl.BlockSpec((tm, tm), const),
        ],
        out_specs=[
            pl.BlockSpec((tm, D), row),
            pl.BlockSpec((tm, D), row),
            pl.BlockSpec((tm, ROUTER_PAD), row),
            pl.BlockSpec((SUBLANES, ROUTER_PAD), const),
        ],
        out_shape=[
            jax.ShapeDtypeStruct((T, D), F32),
            jax.ShapeDtypeStruct((T, D), BF16),
            jax.ShapeDtypeStruct((T, ROUTER_PAD), F32),
            jax.ShapeDtypeStruct((SUBLANES, ROUTER_PAD), F32),
        ],
        scratch_shapes=[pltpu.VMEM((SUBLANES, ROUTER_PAD), F32)],
        compiler_params=_params("arbitrary"),
        name="out_proj",
    )(attn2, yc2, u2, x02, hyena_d.reshape(1, C), hyena_gn.reshape(1, C), grp, w_out_bf, x2,
      g_ffn.reshape(1, D), rwh, rwl, rb, tri)


def _expert_kernel(be_ref, nused_ref, x_ref, wg_ref, bg_ref, wu_ref, bu_ref, wd_ref, bd_ref, y_ref,
                   wg_bf, wu_bf, wd_bf):
    i = pl.program_id(0)
    used = i < nused_ref[0]
    new_expert = jnp.logical_or(i == 0, be_ref[i] != be_ref[jnp.maximum(i - 1, 0)])

    @pl.when(jnp.logical_and(used, new_expert))
    def _():
        wg_bf[...] = wg_ref[0].astype(BF16)
        wu_bf[...] = wu_ref[0].astype(BF16)
        wd_bf[...] = wd_ref[0].astype(BF16)

    @pl.when(used)
    def _():
        x = x_ref[...]
        g = jnp.minimum(jnp.dot(x, wg_bf[...], preferred_element_type=F32) + bg_ref[0], SWIGLU_LIMIT)
        u = jnp.clip(jnp.dot(x, wu_bf[...], preferred_element_type=F32) + bu_ref[0],
                     -SWIGLU_LIMIT, SWIGLU_LIMIT)
        a = (u + 1.0) * (g * jax.nn.sigmoid(SWIGLU_ALPHA * g))
        y = jnp.dot(a.astype(BF16), wd_bf[...], preferred_element_type=F32) + bd_ref[0]
        y_ref[...] = y.astype(y_ref.dtype)

    @pl.when(jnp.logical_not(used))
    def _():
        y_ref[...] = jnp.zeros_like(y_ref)


def _experts(block_e, n_used, x_sorted, wg, bg, wu, bu, wd, bd):
    P, D = x_sorted.shape
    tm = EXPERT_TILE
    E, _, FF = wg.shape
    wmap = lambda i, be, nu: (be[i], 0, 0)
    grid_spec = pltpu.PrefetchScalarGridSpec(
        num_scalar_prefetch=2,
        grid=(P // tm,),
        in_specs=[
            pl.BlockSpec((tm, D), lambda i, be, nu: (i, 0)),
            pl.BlockSpec((1, D, FF), wmap),
            pl.BlockSpec((1, 1, FF), wmap),
            pl.BlockSpec((1, D, FF), wmap),
            pl.BlockSpec((1, 1, FF), wmap),
            pl.BlockSpec((1, FF, D), wmap),
            pl.BlockSpec((1, 1, D), wmap),
        ],
        out_specs=pl.BlockSpec((tm, D), lambda i, be, nu: (i, 0)),
        scratch_shapes=[pltpu.VMEM((D, FF), BF16), pltpu.VMEM((D, FF), BF16), pltpu.VMEM((FF, D), BF16)],
    )
    return pl.pallas_call(
        _expert_kernel,
        grid_spec=grid_spec,
        out_shape=jax.ShapeDtypeStruct((P, D), BF16),
        compiler_params=_params("arbitrary"),
        name="moe_experts",
    )(block_e, n_used, x_sorted, wg, bg.reshape(E, 1, FF), wu, bu.reshape(E, 1, FF),
      wd, bd.reshape(E, 1, D))


def _final_kernel(h_ref, yg_ref, rt_ref, p_ref, wp_ref, gp_ref, wg_ref, bg_ref, gfin_ref, o_ref):
    h = h_ref[...]
    for r in range(TOP_K):
        h = h + yg_ref[r].astype(F32) * rt_ref[:, 2 * TOP_K + r: 2 * TOP_K + r + 1]
    e = _rms(jnp.dot(p_ref[...].astype(BF16), wp_ref[...], preferred_element_type=F32),
             gp_ref[...], EPS)
    gate = jax.nn.sigmoid(jnp.dot(h.astype(BF16), wg_ref[...], preferred_element_type=F32)
                          + bg_ref[...])
    h = h + gate * e
    o_ref[...] = _rms(h, gfin_ref[...], EPS)


def _final(h1, yg, route, p2, w_ple_bf, g_ple, w_gate_bf, b_gate, g_final):
    T, D = h1.shape
    tm = TOKEN_TILE
    PD = p2.shape[1]
    row = lambda i: (i, 0)
    const = lambda i: (0, 0)
    return pl.pallas_call(
        _final_kernel,
        grid=(T // tm,),
        in_specs=[
            pl.BlockSpec((tm, D), row),
            pl.BlockSpec((TOP_K, tm, D), lambda i: (0, i, 0)),
            pl.BlockSpec((tm, ROUTER_PAD), row),
            pl.BlockSpec((tm, PD), row),
            pl.BlockSpec((PD, D), const),
            pl.BlockSpec((1, D), const),
            pl.BlockSpec((D, D), const),
            pl.BlockSpec((1, D), const),
            pl.BlockSpec((1, D), const),
        ],
        out_specs=pl.BlockSpec((tm, D), row),
        out_shape=jax.ShapeDtypeStruct((T, D), F32),
        compiler_params=_params("parallel"),
        name="final",
    )(h1, yg, route, p2, w_ple_bf, g_ple.reshape(1, D), w_gate_bf, b_gate.reshape(1, D),
      g_final.reshape(1, D))


def _rope_tables(seq):
    d = DIFF_HEAD_DIM
    pos = jnp.arange(seq, dtype=F32)
    inv = ROPE_THETA ** (-jnp.arange(0, d, 2, dtype=F32) / d)
    ang = pos[:, None] * inv[None, :]
    cos, sin = jnp.cos(ang), jnp.sin(ang)
    z = jnp.zeros_like(sin)
    cos_t = jnp.tile(jnp.concatenate([cos, cos], -1), (1, LANES // d))
    s1_t = jnp.tile(jnp.concatenate([-sin, z], -1), (1, LANES // d))
    s2_t = jnp.tile(jnp.concatenate([z, sin], -1), (1, LANES // d))
    return cos_t, s1_t, s2_t


def _long_conv(u, hf_t, hb_t):
    B, L, C = u.shape
    assert 2 * L == FFT_N and B % 2 == 0
    P, R = B // 2, FFT_N1 // 2
    cst = _dft_constants()
    kf = _filter_spectrum(hf_t.reshape(C, R, FFT_N2).astype(BF16),
                          hb_t.reshape(C, R, FFT_N2).astype(BF16), cst)
    x_slabs = (u.reshape(2, P, R, FFT_N2, C).transpose(1, 4, 0, 2, 3)
               .reshape(P, C, FFT_N1, FFT_N2).astype(BF16))
    y = _fft_conv(x_slabs, kf, cst)
    return y.reshape(P, C, 2, R, FFT_N2).transpose(2, 0, 3, 4, 1).reshape(B, L, C)


def _dispatch_indices(route, cnt, T):
    tm = EXPERT_TILE
    A = T * TOP_K
    top_e = route[:, :TOP_K].astype(jnp.int32)
    rank = route[:, TOP_K:2 * TOP_K].astype(jnp.int32)
    counts = cnt[0, :N_EXPERTS].astype(jnp.int32)
    padded = (counts + tm - 1) // tm * tm
    pad_end = jnp.cumsum(padded)
    pad_start = pad_end - padded
    start = jnp.cumsum(counts) - counts
    pos = pad_start[top_e] + rank
    n_blocks = -(-A // tm) + N_EXPERTS
    P = n_blocks * tm
    block_first = jnp.arange(n_blocks, dtype=jnp.int32) * tm
    block_e = jnp.minimum(jnp.sum(pad_end[None, :] <= block_first[:, None], axis=1),
                          N_EXPERTS - 1).astype(jnp.int32)
    n_used = (pad_end[-1] // tm).astype(jnp.int32).reshape(1)
    tok = jnp.broadcast_to(jnp.arange(T, dtype=jnp.int32)[:, None], (T, TOP_K))
    _, sorted_tok = lax.sort_key_val(pos.reshape(A), tok.reshape(A))
    slot = jnp.arange(P, dtype=jnp.int32)
    slot_e = jnp.repeat(block_e, tm)
    r = slot - pad_start[slot_e]
    compact = jnp.clip(start[slot_e] + r, 0, A - 1)
    slot_tok = jnp.where(r < counts[slot_e], sorted_tok[compact], slot % T)
    return pos.T, slot_tok, block_e, n_used


def kernel(x, p, g_mix, w_in, hyena_conv_w, hyena_conv_b, flt_w1, flt_b1, flt_w_inner, flt_b_inner, flt_freq, flt_w_out, hyena_d, hyena_gn, lambda_q1, lambda_k1, lambda_q2, lambda_k2, attn_subln, w_out, g_ffn, router_w, router_b, w_gate, b_gate, w_up, b_up, w_down, b_down, w_ple, g_ple, w_ple_gate, b_ple_gate, g_final):
    B, S, D = x.shape
    T = B * S
    i = 0
    x2 = x.reshape(T, D)

    cos_t, s1_t, s2_t = _rope_tables(S)
    q, k, vt, hy = _in_proj(x2, g_mix[i], w_in[i].astype(BF16), cos_t, s1_t, s2_t, S)

    lam = (jnp.exp(jnp.sum(lambda_q1[i] * lambda_k1[i])) - jnp.exp(jnp.sum(lambda_q2[i] * lambda_k2[i]))
           + LAMBDA_INIT).reshape(1).astype(F32)
    attn = _diff_attention(lam, q.reshape(B, S, -1), k.reshape(B, S, -1), vt, attn_subln[i])

    u, hx0 = _short_conv(hy.reshape(B, S, -1), hyena_conv_w[i], hyena_conv_b[i])
    h_fwd, h_bwd = _hyena_filters(S, flt_w1[i], flt_b1[i], flt_w_inner[i], flt_b_inner[i],
                                  flt_freq[i], flt_w_out[i])
    yc = _long_conv(u, h_fwd, h_bwd)

    h1, hn, route, cnt = _out_proj(attn.reshape(T, -1), yc.reshape(T, -1), u.reshape(T, -1),
                                   hx0.reshape(T, -1), hyena_d[i], hyena_gn[i],
                                   w_out[i].astype(BF16), x2, g_ffn[i], router_w[i], router_b[i])

    pos, slot_tok, block_e, n_used = _dispatch_indices(route, cnt, T)
    x_sorted = hn[slot_tok]
    y = _experts(block_e, n_used, x_sorted, w_gate[i], b_gate[i], w_up[i], b_up[i],
                 w_down[i], b_down[i])
    yg = y[pos.reshape(-1)].reshape(TOP_K, T, D)

    out = _final(h1, yg, route, p[i].reshape(T, -1), w_ple[i].astype(BF16), g_ple[i],
                 w_ple_gate[i].astype(BF16), b_ple_gate[i], g_final)
    return out.reshape(B, S, D)
```

```python
import functools
import math

import jax
import jax.numpy as jnp
from jax import lax
from jax.experimental import pallas as pl
from jax.experimental.pallas import tpu as pltpu

F32 = jnp.float32
BF16 = jnp.bfloat16

D_MODEL = 1024
ATTN_WIDTH = 512
HYENA_WIDTH = 512
DIFF_HEADS = 4
DIFF_HEAD_DIM = 64
DIFF_V_DIM = 128
HYENA_GROUPS = 8
FILTER_EMB = 33
FILTER_INNER = 2
FAST_DECAY_PCT = 0.3
SLOW_DECAY_PCT = 1.5
DECAY_TARGET = 1e-2
ROPE_THETA = 10000.0
N_EXPERTS = 32
TOP_K = 4
SWIGLU_ALPHA = 1.702
SWIGLU_LIMIT = 7.0
EPS = 1e-6
SUBLN_EPS = 1e-5
LAMBDA_INIT = 0.8 - 0.6 * math.exp(-0.3 * 0)

V7X_VMEM_LIMIT_BYTES = 56 * 1024 * 1024
LANES = 128
SUBLANES = 8

TOKEN_TILE = 512
ATTN_STREAM_W = 512
ATTN_Q_TILE = 1024
ATTN_ONES_ROWS = 16
ATTN_UNROLL = 4
ATTN_KV_TILE = 512
EXPERT_TILE = 512
ROUTER_PAD = LANES
MOE_OVERLAP_PARTS = 2


def _params(*sem):
    return pltpu.CompilerParams(dimension_semantics=sem, vmem_limit_bytes=V7X_VMEM_LIMIT_BYTES)


def _rms(x, g, eps):
    return x * lax.rsqrt(jnp.mean(x * x, axis=-1, keepdims=True) + eps) * g


def _inproj_kernel(x_ref, g_ref, w_ref, wvt_ref, c_ref, s1_ref, s2_ref, q_ref, k_ref, vt_ref, hy_ref):
    a = _rms(x_ref[...], g_ref[...], EPS).astype(BF16)
    cos, s1, s2 = c_ref[...], s1_ref[...], s2_ref[...]

    def rope(t):
        return t * cos + pltpu.roll(t, LANES - 32, axis=1) * s1 + pltpu.roll(t, 32, axis=1) * s2

    q_scale = (DIFF_HEAD_DIM ** -0.5) * math.log2(math.e)
    qk = jnp.dot(a, w_ref[:, : 2 * ATTN_WIDTH], preferred_element_type=F32)
    for j in range(ATTN_WIDTH // LANES):
        sl = slice(j * LANES, (j + 1) * LANES)
        q_ref[:, sl] = (rope(qk[:, sl]) * q_scale).astype(BF16)
        k_ref[:, sl] = rope(qk[:, ATTN_WIDTH + j * LANES: ATTN_WIDTH + (j + 1) * LANES]).astype(BF16)
    vt_ref[0] = lax.dot_general(wvt_ref[...], a, (((1,), (1,)), ((), ())),
                                preferred_element_type=F32).astype(BF16)
    hy_ref[...] = jnp.dot(a, w_ref[:, 3 * ATTN_WIDTH:], preferred_element_type=F32).astype(hy_ref.dtype)


def _in_proj(x2, g_mix, w_in_bf, cos_t, s1_t, s2_t, seq):
    T, D = x2.shape
    tm = TOKEN_TILE
    nseq = seq // tm
    wvt = w_in_bf[:, 2 * ATTN_WIDTH: 3 * ATTN_WIDTH].T
    row = lambda i: (i, 0)
    const = lambda i: (0, 0)
    pos = lambda i: (i % nseq, 0)
    return pl.pallas_call(
        _inproj_kernel,
        grid=(T // tm,),
        in_specs=[
            pl.BlockSpec((tm, D), row),
            pl.BlockSpec((1, D), const),
            pl.BlockSpec(w_in_bf.shape, const),
            pl.BlockSpec(wvt.shape, const),
            pl.BlockSpec((tm, LANES), pos),
            pl.BlockSpec((tm, LANES), pos),
            pl.BlockSpec((tm, LANES), pos),
        ],
        out_specs=[
            pl.BlockSpec((tm, ATTN_WIDTH), row),
            pl.BlockSpec((tm, ATTN_WIDTH), row),
            pl.BlockSpec((1, ATTN_WIDTH, tm), lambda i: (i, 0, 0)),
            pl.BlockSpec((tm, 3 * HYENA_WIDTH), row),
        ],
        out_shape=[
            jax.ShapeDtypeStruct((T, ATTN_WIDTH), BF16),
            jax.ShapeDtypeStruct((T, ATTN_WIDTH), BF16),
            jax.ShapeDtypeStruct((T // tm, ATTN_WIDTH, tm), BF16),
            jax.ShapeDtypeStruct((T, 3 * HYENA_WIDTH), BF16),
        ],
        compiler_params=_params("parallel"),
        name="in_proj",
    )(x2, g_mix.reshape(1, D), w_in_bf, wvt, cos_t, s1_t, s2_t)


def _attn_kernel(lam_ref, q_ref, k_ref, vt_ref, g_ref, o_ref, s_ref, m_ref, acc_ref, *, kc):
    w = ATTN_STREAM_W
    nc = k_ref.shape[1] // kc
    n_groups = q_ref.shape[1] // w
    streams = [(grp, c) for grp in range(n_groups) for c in range(2)]

    def stream_q(grp, c):
        q = q_ref[0, grp * w:(grp + 1) * w, :]
        lane = lax.broadcasted_iota(jnp.int32, q.shape, 1)
        keep = (lane < DIFF_HEAD_DIM) if c == 0 else (lane >= DIFF_HEAD_DIM)
        return jnp.where(keep, q, jnp.zeros_like(q))

    def fold8(t, op):
        r = t[0:SUBLANES]
        for j in range(1, kc // SUBLANES):
            r = op(r, t[j * SUBLANES:(j + 1) * SUBLANES])
        return r

    def score_chunk(i, qc, j):
        rows = pl.ds(pl.multiple_of(j * kc, kc), kc)
        s = lax.dot_general(k_ref[0, rows, :], qc, (((1,), (1,)), ((), ())),
                            preferred_element_type=F32)
        s_ref[i, rows, :] = s
        m_ref[i] = jnp.maximum(m_ref[i], fold8(s, jnp.maximum))

    ones_rows = jnp.ones((ATTN_ONES_ROWS, kc), BF16)

    def prob_chunk(i, j, m):
        rows = pl.ds(pl.multiple_of(j * kc, kc), kc)
        p = jnp.exp2(s_ref[i, rows, :] - m)
        v_aug = jnp.concatenate([vt_ref[j], ones_rows], axis=0)
        acc_ref[...] += jnp.dot(v_aug, p.astype(BF16), preferred_element_type=F32)

    m_ref[...] = jnp.full(m_ref.shape, -jnp.inf, F32)
    q0 = stream_q(*streams[0])

    def first_body(j, carry):
        score_chunk(0, q0, j)
        return carry

    lax.fori_loop(0, nc, first_body, 0, unroll=ATTN_UNROLL)
    outs = []
    for i, (grp, c) in enumerate(streams):
        m = jnp.max(m_ref[i], axis=0, keepdims=True)
        acc_ref[...] = jnp.zeros_like(acc_ref)
        if i + 1 < len(streams):
            qn = stream_q(*streams[i + 1])

            def body(j, carry, i=i, m=m, qn=qn):
                score_chunk(i + 1, qn, j)
                prob_chunk(i, j, m)
                return carry
        else:
            def body(j, carry, i=i, m=m):
                prob_chunk(i, j, m)
                return carry

        lax.fori_loop(0, nc, body, 0, unroll=ATTN_UNROLL)
        outs.append(acc_ref[:DIFF_V_DIM, :] * (1.0 / acc_ref[DIFF_V_DIM:DIFF_V_DIM + 1, :]))
        if c == 1:
            o = outs[-2] - lam_ref[0] * outs[-1]
            o = o * lax.rsqrt(jnp.mean(o * o, axis=0, keepdims=True) + SUBLN_EPS)
            o = o * (g_ref[...] * (1.0 - LAMBDA_INIT))
            o_ref[0, grp * w:(grp + 1) * w, :] = o.T.astype(o_ref.dtype)


def _diff_attention(lam, q, k, vt, g_subln):
    B, S, _ = q.shape
    tq, kc = ATTN_Q_TILE, ATTN_KV_TILE
    nc = S // kc
    n_streams = 2 * (tq // ATTN_STREAM_W)
    return pl.pallas_call(
        functools.partial(_attn_kernel, kc=kc),
        grid=(B, DIFF_HEADS, S // tq),
        in_specs=[
            pl.BlockSpec(memory_space=pltpu.SMEM),
            pl.BlockSpec((1, tq, LANES), lambda b, h, i: (b, i, h)),
            pl.BlockSpec((1, S, LANES), lambda b, h, i: (b, 0, h)),
            pl.BlockSpec((nc, DIFF_V_DIM, kc), lambda b, h, i: (b, h, 0)),
            pl.BlockSpec((DIFF_V_DIM, 1), lambda b, h, i: (0, 0)),
        ],
        out_specs=pl.BlockSpec((1, tq, LANES), lambda b, h, i: (b, i, h)),
        out_shape=jax.ShapeDtypeStruct((B, S, ATTN_WIDTH), BF16),
        scratch_shapes=[pltpu.VMEM((n_streams, S, ATTN_STREAM_W), F32),
                        pltpu.VMEM((n_streams, SUBLANES, ATTN_STREAM_W), F32),
                        pltpu.VMEM((DIFF_V_DIM + ATTN_ONES_ROWS, ATTN_STREAM_W), F32)],
        compiler_params=_params("parallel", "parallel", "parallel"),
        name="diff_attn",
    )(lam, q, k, vt, g_subln.reshape(DIFF_V_DIM, 1))


def _shortconv_kernel(hy_ref, prev_ref, next_ref, w_ref, b_ref, u_ref, x0_ref):
    i = pl.program_id(1)
    last = pl.num_programs(1) - 1
    x = hy_ref[0].astype(F32)
    ts = x.shape[0]
    prev_row = jnp.where(i == 0, 0.0, prev_ref[0, HALO_ROWS - 1:HALO_ROWS, :].astype(F32))
    next_row = jnp.where(i == last, 0.0, next_ref[0, 0:1, :].astype(F32))
    row = lax.broadcasted_iota(jnp.int32, (ts, 1), 0)
    xm = jnp.where(row == 0, prev_row, pltpu.roll(x, 1, axis=0))
    xp = jnp.where(row == ts - 1, next_row, pltpu.roll(x, ts - 1, axis=0))
    y = b_ref[...] + xm * w_ref[0:1, :] + x * w_ref[1:2, :] + xp * w_ref[2:3, :]
    C = HYENA_WIDTH
    u_ref[0] = (y[:, :C] * y[:, 2 * C:]).astype(u_ref.dtype)
    x0_ref[0] = y[:, C:2 * C].astype(x0_ref.dtype)


HALO_ROWS = 16


def _short_conv(hy, conv_w, conv_b):
    B, S, C3 = hy.shape
    ts = TOKEN_TILE
    nb = ts // HALO_ROWS
    return pl.pallas_call(
        _shortconv_kernel,
        grid=(B, S // ts),
        in_specs=[
            pl.BlockSpec((1, ts, C3), lambda b, i: (b, i, 0)),
            pl.BlockSpec((1, HALO_ROWS, C3), lambda b, i: (b, jnp.maximum(i * nb - 1, 0), 0)),
            pl.BlockSpec((1, HALO_ROWS, C3), lambda b, i: (b, jnp.minimum((i + 1) * nb, S // HALO_ROWS - 1), 0)),
            pl.BlockSpec((3, C3), lambda b, i: (0, 0)),
            pl.BlockSpec((1, C3), lambda b, i: (0, 0)),
        ],
        out_specs=[
            pl.BlockSpec((1, ts, HYENA_WIDTH), lambda b, i: (b, i, 0)),
            pl.BlockSpec((1, ts, HYENA_WIDTH), lambda b, i: (b, i, 0)),
        ],
        out_shape=[
            jax.ShapeDtypeStruct((B, S, HYENA_WIDTH), BF16),
            jax.ShapeDtypeStruct((B, S, HYENA_WIDTH), BF16),
        ],
        compiler_params=_params("parallel", "parallel"),
        name="short_conv",
    )(hy, hy, hy, conv_w, conv_b.reshape(1, C3))


FFT_N = 8192
FFT_N1 = 64
FFT_N2 = 128
FFT_CH_BLOCK = 32
FFT_UNROLL = 4


def _dft_constants():
    import numpy as np
    n1, n2, n = FFT_N1, FFT_N2, FFT_N
    k1 = np.arange(n1)[:, None]
    t1 = np.arange(n1)[None, :]
    f1 = np.exp(-2j * np.pi * k1 * t1 / n1)
    f1h = f1[:, : n1 // 2]
    w1c = np.block([[f1h.real, -f1h.imag], [f1h.imag, f1h.real]])
    w1r = np.concatenate([f1h.real, f1h.imag], axis=0)
    t2 = np.arange(n2)[:, None]
    k2 = np.arange(n2)[None, :]
    f2 = np.exp(-2j * np.pi * t2 * k2 / n2)
    w2a = np.concatenate([f2.real, f2.imag], axis=1)
    w2b = np.concatenate([-f2.imag, f2.real], axis=1)
    g2 = np.conj(f2)
    w3a = np.concatenate([g2.real, g2.imag], axis=1)
    w3b = np.concatenate([-g2.imag, g2.real], axis=1)
    h = np.conj(f1).T[: n1 // 2] / n
    w4 = np.block([[h.real, -h.imag], [h.imag, h.real]])
    tw = np.exp(-2j * np.pi * np.arange(n1)[:, None] * np.arange(n2)[None, :] / n)
    bf = lambda a: jnp.asarray(a, dtype=F32).astype(BF16)
    return dict(w1c=bf(w1c), w1r=bf(w1r), w2a=bf(w2a), w2b=bf(w2b), w3a=bf(w3a), w3b=bf(w3b),
                w4=bf(w4), tc=jnp.asarray(tw.real, F32), ts=jnp.asarray(tw.imag, F32))


def _dft_forward(x_ref, w1_ref, tc_ref, ts_ref, w2a_ref, w2b_ref, ar_ref, ai_ref):
    cb = x_ref.shape[0]
    n1 = FFT_N1
    w1 = w1_ref[...]
    tc, ts = tc_ref[...], ts_ref[...]

    def body(c, carry):
        a = jnp.dot(w1, x_ref[c], preferred_element_type=F32)
        ar, ai = a[:n1], a[n1:]
        rows = pl.ds(pl.multiple_of(c * n1, n1), n1)
        ar_ref[rows, :] = (ar * tc - ai * ts).astype(BF16)
        ai_ref[rows, :] = (ar * ts + ai * tc).astype(BF16)
        return carry

    lax.fori_loop(0, cb, body, 0, unroll=FFT_UNROLL)
    return (jnp.dot(ar_ref[...], w2a_ref[...], preferred_element_type=F32)
            + jnp.dot(ai_ref[...], w2b_ref[...], preferred_element_type=F32))


def _spectrum_kernel(hf_ref, hb_ref, w1_ref, tc_ref, ts_ref, w2a_ref, w2b_ref, o_ref, ar_ref, ai_ref):
    n2 = FFT_N2
    f = _dft_forward(hf_ref, w1_ref, tc_ref, ts_ref, w2a_ref, w2b_ref, ar_ref, ai_ref)
    o_ref[...] = f.reshape(o_ref.shape)
    b = _dft_forward(hb_ref, w1_ref, tc_ref, ts_ref, w2a_ref, w2b_ref, ar_ref, ai_ref)
    b = b.reshape(o_ref.shape)
    o_ref[:, :, :n2] = o_ref[:, :, :n2] + b[:, :, :n2]
    o_ref[:, :, n2:] = o_ref[:, :, n2:] - b[:, :, n2:]


def _fftconv_kernel(x_ref, kf_ref, w1_ref, tc_ref, ts_ref, w2a_ref, w2b_ref, w3a_ref, w3b_ref,
                    w4_ref, y_ref, ar_ref, ai_ref, c_ref):
    cb = x_ref.shape[0]
    n1, n2 = FFT_N1, FFT_N2
    b = _dft_forward(x_ref, w1_ref, tc_ref, ts_ref, w2a_ref, w2b_ref, ar_ref, ai_ref)
    kf = kf_ref[...].reshape(cb * n1, 2 * n2)
    br, bi = b[:, :n2], b[:, n2:]
    kr, ki = kf[:, :n2], kf[:, n2:]
    ar_ref[...] = (br * kr - bi * ki).astype(BF16)
    ai_ref[...] = (br * ki + bi * kr).astype(BF16)
    c_ref[...] = (jnp.dot(ar_ref[...], w3a_ref[...], preferred_element_type=F32)
                  + jnp.dot(ai_ref[...], w3b_ref[...], preferred_element_type=F32))
    w4 = w4_ref[...]
    tc, ts = tc_ref[...], ts_ref[...]

    def body(c, carry):
        rows = pl.ds(pl.multiple_of(c * n1, n1), n1)
        cr, ci = c_ref[rows, :n2], c_ref[rows, n2:]
        dr = (cr * tc + ci * ts).astype(BF16)
        di = (ci * tc - cr * ts).astype(BF16)
        y_ref[c] = (jnp.dot(w4[:, :n1], dr, preferred_element_type=F32)
                    + jnp.dot(w4[:, n1:], di, preferred_element_type=F32)).astype(y_ref.dtype)
        return carry

    lax.fori_loop(0, cb, body, 0, unroll=FFT_UNROLL)


def _const_spec(a):
    nd = a.ndim
    return pl.BlockSpec(a.shape, lambda *_: (0,) * nd)


def _filter_spectrum(hf_slabs, hb_slabs, cst):
    C = hf_slabs.shape[0]
    cb = FFT_CH_BLOCK
    consts = [cst["w1r"], cst["tc"], cst["ts"], cst["w2a"], cst["w2b"]]
    slab = pl.BlockSpec((cb, FFT_N1 // 2, FFT_N2), lambda i: (i, 0, 0))
    return pl.pallas_call(
        _spectrum_kernel,
        grid=(C // cb,),
        in_specs=[slab, slab] + [_const_spec(a) for a in consts],
        out_specs=pl.BlockSpec((cb, FFT_N1, 2 * FFT_N2), lambda i: (i, 0, 0)),
        out_shape=jax.ShapeDtypeStruct((C, FFT_N1, 2 * FFT_N2), F32),
        scratch_shapes=[pltpu.VMEM((cb * FFT_N1, FFT_N2), BF16), pltpu.VMEM((cb * FFT_N1, FFT_N2), BF16)],
        compiler_params=_params("parallel"),
        name="filter_spectrum",
    )(hf_slabs, hb_slabs, *consts)


FILTER_TIME_TILE = 512


def _filter_kernel(zt_ref, w1t_ref, b1_ref, wit_ref, bi_ref, fr_ref, wot_ref, dec_ref, hf_ref, hb_ref):
    hp = lax.Precision.HIGHEST
    h = jnp.sin(fr_ref[0] * (jnp.dot(w1t_ref[...], zt_ref[...], precision=hp,
                                     preferred_element_type=F32) + b1_ref[...]))
    for j in range(FILTER_INNER):
        h = jnp.sin(fr_ref[j + 1] * (jnp.dot(wit_ref[j], h, precision=hp,
                                             preferred_element_type=F32) + bi_ref[j]))
    o = jnp.dot(wot_ref[...], h, precision=hp, preferred_element_type=F32)
    dec = dec_ref[...]
    hf_ref[...] = o[:HYENA_WIDTH] * dec
    hb_ref[...] = o[HYENA_WIDTH:] * dec


def _hyena_filters(seq, w1, b1, w_inner, b_inner, freq, w_out):
    C = HYENA_WIDTH
    order = w1.shape[1]
    pos = jnp.arange(seq, dtype=F32)
    t = pos / (seq - 1)
    bands = (FILTER_EMB - 1) // 2
    f = jnp.linspace(1e-4, bands - 1, bands, dtype=F32)
    fw = ((2.0 * math.pi / seq) * pos)[:, None] * f[None, :]
    z = jnp.concatenate([t[:, None], jnp.cos(fw), -jnp.sin(fw)], axis=-1)
    zt = jnp.zeros((LANES, seq), F32).at[:FILTER_EMB].set(z.T)
    w1t = jnp.zeros((order, LANES), F32).at[:, :FILTER_EMB].set(w1.T)
    max_decay = math.log(DECAY_TARGET) / FAST_DECAY_PCT
    min_decay = math.log(DECAY_TARGET) / SLOW_DECAY_PCT
    deltas = jnp.abs(jnp.linspace(min_decay, max_decay, C, dtype=F32))
    dec_t = jnp.exp(-deltas[:, None] * t[None, :])
    tt = FILTER_TIME_TILE
    lane_blk = lambda r: pl.BlockSpec((r, tt), lambda i: (0, i))
    args = [zt, w1t, b1.reshape(order, 1), jnp.swapaxes(w_inner, 1, 2),
            b_inner.reshape(FILTER_INNER, order, 1), freq.reshape(FILTER_INNER + 1, order, 1),
            w_out.T, dec_t]
    return pl.pallas_call(
        _filter_kernel,
        grid=(seq // tt,),
        in_specs=[lane_blk(LANES)] + [_const_spec(a) for a in args[1:7]] + [lane_blk(C)],
        out_specs=[lane_blk(C), lane_blk(C)],
        out_shape=[jax.ShapeDtypeStruct((C, seq), F32), jax.ShapeDtypeStruct((C, seq), F32)],
        compiler_params=_params("parallel"),
        name="hyena_filters",
    )(*args)


def _fft_conv(x_slabs, kf, cst):
    P, C = x_slabs.shape[:2]
    cb = FFT_CH_BLOCK
    consts = [cst["w1c"], cst["tc"], cst["ts"], cst["w2a"], cst["w2b"], cst["w3a"], cst["w3b"], cst["w4"]]
    return pl.pallas_call(
        _fftconv_kernel,
        grid=(C // cb, P),
        in_specs=[pl.BlockSpec((None, cb, FFT_N1, FFT_N2), lambda i, p: (p, i, 0, 0)),
                  pl.BlockSpec((cb, FFT_N1, 2 * FFT_N2), lambda i, p: (i, 0, 0))]
                 + [_const_spec(a) for a in consts],
        out_specs=pl.BlockSpec((None, cb, FFT_N1, FFT_N2), lambda i, p: (p, i, 0, 0)),
        out_shape=jax.ShapeDtypeStruct((P, C, FFT_N1, FFT_N2), BF16),
        scratch_shapes=[pltpu.VMEM((cb * FFT_N1, FFT_N2), BF16), pltpu.VMEM((cb * FFT_N1, FFT_N2), BF16),
                        pltpu.VMEM((cb * FFT_N1, 2 * FFT_N2), F32)],
        compiler_params=_params("parallel", "arbitrary"),
        name="fft_conv",
    )(x_slabs, kf, *consts)


def _split_bf16(x):
    hi = x.astype(BF16)
    lo = (x - hi.astype(F32)).astype(BF16)
    return hi, lo


def _outproj_kernel(attn_ref, yc_ref, u_ref, x0_ref, d_ref, gn_ref, grp_ref, wo_ref, x_ref,
                    gf_ref, rwh_ref, rwl_ref, rb_ref, tri_ref, h_ref, hn_ref, rt_ref, cnt_out_ref,
                    cnt_ref):
    z = ((yc_ref[...].astype(F32) + u_ref[...].astype(F32) * d_ref[...])
         * x0_ref[...].astype(F32))
    zh, zl = _split_bf16(z * z)
    grp = grp_ref[...]
    ssq = (jnp.dot(zh, grp, preferred_element_type=F32) + jnp.dot(zl, grp, preferred_element_type=F32))
    gsz = HYENA_WIDTH // HYENA_GROUPS
    hy_out = (z * lax.rsqrt(ssq * (1.0 / gsz) + EPS) * gn_ref[...]).astype(BF16)
    mix = (jnp.dot(attn_ref[...], wo_ref[:ATTN_WIDTH, :], preferred_element_type=F32)
           + jnp.dot(hy_out, wo_ref[ATTN_WIDTH:, :], preferred_element_type=F32))
    h = x_ref[...] + mix
    h_ref[...] = h
    hn = _rms(h, gf_ref[...], EPS)
    hn_ref[...] = hn.astype(BF16)
    nh, nl = _split_bf16(hn)
    logits = (jnp.dot(nh, rwh_ref[...], preferred_element_type=F32)
              + jnp.dot(nl, rwh_ref[...], preferred_element_type=F32)
              + jnp.dot(nh, rwl_ref[...], preferred_element_type=F32)) + rb_ref[...]

    @pl.when(pl.program_id(0) == 0)
    def _():
        cnt_ref[...] = jnp.zeros_like(cnt_ref)

    lane = lax.broadcasted_iota(jnp.int32, logits.shape, 1)
    work = logits
    top_val, top_hot = [], []
    for _ in range(TOP_K):
        m = jnp.max(work, axis=-1, keepdims=True)
        idx = jnp.min(jnp.where(work == m, lane, ROUTER_PAD), axis=-1, keepdims=True)
        hot = lane == idx
        top_val.append(m)
        top_hot.append(hot)
        work = jnp.where(hot, -jnp.inf, work)
    ex = [jnp.exp(v - top_val[0]) for v in top_val]
    inv_den = 1.0 / (ex[0] + ex[1] + ex[2] + ex[3])
    sel = jnp.zeros(logits.shape, F32)
    for hot in top_hot:
        sel = sel + hot.astype(F32)
    before = jnp.dot(tri_ref[...], sel.astype(BF16), preferred_element_type=F32) + cnt_ref[0:1, :]
    lane_f = lane.astype(F32)
    packed = jnp.zeros(logits.shape, F32)
    for r, hot in enumerate(top_hot):
        e_r = jnp.sum(jnp.where(hot, lane_f, 0.0), axis=-1, keepdims=True)
        rank_r = jnp.sum(jnp.where(hot, before, 0.0), axis=-1, keepdims=True)
        packed = jnp.where(lane == r, e_r, packed)
        packed = jnp.where(lane == TOP_K + r, rank_r, packed)
        packed = jnp.where(lane == 2 * TOP_K + r, ex[r] * inv_den, packed)
    rt_ref[...] = packed
    cnt_ref[...] = cnt_ref[...] + jnp.sum(sel, axis=0, keepdims=True)
    cnt_out_ref[...] = cnt_ref[...]


def _out_proj(attn2, yc2, u2, x02, hyena_d, hyena_gn, w_out_bf, x2, g_ffn, router_w, router_b):
    T, D = x2.shape
    tm = TOKEN_TILE
    C = HYENA_WIDTH
    gid = jnp.arange(C) // (C // HYENA_GROUPS)
    grp = (gid[:, None] == gid[None, :]).astype(BF16)
    rw = jnp.zeros((D, ROUTER_PAD), F32).at[:, :N_EXPERTS].set(router_w)
    rwh, rwl = _split_bf16(rw)
    rb = jnp.full((1, ROUTER_PAD), -jnp.inf, F32).at[0, :N_EXPERTS].set(router_b)
    tri = (jnp.arange(tm)[:, None] > jnp.arange(tm)[None, :]).astype(BF16)
    row = lambda i: (i, 0)
    const = lambda i: (0, 0)
    return pl.pallas_call(
        _outproj_kernel,
        grid=(T // tm,),
        in_specs=[
            pl.BlockSpec((tm, ATTN_WIDTH), row),
            pl.BlockSpec((tm, C), row),
            pl.BlockSpec((tm, C), row),
            pl.BlockSpec((tm, C), row),
            pl.BlockSpec((1, C), const),
            pl.BlockSpec((1, C), const),
            pl.BlockSpec((C, C), const),
            pl.BlockSpec((D, D), const),
            pl.BlockSpec((tm, D), row),
            pl.BlockSpec((1, D), const),
            pl.BlockSpec((D, ROUTER_PAD), const),
            pl.BlockSpec((D, ROUTER_PAD), const),
            pl.BlockSpec((1, ROUTER_PAD), const),
            pl.BlockSpec((tm, tm), const),
        ],
        out_specs=[
            pl.BlockSpec((tm, D), row),
            pl.BlockSpec((tm, D), row),
            pl.BlockSpec((tm, ROUTER_PAD), row),
            pl.BlockSpec((SUBLANES, ROUTER_PAD), const),
        ],
        out_shape=[
            jax.ShapeDtypeStruct((T, D), F32),
            jax.ShapeDtypeStruct((T, D), BF16),
            jax.ShapeDtypeStruct((T, ROUTER_PAD), F32),
            jax.ShapeDtypeStruct((SUBLANES, ROUTER_PAD), F32),
        ],
        scratch_shapes=[pltpu.VMEM((SUBLANES, ROUTER_PAD), F32)],
        compiler_params=_params("arbitrary"),
        name="out_proj",
    )(attn2, yc2, u2, x02, hyena_d.reshape(1, C), hyena_gn.reshape(1, C), grp, w_out_bf, x2,
      g_ffn.reshape(1, D), rwh, rwl, rb, tri)


def _expert_kernel(be_ref, nused_ref, x_ref, wg_ref, bg_ref, wu_ref, bu_ref, wd_ref, bd_ref, *rest):
    y_ref, wg_bf, wu_bf, wd_bf = rest[-4:]
    i = pl.program_id(0)
    used = i < nused_ref[0]
    new_expert = jnp.logical_or(i == 0, be_ref[i] != be_ref[jnp.maximum(i - 1, 0)])

    @pl.when(jnp.logical_and(used, new_expert))
    def _():
        wg_bf[...] = wg_ref[0].astype(BF16)
        wu_bf[...] = wu_ref[0].astype(BF16)
        wd_bf[...] = wd_ref[0].astype(BF16)

    @pl.when(used)
    def _():
        x = x_ref[...]
        g = jnp.minimum(jnp.dot(x, wg_bf[...], preferred_element_type=F32) + bg_ref[0], SWIGLU_LIMIT)
        u = jnp.clip(jnp.dot(x, wu_bf[...], preferred_element_type=F32) + bu_ref[0],
                     -SWIGLU_LIMIT, SWIGLU_LIMIT)
        a = (u + 1.0) * (g * jax.nn.sigmoid(SWIGLU_ALPHA * g))
        y = jnp.dot(a.astype(BF16), wd_bf[...], preferred_element_type=F32) + bd_ref[0]
        y_ref[...] = y.astype(y_ref.dtype)

    @pl.when(jnp.logical_not(used))
    def _():
        y_ref[...] = jnp.zeros_like(y_ref)


def _experts(block_e, n_used, x_parts, wg, bg, wu, bu, wd, bd):
    Pp, D = x_parts[0].shape
    tm = EXPERT_TILE
    nb = Pp // tm
    P = Pp * len(x_parts)
    E, _, FF = wg.shape
    wmap = lambda i, be, nu: (be[i], 0, 0)
    y = None
    for part, xs in enumerate(x_parts):
        in_specs = [
            pl.BlockSpec((tm, D), lambda i, be, nu: (i, 0)),
            pl.BlockSpec((1, D, FF), wmap),
            pl.BlockSpec((1, 1, FF), wmap),
            pl.BlockSpec((1, D, FF), wmap),
            pl.BlockSpec((1, 1, FF), wmap),
            pl.BlockSpec((1, FF, D), wmap),
            pl.BlockSpec((1, 1, D), wmap),
        ]
        args = [block_e[part * nb:(part + 1) * nb], jnp.clip(n_used - part * nb, 0, nb), xs,
                wg, bg.reshape(E, 1, FF), wu, bu.reshape(E, 1, FF), wd, bd.reshape(E, 1, D)]
        aliases = {}
        if y is not None:
            in_specs.append(pl.BlockSpec(memory_space=pl.ANY))
            args.append(y)
            aliases = {len(args) - 1: 0}
        grid_spec = pltpu.PrefetchScalarGridSpec(
            num_scalar_prefetch=2,
            grid=(nb,),
            in_specs=in_specs,
            out_specs=pl.BlockSpec((tm, D), lambda i, be, nu, part=part: (i + part * nb, 0)),
            scratch_shapes=[pltpu.VMEM((D, FF), BF16), pltpu.VMEM((D, FF), BF16),
                            pltpu.VMEM((FF, D), BF16)],
        )
        y = pl.pallas_call(
            _expert_kernel,
            grid_spec=grid_spec,
            out_shape=jax.ShapeDtypeStruct((P, D), BF16),
            input_output_aliases=aliases,
            compiler_params=_params("arbitrary"),
            name=f"moe_experts_{part}",
        )(*args)
    return y


def _final_kernel(h_ref, yg_ref, rt_ref, p_ref, wp_ref, gp_ref, wg_ref, bg_ref, gfin_ref, *rest):
    o_ref = rest[-1]
    h = h_ref[...]
    for r in range(TOP_K):
        h = h + yg_ref[r].astype(F32) * rt_ref[:, 2 * TOP_K + r: 2 * TOP_K + r + 1]
    e = _rms(jnp.dot(p_ref[...].astype(BF16), wp_ref[...], preferred_element_type=F32),
             gp_ref[...], EPS)
    gate = jax.nn.sigmoid(jnp.dot(h.astype(BF16), wg_ref[...], preferred_element_type=F32)
                          + bg_ref[...])
    h = h + gate * e
    o_ref[...] = _rms(h, gfin_ref[...], EPS)


def _final(h1, yg_parts, route, p2, w_ple_bf, g_ple, w_gate_bf, b_gate, g_final):
    T, D = h1.shape
    tm = TOKEN_TILE
    PD = p2.shape[1]
    nb = T // tm // len(yg_parts)
    const = lambda i: (0, 0)
    out = None
    for part, yg in enumerate(yg_parts):
        row = lambda i, part=part: (i + part * nb, 0)
        in_specs = [
            pl.BlockSpec((tm, D), row),
            pl.BlockSpec((TOP_K, tm, D), lambda i: (0, i, 0)),
            pl.BlockSpec((tm, ROUTER_PAD), row),
            pl.BlockSpec((tm, PD), row),
            pl.BlockSpec((PD, D), const),
            pl.BlockSpec((1, D), const),
            pl.BlockSpec((D, D), const),
            pl.BlockSpec((1, D), const),
            pl.BlockSpec((1, D), const),
        ]
        args = [h1, yg, route, p2, w_ple_bf, g_ple.reshape(1, D), w_gate_bf, b_gate.reshape(1, D),
                g_final.reshape(1, D)]
        aliases = {}
        if out is not None:
            in_specs.append(pl.BlockSpec(memory_space=pl.ANY))
            args.append(out)
            aliases = {len(args) - 1: 0}
        out = pl.pallas_call(
            _final_kernel,
            grid=(nb,),
            in_specs=in_specs,
            out_specs=pl.BlockSpec((tm, D), row),
            out_shape=jax.ShapeDtypeStruct((T, D), F32),
            input_output_aliases=aliases,
            compiler_params=_params("parallel"),
            name=f"final_{part}",
        )(*args)
    return out


def _rope_tables(seq):
    d = DIFF_HEAD_DIM
    pos = jnp.arange(seq, dtype=F32)
    inv = ROPE_THETA ** (-jnp.arange(0, d, 2, dtype=F32) / d)
    ang = pos[:, None] * inv[None, :]
    cos, sin = jnp.cos(ang), jnp.sin(ang)
    z = jnp.zeros_like(sin)
    cos_t = jnp.tile(jnp.concatenate([cos, cos], -1), (1, LANES // d))
    s1_t = jnp.tile(jnp.concatenate([-sin, z], -1), (1, LANES // d))
    s2_t = jnp.tile(jnp.concatenate([z, sin], -1), (1, LANES // d))
    return cos_t, s1_t, s2_t


def _long_conv(u, hf_t, hb_t):
    B, L, C = u.shape
    assert 2 * L == FFT_N and B % 2 == 0
    P, R = B // 2, FFT_N1 // 2
    cst = _dft_constants()
    kf = _filter_spectrum(hf_t.reshape(C, R, FFT_N2).astype(BF16),
                          hb_t.reshape(C, R, FFT_N2).astype(BF16), cst)
    x_slabs = (u.reshape(2, P, R, FFT_N2, C).transpose(1, 4, 0, 2, 3)
               .reshape(P, C, FFT_N1, FFT_N2).astype(BF16))
    y = _fft_conv(x_slabs, kf, cst)
    return y.reshape(P, C, 2, R, FFT_N2).transpose(2, 0, 3, 4, 1).reshape(B, L, C)


def _dispatch_indices(route, cnt, T):
    tm = EXPERT_TILE
    A = T * TOP_K
    top_e = route[:, :TOP_K].astype(jnp.int32)
    rank = route[:, TOP_K:2 * TOP_K].astype(jnp.int32)
    counts = cnt[0, :N_EXPERTS].astype(jnp.int32)
    padded = (counts + tm - 1) // tm * tm
    pad_end = jnp.cumsum(padded)
    pad_start = pad_end - padded
    start = jnp.cumsum(counts) - counts
    pos = pad_start[top_e] + rank
    n_blocks = -(-A // tm) + N_EXPERTS
    P = n_blocks * tm
    block_first = jnp.arange(n_blocks, dtype=jnp.int32) * tm
    block_e = jnp.minimum(jnp.sum(pad_end[None, :] <= block_first[:, None], axis=1),
                          N_EXPERTS - 1).astype(jnp.int32)
    n_used = (pad_end[-1] // tm).astype(jnp.int32).reshape(1)
    tok = jnp.broadcast_to(jnp.arange(T, dtype=jnp.int32)[:, None], (T, TOP_K))
    _, sorted_tok = lax.sort_key_val(pos.reshape(A), tok.reshape(A))
    in_blk = jnp.arange(tm, dtype=jnp.int32)[None, :]
    r = (block_first - pad_start[block_e])[:, None] + in_blk
    compact = jnp.clip(start[block_e][:, None] + r, 0, A - 1)
    filler = (block_first[:, None] + in_blk) % T
    slot_tok = jnp.where(r < counts[block_e][:, None], sorted_tok[compact], filler).reshape(P)
    return pos.T, slot_tok, block_e, n_used


def kernel(x, p, g_mix, w_in, hyena_conv_w, hyena_conv_b, flt_w1, flt_b1, flt_w_inner, flt_b_inner, flt_freq, flt_w_out, hyena_d, hyena_gn, lambda_q1, lambda_k1, lambda_q2, lambda_k2, attn_subln, w_out, g_ffn, router_w, router_b, w_gate, b_gate, w_up, b_up, w_down, b_down, w_ple, g_ple, w_ple_gate, b_ple_gate, g_final):
    B, S, D = x.shape
    T = B * S
    i = 0
    x2 = x.reshape(T, D)

    cos_t, s1_t, s2_t = _rope_tables(S)
    q, k, vt, hy = _in_proj(x2, g_mix[i], w_in[i].astype(BF16), cos_t, s1_t, s2_t, S)

    lam = (jnp.exp(jnp.sum(lambda_q1[i] * lambda_k1[i])) - jnp.exp(jnp.sum(lambda_q2[i] * lambda_k2[i]))
           + LAMBDA_INIT).reshape(1).astype(F32)
    attn = _diff_attention(lam, q.reshape(B, S, -1), k.reshape(B, S, -1), vt, attn_subln[i])

    u, hx0 = _short_conv(hy.reshape(B, S, -1), hyena_conv_w[i], hyena_conv_b[i])
    h_fwd, h_bwd = _hyena_filters(S, flt_w1[i], flt_b1[i], flt_w_inner[i], flt_b_inner[i],
                                  flt_freq[i], flt_w_out[i])
    yc = _long_conv(u, h_fwd, h_bwd)

    h1, hn, route, cnt = _out_proj(attn.reshape(T, -1), yc.reshape(T, -1), u.reshape(T, -1),
                                   hx0.reshape(T, -1), hyena_d[i], hyena_gn[i],
                                   w_out[i].astype(BF16), x2, g_ffn[i], router_w[i], router_b[i])

    pos, slot_tok, block_e, n_used = _dispatch_indices(route, cnt, T)
    n_parts = MOE_OVERLAP_PARTS
    slot_parts = jnp.split(slot_tok, n_parts)
    x_parts = [hn[s] for s in slot_parts]
    y = _experts(block_e, n_used, x_parts, w_gate[i], b_gate[i], w_up[i], b_up[i],
                 w_down[i], b_down[i])
    pos_parts = jnp.split(pos, n_parts, axis=1)
    yg_parts = [y[pp.reshape(-1)].reshape(TOP_K, T // n_parts, D) for pp in pos_parts]

    out = _final(h1, yg_parts, route, p[i].reshape(T, -1), w_ple[i].astype(BF16), g_ple[i],
                 w_ple_gate[i].astype(BF16), b_ple_gate[i], g_final)
    return out.reshape(B, S, D)
```

```python
import functools
import math

import jax
import jax.numpy as jnp
from jax import lax
from jax.experimental import pallas as pl
from jax.experimental.pallas import tpu as pltpu

F32 = jnp.float32
BF16 = jnp.bfloat16

D_MODEL = 1024
ATTN_WIDTH = 512
HYENA_WIDTH = 512
DIFF_HEADS = 4
DIFF_HEAD_DIM = 64
DIFF_V_DIM = 128
HYENA_GROUPS = 8
FILTER_EMB = 33
FILTER_INNER = 2
FAST_DECAY_PCT = 0.3
SLOW_DECAY_PCT = 1.5
DECAY_TARGET = 1e-2
ROPE_THETA = 10000.0
N_EXPERTS = 32
TOP_K = 4
SWIGLU_ALPHA = 1.702
SWIGLU_LIMIT = 7.0
EPS = 1e-6
SUBLN_EPS = 1e-5
LAMBDA_INIT = 0.8 - 0.6 * math.exp(-0.3 * 0)

V7X_VMEM_LIMIT_BYTES = 56 * 1024 * 1024
LANES = 128
SUBLANES = 8

TOKEN_TILE = 512
ATTN_STREAM_W = 512
ATTN_Q_TILE = 1024
ATTN_ONES_ROWS = 16
ATTN_UNROLL = 4
ATTN_KV_TILE = 512
EXPERT_TILE = 512
ROUTER_PAD = LANES
MOE_OVERLAP_PARTS = 4


def _params(*sem):
    return pltpu.CompilerParams(dimension_semantics=sem, vmem_limit_bytes=V7X_VMEM_LIMIT_BYTES)


def _rms(x, g, eps):
    return x * lax.rsqrt(jnp.mean(x * x, axis=-1, keepdims=True) + eps) * g


def _inproj_kernel(x_ref, g_ref, w_ref, wvt_ref, c_ref, s1_ref, s2_ref, q_ref, k_ref, vt_ref, hy_ref):
    a = _rms(x_ref[...], g_ref[...], EPS).astype(BF16)
    cos, s1, s2 = c_ref[...], s1_ref[...], s2_ref[...]

    def rope(t):
        return t * cos + pltpu.roll(t, LANES - 32, axis=1) * s1 + pltpu.roll(t, 32, axis=1) * s2

    q_scale = (DIFF_HEAD_DIM ** -0.5) * math.log2(math.e)
    qk = jnp.dot(a, w_ref[:, : 2 * ATTN_WIDTH], preferred_element_type=F32)
    for j in range(ATTN_WIDTH // LANES):
        sl = slice(j * LANES, (j + 1) * LANES)
        q_ref[:, sl] = (rope(qk[:, sl]) * q_scale).astype(BF16)
        k_ref[:, sl] = rope(qk[:, ATTN_WIDTH + j * LANES: ATTN_WIDTH + (j + 1) * LANES]).astype(BF16)
    vt_ref[0] = lax.dot_general(wvt_ref[...], a, (((1,), (1,)), ((), ())),
                                preferred_element_type=F32).astype(BF16)
    hy_ref[...] = jnp.dot(a, w_ref[:, 3 * ATTN_WIDTH:], preferred_element_type=F32).astype(hy_ref.dtype)


def _in_proj(x2, g_mix, w_in_bf, cos_t, s1_t, s2_t, seq):
    T, D = x2.shape
    tm = TOKEN_TILE
    nseq = seq // tm
    wvt = w_in_bf[:, 2 * ATTN_WIDTH: 3 * ATTN_WIDTH].T
    row = lambda i: (i, 0)
    const = lambda i: (0, 0)
    pos = lambda i: (i % nseq, 0)
    return pl.pallas_call(
        _inproj_kernel,
        grid=(T // tm,),
        in_specs=[
            pl.BlockSpec((tm, D), row),
            pl.BlockSpec((1, D), const),
            pl.BlockSpec(w_in_bf.shape, const),
            pl.BlockSpec(wvt.shape, const),
            pl.BlockSpec((tm, LANES), pos),
            pl.BlockSpec((tm, LANES), pos),
            pl.BlockSpec((tm, LANES), pos),
        ],
        out_specs=[
            pl.BlockSpec((tm, ATTN_WIDTH), row),
            pl.BlockSpec((tm, ATTN_WIDTH), row),
            pl.BlockSpec((1, ATTN_WIDTH, tm), lambda i: (i, 0, 0)),
            pl.BlockSpec((tm, 3 * HYENA_WIDTH), row),
        ],
        out_shape=[
            jax.ShapeDtypeStruct((T, ATTN_WIDTH), BF16),
            jax.ShapeDtypeStruct((T, ATTN_WIDTH), BF16),
            jax.ShapeDtypeStruct((T // tm, ATTN_WIDTH, tm), BF16),
            jax.ShapeDtypeStruct((T, 3 * HYENA_WIDTH), BF16),
        ],
        compiler_params=_params("parallel"),
        name="in_proj",
    )(x2, g_mix.reshape(1, D), w_in_bf, wvt, cos_t, s1_t, s2_t)


def _attn_kernel(lam_ref, q_ref, k_ref, vt_ref, g_ref, o_ref, s_ref, m_ref, acc_ref, *, kc):
    w = ATTN_STREAM_W
    nc = k_ref.shape[1] // kc
    n_groups = q_ref.shape[1] // w
    streams = [(grp, c) for grp in range(n_groups) for c in range(2)]

    def stream_q(grp, c):
        q = q_ref[0, grp * w:(grp + 1) * w, :]
        lane = lax.broadcasted_iota(jnp.int32, q.shape, 1)
        keep = (lane < DIFF_HEAD_DIM) if c == 0 else (lane >= DIFF_HEAD_DIM)
        return jnp.where(keep, q, jnp.zeros_like(q))

    def fold8(t, op):
        r = t[0:SUBLANES]
        for j in range(1, kc // SUBLANES):
            r = op(r, t[j * SUBLANES:(j + 1) * SUBLANES])
        return r

    def score_chunk(i, qc, j):
        rows = pl.ds(pl.multiple_of(j * kc, kc), kc)
        s = lax.dot_general(k_ref[0, rows, :], qc, (((1,), (1,)), ((), ())),
                            preferred_element_type=F32)
        s_ref[i, rows, :] = s
        m_ref[i] = jnp.maximum(m_ref[i], fold8(s, jnp.maximum))

    ones_rows = jnp.ones((ATTN_ONES_ROWS, kc), BF16)

    def prob_chunk(i, j, m):
        rows = pl.ds(pl.multiple_of(j * kc, kc), kc)
        p = jnp.exp2(s_ref[i, rows, :] - m)
        v_aug = jnp.concatenate([vt_ref[j], ones_rows], axis=0)
        acc_ref[...] += jnp.dot(v_aug, p.astype(BF16), preferred_element_type=F32)

    m_ref[...] = jnp.full(m_ref.shape, -jnp.inf, F32)
    q0 = stream_q(*streams[0])

    def first_body(j, carry):
        score_chunk(0, q0, j)
        return carry

    lax.fori_loop(0, nc, first_body, 0, unroll=ATTN_UNROLL)
    outs = []
    for i, (grp, c) in enumerate(streams):
        m = jnp.max(m_ref[i], axis=0, keepdims=True)
        acc_ref[...] = jnp.zeros_like(acc_ref)
        if i + 1 < len(streams):
            qn = stream_q(*streams[i + 1])

            def body(j, carry, i=i, m=m, qn=qn):
                score_chunk(i + 1, qn, j)
                prob_chunk(i, j, m)
                return carry
        else:
            def body(j, carry, i=i, m=m):
                prob_chunk(i, j, m)
                return carry

        lax.fori_loop(0, nc, body, 0, unroll=ATTN_UNROLL)
        outs.append(acc_ref[:DIFF_V_DIM, :] * (1.0 / acc_ref[DIFF_V_DIM:DIFF_V_DIM + 1, :]))
        if c == 1:
            o = outs[-2] - lam_ref[0] * outs[-1]
            o = o * lax.rsqrt(jnp.mean(o * o, axis=0, keepdims=True) + SUBLN_EPS)
            o = o * (g_ref[...] * (1.0 - LAMBDA_INIT))
            o_ref[0, grp * w:(grp + 1) * w, :] = o.T.astype(o_ref.dtype)


def _diff_attention(lam, q, k, vt, g_subln):
    B, S, _ = q.shape
    tq, kc = ATTN_Q_TILE, ATTN_KV_TILE
    nc = S // kc
    n_streams = 2 * (tq // ATTN_STREAM_W)
    return pl.pallas_call(
        functools.partial(_attn_kernel, kc=kc),
        grid=(B, DIFF_HEADS, S // tq),
        in_specs=[
            pl.BlockSpec(memory_space=pltpu.SMEM),
            pl.BlockSpec((1, tq, LANES), lambda b, h, i: (b, i, h)),
            pl.BlockSpec((1, S, LANES), lambda b, h, i: (b, 0, h)),
            pl.BlockSpec((nc, DIFF_V_DIM, kc), lambda b, h, i: (b, h, 0)),
            pl.BlockSpec((DIFF_V_DIM, 1), lambda b, h, i: (0, 0)),
        ],
        out_specs=pl.BlockSpec((1, tq, LANES), lambda b, h, i: (b, i, h)),
        out_shape=jax.ShapeDtypeStruct((B, S, ATTN_WIDTH), BF16),
        scratch_shapes=[pltpu.VMEM((n_streams, S, ATTN_STREAM_W), F32),
                        pltpu.VMEM((n_streams, SUBLANES, ATTN_STREAM_W), F32),
                        pltpu.VMEM((DIFF_V_DIM + ATTN_ONES_ROWS, ATTN_STREAM_W), F32)],
        compiler_params=_params("parallel", "parallel", "parallel"),
        name="diff_attn",
    )(lam, q, k, vt, g_subln.reshape(DIFF_V_DIM, 1))


def _shortconv_kernel(hy_ref, prev_ref, next_ref, w_ref, b_ref, u_ref, x0_ref):
    i = pl.program_id(1)
    last = pl.num_programs(1) - 1
    x = hy_ref[0].astype(F32)
    ts = x.shape[0]
    prev_row = jnp.where(i == 0, 0.0, prev_ref[0, HALO_ROWS - 1:HALO_ROWS, :].astype(F32))
    next_row = jnp.where(i == last, 0.0, next_ref[0, 0:1, :].astype(F32))
    row = lax.broadcasted_iota(jnp.int32, (ts, 1), 0)
    xm = jnp.where(row == 0, prev_row, pltpu.roll(x, 1, axis=0))
    xp = jnp.where(row == ts - 1, next_row, pltpu.roll(x, ts - 1, axis=0))
    y = b_ref[...] + xm * w_ref[0:1, :] + x * w_ref[1:2, :] + xp * w_ref[2:3, :]
    C = HYENA_WIDTH
    u_ref[0] = (y[:, :C] * y[:, 2 * C:]).astype(u_ref.dtype)
    x0_ref[0] = y[:, C:2 * C].astype(x0_ref.dtype)


HALO_ROWS = 16


def _short_conv(hy, conv_w, conv_b):
    B, S, C3 = hy.shape
    ts = TOKEN_TILE
    nb = ts // HALO_ROWS
    return pl.pallas_call(
        _shortconv_kernel,
        grid=(B, S // ts),
        in_specs=[
            pl.BlockSpec((1, ts, C3), lambda b, i: (b, i, 0)),
            pl.BlockSpec((1, HALO_ROWS, C3), lambda b, i: (b, jnp.maximum(i * nb - 1, 0), 0)),
            pl.BlockSpec((1, HALO_ROWS, C3), lambda b, i: (b, jnp.minimum((i + 1) * nb, S // HALO_ROWS - 1), 0)),
            pl.BlockSpec((3, C3), lambda b, i: (0, 0)),
            pl.BlockSpec((1, C3), lambda b, i: (0, 0)),
        ],
        out_specs=[
            pl.BlockSpec((1, ts, HYENA_WIDTH), lambda b, i: (b, i, 0)),
            pl.BlockSpec((1, ts, HYENA_WIDTH), lambda b, i: (b, i, 0)),
        ],
        out_shape=[
            jax.ShapeDtypeStruct((B, S, HYENA_WIDTH), BF16),
            jax.ShapeDtypeStruct((B, S, HYENA_WIDTH), BF16),
        ],
        compiler_params=_params("parallel", "parallel"),
        name="short_conv",
    )(hy, hy, hy, conv_w, conv_b.reshape(1, C3))


FFT_N = 8192
FFT_N1 = 64
FFT_N2 = 128
FFT_CH_BLOCK = 32
FFT_GROUP = 4
FFT_UNROLL = 2


def _dft_constants():
    import numpy as np
    n1, n2, n = FFT_N1, FFT_N2, FFT_N
    k1 = np.arange(n1)[:, None]
    t1 = np.arange(n1)[None, :]
    f1 = np.exp(-2j * np.pi * k1 * t1 / n1)
    f1h = f1[:, : n1 // 2]
    w1c = np.block([[f1h.real, -f1h.imag], [f1h.imag, f1h.real]])
    w1r = np.concatenate([f1h.real, f1h.imag], axis=0)
    t2 = np.arange(n2)[:, None]
    k2 = np.arange(n2)[None, :]
    f2 = np.exp(-2j * np.pi * t2 * k2 / n2)
    w2a = np.concatenate([f2.real, f2.imag], axis=1)
    w2b = np.concatenate([-f2.imag, f2.real], axis=1)
    g2 = np.conj(f2)
    w3a = np.concatenate([g2.real, g2.imag], axis=1)
    w3b = np.concatenate([-g2.imag, g2.real], axis=1)
    h = np.conj(f1).T[: n1 // 2] / n
    w4 = np.block([[h.real, -h.imag], [h.imag, h.real]])
    tw = np.exp(-2j * np.pi * np.arange(n1)[:, None] * np.arange(n2)[None, :] / n)
    bf = lambda a: jnp.asarray(a, dtype=F32).astype(BF16)
    return dict(w1c=bf(w1c), w1r=bf(w1r), w2a=bf(w2a), w2b=bf(w2b), w3a=bf(w3a), w3b=bf(w3b),
                w4=bf(w4), tc=jnp.asarray(np.tile(tw.real, (1, FFT_GROUP)), F32),
                ts=jnp.asarray(np.tile(tw.imag, (1, FFT_GROUP)), F32))


def _dft_forward(x_ref, w1_ref, tc_ref, ts_ref, w2a_ref, w2b_ref, ar_ref, ai_ref):
    ng = x_ref.shape[0]
    n1, n2 = FFT_N1, FFT_N2
    w1 = w1_ref[...]
    tc, ts = tc_ref[...], ts_ref[...]

    def body(g, carry):
        a = jnp.dot(w1, x_ref[g], preferred_element_type=F32)
        ar, ai = a[:n1], a[n1:]
        tr = (ar * tc - ai * ts).astype(BF16)
        ti = (ar * ts + ai * tc).astype(BF16)
        for cl in range(FFT_GROUP):
            rows = pl.ds(pl.multiple_of((g * FFT_GROUP + cl) * n1, n1), n1)
            ar_ref[rows, :] = tr[:, cl * n2:(cl + 1) * n2]
            ai_ref[rows, :] = ti[:, cl * n2:(cl + 1) * n2]
        return carry

    lax.fori_loop(0, ng, body, 0, unroll=FFT_UNROLL)
    return (jnp.dot(ar_ref[...], w2a_ref[...], preferred_element_type=F32)
            + jnp.dot(ai_ref[...], w2b_ref[...], preferred_element_type=F32))


def _spectrum_kernel(hf_ref, hb_ref, w1_ref, tc_ref, ts_ref, w2a_ref, w2b_ref, o_ref, ar_ref, ai_ref):
    n2 = FFT_N2
    f = _dft_forward(hf_ref, w1_ref, tc_ref, ts_ref, w2a_ref, w2b_ref, ar_ref, ai_ref)
    o_ref[...] = f.reshape(o_ref.shape)
    b = _dft_forward(hb_ref, w1_ref, tc_ref, ts_ref, w2a_ref, w2b_ref, ar_ref, ai_ref)
    b = b.reshape(o_ref.shape)
    o_ref[:, :, :n2] = o_ref[:, :, :n2] + b[:, :, :n2]
    o_ref[:, :, n2:] = o_ref[:, :, n2:] - b[:, :, n2:]


def _fftconv_kernel(x_ref, kf_ref, w1_ref, tc_ref, ts_ref, w2a_ref, w2b_ref, w3a_ref, w3b_ref,
                    w4_ref, y_ref, ar_ref, ai_ref, c_ref):
    ng = x_ref.shape[0]
    cb = ng * FFT_GROUP
    n1, n2 = FFT_N1, FFT_N2
    b = _dft_forward(x_ref, w1_ref, tc_ref, ts_ref, w2a_ref, w2b_ref, ar_ref, ai_ref)
    kf = kf_ref[...].reshape(cb * n1, 2 * n2)
    br, bi = b[:, :n2], b[:, n2:]
    kr, ki = kf[:, :n2], kf[:, n2:]
    ar_ref[...] = (br * kr - bi * ki).astype(BF16)
    ai_ref[...] = (br * ki + bi * kr).astype(BF16)
    c_ref[...] = (jnp.dot(ar_ref[...], w3a_ref[...], preferred_element_type=F32)
                  + jnp.dot(ai_ref[...], w3b_ref[...], preferred_element_type=F32))
    w4 = w4_ref[...]
    tc, ts = tc_ref[...], ts_ref[...]

    def body(g, carry):
        rows = [pl.ds(pl.multiple_of((g * FFT_GROUP + cl) * n1, n1), n1) for cl in range(FFT_GROUP)]
        cr = jnp.concatenate([c_ref[r, :n2] for r in rows], axis=1)
        ci = jnp.concatenate([c_ref[r, n2:] for r in rows], axis=1)
        dr = (cr * tc + ci * ts).astype(BF16)
        di = (ci * tc - cr * ts).astype(BF16)
        y_ref[g] = (jnp.dot(w4[:, :n1], dr, preferred_element_type=F32)
                    + jnp.dot(w4[:, n1:], di, preferred_element_type=F32)).astype(y_ref.dtype)
        return carry

    lax.fori_loop(0, ng, body, 0, unroll=FFT_UNROLL)


def _const_spec(a):
    nd = a.ndim
    return pl.BlockSpec(a.shape, lambda *_: (0,) * nd)


def _filter_spectrum(hf_slabs, hb_slabs, cst):
    C = hf_slabs.shape[0] * FFT_GROUP
    cb = FFT_CH_BLOCK
    consts = [cst["w1r"], cst["tc"], cst["ts"], cst["w2a"], cst["w2b"]]
    slab = pl.BlockSpec((cb // FFT_GROUP, FFT_N1 // 2, FFT_GROUP * FFT_N2), lambda i: (i, 0, 0))
    return pl.pallas_call(
        _spectrum_kernel,
        grid=(C // cb,),
        in_specs=[slab, slab] + [_const_spec(a) for a in consts],
        out_specs=pl.BlockSpec((cb, FFT_N1, 2 * FFT_N2), lambda i: (i, 0, 0)),
        out_shape=jax.ShapeDtypeStruct((C, FFT_N1, 2 * FFT_N2), F32),
        scratch_shapes=[pltpu.VMEM((cb * FFT_N1, FFT_N2), BF16), pltpu.VMEM((cb * FFT_N1, FFT_N2), BF16)],
        compiler_params=_params("parallel"),
        name="filter_spectrum",
    )(hf_slabs, hb_slabs, *consts)


FILTER_TIME_TILE = 512


def _filter_kernel(zt_ref, w1t_ref, b1_ref, wit_ref, bi_ref, fr_ref, wot_ref, dec_ref, hf_ref, hb_ref):
    hp = lax.Precision.HIGHEST
    h = jnp.sin(fr_ref[0] * (jnp.dot(w1t_ref[...], zt_ref[...], precision=hp,
                                     preferred_element_type=F32) + b1_ref[...]))
    for j in range(FILTER_INNER):
        h = jnp.sin(fr_ref[j + 1] * (jnp.dot(wit_ref[j], h, precision=hp,
                                             preferred_element_type=F32) + bi_ref[j]))
    o = jnp.dot(wot_ref[...], h, precision=hp, preferred_element_type=F32)
    dec = dec_ref[...]
    hf_ref[...] = o[:HYENA_WIDTH] * dec
    hb_ref[...] = o[HYENA_WIDTH:] * dec


def _hyena_filters(seq, w1, b1, w_inner, b_inner, freq, w_out):
    C = HYENA_WIDTH
    order = w1.shape[1]
    pos = jnp.arange(seq, dtype=F32)
    t = pos / (seq - 1)
    bands = (FILTER_EMB - 1) // 2
    f = jnp.linspace(1e-4, bands - 1, bands, dtype=F32)
    fw = ((2.0 * math.pi / seq) * pos)[:, None] * f[None, :]
    z = jnp.concatenate([t[:, None], jnp.cos(fw), -jnp.sin(fw)], axis=-1)
    zt = jnp.zeros((LANES, seq), F32).at[:FILTER_EMB].set(z.T)
    w1t = jnp.zeros((order, LANES), F32).at[:, :FILTER_EMB].set(w1.T)
    max_decay = math.log(DECAY_TARGET) / FAST_DECAY_PCT
    min_decay = math.log(DECAY_TARGET) / SLOW_DECAY_PCT
    deltas = jnp.abs(jnp.linspace(min_decay, max_decay, C, dtype=F32))
    dec_t = jnp.exp(-deltas[:, None] * t[None, :])
    tt = FILTER_TIME_TILE
    lane_blk = lambda r: pl.BlockSpec((r, tt), lambda i: (0, i))
    args = [zt, w1t, b1.reshape(order, 1), jnp.swapaxes(w_inner, 1, 2),
            b_inner.reshape(FILTER_INNER, order, 1), freq.reshape(FILTER_INNER + 1, order, 1),
            w_out.T, dec_t]
    return pl.pallas_call(
        _filter_kernel,
        grid=(seq // tt,),
        in_specs=[lane_blk(LANES)] + [_const_spec(a) for a in args[1:7]] + [lane_blk(C)],
        out_specs=[lane_blk(C), lane_blk(C)],
        out_shape=[jax.ShapeDtypeStruct((C, seq), F32), jax.ShapeDtypeStruct((C, seq), F32)],
        compiler_params=_params("parallel"),
        name="hyena_filters",
    )(*args)


def _fft_conv(x_slabs, kf, cst):
    P = x_slabs.shape[0]
    C = x_slabs.shape[1] * FFT_GROUP
    cb = FFT_CH_BLOCK
    gb, gl = cb // FFT_GROUP, FFT_GROUP * FFT_N2
    consts = [cst["w1c"], cst["tc"], cst["ts"], cst["w2a"], cst["w2b"], cst["w3a"], cst["w3b"], cst["w4"]]
    return pl.pallas_call(
        _fftconv_kernel,
        grid=(C // cb, P),
        in_specs=[pl.BlockSpec((None, gb, FFT_N1, gl), lambda i, p: (p, i, 0, 0)),
                  pl.BlockSpec((cb, FFT_N1, 2 * FFT_N2), lambda i, p: (i, 0, 0))]
                 + [_const_spec(a) for a in consts],
        out_specs=pl.BlockSpec((None, gb, FFT_N1, gl), lambda i, p: (p, i, 0, 0)),
        out_shape=jax.ShapeDtypeStruct((P, C // FFT_GROUP, FFT_N1, gl), BF16),
        scratch_shapes=[pltpu.VMEM((cb * FFT_N1, FFT_N2), BF16), pltpu.VMEM((cb * FFT_N1, FFT_N2), BF16),
                        pltpu.VMEM((cb * FFT_N1, 2 * FFT_N2), F32)],
        compiler_params=_params("parallel", "arbitrary"),
        name="fft_conv",
    )(x_slabs, kf, *consts)


def _split_bf16(x):
    hi = x.astype(BF16)
    lo = (x - hi.astype(F32)).astype(BF16)
    return hi, lo


def _outproj_kernel(attn_ref, yc_ref, u_ref, x0_ref, d_ref, gn_ref, grp_ref, wo_ref, x_ref,
                    gf_ref, rwh_ref, rwl_ref, rb_ref, tri_ref, h_ref, hn_ref, rt_ref, cnt_out_ref,
                    cnt_ref):
    z = ((yc_ref[...].astype(F32) + u_ref[...].astype(F32) * d_ref[...])
         * x0_ref[...].astype(F32))
    zh, zl = _split_bf16(z * z)
    grp = grp_ref[...]
    ssq = (jnp.dot(zh, grp, preferred_element_type=F32) + jnp.dot(zl, grp, preferred_element_type=F32))
    gsz = HYENA_WIDTH // HYENA_GROUPS
    hy_out = (z * lax.rsqrt(ssq * (1.0 / gsz) + EPS) * gn_ref[...]).astype(BF16)
    mix = (jnp.dot(attn_ref[...], wo_ref[:ATTN_WIDTH, :], preferred_element_type=F32)
           + jnp.dot(hy_out, wo_ref[ATTN_WIDTH:, :], preferred_element_type=F32))
    h = x_ref[...] + mix
    h_ref[...] = h
    hn = _rms(h, gf_ref[...], EPS)
    hn_ref[...] = hn.astype(BF16)
    nh, nl = _split_bf16(hn)
    logits = (jnp.dot(nh, rwh_ref[...], preferred_element_type=F32)
              + jnp.dot(nl, rwh_ref[...], preferred_element_type=F32)
              + jnp.dot(nh, rwl_ref[...], preferred_element_type=F32)) + rb_ref[...]

    @pl.when(pl.program_id(0) == 0)
    def _():
        cnt_ref[...] = jnp.zeros_like(cnt_ref)

    lane = lax.broadcasted_iota(jnp.int32, logits.shape, 1)
    work = logits
    top_val, top_hot = [], []
    for _ in range(TOP_K):
        m = jnp.max(work, axis=-1, keepdims=True)
        idx = jnp.min(jnp.where(work == m, lane, ROUTER_PAD), axis=-1, keepdims=True)
        hot = lane == idx
        top_val.append(m)
        top_hot.append(hot)
        work = jnp.where(hot, -jnp.inf, work)
    ex = [jnp.exp(v - top_val[0]) for v in top_val]
    inv_den = 1.0 / (ex[0] + ex[1] + ex[2] + ex[3])
    sel = jnp.zeros(logits.shape, F32)
    for hot in top_hot:
        sel = sel + hot.astype(F32)
    before = jnp.dot(tri_ref[...], sel.astype(BF16), preferred_element_type=F32) + cnt_ref[0:1, :]
    lane_f = lane.astype(F32)
    packed = jnp.zeros(logits.shape, F32)
    for r, hot in enumerate(top_hot):
        e_r = jnp.sum(jnp.where(hot, lane_f, 0.0), axis=-1, keepdims=True)
        rank_r = jnp.sum(jnp.where(hot, before, 0.0), axis=-1, keepdims=True)
        packed = jnp.where(lane == r, e_r, packed)
        packed = jnp.where(lane == TOP_K + r, rank_r, packed)
        packed = jnp.where(lane == 2 * TOP_K + r, ex[r] * inv_den, packed)
    rt_ref[...] = packed
    cnt_ref[...] = cnt_ref[...] + jnp.sum(sel, axis=0, keepdims=True)
    cnt_out_ref[...] = cnt_ref[...]


def _out_proj(attn2, yc2, u2, x02, hyena_d, hyena_gn, w_out_bf, x2, g_ffn, router_w, router_b):
    T, D = x2.shape
    tm = TOKEN_TILE
    C = HYENA_WIDTH
    gid = jnp.arange(C) // (C // HYENA_GROUPS)
    grp = (gid[:, None] == gid[None, :]).astype(BF16)
    rw = jnp.zeros((D, ROUTER_PAD), F32).at[:, :N_EXPERTS].set(router_w)
    rwh, rwl = _split_bf16(rw)
    rb = jnp.full((1, ROUTER_PAD), -jnp.inf, F32).at[0, :N_EXPERTS].set(router_b)
    tri = (jnp.arange(tm)[:, None] > jnp.arange(tm)[None, :]).astype(BF16)
    row = lambda i: (i, 0)
    const = lambda i: (0, 0)
    return pl.pallas_call(
        _outproj_kernel,
        grid=(T // tm,),
        in_specs=[
            pl.BlockSpec((tm, ATTN_WIDTH), row),
            pl.BlockSpec((tm, C), row),
            pl.BlockSpec((tm, C), row),
            pl.BlockSpec((tm, C), row),
            pl.BlockSpec((1, C), const),
            pl.BlockSpec((1, C), const),
            pl.BlockSpec((C, C), const),
            pl.BlockSpec((D, D), const),
            pl.BlockSpec((tm, D), row),
            pl.BlockSpec((1, D), const),
            pl.BlockSpec((D, ROUTER_PAD), const),
            pl.BlockSpec((D, ROUTER_PAD), const),
            pl.BlockSpec((1, ROUTER_PAD), const),
            pl.BlockSpec((tm, tm), const),
        ],
        out_specs=[
            pl.BlockSpec((tm, D), row),
            pl.BlockSpec((tm, D), row),
            pl.BlockSpec((tm, ROUTER_PAD), row),
            pl.BlockSpec((SUBLANES, ROUTER_PAD), const),
        ],
        out_shape=[
            jax.ShapeDtypeStruct((T, D), F32),
            jax.ShapeDtypeStruct((T, D), BF16),
            jax.ShapeDtypeStruct((T, ROUTER_PAD), F32),
            jax.ShapeDtypeStruct((SUBLANES, ROUTER_PAD), F32),
        ],
        scratch_shapes=[pltpu.VMEM((SUBLANES, ROUTER_PAD), F32)],
        compiler_params=_params("arbitrary"),
        name="out_proj",
    )(attn2, yc2, u2, x02, hyena_d.reshape(1, C), hyena_gn.reshape(1, C), grp, w_out_bf, x2,
      g_ffn.reshape(1, D), rwh, rwl, rb, tri)


def _expert_kernel(be_ref, nused_ref, x_ref, wg_ref, bg_ref, wu_ref, bu_ref, wd_ref, bd_ref, *rest):
    y_ref, wg_bf, wu_bf, wd_bf = rest[-4:]
    i = pl.program_id(0)
    used = i < nused_ref[0]
    new_expert = jnp.logical_or(i == 0, be_ref[i] != be_ref[jnp.maximum(i - 1, 0)])

    @pl.when(jnp.logical_and(used, new_expert))
    def _():
        wg_bf[...] = wg_ref[0].astype(BF16)
        wu_bf[...] = wu_ref[0].astype(BF16)
        wd_bf[...] = wd_ref[0].astype(BF16)

    @pl.when(used)
    def _():
        x = x_ref[...]
        g = jnp.minimum(jnp.dot(x, wg_bf[...], preferred_element_type=F32) + bg_ref[0], SWIGLU_LIMIT)
        u = jnp.clip(jnp.dot(x, wu_bf[...], preferred_element_type=F32) + bu_ref[0],
                     -SWIGLU_LIMIT, SWIGLU_LIMIT)
        a = (u + 1.0) * (g * jax.nn.sigmoid(SWIGLU_ALPHA * g))
        y = jnp.dot(a.astype(BF16), wd_bf[...], preferred_element_type=F32) + bd_ref[0]
        y_ref[...] = y.astype(y_ref.dtype)

    @pl.when(jnp.logical_not(used))
    def _():
        y_ref[...] = jnp.zeros_like(y_ref)


def _experts(block_e, n_used, x_parts, wg, bg, wu, bu, wd, bd):
    Pp, D = x_parts[0].shape
    tm = EXPERT_TILE
    nb = Pp // tm
    P = Pp * len(x_parts)
    E, _, FF = wg.shape
    wmap = lambda i, be, nu: (be[i], 0, 0)
    y = None
    for part, xs in enumerate(x_parts):
        in_specs = [
            pl.BlockSpec((tm, D), lambda i, be, nu: (i, 0)),
            pl.BlockSpec((1, D, FF), wmap),
            pl.BlockSpec((1, 1, FF), wmap),
            pl.BlockSpec((1, D, FF), wmap),
            pl.BlockSpec((1, 1, FF), wmap),
            pl.BlockSpec((1, FF, D), wmap),
            pl.BlockSpec((1, 1, D), wmap),
        ]
        args = [block_e[part * nb:(part + 1) * nb], jnp.clip(n_used - part * nb, 0, nb), xs,
                wg, bg.reshape(E, 1, FF), wu, bu.reshape(E, 1, FF), wd, bd.reshape(E, 1, D)]
        aliases = {}
        if y is not None:
            in_specs.append(pl.BlockSpec(memory_space=pl.ANY))
            args.append(y)
            aliases = {len(args) - 1: 0}
        grid_spec = pltpu.PrefetchScalarGridSpec(
            num_scalar_prefetch=2,
            grid=(nb,),
            in_specs=in_specs,
            out_specs=pl.BlockSpec((tm, D), lambda i, be, nu, part=part: (i + part * nb, 0)),
            scratch_shapes=[pltpu.VMEM((D, FF), BF16), pltpu.VMEM((D, FF), BF16),
                            pltpu.VMEM((FF, D), BF16)],
        )
        y = pl.pallas_call(
            _expert_kernel,
            grid_spec=grid_spec,
            out_shape=jax.ShapeDtypeStruct((P, D), BF16),
            input_output_aliases=aliases,
            compiler_params=_params("arbitrary"),
            name=f"moe_experts_{part}",
        )(*args)
    return y


def _final_kernel(h_ref, yg_ref, rt_ref, p_ref, wp_ref, gp_ref, wg_ref, bg_ref, gfin_ref, *rest):
    o_ref = rest[-1]
    h = h_ref[...]
    for r in range(TOP_K):
        h = h + yg_ref[r].astype(F32) * rt_ref[:, 2 * TOP_K + r: 2 * TOP_K + r + 1]
    e = _rms(jnp.dot(p_ref[...].astype(BF16), wp_ref[...], preferred_element_type=F32),
             gp_ref[...], EPS)
    gate = jax.nn.sigmoid(jnp.dot(h.astype(BF16), wg_ref[...], preferred_element_type=F32)
                          + bg_ref[...])
    h = h + gate * e
    o_ref[...] = _rms(h, gfin_ref[...], EPS)


def _final(h1, yg_parts, route, p2, w_ple_bf, g_ple, w_gate_bf, b_gate, g_final):
    T, D = h1.shape
    tm = TOKEN_TILE
    PD = p2.shape[1]
    nb = T // tm // len(yg_parts)
    const = lambda i: (0, 0)
    out = None
    for part, yg in enumerate(yg_parts):
        row = lambda i, part=part: (i + part * nb, 0)
        in_specs = [
            pl.BlockSpec((tm, D), row),
            pl.BlockSpec((TOP_K, tm, D), lambda i: (0, i, 0)),
            pl.BlockSpec((tm, ROUTER_PAD), row),
            pl.BlockSpec((tm, PD), row),
            pl.BlockSpec((PD, D), const),
            pl.BlockSpec((1, D), const),
            pl.BlockSpec((D, D), const),
            pl.BlockSpec((1, D), const),
            pl.BlockSpec((1, D), const),
        ]
        args = [h1, yg, route, p2, w_ple_bf, g_ple.reshape(1, D), w_gate_bf, b_gate.reshape(1, D),
                g_final.reshape(1, D)]
        aliases = {}
        if out is not None:
            in_specs.append(pl.BlockSpec(memory_space=pl.ANY))
            args.append(out)
            aliases = {len(args) - 1: 0}
        out = pl.pallas_call(
            _final_kernel,
            grid=(nb,),
            in_specs=in_specs,
            out_specs=pl.BlockSpec((tm, D), row),
            out_shape=jax.ShapeDtypeStruct((T, D), F32),
            input_output_aliases=aliases,
            compiler_params=_params("parallel"),
            name=f"final_{part}",
        )(*args)
    return out


def _rope_tables(seq):
    d = DIFF_HEAD_DIM
    pos = jnp.arange(seq, dtype=F32)
    inv = ROPE_THETA ** (-jnp.arange(0, d, 2, dtype=F32) / d)
    ang = pos[:, None] * inv[None, :]
    cos, sin = jnp.cos(ang), jnp.sin(ang)
    z = jnp.zeros_like(sin)
    cos_t = jnp.tile(jnp.concatenate([cos, cos], -1), (1, LANES // d))
    s1_t = jnp.tile(jnp.concatenate([-sin, z], -1), (1, LANES // d))
    s2_t = jnp.tile(jnp.concatenate([z, sin], -1), (1, LANES // d))
    return cos_t, s1_t, s2_t


def _long_conv(u, hf_t, hb_t):
    B, L, C = u.shape
    assert 2 * L == FFT_N and B % 2 == 0
    P, R = B // 2, FFT_N1 // 2
    cst = _dft_constants()
    G, Cg = FFT_GROUP, C // FFT_GROUP

    def filter_slabs(h_t):
        return (h_t.reshape(Cg, G, R, FFT_N2).transpose(0, 2, 1, 3)
                .reshape(Cg, R, G * FFT_N2).astype(BF16))

    kf = _filter_spectrum(filter_slabs(hf_t), filter_slabs(hb_t), cst)
    x_slabs = (u.reshape(2, P, R, FFT_N2, Cg, G).transpose(1, 4, 0, 2, 5, 3)
               .reshape(P, Cg, FFT_N1, G * FFT_N2).astype(BF16))
    y = _fft_conv(x_slabs, kf, cst)
    return (y.reshape(P, Cg, 2, R, G, FFT_N2).transpose(2, 0, 3, 5, 1, 4).reshape(B, L, C))


def _dispatch_indices(route, cnt, T):
    tm = EXPERT_TILE
    A = T * TOP_K
    top_e = route[:, :TOP_K].astype(jnp.int32)
    rank = route[:, TOP_K:2 * TOP_K].astype(jnp.int32)
    counts = cnt[0, :N_EXPERTS].astype(jnp.int32)
    padded = (counts + tm - 1) // tm * tm
    pad_end = jnp.cumsum(padded)
    pad_start = pad_end - padded
    start = jnp.cumsum(counts) - counts
    pos = pad_start[top_e] + rank
    n_blocks = -(-A // tm) + N_EXPERTS
    P = n_blocks * tm
    block_first = jnp.arange(n_blocks, dtype=jnp.int32) * tm
    block_e = jnp.minimum(jnp.sum(pad_end[None, :] <= block_first[:, None], axis=1),
                          N_EXPERTS - 1).astype(jnp.int32)
    n_used = (pad_end[-1] // tm).astype(jnp.int32).reshape(1)
    tok = jnp.broadcast_to(jnp.arange(T, dtype=jnp.int32)[:, None], (T, TOP_K))
    _, sorted_tok = lax.sort_key_val(pos.reshape(A), tok.reshape(A))
    in_blk = jnp.arange(tm, dtype=jnp.int32)[None, :]
    r = (block_first - pad_start[block_e])[:, None] + in_blk
    compact = jnp.clip(start[block_e][:, None] + r, 0, A - 1)
    filler = (block_first[:, None] + in_blk) % T
    slot_tok = jnp.where(r < counts[block_e][:, None], sorted_tok[compact], filler).reshape(P)
    return pos.T, slot_tok, block_e, n_used


def kernel(x, p, g_mix, w_in, hyena_conv_w, hyena_conv_b, flt_w1, flt_b1, flt_w_inner, flt_b_inner, flt_freq, flt_w_out, hyena_d, hyena_gn, lambda_q1, lambda_k1, lambda_q2, lambda_k2, attn_subln, w_out, g_ffn, router_w, router_b, w_gate, b_gate, w_up, b_up, w_down, b_down, w_ple, g_ple, w_ple_gate, b_ple_gate, g_final):
    B, S, D = x.shape
    T = B * S
    i = 0
    x2 = x.reshape(T, D)

    cos_t, s1_t, s2_t = _rope_tables(S)
    q, k, vt, hy = _in_proj(x2, g_mix[i], w_in[i].astype(BF16), cos_t, s1_t, s2_t, S)

    lam = (jnp.exp(jnp.sum(lambda_q1[i] * lambda_k1[i])) - jnp.exp(jnp.sum(lambda_q2[i] * lambda_k2[i]))
           + LAMBDA_INIT).reshape(1).astype(F32)
    attn = _diff_attention(lam, q.reshape(B, S, -1), k.reshape(B, S, -1), vt, attn_subln[i])

    u, hx0 = _short_conv(hy.reshape(B, S, -1), hyena_conv_w[i], hyena_conv_b[i])
    h_fwd, h_bwd = _hyena_filters(S, flt_w1[i], flt_b1[i], flt_w_inner[i], flt_b_inner[i],
                                  flt_freq[i], flt_w_out[i])
    yc = _long_conv(u, h_fwd, h_bwd)

    h1, hn, route, cnt = _out_proj(attn.reshape(T, -1), yc.reshape(T, -1), u.reshape(T, -1),
                                   hx0.reshape(T, -1), hyena_d[i], hyena_gn[i],
                                   w_out[i].astype(BF16), x2, g_ffn[i], router_w[i], router_b[i])

    pos, slot_tok, block_e, n_used = _dispatch_indices(route, cnt, T)
    n_parts = MOE_OVERLAP_PARTS
    slot_parts = jnp.split(slot_tok, n_parts)
    x_parts = [hn[s] for s in slot_parts]
    y = _experts(block_e, n_used, x_parts, w_gate[i], b_gate[i], w_up[i], b_up[i],
                 w_down[i], b_down[i])
    pos_parts = jnp.split(pos, n_parts, axis=1)
    yg_parts = [y[pp.reshape(-1)].reshape(TOP_K, T // n_parts, D) for pp in pos_parts]

    out = _final(h1, yg_parts, route, p[i].reshape(T, -1), w_ple[i].astype(BF16), g_ple[i],
                 w_ple_gate[i].astype(BF16), b_ple_gate[i], g_final)
    return out.reshape(B, S, D)
```

```python
import functools
import math

import jax
import jax.numpy as jnp
from jax import lax
from jax.experimental import pallas as pl
from jax.experimental.pallas import tpu as pltpu

F32 = jnp.float32
BF16 = jnp.bfloat16

D_MODEL = 1024
ATTN_WIDTH = 512
HYENA_WIDTH = 512
DIFF_HEADS = 4
DIFF_HEAD_DIM = 64
DIFF_V_DIM = 128
HYENA_GROUPS = 8
FILTER_EMB = 33
FILTER_INNER = 2
FAST_DECAY_PCT = 0.3
SLOW_DECAY_PCT = 1.5
DECAY_TARGET = 1e-2
ROPE_THETA = 10000.0
N_EXPERTS = 32
TOP_K = 4
SWIGLU_ALPHA = 1.702
SWIGLU_LIMIT = 7.0
EPS = 1e-6
SUBLN_EPS = 1e-5
LAMBDA_INIT = 0.8 - 0.6 * math.exp(-0.3 * 0)

V7X_VMEM_LIMIT_BYTES = 56 * 1024 * 1024
LANES = 128
SUBLANES = 8

TOKEN_TILE = 512
ATTN_STREAM_W = 512
ATTN_Q_TILE = 1024
ATTN_ONES_ROWS = 16
ATTN_UNROLL = 4
ATTN_KV_TILE = 512
EXPERT_TILE = 512
ROUTER_PAD = LANES
OUTPROJ_TILE = 512
OUTPROJ_SUBTILES = 1
MOE_OVERLAP_PARTS = 2


def _params(*sem):
    return pltpu.CompilerParams(dimension_semantics=sem, vmem_limit_bytes=V7X_VMEM_LIMIT_BYTES)


def _rms(x, g, eps):
    return x * lax.rsqrt(jnp.mean(x * x, axis=-1, keepdims=True) + eps) * g


def _inproj_kernel(x_ref, g_ref, w_ref, wvt_ref, c_ref, s1_ref, s2_ref, q_ref, k_ref, vt_ref, hy_ref):
    a = _rms(x_ref[...], g_ref[...], EPS).astype(BF16)
    cos, s1, s2 = c_ref[...], s1_ref[...], s2_ref[...]

    def rope(t):
        return t * cos + pltpu.roll(t, LANES - 32, axis=1) * s1 + pltpu.roll(t, 32, axis=1) * s2

    q_scale = (DIFF_HEAD_DIM ** -0.5) * math.log2(math.e)
    qk = jnp.dot(a, w_ref[:, : 2 * ATTN_WIDTH], preferred_element_type=F32)
    for j in range(ATTN_WIDTH // LANES):
        sl = slice(j * LANES, (j + 1) * LANES)
        q_ref[:, sl] = (rope(qk[:, sl]) * q_scale).astype(BF16)
        k_ref[:, sl] = rope(qk[:, ATTN_WIDTH + j * LANES: ATTN_WIDTH + (j + 1) * LANES]).astype(BF16)
    vt_ref[0] = lax.dot_general(wvt_ref[...], a, (((1,), (1,)), ((), ())),
                                preferred_element_type=F32).astype(BF16)
    hy_ref[...] = jnp.dot(a, w_ref[:, 3 * ATTN_WIDTH:], preferred_element_type=F32).astype(hy_ref.dtype)


def _in_proj(x2, g_mix, w_in_bf, cos_t, s1_t, s2_t, seq):
    T, D = x2.shape
    tm = TOKEN_TILE
    nseq = seq // tm
    wvt = w_in_bf[:, 2 * ATTN_WIDTH: 3 * ATTN_WIDTH].T
    row = lambda i: (i, 0)
    const = lambda i: (0, 0)
    pos = lambda i: (i % nseq, 0)
    return pl.pallas_call(
        _inproj_kernel,
        grid=(T // tm,),
        in_specs=[
            pl.BlockSpec((tm, D), row),
            pl.BlockSpec((1, D), const),
            pl.BlockSpec(w_in_bf.shape, const),
            pl.BlockSpec(wvt.shape, const),
            pl.BlockSpec((tm, LANES), pos),
            pl.BlockSpec((tm, LANES), pos),
            pl.BlockSpec((tm, LANES), pos),
        ],
        out_specs=[
            pl.BlockSpec((tm, ATTN_WIDTH), row),
            pl.BlockSpec((tm, ATTN_WIDTH), row),
            pl.BlockSpec((1, ATTN_WIDTH, tm), lambda i: (i, 0, 0)),
            pl.BlockSpec((tm, 3 * HYENA_WIDTH), row),
        ],
        out_shape=[
            jax.ShapeDtypeStruct((T, ATTN_WIDTH), BF16),
            jax.ShapeDtypeStruct((T, ATTN_WIDTH), BF16),
            jax.ShapeDtypeStruct((T // tm, ATTN_WIDTH, tm), BF16),
            jax.ShapeDtypeStruct((T, 3 * HYENA_WIDTH), BF16),
        ],
        compiler_params=_params("parallel"),
        name="in_proj",
    )(x2, g_mix.reshape(1, D), w_in_bf, wvt, cos_t, s1_t, s2_t)


def _attn_kernel(lam_ref, q_ref, k_ref, vt_ref, g_ref, o_ref, s_ref, m_ref, acc_ref, *, kc):
    w = ATTN_STREAM_W
    nc = k_ref.shape[1] // kc
    n_groups = q_ref.shape[1] // w
    streams = [(grp, c) for grp in range(n_groups) for c in range(2)]

    def stream_q(grp, c):
        q = q_ref[0, grp * w:(grp + 1) * w, :]
        lane = lax.broadcasted_iota(jnp.int32, q.shape, 1)
        keep = (lane < DIFF_HEAD_DIM) if c == 0 else (lane >= DIFF_HEAD_DIM)
        return jnp.where(keep, q, jnp.zeros_like(q))

    def fold8(t, op):
        r = t[0:SUBLANES]
        for j in range(1, kc // SUBLANES):
            r = op(r, t[j * SUBLANES:(j + 1) * SUBLANES])
        return r

    def score_chunk(i, qc, j):
        rows = pl.ds(pl.multiple_of(j * kc, kc), kc)
        s = lax.dot_general(k_ref[0, rows, :], qc, (((1,), (1,)), ((), ())),
                            preferred_element_type=F32)
        s_ref[i, rows, :] = s
        m_ref[i] = jnp.maximum(m_ref[i], fold8(s, jnp.maximum))

    ones_rows = jnp.ones((ATTN_ONES_ROWS, kc), BF16)

    def prob_chunk(i, j, m):
        rows = pl.ds(pl.multiple_of(j * kc, kc), kc)
        p = jnp.exp2(s_ref[i, rows, :] - m)
        v_aug = jnp.concatenate([vt_ref[j], ones_rows], axis=0)
        acc_ref[...] += jnp.dot(v_aug, p.astype(BF16), preferred_element_type=F32)

    m_ref[...] = jnp.full(m_ref.shape, -jnp.inf, F32)
    q0 = stream_q(*streams[0])

    def first_body(j, carry):
        score_chunk(0, q0, j)
        return carry

    lax.fori_loop(0, nc, first_body, 0, unroll=ATTN_UNROLL)
    outs = []
    for i, (grp, c) in enumerate(streams):
        m = jnp.max(m_ref[i], axis=0, keepdims=True)
        acc_ref[...] = jnp.zeros_like(acc_ref)
        if i + 1 < len(streams):
            qn = stream_q(*streams[i + 1])

            def body(j, carry, i=i, m=m, qn=qn):
                score_chunk(i + 1, qn, j)
                prob_chunk(i, j, m)
                return carry
        else:
            def body(j, carry, i=i, m=m):
                prob_chunk(i, j, m)
                return carry

        lax.fori_loop(0, nc, body, 0, unroll=ATTN_UNROLL)
        outs.append(acc_ref[:DIFF_V_DIM, :] * (1.0 / acc_ref[DIFF_V_DIM:DIFF_V_DIM + 1, :]))
        if c == 1:
            o = outs[-2] - lam_ref[0] * outs[-1]
            o = o * lax.rsqrt(jnp.mean(o * o, axis=0, keepdims=True) + SUBLN_EPS)
            o = o * (g_ref[...] * (1.0 - LAMBDA_INIT))
            o_ref[0, grp * w:(grp + 1) * w, :] = o.T.astype(o_ref.dtype)


def _diff_attention(lam, q, k, vt, g_subln):
    B, S, _ = q.shape
    tq, kc = ATTN_Q_TILE, ATTN_KV_TILE
    nc = S // kc
    n_streams = 2 * (tq // ATTN_STREAM_W)
    return pl.pallas_call(
        functools.partial(_attn_kernel, kc=kc),
        grid=(B, DIFF_HEADS, S // tq),
        in_specs=[
            pl.BlockSpec(memory_space=pltpu.SMEM),
            pl.BlockSpec((1, tq, LANES), lambda b, h, i: (b, i, h)),
            pl.BlockSpec((1, S, LANES), lambda b, h, i: (b, 0, h)),
            pl.BlockSpec((nc, DIFF_V_DIM, kc), lambda b, h, i: (b, h, 0)),
            pl.BlockSpec((DIFF_V_DIM, 1), lambda b, h, i: (0, 0)),
        ],
        out_specs=pl.BlockSpec((1, tq, LANES), lambda b, h, i: (b, i, h)),
        out_shape=jax.ShapeDtypeStruct((B, S, ATTN_WIDTH), BF16),
        scratch_shapes=[pltpu.VMEM((n_streams, S, ATTN_STREAM_W), F32),
                        pltpu.VMEM((n_streams, SUBLANES, ATTN_STREAM_W), F32),
                        pltpu.VMEM((DIFF_V_DIM + ATTN_ONES_ROWS, ATTN_STREAM_W), F32)],
        compiler_params=_params("parallel", "parallel", "parallel"),
        name="diff_attn",
    )(lam, q, k, vt, g_subln.reshape(DIFF_V_DIM, 1))


def _shortconv_kernel(hy_ref, prev_ref, next_ref, w_ref, b_ref, u_ref, x0_ref):
    i = pl.program_id(1)
    last = pl.num_programs(1) - 1
    x = hy_ref[0].astype(F32)
    ts = x.shape[0]
    prev_row = jnp.where(i == 0, 0.0, prev_ref[0, HALO_ROWS - 1:HALO_ROWS, :].astype(F32))
    next_row = jnp.where(i == last, 0.0, next_ref[0, 0:1, :].astype(F32))
    row = lax.broadcasted_iota(jnp.int32, (ts, 1), 0)
    xm = jnp.where(row == 0, prev_row, pltpu.roll(x, 1, axis=0))
    xp = jnp.where(row == ts - 1, next_row, pltpu.roll(x, ts - 1, axis=0))
    y = b_ref[...] + xm * w_ref[0:1, :] + x * w_ref[1:2, :] + xp * w_ref[2:3, :]
    C = HYENA_WIDTH
    u_ref[0] = (y[:, :C] * y[:, 2 * C:]).astype(u_ref.dtype)
    x0_ref[0] = y[:, C:2 * C].astype(x0_ref.dtype)


HALO_ROWS = 16


def _short_conv(hy, conv_w, conv_b):
    B, S, C3 = hy.shape
    ts = TOKEN_TILE
    nb = ts // HALO_ROWS
    return pl.pallas_call(
        _shortconv_kernel,
        grid=(B, S // ts),
        in_specs=[
            pl.BlockSpec((1, ts, C3), lambda b, i: (b, i, 0)),
            pl.BlockSpec((1, HALO_ROWS, C3), lambda b, i: (b, jnp.maximum(i * nb - 1, 0), 0)),
            pl.BlockSpec((1, HALO_ROWS, C3), lambda b, i: (b, jnp.minimum((i + 1) * nb, S // HALO_ROWS - 1), 0)),
            pl.BlockSpec((3, C3), lambda b, i: (0, 0)),
            pl.BlockSpec((1, C3), lambda b, i: (0, 0)),
        ],
        out_specs=[
            pl.BlockSpec((1, ts, HYENA_WIDTH), lambda b, i: (b, i, 0)),
            pl.BlockSpec((1, ts, HYENA_WIDTH), lambda b, i: (b, i, 0)),
        ],
        out_shape=[
            jax.ShapeDtypeStruct((B, S, HYENA_WIDTH), BF16),
            jax.ShapeDtypeStruct((B, S, HYENA_WIDTH), BF16),
        ],
        compiler_params=_params("parallel", "parallel"),
        name="short_conv",
    )(hy, hy, hy, conv_w, conv_b.reshape(1, C3))


FFT_N = 8192
FFT_N1 = 64
FFT_N2 = 128
FFT_CH_BLOCK = 32
FFT_GROUP = 4
FFT_UNROLL = 2


def _dft_constants():
    import numpy as np
    n1, n2, n = FFT_N1, FFT_N2, FFT_N
    k1 = np.arange(n1)[:, None]
    t1 = np.arange(n1)[None, :]
    f1 = np.exp(-2j * np.pi * k1 * t1 / n1)
    f1h = f1[:, : n1 // 2]
    w1c = np.block([[f1h.real, -f1h.imag], [f1h.imag, f1h.real]])
    w1r = np.concatenate([f1h.real, f1h.imag], axis=0)
    t2 = np.arange(n2)[:, None]
    k2 = np.arange(n2)[None, :]
    f2 = np.exp(-2j * np.pi * t2 * k2 / n2)
    w2a = np.concatenate([f2.real, f2.imag], axis=1)
    w2b = np.concatenate([-f2.imag, f2.real], axis=1)
    g2 = np.conj(f2)
    w3a = np.concatenate([g2.real, g2.imag], axis=1)
    w3b = np.concatenate([-g2.imag, g2.real], axis=1)
    h = np.conj(f1).T[: n1 // 2] / n
    w4 = np.block([[h.real, -h.imag], [h.imag, h.real]])
    tw = np.exp(-2j * np.pi * np.arange(n1)[:, None] * np.arange(n2)[None, :] / n)
    bf = lambda a: jnp.asarray(a, dtype=F32).astype(BF16)
    return dict(w1c=bf(w1c), w1r=bf(w1r), w2a=bf(w2a), w2b=bf(w2b), w3a=bf(w3a), w3b=bf(w3b),
                w4=bf(w4), tc=jnp.asarray(np.tile(tw.real, (1, FFT_GROUP)), F32),
                ts=jnp.asarray(np.tile(tw.imag, (1, FFT_GROUP)), F32))


def _dft_forward(x_ref, w1_ref, tc_ref, ts_ref, w2a_ref, w2b_ref, ar_ref, ai_ref):
    ng = x_ref.shape[0]
    n1, n2 = FFT_N1, FFT_N2
    w1 = w1_ref[...]
    tc, ts = tc_ref[...], ts_ref[...]

    def body(g, carry):
        a = jnp.dot(w1, x_ref[g], preferred_element_type=F32)
        ar, ai = a[:n1], a[n1:]
        tr = (ar * tc - ai * ts).astype(BF16)
        ti = (ar * ts + ai * tc).astype(BF16)
        for cl in range(FFT_GROUP):
            rows = pl.ds(pl.multiple_of((g * FFT_GROUP + cl) * n1, n1), n1)
            ar_ref[rows, :] = tr[:, cl * n2:(cl + 1) * n2]
            ai_ref[rows, :] = ti[:, cl * n2:(cl + 1) * n2]
        return carry

    lax.fori_loop(0, ng, body, 0, unroll=FFT_UNROLL)
    return (jnp.dot(ar_ref[...], w2a_ref[...], preferred_element_type=F32)
            + jnp.dot(ai_ref[...], w2b_ref[...], preferred_element_type=F32))


def _spectrum_kernel(hf_ref, hb_ref, w1_ref, tc_ref, ts_ref, w2a_ref, w2b_ref, o_ref, ar_ref, ai_ref):
    n2 = FFT_N2
    f = _dft_forward(hf_ref, w1_ref, tc_ref, ts_ref, w2a_ref, w2b_ref, ar_ref, ai_ref)
    o_ref[...] = f.reshape(o_ref.shape)
    b = _dft_forward(hb_ref, w1_ref, tc_ref, ts_ref, w2a_ref, w2b_ref, ar_ref, ai_ref)
    b = b.reshape(o_ref.shape)
    o_ref[:, :, :n2] = o_ref[:, :, :n2] + b[:, :, :n2]
    o_ref[:, :, n2:] = o_ref[:, :, n2:] - b[:, :, n2:]


def _fftconv_kernel(x_ref, kf_ref, w1_ref, tc_ref, ts_ref, w2a_ref, w2b_ref, w3a_ref, w3b_ref,
                    w4_ref, y_ref, ar_ref, ai_ref, c_ref):
    ng = x_ref.shape[0]
    cb = ng * FFT_GROUP
    n1, n2 = FFT_N1, FFT_N2
    b = _dft_forward(x_ref, w1_ref, tc_ref, ts_ref, w2a_ref, w2b_ref, ar_ref, ai_ref)
    kf = kf_ref[...].reshape(cb * n1, 2 * n2)
    br, bi = b[:, :n2], b[:, n2:]
    kr, ki = kf[:, :n2], kf[:, n2:]
    ar_ref[...] = (br * kr - bi * ki).astype(BF16)
    ai_ref[...] = (br * ki + bi * kr).astype(BF16)
    c_ref[...] = (jnp.dot(ar_ref[...], w3a_ref[...], preferred_element_type=F32)
                  + jnp.dot(ai_ref[...], w3b_ref[...], preferred_element_type=F32))
    w4 = w4_ref[...]
    tc, ts = tc_ref[...], ts_ref[...]

    def body(g, carry):
        rows = [pl.ds(pl.multiple_of((g * FFT_GROUP + cl) * n1, n1), n1) for cl in range(FFT_GROUP)]
        cr = jnp.concatenate([c_ref[r, :n2] for r in rows], axis=1)
        ci = jnp.concatenate([c_ref[r, n2:] for r in rows], axis=1)
        dr = (cr * tc + ci * ts).astype(BF16)
        di = (ci * tc - cr * ts).astype(BF16)
        y_ref[g] = (jnp.dot(w4[:, :n1], dr, preferred_element_type=F32)
                    + jnp.dot(w4[:, n1:], di, preferred_element_type=F32)).astype(y_ref.dtype)
        return carry

    lax.fori_loop(0, ng, body, 0, unroll=FFT_UNROLL)


def _const_spec(a):
    nd = a.ndim
    return pl.BlockSpec(a.shape, lambda *_: (0,) * nd)


def _filter_spectrum(hf_slabs, hb_slabs, cst):
    C = hf_slabs.shape[0] * FFT_GROUP
    cb = FFT_CH_BLOCK
    consts = [cst["w1r"], cst["tc"], cst["ts"], cst["w2a"], cst["w2b"]]
    slab = pl.BlockSpec((cb // FFT_GROUP, FFT_N1 // 2, FFT_GROUP * FFT_N2), lambda i: (i, 0, 0))
    return pl.pallas_call(
        _spectrum_kernel,
        grid=(C // cb,),
        in_specs=[slab, slab] + [_const_spec(a) for a in consts],
        out_specs=pl.BlockSpec((cb, FFT_N1, 2 * FFT_N2), lambda i: (i, 0, 0)),
        out_shape=jax.ShapeDtypeStruct((C, FFT_N1, 2 * FFT_N2), F32),
        scratch_shapes=[pltpu.VMEM((cb * FFT_N1, FFT_N2), BF16), pltpu.VMEM((cb * FFT_N1, FFT_N2), BF16)],
        compiler_params=_params("parallel"),
        name="filter_spectrum",
    )(hf_slabs, hb_slabs, *consts)


FILTER_TIME_TILE = 512


def _filter_kernel(zt_ref, w1t_ref, b1_ref, wit_ref, bi_ref, fr_ref, wot_ref, dec_ref, hf_ref, hb_ref):
    hp = lax.Precision.HIGHEST
    h = jnp.sin(fr_ref[0] * (jnp.dot(w1t_ref[...], zt_ref[...], precision=hp,
                                     preferred_element_type=F32) + b1_ref[...]))
    for j in range(FILTER_INNER):
        h = jnp.sin(fr_ref[j + 1] * (jnp.dot(wit_ref[j], h, precision=hp,
                                             preferred_element_type=F32) + bi_ref[j]))
    o = jnp.dot(wot_ref[...], h, precision=hp, preferred_element_type=F32)
    dec = dec_ref[...]
    hf_ref[...] = o[:HYENA_WIDTH] * dec
    hb_ref[...] = o[HYENA_WIDTH:] * dec


def _hyena_filters(seq, w1, b1, w_inner, b_inner, freq, w_out):
    C = HYENA_WIDTH
    order = w1.shape[1]
    pos = jnp.arange(seq, dtype=F32)
    t = pos / (seq - 1)
    bands = (FILTER_EMB - 1) // 2
    f = jnp.linspace(1e-4, bands - 1, bands, dtype=F32)
    fw = ((2.0 * math.pi / seq) * pos)[:, None] * f[None, :]
    z = jnp.concatenate([t[:, None], jnp.cos(fw), -jnp.sin(fw)], axis=-1)
    zt = jnp.zeros((LANES, seq), F32).at[:FILTER_EMB].set(z.T)
    w1t = jnp.zeros((order, LANES), F32).at[:, :FILTER_EMB].set(w1.T)
    max_decay = math.log(DECAY_TARGET) / FAST_DECAY_PCT
    min_decay = math.log(DECAY_TARGET) / SLOW_DECAY_PCT
    deltas = jnp.abs(jnp.linspace(min_decay, max_decay, C, dtype=F32))
    dec_t = jnp.exp(-deltas[:, None] * t[None, :])
    tt = FILTER_TIME_TILE
    lane_blk = lambda r: pl.BlockSpec((r, tt), lambda i: (0, i))
    args = [zt, w1t, b1.reshape(order, 1), jnp.swapaxes(w_inner, 1, 2),
            b_inner.reshape(FILTER_INNER, order, 1), freq.reshape(FILTER_INNER + 1, order, 1),
            w_out.T, dec_t]
    return pl.pallas_call(
        _filter_kernel,
        grid=(seq // tt,),
        in_specs=[lane_blk(LANES)] + [_const_spec(a) for a in args[1:7]] + [lane_blk(C)],
        out_specs=[lane_blk(C), lane_blk(C)],
        out_shape=[jax.ShapeDtypeStruct((C, seq), F32), jax.ShapeDtypeStruct((C, seq), F32)],
        compiler_params=_params("parallel"),
        name="hyena_filters",
    )(*args)


def _fft_conv(x_slabs, kf, cst):
    P = x_slabs.shape[0]
    C = x_slabs.shape[1] * FFT_GROUP
    cb = FFT_CH_BLOCK
    gb, gl = cb // FFT_GROUP, FFT_GROUP * FFT_N2
    consts = [cst["w1c"], cst["tc"], cst["ts"], cst["w2a"], cst["w2b"], cst["w3a"], cst["w3b"], cst["w4"]]
    return pl.pallas_call(
        _fftconv_kernel,
        grid=(C // cb, P),
        in_specs=[pl.BlockSpec((None, gb, FFT_N1, gl), lambda i, p: (p, i, 0, 0)),
                  pl.BlockSpec((cb, FFT_N1, 2 * FFT_N2), lambda i, p: (i, 0, 0))]
                 + [_const_spec(a) for a in consts],
        out_specs=pl.BlockSpec((None, gb, FFT_N1, gl), lambda i, p: (p, i, 0, 0)),
        out_shape=jax.ShapeDtypeStruct((P, C // FFT_GROUP, FFT_N1, gl), BF16),
        scratch_shapes=[pltpu.VMEM((cb * FFT_N1, FFT_N2), BF16), pltpu.VMEM((cb * FFT_N1, FFT_N2), BF16),
                        pltpu.VMEM((cb * FFT_N1, 2 * FFT_N2), F32)],
        compiler_params=_params("parallel", "arbitrary"),
        name="fft_conv",
    )(x_slabs, kf, *consts)


def _split_bf16(x):
    hi = x.astype(BF16)
    lo = (x - hi.astype(F32)).astype(BF16)
    return hi, lo


def _outproj_kernel(attn_ref, yc_ref, u_ref, x0_ref, d_ref, gn_ref, grp_ref, wo_ref, x_ref,
                    gf_ref, rwh_ref, rwl_ref, rb_ref, tri_ref, h_ref, hn_ref, rt_ref, cnt_out_ref,
                    cnt_ref):
    @pl.when(pl.program_id(0) == 0)
    def _():
        cnt_ref[...] = jnp.zeros_like(cnt_ref)

    tm = x_ref.shape[0]
    sub = tm // OUTPROJ_SUBTILES
    grp = grp_ref[...]
    gsz = HYENA_WIDTH // HYENA_GROUPS

    def sub_tile(r0, cnt_row):
        rows = slice(r0, r0 + sub)
        z = ((yc_ref[rows, :].astype(F32) + u_ref[rows, :].astype(F32) * d_ref[...])
             * x0_ref[rows, :].astype(F32))
        zh, zl = _split_bf16(z * z)
        ssq = (jnp.dot(zh, grp, preferred_element_type=F32)
               + jnp.dot(zl, grp, preferred_element_type=F32))
        hy_out = (z * lax.rsqrt(ssq * (1.0 / gsz) + EPS) * gn_ref[...]).astype(BF16)
        mix = (jnp.dot(attn_ref[rows, :], wo_ref[:ATTN_WIDTH, :], preferred_element_type=F32)
               + jnp.dot(hy_out, wo_ref[ATTN_WIDTH:, :], preferred_element_type=F32))
        h = x_ref[rows, :] + mix
        h_ref[rows, :] = h
        hn = _rms(h, gf_ref[...], EPS)
        hn_ref[rows, :] = hn.astype(BF16)
        nh, nl = _split_bf16(hn)
        logits = (jnp.dot(nh, rwh_ref[...], preferred_element_type=F32)
                  + jnp.dot(nl, rwh_ref[...], preferred_element_type=F32)
                  + jnp.dot(nh, rwl_ref[...], preferred_element_type=F32)) + rb_ref[...]

        lane = lax.broadcasted_iota(jnp.int32, logits.shape, 1)
        work = logits
        top_val, top_idx, top_hot = [], [], []
        for _ in range(TOP_K):
            m = jnp.max(work, axis=-1, keepdims=True)
            idx = jnp.min(jnp.where(work == m, lane, ROUTER_PAD), axis=-1, keepdims=True)
            hot = lane == idx
            top_val.append(m)
            top_idx.append(idx)
            top_hot.append(hot)
            work = jnp.where(hot, -jnp.inf, work)
        ex = [jnp.exp(v - top_val[0]) for v in top_val]
        inv_den = 1.0 / (ex[0] + ex[1] + ex[2] + ex[3])
        sel = jnp.zeros(logits.shape, F32)
        for hot in top_hot:
            sel = sel + hot.astype(F32)
        before = jnp.dot(tri_ref[...], sel.astype(BF16), preferred_element_type=F32) + cnt_row
        packed = jnp.zeros(logits.shape, F32)
        for r, hot in enumerate(top_hot):
            rank_r = jnp.sum(jnp.where(hot, before, 0.0), axis=-1, keepdims=True)
            packed = jnp.where(lane == r, top_idx[r].astype(F32), packed)
            packed = jnp.where(lane == TOP_K + r, rank_r, packed)
            packed = jnp.where(lane == 2 * TOP_K + r, ex[r] * inv_den, packed)
        rt_ref[rows, :] = packed
        return cnt_row + jnp.sum(sel, axis=0, keepdims=True)

    cnt_row = cnt_ref[0:1, :]
    for t in range(OUTPROJ_SUBTILES):
        cnt_row = sub_tile(t * sub, cnt_row)
    cnt_ref[...] = jnp.broadcast_to(cnt_row, cnt_ref.shape)
    cnt_out_ref[...] = cnt_ref[...]


def _out_proj(attn2, yc2, u2, x02, hyena_d, hyena_gn, w_out_bf, x2, g_ffn, router_w, router_b):
    T, D = x2.shape
    tm = OUTPROJ_TILE
    sub = tm // OUTPROJ_SUBTILES
    C = HYENA_WIDTH
    gid = jnp.arange(C) // (C // HYENA_GROUPS)
    grp = (gid[:, None] == gid[None, :]).astype(BF16)
    rw = jnp.zeros((D, ROUTER_PAD), F32).at[:, :N_EXPERTS].set(router_w)
    rwh, rwl = _split_bf16(rw)
    rb = jnp.full((1, ROUTER_PAD), -jnp.inf, F32).at[0, :N_EXPERTS].set(router_b)
    tri = (jnp.arange(sub)[:, None] > jnp.arange(sub)[None, :]).astype(BF16)
    row = lambda i: (i, 0)
    const = lambda i: (0, 0)
    return pl.pallas_call(
        _outproj_kernel,
        grid=(T // tm,),
        in_specs=[
            pl.BlockSpec((tm, ATTN_WIDTH), row),
            pl.BlockSpec((tm, C), row),
            pl.BlockSpec((tm, C), row),
            pl.BlockSpec((tm, C), row),
            pl.BlockSpec((1, C), const),
            pl.BlockSpec((1, C), const),
            pl.BlockSpec((C, C), const),
            pl.BlockSpec((D, D), const),
            pl.BlockSpec((tm, D), row),
            pl.BlockSpec((1, D), const),
            pl.BlockSpec((D, ROUTER_PAD), const),
            pl.BlockSpec((D, ROUTER_PAD), const),
            pl.BlockSpec((1, ROUTER_PAD), const),
            pl.BlockSpec((sub, sub), const),
        ],
        out_specs=[
            pl.BlockSpec((tm, D), row),
            pl.BlockSpec((tm, D), row),
            pl.BlockSpec((tm, ROUTER_PAD), row),
            pl.BlockSpec((SUBLANES, ROUTER_PAD), const),
        ],
        out_shape=[
            jax.ShapeDtypeStruct((T, D), F32),
            jax.ShapeDtypeStruct((T, D), BF16),
            jax.ShapeDtypeStruct((T, ROUTER_PAD), F32),
            jax.ShapeDtypeStruct((SUBLANES, ROUTER_PAD), F32),
        ],
        scratch_shapes=[pltpu.VMEM((SUBLANES, ROUTER_PAD), F32)],
        compiler_params=_params("arbitrary"),
        name="out_proj",
    )(attn2, yc2, u2, x02, hyena_d.reshape(1, C), hyena_gn.reshape(1, C), grp, w_out_bf, x2,
      g_ffn.reshape(1, D), rwh, rwl, rb, tri)


def _expert_kernel(be_ref, nused_ref, x_ref, wg_ref, bg_ref, wu_ref, bu_ref, wd_ref, bd_ref, *rest):
    y_ref, wg_bf, wu_bf, wd_bf = rest[-4:]
    i = pl.program_id(0)
    used = i < nused_ref[0]
    new_expert = jnp.logical_or(i == 0, be_ref[i] != be_ref[jnp.maximum(i - 1, 0)])

    @pl.when(jnp.logical_and(used, new_expert))
    def _():
        wg_bf[...] = wg_ref[0].astype(BF16)
        wu_bf[...] = wu_ref[0].astype(BF16)
        wd_bf[...] = wd_ref[0].astype(BF16)

    @pl.when(used)
    def _():
        x = x_ref[...]
        g = jnp.minimum(jnp.dot(x, wg_bf[...], preferred_element_type=F32) + bg_ref[0], SWIGLU_LIMIT)
        u = jnp.clip(jnp.dot(x, wu_bf[...], preferred_element_type=F32) + bu_ref[0],
                     -SWIGLU_LIMIT, SWIGLU_LIMIT)
        a = (u + 1.0) * (g * jax.nn.sigmoid(SWIGLU_ALPHA * g))
        y = jnp.dot(a.astype(BF16), wd_bf[...], preferred_element_type=F32) + bd_ref[0]
        y_ref[...] = y.astype(y_ref.dtype)

    @pl.when(jnp.logical_not(used))
    def _():
        y_ref[...] = jnp.zeros_like(y_ref)


def _experts(block_e, n_used, x_parts, wg, bg, wu, bu, wd, bd):
    Pp, D = x_parts[0].shape
    tm = EXPERT_TILE
    nb = Pp // tm
    P = Pp * len(x_parts)
    E, _, FF = wg.shape
    wmap = lambda i, be, nu: (be[i], 0, 0)
    y = None
    for part, xs in enumerate(x_parts):
        in_specs = [
            pl.BlockSpec((tm, D), lambda i, be, nu: (i, 0)),
            pl.BlockSpec((1, D, FF), wmap),
            pl.BlockSpec((1, 1, FF), wmap),
            pl.BlockSpec((1, D, FF), wmap),
            pl.BlockSpec((1, 1, FF), wmap),
            pl.BlockSpec((1, FF, D), wmap),
            pl.BlockSpec((1, 1, D), wmap),
        ]
        args = [block_e[part * nb:(part + 1) * nb], jnp.clip(n_used - part * nb, 0, nb), xs,
                wg, bg.reshape(E, 1, FF), wu, bu.reshape(E, 1, FF), wd, bd.reshape(E, 1, D)]
        aliases = {}
        if y is not None:
            in_specs.append(pl.BlockSpec(memory_space=pl.ANY))
            args.append(y)
            aliases = {len(args) - 1: 0}
        grid_spec = pltpu.PrefetchScalarGridSpec(
            num_scalar_prefetch=2,
            grid=(nb,),
            in_specs=in_specs,
            out_specs=pl.BlockSpec((tm, D), lambda i, be, nu, part=part: (i + part * nb, 0)),
            scratch_shapes=[pltpu.VMEM((D, FF), BF16), pltpu.VMEM((D, FF), BF16),
                            pltpu.VMEM((FF, D), BF16)],
        )
        y = pl.pallas_call(
            _expert_kernel,
            grid_spec=grid_spec,
            out_shape=jax.ShapeDtypeStruct((P, D), BF16),
            input_output_aliases=aliases,
            compiler_params=_params("arbitrary"),
            name=f"moe_experts_{part}",
        )(*args)
    return y


def _final_kernel(h_ref, yg_ref, rt_ref, p_ref, wp_ref, gp_ref, wg_ref, bg_ref, gfin_ref, *rest):
    o_ref = rest[-1]
    h = h_ref[...]
    for r in range(TOP_K):
        h = h + yg_ref[r].astype(F32) * rt_ref[:, 2 * TOP_K + r: 2 * TOP_K + r + 1]
    e = _rms(jnp.dot(p_ref[...].astype(BF16), wp_ref[...], preferred_element_type=F32),
             gp_ref[...], EPS)
    gate = jax.nn.sigmoid(jnp.dot(h.astype(BF16), wg_ref[...], preferred_element_type=F32)
                          + bg_ref[...])
    h = h + gate * e
    o_ref[...] = _rms(h, gfin_ref[...], EPS)


def _final(h1, yg_parts, route, p2, w_ple_bf, g_ple, w_gate_bf, b_gate, g_final):
    T, D = h1.shape
    tm = TOKEN_TILE
    PD = p2.shape[1]
    nb = T // tm // len(yg_parts)
    const = lambda i: (0, 0)
    out = None
    for part, yg in enumerate(yg_parts):
        row = lambda i, part=part: (i + part * nb, 0)
        in_specs = [
            pl.BlockSpec((tm, D), row),
            pl.BlockSpec((TOP_K, tm, D), lambda i: (0, i, 0)),
            pl.BlockSpec((tm, ROUTER_PAD), row),
            pl.BlockSpec((tm, PD), row),
            pl.BlockSpec((PD, D), const),
            pl.BlockSpec((1, D), const),
            pl.BlockSpec((D, D), const),
            pl.BlockSpec((1, D), const),
            pl.BlockSpec((1, D), const),
        ]
        args = [h1, yg, route, p2, w_ple_bf, g_ple.reshape(1, D), w_gate_bf, b_gate.reshape(1, D),
                g_final.reshape(1, D)]
        aliases = {}
        if out is not None:
            in_specs.append(pl.BlockSpec(memory_space=pl.ANY))
            args.append(out)
            aliases = {len(args) - 1: 0}
        out = pl.pallas_call(
            _final_kernel,
            grid=(nb,),
            in_specs=in_specs,
            out_specs=pl.BlockSpec((tm, D), row),
            out_shape=jax.ShapeDtypeStruct((T, D), F32),
            input_output_aliases=aliases,
            compiler_params=_params("parallel"),
            name=f"final_{part}",
        )(*args)
    return out


def _rope_tables(seq):
    d = DIFF_HEAD_DIM
    pos = jnp.arange(seq, dtype=F32)
    inv = ROPE_THETA ** (-jnp.arange(0, d, 2, dtype=F32) / d)
    ang = pos[:, None] * inv[None, :]
    cos, sin = jnp.cos(ang), jnp.sin(ang)
    z = jnp.zeros_like(sin)
    cos_t = jnp.tile(jnp.concatenate([cos, cos], -1), (1, LANES // d))
    s1_t = jnp.tile(jnp.concatenate([-sin, z], -1), (1, LANES // d))
    s2_t = jnp.tile(jnp.concatenate([z, sin], -1), (1, LANES // d))
    return cos_t, s1_t, s2_t


def _long_conv(u, hf_t, hb_t):
    B, L, C = u.shape
    assert 2 * L == FFT_N and B % 2 == 0
    P, R = B // 2, FFT_N1 // 2
    cst = _dft_constants()
    G, Cg = FFT_GROUP, C // FFT_GROUP

    def filter_slabs(h_t):
        return (h_t.reshape(Cg, G, R, FFT_N2).transpose(0, 2, 1, 3)
                .reshape(Cg, R, G * FFT_N2).astype(BF16))

    kf = _filter_spectrum(filter_slabs(hf_t), filter_slabs(hb_t), cst)
    x_slabs = (u.reshape(2, P, R, FFT_N2, Cg, G).transpose(1, 4, 0, 2, 5, 3)
               .reshape(P, Cg, FFT_N1, G * FFT_N2).astype(BF16))
    y = _fft_conv(x_slabs, kf, cst)
    return (y.reshape(P, Cg, 2, R, G, FFT_N2).transpose(2, 0, 3, 5, 1, 4).reshape(B, L, C))


def _dispatch_indices(route, cnt, T):
    tm = EXPERT_TILE
    A = T * TOP_K
    top_e = route[:, :TOP_K].astype(jnp.int32)
    rank = route[:, TOP_K:2 * TOP_K].astype(jnp.int32)
    counts = cnt[0, :N_EXPERTS].astype(jnp.int32)
    padded = (counts + tm - 1) // tm * tm
    pad_end = jnp.cumsum(padded)
    pad_start = pad_end - padded
    start = jnp.cumsum(counts) - counts
    pos = pad_start[top_e] + rank
    n_blocks = -(-A // tm) + N_EXPERTS
    P = n_blocks * tm
    block_first = jnp.arange(n_blocks, dtype=jnp.int32) * tm
    block_e = jnp.minimum(jnp.sum(pad_end[None, :] <= block_first[:, None], axis=1),
                          N_EXPERTS - 1).astype(jnp.int32)
    n_used = (pad_end[-1] // tm).astype(jnp.int32).reshape(1)
    tok = jnp.broadcast_to(jnp.arange(T, dtype=jnp.int32)[:, None], (T, TOP_K))
    _, sorted_tok = lax.sort_key_val(pos.reshape(A), tok.reshape(A))
    in_blk = jnp.arange(tm, dtype=jnp.int32)[None, :]
    r = (block_first - pad_start[block_e])[:, None] + in_blk
    compact = jnp.clip(start[block_e][:, None] + r, 0, A - 1)
    filler = (block_first[:, None] + in_blk) % T
    slot_tok = jnp.where(r < counts[block_e][:, None], sorted_tok[compact], filler).reshape(P)
    return pos.T, slot_tok, block_e, n_used


def kernel(x, p, g_mix, w_in, hyena_conv_w, hyena_conv_b, flt_w1, flt_b1, flt_w_inner, flt_b_inner, flt_freq, flt_w_out, hyena_d, hyena_gn, lambda_q1, lambda_k1, lambda_q2, lambda_k2, attn_subln, w_out, g_ffn, router_w, router_b, w_gate, b_gate, w_up, b_up, w_down, b_down, w_ple, g_ple, w_ple_gate, b_ple_gate, g_final):
    B, S, D = x.shape
    T = B * S
    i = 0
    x2 = x.reshape(T, D)

    cos_t, s1_t, s2_t = _rope_tables(S)
    q, k, vt, hy = _in_proj(x2, g_mix[i], w_in[i].astype(BF16), cos_t, s1_t, s2_t, S)

    lam = (jnp.exp(jnp.sum(lambda_q1[i] * lambda_k1[i])) - jnp.exp(jnp.sum(lambda_q2[i] * lambda_k2[i]))
           + LAMBDA_INIT).reshape(1).astype(F32)
    attn = _diff_attention(lam, q.reshape(B, S, -1), k.reshape(B, S, -1), vt, attn_subln[i])

    u, hx0 = _short_conv(hy.reshape(B, S, -1), hyena_conv_w[i], hyena_conv_b[i])
    h_fwd, h_bwd = _hyena_filters(S, flt_w1[i], flt_b1[i], flt_w_inner[i], flt_b_inner[i],
                                  flt_freq[i], flt_w_out[i])
    yc = _long_conv(u, h_fwd, h_bwd)

    h1, hn, route, cnt = _out_proj(attn.reshape(T, -1), yc.reshape(T, -1), u.reshape(T, -1),
                                   hx0.reshape(T, -1), hyena_d[i], hyena_gn[i],
                                   w_out[i].astype(BF16), x2, g_ffn[i], router_w[i], router_b[i])

    pos, slot_tok, block_e, n_used = _dispatch_indices(route, cnt, T)
    n_parts = MOE_OVERLAP_PARTS
    slot_parts = jnp.split(slot_tok, n_parts)
    x_parts = [hn[s] for s in slot_parts]
    y = _experts(block_e, n_used, x_parts, w_gate[i], b_gate[i], w_up[i], b_up[i],
                 w_down[i], b_down[i])
    pos_parts = jnp.split(pos, n_parts, axis=1)
    yg_parts = [y[pp.reshape(-1)].reshape(TOP_K, T // n_parts, D) for pp in pos_parts]

    out = _final(h1, yg_parts, route, p[i].reshape(T, -1), w_ple[i].astype(BF16), g_ple[i],
                 w_ple_gate[i].astype(BF16), b_ple_gate[i], g_final)
    return out.reshape(B, S, D)
```

```python
import functools
import math

import jax
import jax.numpy as jnp
from jax import lax
from jax.experimental import pallas as pl
from jax.experimental.pallas import tpu as pltpu

F32 = jnp.float32
BF16 = jnp.bfloat16

D_MODEL = 1024
ATTN_WIDTH = 512
HYENA_WIDTH = 512
DIFF_HEADS = 4
DIFF_HEAD_DIM = 64
DIFF_V_DIM = 128
HYENA_GROUPS = 8
FILTER_EMB = 33
FILTER_INNER = 2
FAST_DECAY_PCT = 0.3
SLOW_DECAY_PCT = 1.5
DECAY_TARGET = 1e-2
ROPE_THETA = 10000.0
N_EXPERTS = 32
TOP_K = 4
SWIGLU_ALPHA = 1.702
SWIGLU_LIMIT = 7.0
EPS = 1e-6
SUBLN_EPS = 1e-5
LAMBDA_INIT = 0.8 - 0.6 * math.exp(-0.3 * 0)

V7X_VMEM_LIMIT_BYTES = 56 * 1024 * 1024
LANES = 128
SUBLANES = 8

TOKEN_TILE = 512
ATTN_STREAM_W = 512
ATTN_Q_TILE = 1024
ATTN_ONES_ROWS = 16
ATTN_UNROLL = 4
ATTN_KV_TILE = 512
EXPERT_TILE = 512
ROUTER_PAD = LANES
OUTPROJ_TILE = 512
OUTPROJ_SUBTILES = 1
MOE_OVERLAP_PARTS = 2


def _params(*sem):
    return pltpu.CompilerParams(dimension_semantics=sem, vmem_limit_bytes=V7X_VMEM_LIMIT_BYTES)


def _rms(x, g, eps):
    return x * lax.rsqrt(jnp.mean(x * x, axis=-1, keepdims=True) + eps) * g


def _inproj_kernel(x_ref, g_ref, w_ref, wvt_ref, c_ref, s1_ref, s2_ref, q_ref, k_ref, vt_ref, hy_ref):
    a = _rms(x_ref[...], g_ref[...], EPS).astype(BF16)
    cos, s1, s2 = c_ref[...], s1_ref[...], s2_ref[...]

    def rope(t):
        return t * cos + pltpu.roll(t, LANES - 32, axis=1) * s1 + pltpu.roll(t, 32, axis=1) * s2

    q_scale = (DIFF_HEAD_DIM ** -0.5) * math.log2(math.e)
    qk = jnp.dot(a, w_ref[:, : 2 * ATTN_WIDTH], preferred_element_type=F32)
    for j in range(ATTN_WIDTH // LANES):
        sl = slice(j * LANES, (j + 1) * LANES)
        q_ref[:, sl] = (rope(qk[:, sl]) * q_scale).astype(BF16)
        k_ref[:, sl] = rope(qk[:, ATTN_WIDTH + j * LANES: ATTN_WIDTH + (j + 1) * LANES]).astype(BF16)
    vt_ref[0] = lax.dot_general(wvt_ref[...], a, (((1,), (1,)), ((), ())),
                                preferred_element_type=F32).astype(BF16)
    hy_ref[...] = jnp.dot(a, w_ref[:, 3 * ATTN_WIDTH:], preferred_element_type=F32).astype(hy_ref.dtype)


def _in_proj(x2, g_mix, w_in_bf, cos_t, s1_t, s2_t, seq):
    T, D = x2.shape
    tm = TOKEN_TILE
    nseq = seq // tm
    wvt = w_in_bf[:, 2 * ATTN_WIDTH: 3 * ATTN_WIDTH].T
    row = lambda i: (i, 0)
    const = lambda i: (0, 0)
    pos = lambda i: (i % nseq, 0)
    return pl.pallas_call(
        _inproj_kernel,
        grid=(T // tm,),
        in_specs=[
            pl.BlockSpec((tm, D), row),
            pl.BlockSpec((1, D), const),
            pl.BlockSpec(w_in_bf.shape, const),
            pl.BlockSpec(wvt.shape, const),
            pl.BlockSpec((tm, LANES), pos),
            pl.BlockSpec((tm, LANES), pos),
            pl.BlockSpec((tm, LANES), pos),
        ],
        out_specs=[
            pl.BlockSpec((tm, ATTN_WIDTH), row),
            pl.BlockSpec((tm, ATTN_WIDTH), row),
            pl.BlockSpec((1, ATTN_WIDTH, tm), lambda i: (i, 0, 0)),
            pl.BlockSpec((tm, 3 * HYENA_WIDTH), row),
        ],
        out_shape=[
            jax.ShapeDtypeStruct((T, ATTN_WIDTH), BF16),
            jax.ShapeDtypeStruct((T, ATTN_WIDTH), BF16),
            jax.ShapeDtypeStruct((T // tm, ATTN_WIDTH, tm), BF16),
            jax.ShapeDtypeStruct((T, 3 * HYENA_WIDTH), BF16),
        ],
        compiler_params=_params("parallel"),
        name="in_proj",
    )(x2, g_mix.reshape(1, D), w_in_bf, wvt, cos_t, s1_t, s2_t)


def _attn_kernel(lam_ref, q_ref, k_ref, vt_ref, g_ref, o_ref, s_ref, m_ref, acc_ref, *, kc):
    w = ATTN_STREAM_W
    nc = k_ref.shape[1] // kc
    n_groups = q_ref.shape[1] // w
    streams = [(grp, c) for grp in range(n_groups) for c in range(2)]

    def stream_q(grp, c):
        q = q_ref[0, grp * w:(grp + 1) * w, :]
        lane = lax.broadcasted_iota(jnp.int32, q.shape, 1)
        keep = (lane < DIFF_HEAD_DIM) if c == 0 else (lane >= DIFF_HEAD_DIM)
        return jnp.where(keep, q, jnp.zeros_like(q))

    def fold8(t, op):
        r = t[0:SUBLANES]
        for j in range(1, kc // SUBLANES):
            r = op(r, t[j * SUBLANES:(j + 1) * SUBLANES])
        return r

    def score_chunk(i, qc, j):
        rows = pl.ds(pl.multiple_of(j * kc, kc), kc)
        s = lax.dot_general(k_ref[0, rows, :], qc, (((1,), (1,)), ((), ())),
                            preferred_element_type=F32)
        s_ref[i, rows, :] = s
        m_ref[i] = jnp.maximum(m_ref[i], fold8(s, jnp.maximum))

    ones_rows = jnp.ones((ATTN_ONES_ROWS, kc), BF16)

    def prob_chunk(i, j, m):
        rows = pl.ds(pl.multiple_of(j * kc, kc), kc)
        p = jnp.exp2(s_ref[i, rows, :] - m)
        v_aug = jnp.concatenate([vt_ref[j], ones_rows], axis=0)
        acc_ref[...] += jnp.dot(v_aug, p.astype(BF16), preferred_element_type=F32)

    m_ref[...] = jnp.full(m_ref.shape, -jnp.inf, F32)
    q0 = stream_q(*streams[0])

    def first_body(j, carry):
        score_chunk(0, q0, j)
        return carry

    lax.fori_loop(0, nc, first_body, 0, unroll=ATTN_UNROLL)
    outs = []
    for i, (grp, c) in enumerate(streams):
        m = jnp.max(m_ref[i], axis=0, keepdims=True)
        acc_ref[...] = jnp.zeros_like(acc_ref)
        if i + 1 < len(streams):
            qn = stream_q(*streams[i + 1])

            def body(j, carry, i=i, m=m, qn=qn):
                score_chunk(i + 1, qn, j)
                prob_chunk(i, j, m)
                return carry
        else:
            def body(j, carry, i=i, m=m):
                prob_chunk(i, j, m)
                return carry

        lax.fori_loop(0, nc, body, 0, unroll=ATTN_UNROLL)
        outs.append(acc_ref[:DIFF_V_DIM, :] * (1.0 / acc_ref[DIFF_V_DIM:DIFF_V_DIM + 1, :]))
        if c == 1:
            o = outs[-2] - lam_ref[0] * outs[-1]
            o = o * lax.rsqrt(jnp.mean(o * o, axis=0, keepdims=True) + SUBLN_EPS)
            o = o * (g_ref[...] * (1.0 - LAMBDA_INIT))
            o_ref[0, grp * w:(grp + 1) * w, :] = o.T.astype(o_ref.dtype)


def _diff_attention(lam, q, k, vt, g_subln):
    B, S, _ = q.shape
    tq, kc = ATTN_Q_TILE, ATTN_KV_TILE
    nc = S // kc
    n_streams = 2 * (tq // ATTN_STREAM_W)
    return pl.pallas_call(
        functools.partial(_attn_kernel, kc=kc),
        grid=(B, DIFF_HEADS, S // tq),
        in_specs=[
            pl.BlockSpec(memory_space=pltpu.SMEM),
            pl.BlockSpec((1, tq, LANES), lambda b, h, i: (b, i, h)),
            pl.BlockSpec((1, S, LANES), lambda b, h, i: (b, 0, h)),
            pl.BlockSpec((nc, DIFF_V_DIM, kc), lambda b, h, i: (b, h, 0)),
            pl.BlockSpec((DIFF_V_DIM, 1), lambda b, h, i: (0, 0)),
        ],
        out_specs=pl.BlockSpec((1, tq, LANES), lambda b, h, i: (b, i, h)),
        out_shape=jax.ShapeDtypeStruct((B, S, ATTN_WIDTH), BF16),
        scratch_shapes=[pltpu.VMEM((n_streams, S, ATTN_STREAM_W), F32),
                        pltpu.VMEM((n_streams, SUBLANES, ATTN_STREAM_W), F32),
                        pltpu.VMEM((DIFF_V_DIM + ATTN_ONES_ROWS, ATTN_STREAM_W), F32)],
        compiler_params=_params("parallel", "parallel", "parallel"),
        name="diff_attn",
    )(lam, q, k, vt, g_subln.reshape(DIFF_V_DIM, 1))


def _shortconv_kernel(hy_ref, prev_ref, next_ref, w_ref, b_ref, u_ref, x0_ref):
    i = pl.program_id(1)
    last = pl.num_programs(1) - 1
    x = hy_ref[0].astype(F32)
    ts = x.shape[0]
    prev_row = jnp.where(i == 0, 0.0, prev_ref[0, HALO_ROWS - 1:HALO_ROWS, :].astype(F32))
    next_row = jnp.where(i == last, 0.0, next_ref[0, 0:1, :].astype(F32))
    row = lax.broadcasted_iota(jnp.int32, (ts, 1), 0)
    xm = jnp.where(row == 0, prev_row, pltpu.roll(x, 1, axis=0))
    xp = jnp.where(row == ts - 1, next_row, pltpu.roll(x, ts - 1, axis=0))
    y = b_ref[...] + xm * w_ref[0:1, :] + x * w_ref[1:2, :] + xp * w_ref[2:3, :]
    C = HYENA_WIDTH
    u_ref[0] = (y[:, :C] * y[:, 2 * C:]).astype(u_ref.dtype)
    x0_ref[0] = y[:, C:2 * C].astype(x0_ref.dtype)


HALO_ROWS = 16


def _short_conv(hy, conv_w, conv_b):
    B, S, C3 = hy.shape
    ts = TOKEN_TILE
    nb = ts // HALO_ROWS
    return pl.pallas_call(
        _shortconv_kernel,
        grid=(B, S // ts),
        in_specs=[
            pl.BlockSpec((1, ts, C3), lambda b, i: (b, i, 0)),
            pl.BlockSpec((1, HALO_ROWS, C3), lambda b, i: (b, jnp.maximum(i * nb - 1, 0), 0)),
            pl.BlockSpec((1, HALO_ROWS, C3), lambda b, i: (b, jnp.minimum((i + 1) * nb, S // HALO_ROWS - 1), 0)),
            pl.BlockSpec((3, C3), lambda b, i: (0, 0)),
            pl.BlockSpec((1, C3), lambda b, i: (0, 0)),
        ],
        out_specs=[
            pl.BlockSpec((1, ts, HYENA_WIDTH), lambda b, i: (b, i, 0)),
            pl.BlockSpec((1, ts, HYENA_WIDTH), lambda b, i: (b, i, 0)),
        ],
        out_shape=[
            jax.ShapeDtypeStruct((B, S, HYENA_WIDTH), BF16),
            jax.ShapeDtypeStruct((B, S, HYENA_WIDTH), BF16),
        ],
        compiler_params=_params("parallel", "parallel"),
        name="short_conv",
    )(hy, hy, hy, conv_w, conv_b.reshape(1, C3))


FFT_N = 8192
FFT_N1 = 64
FFT_N2 = 128
FFT_CH_BLOCK = 32
FFT_GROUP = 4
FFT_UNROLL = 2


def _dft_constants():
    import numpy as np
    n1, n2, n = FFT_N1, FFT_N2, FFT_N
    k1 = np.arange(n1)[:, None]
    t1 = np.arange(n1)[None, :]
    f1 = np.exp(-2j * np.pi * k1 * t1 / n1)
    f1h = f1[:, : n1 // 2]
    w1c = np.block([[f1h.real, -f1h.imag], [f1h.imag, f1h.real]])
    w1r = np.concatenate([f1h.real, f1h.imag], axis=0)
    t2 = np.arange(n2)[:, None]
    k2 = np.arange(n2)[None, :]
    f2 = np.exp(-2j * np.pi * t2 * k2 / n2)
    w2a = np.concatenate([f2.real, f2.imag], axis=1)
    w2b = np.concatenate([-f2.imag, f2.real], axis=1)
    g2 = np.conj(f2)
    w3a = np.concatenate([g2.real, g2.imag], axis=1)
    w3b = np.concatenate([-g2.imag, g2.real], axis=1)
    h = np.conj(f1).T[: n1 // 2] / n
    w4 = np.block([[h.real, -h.imag], [h.imag, h.real]])
    tw = np.exp(-2j * np.pi * np.arange(n1)[:, None] * np.arange(n2)[None, :] / n)
    bf = lambda a: jnp.asarray(a, dtype=F32).astype(BF16)
    return dict(w1c=bf(w1c), w1r=bf(w1r), w2a=bf(w2a), w2b=bf(w2b), w3a=bf(w3a), w3b=bf(w3b),
                w4=bf(w4), tc=jnp.asarray(np.tile(tw.real, (1, FFT_GROUP)), F32),
                ts=jnp.asarray(np.tile(tw.imag, (1, FFT_GROUP)), F32))


def _dft_forward(x_ref, w1_ref, tc_ref, ts_ref, w2a_ref, w2b_ref, ar_ref, ai_ref):
    ng = x_ref.shape[0] // FFT_GROUP
    n1, n2 = FFT_N1, FFT_N2
    w1 = w1_ref[...]
    tc, ts = tc_ref[...], ts_ref[...]

    def body(g, carry):
        xg = jnp.concatenate([x_ref[g * FFT_GROUP + cl] for cl in range(FFT_GROUP)], axis=1)
        a = jnp.dot(w1, xg, preferred_element_type=F32)
        ar, ai = a[:n1], a[n1:]
        tr = (ar * tc - ai * ts).astype(BF16)
        ti = (ar * ts + ai * tc).astype(BF16)
        for cl in range(FFT_GROUP):
            rows = pl.ds(pl.multiple_of((g * FFT_GROUP + cl) * n1, n1), n1)
            ar_ref[rows, :] = tr[:, cl * n2:(cl + 1) * n2]
            ai_ref[rows, :] = ti[:, cl * n2:(cl + 1) * n2]
        return carry

    lax.fori_loop(0, ng, body, 0, unroll=FFT_UNROLL)
    return (jnp.dot(ar_ref[...], w2a_ref[...], preferred_element_type=F32)
            + jnp.dot(ai_ref[...], w2b_ref[...], preferred_element_type=F32))


def _spectrum_kernel(hf_ref, hb_ref, w1_ref, tc_ref, ts_ref, w2a_ref, w2b_ref, o_ref, ar_ref, ai_ref):
    n2 = FFT_N2
    f = _dft_forward(hf_ref, w1_ref, tc_ref, ts_ref, w2a_ref, w2b_ref, ar_ref, ai_ref)
    o_ref[...] = f.reshape(o_ref.shape)
    b = _dft_forward(hb_ref, w1_ref, tc_ref, ts_ref, w2a_ref, w2b_ref, ar_ref, ai_ref)
    b = b.reshape(o_ref.shape)
    o_ref[:, :, :n2] = o_ref[:, :, :n2] + b[:, :, :n2]
    o_ref[:, :, n2:] = o_ref[:, :, n2:] - b[:, :, n2:]


def _fftconv_kernel(x_ref, kf_ref, w1_ref, tc_ref, ts_ref, w2a_ref, w2b_ref, w3a_ref, w3b_ref,
                    w4_ref, y_ref, ar_ref, ai_ref, c_ref):
    cb = x_ref.shape[0]
    ng = cb // FFT_GROUP
    n1, n2 = FFT_N1, FFT_N2
    b = _dft_forward(x_ref, w1_ref, tc_ref, ts_ref, w2a_ref, w2b_ref, ar_ref, ai_ref)
    kf = kf_ref[...].reshape(cb * n1, 2 * n2)
    br, bi = b[:, :n2], b[:, n2:]
    kr, ki = kf[:, :n2], kf[:, n2:]
    ar_ref[...] = (br * kr - bi * ki).astype(BF16)
    ai_ref[...] = (br * ki + bi * kr).astype(BF16)
    c_ref[...] = (jnp.dot(ar_ref[...], w3a_ref[...], preferred_element_type=F32)
                  + jnp.dot(ai_ref[...], w3b_ref[...], preferred_element_type=F32))
    w4 = w4_ref[...]
    tc, ts = tc_ref[...], ts_ref[...]

    def body(g, carry):
        rows = [pl.ds(pl.multiple_of((g * FFT_GROUP + cl) * n1, n1), n1) for cl in range(FFT_GROUP)]
        cr = jnp.concatenate([c_ref[r, :n2] for r in rows], axis=1)
        ci = jnp.concatenate([c_ref[r, n2:] for r in rows], axis=1)
        dr = (cr * tc + ci * ts).astype(BF16)
        di = (ci * tc - cr * ts).astype(BF16)
        yg = (jnp.dot(w4[:, :n1], dr, preferred_element_type=F32)
              + jnp.dot(w4[:, n1:], di, preferred_element_type=F32)).astype(y_ref.dtype)
        for cl in range(FFT_GROUP):
            y_ref[g * FFT_GROUP + cl] = yg[:, cl * n2:(cl + 1) * n2]
        return carry

    lax.fori_loop(0, ng, body, 0, unroll=FFT_UNROLL)


def _const_spec(a):
    nd = a.ndim
    return pl.BlockSpec(a.shape, lambda *_: (0,) * nd)


def _filter_spectrum(hf_slabs, hb_slabs, cst):
    C = hf_slabs.shape[0]
    cb = FFT_CH_BLOCK
    consts = [cst["w1r"], cst["tc"], cst["ts"], cst["w2a"], cst["w2b"]]
    slab = pl.BlockSpec((cb, FFT_N1 // 2, FFT_N2), lambda i: (i, 0, 0))
    return pl.pallas_call(
        _spectrum_kernel,
        grid=(C // cb,),
        in_specs=[slab, slab] + [_const_spec(a) for a in consts],
        out_specs=pl.BlockSpec((cb, FFT_N1, 2 * FFT_N2), lambda i: (i, 0, 0)),
        out_shape=jax.ShapeDtypeStruct((C, FFT_N1, 2 * FFT_N2), F32),
        scratch_shapes=[pltpu.VMEM((cb * FFT_N1, FFT_N2), BF16), pltpu.VMEM((cb * FFT_N1, FFT_N2), BF16)],
        compiler_params=_params("parallel"),
        name="filter_spectrum",
    )(hf_slabs, hb_slabs, *consts)


FILTER_TIME_TILE = 512


def _filter_kernel(zt_ref, w1t_ref, b1_ref, wit_ref, bi_ref, fr_ref, wot_ref, dec_ref, hf_ref, hb_ref):
    hp = lax.Precision.HIGHEST
    h = jnp.sin(fr_ref[0] * (jnp.dot(w1t_ref[...], zt_ref[...], precision=hp,
                                     preferred_element_type=F32) + b1_ref[...]))
    for j in range(FILTER_INNER):
        h = jnp.sin(fr_ref[j + 1] * (jnp.dot(wit_ref[j], h, precision=hp,
                                             preferred_element_type=F32) + bi_ref[j]))
    o = jnp.dot(wot_ref[...], h, precision=hp, preferred_element_type=F32)
    dec = dec_ref[...]
    hf_ref[...] = o[:HYENA_WIDTH] * dec
    hb_ref[...] = o[HYENA_WIDTH:] * dec


def _hyena_filters(seq, w1, b1, w_inner, b_inner, freq, w_out):
    C = HYENA_WIDTH
    order = w1.shape[1]
    pos = jnp.arange(seq, dtype=F32)
    t = pos / (seq - 1)
    bands = (FILTER_EMB - 1) // 2
    f = jnp.linspace(1e-4, bands - 1, bands, dtype=F32)
    fw = ((2.0 * math.pi / seq) * pos)[:, None] * f[None, :]
    z = jnp.concatenate([t[:, None], jnp.cos(fw), -jnp.sin(fw)], axis=-1)
    zt = jnp.zeros((LANES, seq), F32).at[:FILTER_EMB].set(z.T)
    w1t = jnp.zeros((order, LANES), F32).at[:, :FILTER_EMB].set(w1.T)
    max_decay = math.log(DECAY_TARGET) / FAST_DECAY_PCT
    min_decay = math.log(DECAY_TARGET) / SLOW_DECAY_PCT
    deltas = jnp.abs(jnp.linspace(min_decay, max_decay, C, dtype=F32))
    dec_t = jnp.exp(-deltas[:, None] * t[None, :])
    tt = FILTER_TIME_TILE
    lane_blk = lambda r: pl.BlockSpec((r, tt), lambda i: (0, i))
    args = [zt, w1t, b1.reshape(order, 1), jnp.swapaxes(w_inner, 1, 2),
            b_inner.reshape(FILTER_INNER, order, 1), freq.reshape(FILTER_INNER + 1, order, 1),
            w_out.T, dec_t]
    return pl.pallas_call(
        _filter_kernel,
        grid=(seq // tt,),
        in_specs=[lane_blk(LANES)] + [_const_spec(a) for a in args[1:7]] + [lane_blk(C)],
        out_specs=[lane_blk(C), lane_blk(C)],
        out_shape=[jax.ShapeDtypeStruct((C, seq), F32), jax.ShapeDtypeStruct((C, seq), F32)],
        compiler_params=_params("parallel"),
        name="hyena_filters",
    )(*args)


def _fft_conv(x_slabs, kf, cst):
    P, C = x_slabs.shape[:2]
    cb = FFT_CH_BLOCK
    consts = [cst["w1c"], cst["tc"], cst["ts"], cst["w2a"], cst["w2b"], cst["w3a"], cst["w3b"], cst["w4"]]
    return pl.pallas_call(
        _fftconv_kernel,
        grid=(C // cb, P),
        in_specs=[pl.BlockSpec((None, cb, FFT_N1, FFT_N2), lambda i, p: (p, i, 0, 0)),
                  pl.BlockSpec((cb, FFT_N1, 2 * FFT_N2), lambda i, p: (i, 0, 0))]
                 + [_const_spec(a) for a in consts],
        out_specs=pl.BlockSpec((None, cb, FFT_N1, FFT_N2), lambda i, p: (p, i, 0, 0)),
        out_shape=jax.ShapeDtypeStruct((P, C, FFT_N1, FFT_N2), BF16),
        scratch_shapes=[pltpu.VMEM((cb * FFT_N1, FFT_N2), BF16), pltpu.VMEM((cb * FFT_N1, FFT_N2), BF16),
                        pltpu.VMEM((cb * FFT_N1, 2 * FFT_N2), F32)],
        compiler_params=_params("parallel", "arbitrary"),
        name="fft_conv",
    )(x_slabs, kf, *consts)


def _split_bf16(x):
    hi = x.astype(BF16)
    lo = (x - hi.astype(F32)).astype(BF16)
    return hi, lo


def _outproj_kernel(attn_ref, yc_ref, u_ref, x0_ref, d_ref, gn_ref, grp_ref, wo_ref, x_ref,
                    gf_ref, rwh_ref, rwl_ref, rb_ref, tri_ref, h_ref, hn_ref, rt_ref, cnt_out_ref,
                    cnt_ref):
    @pl.when(pl.program_id(0) == 0)
    def _():
        cnt_ref[...] = jnp.zeros_like(cnt_ref)

    tm = x_ref.shape[0]
    sub = tm // OUTPROJ_SUBTILES
    grp = grp_ref[...]
    gsz = HYENA_WIDTH // HYENA_GROUPS

    def sub_tile(r0, cnt_row):
        rows = slice(r0, r0 + sub)
        z = ((yc_ref[rows, :].astype(F32) + u_ref[rows, :].astype(F32) * d_ref[...])
             * x0_ref[rows, :].astype(F32))
        zh, zl = _split_bf16(z * z)
        ssq = (jnp.dot(zh, grp, preferred_element_type=F32)
               + jnp.dot(zl, grp, preferred_element_type=F32))
        hy_out = (z * lax.rsqrt(ssq * (1.0 / gsz) + EPS) * gn_ref[...]).astype(BF16)
        mix = (jnp.dot(attn_ref[rows, :], wo_ref[:ATTN_WIDTH, :], preferred_element_type=F32)
               + jnp.dot(hy_out, wo_ref[ATTN_WIDTH:, :], preferred_element_type=F32))
        h = x_ref[rows, :] + mix
        h_ref[rows, :] = h
        hn = _rms(h, gf_ref[...], EPS)
        hn_ref[rows, :] = hn.astype(BF16)
        nh, nl = _split_bf16(hn)
        logits = (jnp.dot(nh, rwh_ref[...], preferred_element_type=F32)
                  + jnp.dot(nl, rwh_ref[...], preferred_element_type=F32)
                  + jnp.dot(nh, rwl_ref[...], preferred_element_type=F32)) + rb_ref[...]

        lane = lax.broadcasted_iota(jnp.int32, logits.shape, 1)
        work = logits
        top_val, top_idx, top_hot = [], [], []
        for _ in range(TOP_K):
            m = jnp.max(work, axis=-1, keepdims=True)
            idx = jnp.min(jnp.where(work == m, lane, ROUTER_PAD), axis=-1, keepdims=True)
            hot = lane == idx
            top_val.append(m)
            top_idx.append(idx)
            top_hot.append(hot)
            work = jnp.where(hot, -jnp.inf, work)
        ex = [jnp.exp(v - top_val[0]) for v in top_val]
        inv_den = 1.0 / (ex[0] + ex[1] + ex[2] + ex[3])
        sel = jnp.zeros(logits.shape, F32)
        for hot in top_hot:
            sel = sel + hot.astype(F32)
        before = jnp.dot(tri_ref[...], sel.astype(BF16), preferred_element_type=F32) + cnt_row
        packed = jnp.zeros(logits.shape, F32)
        for r, hot in enumerate(top_hot):
            rank_r = jnp.sum(jnp.where(hot, before, 0.0), axis=-1, keepdims=True)
            packed = jnp.where(lane == r, top_idx[r].astype(F32), packed)
            packed = jnp.where(lane == TOP_K + r, rank_r, packed)
            packed = jnp.where(lane == 2 * TOP_K + r, ex[r] * inv_den, packed)
        rt_ref[rows, :] = packed
        return cnt_row + jnp.sum(sel, axis=0, keepdims=True)

    cnt_row = cnt_ref[0:1, :]
    for t in range(OUTPROJ_SUBTILES):
        cnt_row = sub_tile(t * sub, cnt_row)
    cnt_ref[...] = jnp.broadcast_to(cnt_row, cnt_ref.shape)
    cnt_out_ref[...] = cnt_ref[...]


def _out_proj(attn2, yc2, u2, x02, hyena_d, hyena_gn, w_out_bf, x2, g_ffn, router_w, router_b):
    T, D = x2.shape
    tm = OUTPROJ_TILE
    sub = tm // OUTPROJ_SUBTILES
    C = HYENA_WIDTH
    gid = jnp.arange(C) // (C // HYENA_GROUPS)
    grp = (gid[:, None] == gid[None, :]).astype(BF16)
    rw = jnp.zeros((D, ROUTER_PAD), F32).at[:, :N_EXPERTS].set(router_w)
    rwh, rwl = _split_bf16(rw)
    rb = jnp.full((1, ROUTER_PAD), -jnp.inf, F32).at[0, :N_EXPERTS].set(router_b)
    tri = (jnp.arange(sub)[:, None] > jnp.arange(sub)[None, :]).astype(BF16)
    row = lambda i: (i, 0)
    const = lambda i: (0, 0)
    return pl.pallas_call(
        _outproj_kernel,
        grid=(T // tm,),
        in_specs=[
            pl.BlockSpec((tm, ATTN_WIDTH), row),
            pl.BlockSpec((tm, C), row),
            pl.BlockSpec((tm, C), row),
            pl.BlockSpec((tm, C), row),
            pl.BlockSpec((1, C), const),
            pl.BlockSpec((1, C), const),
            pl.BlockSpec((C, C), const),
            pl.BlockSpec((D, D), const),
            pl.BlockSpec((tm, D), row),
            pl.BlockSpec((1, D), const),
            pl.BlockSpec((D, ROUTER_PAD), const),
            pl.BlockSpec((D, ROUTER_PAD), const),
            pl.BlockSpec((1, ROUTER_PAD), const),
            pl.BlockSpec((sub, sub), const),
        ],
        out_specs=[
            pl.BlockSpec((tm, D), row),
            pl.BlockSpec((tm, D), row),
            pl.BlockSpec((tm, ROUTER_PAD), row),
            pl.BlockSpec((SUBLANES, ROUTER_PAD), const),
        ],
        out_shape=[
            jax.ShapeDtypeStruct((T, D), F32),
            jax.ShapeDtypeStruct((T, D), BF16),
            jax.ShapeDtypeStruct((T, ROUTER_PAD), F32),
            jax.ShapeDtypeStruct((SUBLANES, ROUTER_PAD), F32),
        ],
        scratch_shapes=[pltpu.VMEM((SUBLANES, ROUTER_PAD), F32)],
        compiler_params=_params("arbitrary"),
        name="out_proj",
    )(attn2, yc2, u2, x02, hyena_d.reshape(1, C), hyena_gn.reshape(1, C), grp, w_out_bf, x2,
      g_ffn.reshape(1, D), rwh, rwl, rb, tri)


def _expert_kernel(be_ref, nused_ref, x_ref, wg_ref, bg_ref, wu_ref, bu_ref, wd_ref, bd_ref, *rest):
    y_ref, wg_bf, wu_bf, wd_bf = rest[-4:]
    i = pl.program_id(0)
    used = i < nused_ref[0]
    new_expert = jnp.logical_or(i == 0, be_ref[i] != be_ref[jnp.maximum(i - 1, 0)])

    @pl.when(jnp.logical_and(used, new_expert))
    def _():
        wg_bf[...] = wg_ref[0].astype(BF16)
        wu_bf[...] = wu_ref[0].astype(BF16)
        wd_bf[...] = wd_ref[0].astype(BF16)

    @pl.when(used)
    def _():
        x = x_ref[...]
        g = jnp.minimum(jnp.dot(x, wg_bf[...], preferred_element_type=F32) + bg_ref[0], SWIGLU_LIMIT)
        u = jnp.clip(jnp.dot(x, wu_bf[...], preferred_element_type=F32) + bu_ref[0],
                     -SWIGLU_LIMIT, SWIGLU_LIMIT)
        a = (u + 1.0) * (g * jax.nn.sigmoid(SWIGLU_ALPHA * g))
        y = jnp.dot(a.astype(BF16), wd_bf[...], preferred_element_type=F32) + bd_ref[0]
        y_ref[...] = y.astype(y_ref.dtype)

    @pl.when(jnp.logical_not(used))
    def _():
        y_ref[...] = jnp.zeros_like(y_ref)


def _experts(block_e, n_used, x_parts, wg, bg, wu, bu, wd, bd):
    Pp, D = x_parts[0].shape
    tm = EXPERT_TILE
    nb = Pp // tm
    P = Pp * len(x_parts)
    E, _, FF = wg.shape
    wmap = lambda i, be, nu: (be[i], 0, 0)
    y = None
    for part, xs in enumerate(x_parts):
        in_specs = [
            pl.BlockSpec((tm, D), lambda i, be, nu: (i, 0)),
            pl.BlockSpec((1, D, FF), wmap),
            pl.BlockSpec((1, 1, FF), wmap),
            pl.BlockSpec((1, D, FF), wmap),
            pl.BlockSpec((1, 1, FF), wmap),
            pl.BlockSpec((1, FF, D), wmap),
            pl.BlockSpec((1, 1, D), wmap),
        ]
        args = [block_e[part * nb:(part + 1) * nb], jnp.clip(n_used - part * nb, 0, nb), xs,
                wg, bg.reshape(E, 1, FF), wu, bu.reshape(E, 1, FF), wd, bd.reshape(E, 1, D)]
        aliases = {}
        if y is not None:
            in_specs.append(pl.BlockSpec(memory_space=pl.ANY))
            args.append(y)
            aliases = {len(args) - 1: 0}
        grid_spec = pltpu.PrefetchScalarGridSpec(
            num_scalar_prefetch=2,
            grid=(nb,),
            in_specs=in_specs,
            out_specs=pl.BlockSpec((tm, D), lambda i, be, nu, part=part: (i + part * nb, 0)),
            scratch_shapes=[pltpu.VMEM((D, FF), BF16), pltpu.VMEM((D, FF), BF16),
                            pltpu.VMEM((FF, D), BF16)],
        )
        y = pl.pallas_call(
            _expert_kernel,
            grid_spec=grid_spec,
            out_shape=jax.ShapeDtypeStruct((P, D), BF16),
            input_output_aliases=aliases,
            compiler_params=_params("arbitrary"),
            name=f"moe_experts_{part}",
        )(*args)
    return y


def _final_kernel(h_ref, yg_ref, rt_ref, p_ref, wp_ref, gp_ref, wg_ref, bg_ref, gfin_ref, *rest):
    o_ref = rest[-1]
    h = h_ref[...]
    for r in range(TOP_K):
        h = h + yg_ref[r].astype(F32) * rt_ref[:, 2 * TOP_K + r: 2 * TOP_K + r + 1]
    e = _rms(jnp.dot(p_ref[...].astype(BF16), wp_ref[...], preferred_element_type=F32),
             gp_ref[...], EPS)
    gate = jax.nn.sigmoid(jnp.dot(h.astype(BF16), wg_ref[...], preferred_element_type=F32)
                          + bg_ref[...])
    h = h + gate * e
    o_ref[...] = _rms(h, gfin_ref[...], EPS)


def _final(h1, yg_parts, route, p2, w_ple_bf, g_ple, w_gate_bf, b_gate, g_final):
    T, D = h1.shape
    tm = TOKEN_TILE
    PD = p2.shape[1]
    nb = T // tm // len(yg_parts)
    const = lambda i: (0, 0)
    out = None
    for part, yg in enumerate(yg_parts):
        row = lambda i, part=part: (i + part * nb, 0)
        in_specs = [
            pl.BlockSpec((tm, D), row),
            pl.BlockSpec((TOP_K, tm, D), lambda i: (0, i, 0)),
            pl.BlockSpec((tm, ROUTER_PAD), row),
            pl.BlockSpec((tm, PD), row),
            pl.BlockSpec((PD, D), const),
            pl.BlockSpec((1, D), const),
            pl.BlockSpec((D, D), const),
            pl.BlockSpec((1, D), const),
            pl.BlockSpec((1, D), const),
        ]
        args = [h1, yg, route, p2, w_ple_bf, g_ple.reshape(1, D), w_gate_bf, b_gate.reshape(1, D),
                g_final.reshape(1, D)]
        aliases = {}
        if out is not None:
            in_specs.append(pl.BlockSpec(memory_space=pl.ANY))
            args.append(out)
            aliases = {len(args) - 1: 0}
        out = pl.pallas_call(
            _final_kernel,
            grid=(nb,),
            in_specs=in_specs,
            out_specs=pl.BlockSpec((tm, D), row),
            out_shape=jax.ShapeDtypeStruct((T, D), F32),
            input_output_aliases=aliases,
            compiler_params=_params("parallel"),
            name=f"final_{part}",
        )(*args)
    return out


def _rope_tables(seq):
    d = DIFF_HEAD_DIM
    pos = jnp.arange(seq, dtype=F32)
    inv = ROPE_THETA ** (-jnp.arange(0, d, 2, dtype=F32) / d)
    ang = pos[:, None] * inv[None, :]
    cos, sin = jnp.cos(ang), jnp.sin(ang)
    z = jnp.zeros_like(sin)
    cos_t = jnp.tile(jnp.concatenate([cos, cos], -1), (1, LANES // d))
    s1_t = jnp.tile(jnp.concatenate([-sin, z], -1), (1, LANES // d))
    s2_t = jnp.tile(jnp.concatenate([z, sin], -1), (1, LANES // d))
    return cos_t, s1_t, s2_t


def _long_conv(u, hf_t, hb_t):
    B, L, C = u.shape
    assert 2 * L == FFT_N and B % 2 == 0
    P, R = B // 2, FFT_N1 // 2
    cst = _dft_constants()
    kf = _filter_spectrum(hf_t.reshape(C, R, FFT_N2).astype(BF16),
                          hb_t.reshape(C, R, FFT_N2).astype(BF16), cst)
    x_slabs = (u.reshape(2, P, R, FFT_N2, C).transpose(1, 4, 0, 2, 3)
               .reshape(P, C, FFT_N1, FFT_N2).astype(BF16))
    y = _fft_conv(x_slabs, kf, cst)
    return y.reshape(P, C, 2, R, FFT_N2).transpose(2, 0, 3, 4, 1).reshape(B, L, C)


def _dispatch_indices(route, cnt, T):
    tm = EXPERT_TILE
    A = T * TOP_K
    top_e = route[:, :TOP_K].astype(jnp.int32)
    rank = route[:, TOP_K:2 * TOP_K].astype(jnp.int32)
    counts = cnt[0, :N_EXPERTS].astype(jnp.int32)
    padded = (counts + tm - 1) // tm * tm
    pad_end = jnp.cumsum(padded)
    pad_start = pad_end - padded
    start = jnp.cumsum(counts) - counts
    pos = pad_start[top_e] + rank
    n_blocks = -(-A // tm) + N_EXPERTS
    P = n_blocks * tm
    block_first = jnp.arange(n_blocks, dtype=jnp.int32) * tm
    block_e = jnp.minimum(jnp.sum(pad_end[None, :] <= block_first[:, None], axis=1),
                          N_EXPERTS - 1).astype(jnp.int32)
    n_used = (pad_end[-1] // tm).astype(jnp.int32).reshape(1)
    tok = jnp.broadcast_to(jnp.arange(T, dtype=jnp.int32)[:, None], (T, TOP_K))
    _, sorted_tok = lax.sort_key_val(pos.reshape(A), tok.reshape(A))
    in_blk = jnp.arange(tm, dtype=jnp.int32)[None, :]
    r = (block_first - pad_start[block_e])[:, None] + in_blk
    compact = jnp.clip(start[block_e][:, None] + r, 0, A - 1)
    filler = (block_first[:, None] + in_blk) % T
    slot_tok = jnp.where(r < counts[block_e][:, None], sorted_tok[compact], filler).reshape(P)
    return pos.T, slot_tok, block_e, n_used


def kernel(x, p, g_mix, w_in, hyena_conv_w, hyena_conv_b, flt_w1, flt_b1, flt_w_inner, flt_b_inner, flt_freq, flt_w_out, hyena_d, hyena_gn, lambda_q1, lambda_k1, lambda_q2, lambda_k2, attn_subln, w_out, g_ffn, router_w, router_b, w_gate, b_gate, w_up, b_up, w_down, b_down, w_ple, g_ple, w_ple_gate, b_ple_gate, g_final):
    B, S, D = x.shape
    T = B * S
    i = 0
    x2 = x.reshape(T, D)

    cos_t, s1_t, s2_t = _rope_tables(S)
    q, k, vt, hy = _in_proj(x2, g_mix[i], w_in[i].astype(BF16), cos_t, s1_t, s2_t, S)

    lam = (jnp.exp(jnp.sum(lambda_q1[i] * lambda_k1[i])) - jnp.exp(jnp.sum(lambda_q2[i] * lambda_k2[i]))
           + LAMBDA_INIT).reshape(1).astype(F32)
    attn = _diff_attention(lam, q.reshape(B, S, -1), k.reshape(B, S, -1), vt, attn_subln[i])

    u, hx0 = _short_conv(hy.reshape(B, S, -1), hyena_conv_w[i], hyena_conv_b[i])
    h_fwd, h_bwd = _hyena_filters(S, flt_w1[i], flt_b1[i], flt_w_inner[i], flt_b_inner[i],
                                  flt_freq[i], flt_w_out[i])
    yc = _long_conv(u, h_fwd, h_bwd)

    h1, hn, route, cnt = _out_proj(attn.reshape(T, -1), yc.reshape(T, -1), u.reshape(T, -1),
                                   hx0.reshape(T, -1), hyena_d[i], hyena_gn[i],
                                   w_out[i].astype(BF16), x2, g_ffn[i], router_w[i], router_b[i])

    pos, slot_tok, block_e, n_used = _dispatch_indices(route, cnt, T)
    n_parts = MOE_OVERLAP_PARTS
    slot_parts = jnp.split(slot_tok, n_parts)
    x_parts = [hn[s] for s in slot_parts]
    y = _experts(block_e, n_used, x_parts, w_gate[i], b_gate[i], w_up[i], b_up[i],
                 w_down[i], b_down[i])
    pos_parts = jnp.split(pos, n_parts, axis=1)
    yg_parts = [y[pp.reshape(-1)].reshape(TOP_K, T // n_parts, D) for pp in pos_parts]

    out = _final(h1, yg_parts, route, p[i].reshape(T, -1), w_ple[i].astype(BF16), g_ple[i],
                 w_ple_gate[i].astype(BF16), b_ple_gate[i], g_final)
    return out.reshape(B, S, D)
```

```python
import functools
import math

import jax
import jax.numpy as jnp
from jax import lax
from jax.experimental import pallas as pl
from jax.experimental.pallas import tpu as pltpu

F32 = jnp.float32
BF16 = jnp.bfloat16

D_MODEL = 1024
ATTN_WIDTH = 512
HYENA_WIDTH = 512
DIFF_HEADS = 4
DIFF_HEAD_DIM = 64
DIFF_V_DIM = 128
HYENA_GROUPS = 8
FILTER_EMB = 33
FILTER_INNER = 2
FAST_DECAY_PCT = 0.3
SLOW_DECAY_PCT = 1.5
DECAY_TARGET = 1e-2
ROPE_THETA = 10000.0
N_EXPERTS = 32
TOP_K = 4
SWIGLU_ALPHA = 1.702
SWIGLU_LIMIT = 7.0
EPS = 1e-6
SUBLN_EPS = 1e-5
LAMBDA_INIT = 0.8 - 0.6 * math.exp(-0.3 * 0)

V7X_VMEM_LIMIT_BYTES = 56 * 1024 * 1024
LANES = 128
SUBLANES = 8

TOKEN_TILE = 512
ATTN_STREAM_W = 512
ATTN_Q_TILE = 1024
ATTN_ONES_ROWS = 16
ATTN_UNROLL = 4
ATTN_KV_TILE = 512
EXPERT_TILE = 512
ROUTER_PAD = LANES
OUTPROJ_TILE = 512
OUTPROJ_SUBTILES = 1
MOE_OVERLAP_PARTS = 4


def _params(*sem):
    return pltpu.CompilerParams(dimension_semantics=sem, vmem_limit_bytes=V7X_VMEM_LIMIT_BYTES)


def _rms(x, g, eps):
    return x * lax.rsqrt(jnp.mean(x * x, axis=-1, keepdims=True) + eps) * g


def _inproj_kernel(x_ref, g_ref, w_ref, wvt_ref, c_ref, s1_ref, s2_ref, q_ref, k_ref, vt_ref, hy_ref):
    a = _rms(x_ref[...], g_ref[...], EPS).astype(BF16)
    cos, s1, s2 = c_ref[...], s1_ref[...], s2_ref[...]

    def rope(t):
        return t * cos + pltpu.roll(t, LANES - 32, axis=1) * s1 + pltpu.roll(t, 32, axis=1) * s2

    q_scale = (DIFF_HEAD_DIM ** -0.5) * math.log2(math.e)
    qk = jnp.dot(a, w_ref[:, : 2 * ATTN_WIDTH], preferred_element_type=F32)
    for j in range(ATTN_WIDTH // LANES):
        sl = slice(j * LANES, (j + 1) * LANES)
        q_ref[:, sl] = (rope(qk[:, sl]) * q_scale).astype(BF16)
        k_ref[:, sl] = rope(qk[:, ATTN_WIDTH + j * LANES: ATTN_WIDTH + (j + 1) * LANES]).astype(BF16)
    vt_ref[0] = lax.dot_general(wvt_ref[...], a, (((1,), (1,)), ((), ())),
                                preferred_element_type=F32).astype(BF16)
    hy_ref[...] = jnp.dot(a, w_ref[:, 3 * ATTN_WIDTH:], preferred_element_type=F32).astype(hy_ref.dtype)


def _in_proj(x2, g_mix, w_in_bf, cos_t, s1_t, s2_t, seq):
    T, D = x2.shape
    tm = TOKEN_TILE
    nseq = seq // tm
    wvt = w_in_bf[:, 2 * ATTN_WIDTH: 3 * ATTN_WIDTH].T
    row = lambda i: (i, 0)
    const = lambda i: (0, 0)
    pos = lambda i: (i % nseq, 0)
    return pl.pallas_call(
        _inproj_kernel,
        grid=(T // tm,),
        in_specs=[
            pl.BlockSpec((tm, D), row),
            pl.BlockSpec((1, D), const),
            pl.BlockSpec(w_in_bf.shape, const),
            pl.BlockSpec(wvt.shape, const),
            pl.BlockSpec((tm, LANES), pos),
            pl.BlockSpec((tm, LANES), pos),
            pl.BlockSpec((tm, LANES), pos),
        ],
        out_specs=[
            pl.BlockSpec((tm, ATTN_WIDTH), row),
            pl.BlockSpec((tm, ATTN_WIDTH), row),
            pl.BlockSpec((1, ATTN_WIDTH, tm), lambda i: (i, 0, 0)),
            pl.BlockSpec((tm, 3 * HYENA_WIDTH), row),
        ],
        out_shape=[
            jax.ShapeDtypeStruct((T, ATTN_WIDTH), BF16),
            jax.ShapeDtypeStruct((T, ATTN_WIDTH), BF16),
            jax.ShapeDtypeStruct((T // tm, ATTN_WIDTH, tm), BF16),
            jax.ShapeDtypeStruct((T, 3 * HYENA_WIDTH), BF16),
        ],
        compiler_params=_params("parallel"),
        name="in_proj",
    )(x2, g_mix.reshape(1, D), w_in_bf, wvt, cos_t, s1_t, s2_t)


def _attn_kernel(lam_ref, q_ref, k_ref, vt_ref, g_ref, o_ref, s_ref, m_ref, acc_ref, *, kc):
    w = ATTN_STREAM_W
    nc = k_ref.shape[1] // kc
    n_groups = q_ref.shape[1] // w
    streams = [(grp, c) for grp in range(n_groups) for c in range(2)]

    def stream_q(grp, c):
        q = q_ref[0, grp * w:(grp + 1) * w, :]
        lane = lax.broadcasted_iota(jnp.int32, q.shape, 1)
        keep = (lane < DIFF_HEAD_DIM) if c == 0 else (lane >= DIFF_HEAD_DIM)
        return jnp.where(keep, q, jnp.zeros_like(q))

    def fold8(t, op):
        r = t[0:SUBLANES]
        for j in range(1, kc // SUBLANES):
            r = op(r, t[j * SUBLANES:(j + 1) * SUBLANES])
        return r

    def score_chunk(i, qc, j):
        rows = pl.ds(pl.multiple_of(j * kc, kc), kc)
        s = lax.dot_general(k_ref[0, rows, :], qc, (((1,), (1,)), ((), ())),
                            preferred_element_type=F32)
        s_ref[i, rows, :] = s
        m_ref[i] = jnp.maximum(m_ref[i], fold8(s, jnp.maximum))

    ones_rows = jnp.ones((ATTN_ONES_ROWS, kc), BF16)

    def prob_chunk(i, j, m):
        rows = pl.ds(pl.multiple_of(j * kc, kc), kc)
        p = jnp.exp2(s_ref[i, rows, :] - m)
        v_aug = jnp.concatenate([vt_ref[j], ones_rows], axis=0)
        acc_ref[...] += jnp.dot(v_aug, p.astype(BF16), preferred_element_type=F32)

    m_ref[...] = jnp.full(m_ref.shape, -jnp.inf, F32)
    q0 = stream_q(*streams[0])

    def first_body(j, carry):
        score_chunk(0, q0, j)
        return carry

    lax.fori_loop(0, nc, first_body, 0, unroll=ATTN_UNROLL)
    outs = []
    for i, (grp, c) in enumerate(streams):
        m = jnp.max(m_ref[i], axis=0, keepdims=True)
        acc_ref[...] = jnp.zeros_like(acc_ref)
        if i + 1 < len(streams):
            qn = stream_q(*streams[i + 1])

            def body(j, carry, i=i, m=m, qn=qn):
                score_chunk(i + 1, qn, j)
                prob_chunk(i, j, m)
                return carry
        else:
            def body(j, carry, i=i, m=m):
                prob_chunk(i, j, m)
                return carry

        lax.fori_loop(0, nc, body, 0, unroll=ATTN_UNROLL)
        outs.append(acc_ref[:DIFF_V_DIM, :] * (1.0 / acc_ref[DIFF_V_DIM:DIFF_V_DIM + 1, :]))
        if c == 1:
            o = outs[-2] - lam_ref[0] * outs[-1]
            o = o * lax.rsqrt(jnp.mean(o * o, axis=0, keepdims=True) + SUBLN_EPS)
            o = o * (g_ref[...] * (1.0 - LAMBDA_INIT))
            o_ref[0, grp * w:(grp + 1) * w, :] = o.T.astype(o_ref.dtype)


def _diff_attention(lam, q, k, vt, g_subln):
    B, S, _ = q.shape
    tq, kc = ATTN_Q_TILE, ATTN_KV_TILE
    nc = S // kc
    n_streams = 2 * (tq // ATTN_STREAM_W)
    return pl.pallas_call(
        functools.partial(_attn_kernel, kc=kc),
        grid=(B, DIFF_HEADS, S // tq),
        in_specs=[
            pl.BlockSpec(memory_space=pltpu.SMEM),
            pl.BlockSpec((1, tq, LANES), lambda b, h, i: (b, i, h)),
            pl.BlockSpec((1, S, LANES), lambda b, h, i: (b, 0, h)),
            pl.BlockSpec((nc, DIFF_V_DIM, kc), lambda b, h, i: (b, h, 0)),
            pl.BlockSpec((DIFF_V_DIM, 1), lambda b, h, i: (0, 0)),
        ],
        out_specs=pl.BlockSpec((1, tq, LANES), lambda b, h, i: (b, i, h)),
        out_shape=jax.ShapeDtypeStruct((B, S, ATTN_WIDTH), BF16),
        scratch_shapes=[pltpu.VMEM((n_streams, S, ATTN_STREAM_W), F32),
                        pltpu.VMEM((n_streams, SUBLANES, ATTN_STREAM_W), F32),
                        pltpu.VMEM((DIFF_V_DIM + ATTN_ONES_ROWS, ATTN_STREAM_W), F32)],
        compiler_params=_params("parallel", "parallel", "parallel"),
        name="diff_attn",
    )(lam, q, k, vt, g_subln.reshape(DIFF_V_DIM, 1))


def _shortconv_kernel(hy_ref, prev_ref, next_ref, w_ref, b_ref, u_ref, x0_ref):
    i = pl.program_id(1)
    last = pl.num_programs(1) - 1
    x = hy_ref[0].astype(F32)
    ts = x.shape[0]
    prev_row = jnp.where(i == 0, 0.0, prev_ref[0, HALO_ROWS - 1:HALO_ROWS, :].astype(F32))
    next_row = jnp.where(i == last, 0.0, next_ref[0, 0:1, :].astype(F32))
    row = lax.broadcasted_iota(jnp.int32, (ts, 1), 0)
    xm = jnp.where(row == 0, prev_row, pltpu.roll(x, 1, axis=0))
    xp = jnp.where(row == ts - 1, next_row, pltpu.roll(x, ts - 1, axis=0))
    y = b_ref[...] + xm * w_ref[0:1, :] + x * w_ref[1:2, :] + xp * w_ref[2:3, :]
    C = HYENA_WIDTH
    u_ref[0] = (y[:, :C] * y[:, 2 * C:]).astype(u_ref.dtype)
    x0_ref[0] = y[:, C:2 * C].astype(x0_ref.dtype)


HALO_ROWS = 16


def _short_conv(hy, conv_w, conv_b):
    B, S, C3 = hy.shape
    ts = TOKEN_TILE
    nb = ts // HALO_ROWS
    return pl.pallas_call(
        _shortconv_kernel,
        grid=(B, S // ts),
        in_specs=[
            pl.BlockSpec((1, ts, C3), lambda b, i: (b, i, 0)),
            pl.BlockSpec((1, HALO_ROWS, C3), lambda b, i: (b, jnp.maximum(i * nb - 1, 0), 0)),
            pl.BlockSpec((1, HALO_ROWS, C3), lambda b, i: (b, jnp.minimum((i + 1) * nb, S // HALO_ROWS - 1), 0)),
            pl.BlockSpec((3, C3), lambda b, i: (0, 0)),
            pl.BlockSpec((1, C3), lambda b, i: (0, 0)),
        ],
        out_specs=[
            pl.BlockSpec((1, ts, HYENA_WIDTH), lambda b, i: (b, i, 0)),
            pl.BlockSpec((1, ts, HYENA_WIDTH), lambda b, i: (b, i, 0)),
        ],
        out_shape=[
            jax.ShapeDtypeStruct((B, S, HYENA_WIDTH), BF16),
            jax.ShapeDtypeStruct((B, S, HYENA_WIDTH), BF16),
        ],
        compiler_params=_params("parallel", "parallel"),
        name="short_conv",
    )(hy, hy, hy, conv_w, conv_b.reshape(1, C3))


FFT_N = 8192
FFT_N1 = 64
FFT_N2 = 128
FFT_CH_BLOCK = 32
FFT_GROUP = 4
FFT_UNROLL = 2


def _dft_constants():
    import numpy as np
    n1, n2, n = FFT_N1, FFT_N2, FFT_N
    k1 = np.arange(n1)[:, None]
    t1 = np.arange(n1)[None, :]
    f1 = np.exp(-2j * np.pi * k1 * t1 / n1)
    f1h = f1[:, : n1 // 2]
    w1c = np.block([[f1h.real, -f1h.imag], [f1h.imag, f1h.real]])
    w1r = np.concatenate([f1h.real, f1h.imag], axis=0)
    t2 = np.arange(n2)[:, None]
    k2 = np.arange(n2)[None, :]
    f2 = np.exp(-2j * np.pi * t2 * k2 / n2)
    w2a = np.concatenate([f2.real, f2.imag], axis=1)
    w2b = np.concatenate([-f2.imag, f2.real], axis=1)
    g2 = np.conj(f2)
    w3a = np.concatenate([g2.real, g2.imag], axis=1)
    w3b = np.concatenate([-g2.imag, g2.real], axis=1)
    h = np.conj(f1).T[: n1 // 2] / n
    w4 = np.block([[h.real, -h.imag], [h.imag, h.real]])
    tw = np.exp(-2j * np.pi * np.arange(n1)[:, None] * np.arange(n2)[None, :] / n)
    bf = lambda a: jnp.asarray(a, dtype=F32).astype(BF16)
    return dict(w1c=bf(w1c), w1r=bf(w1r), w2a=bf(w2a), w2b=bf(w2b), w3a=bf(w3a), w3b=bf(w3b),
                w4=bf(w4), tc=jnp.asarray(np.tile(tw.real, (1, FFT_GROUP)), F32),
                ts=jnp.asarray(np.tile(tw.imag, (1, FFT_GROUP)), F32))


def _dft_forward(x_ref, w1_ref, tc_ref, ts_ref, w2a_ref, w2b_ref, ar_ref, ai_ref):
    ng = x_ref.shape[0] // FFT_GROUP
    n1, n2 = FFT_N1, FFT_N2
    w1 = w1_ref[...]
    tc, ts = tc_ref[...], ts_ref[...]

    def body(g, carry):
        xg = jnp.concatenate([x_ref[g * FFT_GROUP + cl] for cl in range(FFT_GROUP)], axis=1)
        a = jnp.dot(w1, xg, preferred_element_type=F32)
        ar, ai = a[:n1], a[n1:]
        tr = (ar * tc - ai * ts).astype(BF16)
        ti = (ar * ts + ai * tc).astype(BF16)
        for cl in range(FFT_GROUP):
            rows = pl.ds(pl.multiple_of((g * FFT_GROUP + cl) * n1, n1), n1)
            ar_ref[rows, :] = tr[:, cl * n2:(cl + 1) * n2]
            ai_ref[rows, :] = ti[:, cl * n2:(cl + 1) * n2]
        return carry

    lax.fori_loop(0, ng, body, 0, unroll=FFT_UNROLL)
    return (jnp.dot(ar_ref[...], w2a_ref[...], preferred_element_type=F32)
            + jnp.dot(ai_ref[...], w2b_ref[...], preferred_element_type=F32))


def _spectrum_kernel(hf_ref, hb_ref, w1_ref, tc_ref, ts_ref, w2a_ref, w2b_ref, o_ref, ar_ref, ai_ref):
    n2 = FFT_N2
    f = _dft_forward(hf_ref, w1_ref, tc_ref, ts_ref, w2a_ref, w2b_ref, ar_ref, ai_ref)
    o_ref[...] = f.reshape(o_ref.shape)
    b = _dft_forward(hb_ref, w1_ref, tc_ref, ts_ref, w2a_ref, w2b_ref, ar_ref, ai_ref)
    b = b.reshape(o_ref.shape)
    o_ref[:, :, :n2] = o_ref[:, :, :n2] + b[:, :, :n2]
    o_ref[:, :, n2:] = o_ref[:, :, n2:] - b[:, :, n2:]


def _fftconv_kernel(x_ref, kf_ref, w1_ref, tc_ref, ts_ref, w2a_ref, w2b_ref, w3a_ref, w3b_ref,
                    w4_ref, y_ref, ar_ref, ai_ref, c_ref):
    cb = x_ref.shape[0]
    ng = cb // FFT_GROUP
    n1, n2 = FFT_N1, FFT_N2
    b = _dft_forward(x_ref, w1_ref, tc_ref, ts_ref, w2a_ref, w2b_ref, ar_ref, ai_ref)
    kf = kf_ref[...].reshape(cb * n1, 2 * n2)
    br, bi = b[:, :n2], b[:, n2:]
    kr, ki = kf[:, :n2], kf[:, n2:]
    ar_ref[...] = (br * kr - bi * ki).astype(BF16)
    ai_ref[...] = (br * ki + bi * kr).astype(BF16)
    c_ref[...] = (jnp.dot(ar_ref[...], w3a_ref[...], preferred_element_type=F32)
                  + jnp.dot(ai_ref[...], w3b_ref[...], preferred_element_type=F32))
    w4 = w4_ref[...]
    tc, ts = tc_ref[...], ts_ref[...]

    def body(g, carry):
        rows = [pl.ds(pl.multiple_of((g * FFT_GROUP + cl) * n1, n1), n1) for cl in range(FFT_GROUP)]
        cr = jnp.concatenate([c_ref[r, :n2] for r in rows], axis=1)
        ci = jnp.concatenate([c_ref[r, n2:] for r in rows], axis=1)
        dr = (cr * tc + ci * ts).astype(BF16)
        di = (ci * tc - cr * ts).astype(BF16)
        yg = (jnp.dot(w4[:, :n1], dr, preferred_element_type=F32)
              + jnp.dot(w4[:, n1:], di, preferred_element_type=F32)).astype(y_ref.dtype)
        for cl in range(FFT_GROUP):
            y_ref[g * FFT_GROUP + cl] = yg[:, cl * n2:(cl + 1) * n2]
        return carry

    lax.fori_loop(0, ng, body, 0, unroll=FFT_UNROLL)


def _const_spec(a):
    nd = a.ndim
    return pl.BlockSpec(a.shape, lambda *_: (0,) * nd)


def _filter_spectrum(hf_slabs, hb_slabs, cst):
    C = hf_slabs.shape[0]
    cb = FFT_CH_BLOCK
    consts = [cst["w1r"], cst["tc"], cst["ts"], cst["w2a"], cst["w2b"]]
    slab = pl.BlockSpec((cb, FFT_N1 // 2, FFT_N2), lambda i: (i, 0, 0))
    return pl.pallas_call(
        _spectrum_kernel,
        grid=(C // cb,),
        in_specs=[slab, slab] + [_const_spec(a) for a in consts],
        out_specs=pl.BlockSpec((cb, FFT_N1, 2 * FFT_N2), lambda i: (i, 0, 0)),
        out_shape=jax.ShapeDtypeStruct((C, FFT_N1, 2 * FFT_N2), F32),
        scratch_shapes=[pltpu.VMEM((cb * FFT_N1, FFT_N2), BF16), pltpu.VMEM((cb * FFT_N1, FFT_N2), BF16)],
        compiler_params=_params("parallel"),
        name="filter_spectrum",
    )(hf_slabs, hb_slabs, *consts)


FILTER_TIME_TILE = 512


def _filter_kernel(zt_ref, w1t_ref, b1_ref, wit_ref, bi_ref, fr_ref, wot_ref, dec_ref, hf_ref, hb_ref):
    hp = lax.Precision.HIGHEST
    h = jnp.sin(fr_ref[0] * (jnp.dot(w1t_ref[...], zt_ref[...], precision=hp,
                                     preferred_element_type=F32) + b1_ref[...]))
    for j in range(FILTER_INNER):
        h = jnp.sin(fr_ref[j + 1] * (jnp.dot(wit_ref[j], h, precision=hp,
                                             preferred_element_type=F32) + bi_ref[j]))
    o = jnp.dot(wot_ref[...], h, precision=hp, preferred_element_type=F32)
    dec = dec_ref[...]
    hf_ref[...] = o[:HYENA_WIDTH] * dec
    hb_ref[...] = o[HYENA_WIDTH:] * dec


def _hyena_filters(seq, w1, b1, w_inner, b_inner, freq, w_out):
    C = HYENA_WIDTH
    order = w1.shape[1]
    pos = jnp.arange(seq, dtype=F32)
    t = pos / (seq - 1)
    bands = (FILTER_EMB - 1) // 2
    f = jnp.linspace(1e-4, bands - 1, bands, dtype=F32)
    fw = ((2.0 * math.pi / seq) * pos)[:, None] * f[None, :]
    z = jnp.concatenate([t[:, None], jnp.cos(fw), -jnp.sin(fw)], axis=-1)
    zt = jnp.zeros((LANES, seq), F32).at[:FILTER_EMB].set(z.T)
    w1t = jnp.zeros((order, LANES), F32).at[:, :FILTER_EMB].set(w1.T)
    max_decay = math.log(DECAY_TARGET) / FAST_DECAY_PCT
    min_decay = math.log(DECAY_TARGET) / SLOW_DECAY_PCT
    deltas = jnp.abs(jnp.linspace(min_decay, max_decay, C, dtype=F32))
    dec_t = jnp.exp(-deltas[:, None] * t[None, :])
    tt = FILTER_TIME_TILE
    lane_blk = lambda r: pl.BlockSpec((r, tt), lambda i: (0, i))
    args = [zt, w1t, b1.reshape(order, 1), jnp.swapaxes(w_inner, 1, 2),
            b_inner.reshape(FILTER_INNER, order, 1), freq.reshape(FILTER_INNER + 1, order, 1),
            w_out.T, dec_t]
    return pl.pallas_call(
        _filter_kernel,
        grid=(seq // tt,),
        in_specs=[lane_blk(LANES)] + [_const_spec(a) for a in args[1:7]] + [lane_blk(C)],
        out_specs=[lane_blk(C), lane_blk(C)],
        out_shape=[jax.ShapeDtypeStruct((C, seq), F32), jax.ShapeDtypeStruct((C, seq), F32)],
        compiler_params=_params("parallel"),
        name="hyena_filters",
    )(*args)


def _fft_conv(x_slabs, kf, cst):
    P, C = x_slabs.shape[:2]
    cb = FFT_CH_BLOCK
    consts = [cst["w1c"], cst["tc"], cst["ts"], cst["w2a"], cst["w2b"], cst["w3a"], cst["w3b"], cst["w4"]]
    return pl.pallas_call(
        _fftconv_kernel,
        grid=(C // cb, P),
        in_specs=[pl.BlockSpec((None, cb, FFT_N1, FFT_N2), lambda i, p: (p, i, 0, 0)),
                  pl.BlockSpec((cb, FFT_N1, 2 * FFT_N2), lambda i, p: (i, 0, 0))]
                 + [_const_spec(a) for a in consts],
        out_specs=pl.BlockSpec((None, cb, FFT_N1, FFT_N2), lambda i, p: (p, i, 0, 0)),
        out_shape=jax.ShapeDtypeStruct((P, C, FFT_N1, FFT_N2), BF16),
        scratch_shapes=[pltpu.VMEM((cb * FFT_N1, FFT_N2), BF16), pltpu.VMEM((cb * FFT_N1, FFT_N2), BF16),
                        pltpu.VMEM((cb * FFT_N1, 2 * FFT_N2), F32)],
        compiler_params=_params("parallel", "arbitrary"),
        name="fft_conv",
    )(x_slabs, kf, *consts)


def _split_bf16(x):
    hi = x.astype(BF16)
    lo = (x - hi.astype(F32)).astype(BF16)
    return hi, lo


def _outproj_kernel(attn_ref, yc_ref, u_ref, x0_ref, d_ref, gn_ref, grp_ref, wo_ref, x_ref,
                    gf_ref, rwh_ref, rwl_ref, rb_ref, tri_ref, h_ref, hn_ref, rt_ref, cnt_out_ref,
                    cnt_ref):
    @pl.when(pl.program_id(0) == 0)
    def _():
        cnt_ref[...] = jnp.zeros_like(cnt_ref)

    tm = x_ref.shape[0]
    sub = tm // OUTPROJ_SUBTILES
    grp = grp_ref[...]
    gsz = HYENA_WIDTH // HYENA_GROUPS

    def sub_tile(r0, cnt_row):
        rows = slice(r0, r0 + sub)
        z = ((yc_ref[rows, :].astype(F32) + u_ref[rows, :].astype(F32) * d_ref[...])
             * x0_ref[rows, :].astype(F32))
        zh, zl = _split_bf16(z * z)
        ssq = (jnp.dot(zh, grp, preferred_element_type=F32)
               + jnp.dot(zl, grp, preferred_element_type=F32))
        hy_out = (z * lax.rsqrt(ssq * (1.0 / gsz) + EPS) * gn_ref[...]).astype(BF16)
        mix = (jnp.dot(attn_ref[rows, :], wo_ref[:ATTN_WIDTH, :], preferred_element_type=F32)
               + jnp.dot(hy_out, wo_ref[ATTN_WIDTH:, :], preferred_element_type=F32))
        h = x_ref[rows, :] + mix
        h_ref[rows, :] = h
        hn = _rms(h, gf_ref[...], EPS)
        hn_ref[rows, :] = hn.astype(BF16)
        nh, nl = _split_bf16(hn)
        logits = (jnp.dot(nh, rwh_ref[...], preferred_element_type=F32)
                  + jnp.dot(nl, rwh_ref[...], preferred_element_type=F32)
                  + jnp.dot(nh, rwl_ref[...], preferred_element_type=F32)) + rb_ref[...]

        lane = lax.broadcasted_iota(jnp.int32, logits.shape, 1)
        work = logits
        top_val, top_idx, top_hot = [], [], []
        for _ in range(TOP_K):
            m = jnp.max(work, axis=-1, keepdims=True)
            idx = jnp.min(jnp.where(work == m, lane, ROUTER_PAD), axis=-1, keepdims=True)
            hot = lane == idx
            top_val.append(m)
            top_idx.append(idx)
            top_hot.append(hot)
            work = jnp.where(hot, -jnp.inf, work)
        ex = [jnp.exp(v - top_val[0]) for v in top_val]
        inv_den = 1.0 / (ex[0] + ex[1] + ex[2] + ex[3])
        sel = jnp.zeros(logits.shape, F32)
        for hot in top_hot:
            sel = sel + hot.astype(F32)
        before = jnp.dot(tri_ref[...], sel.astype(BF16), preferred_element_type=F32) + cnt_row
        packed = jnp.zeros(logits.shape, F32)
        for r, hot in enumerate(top_hot):
            rank_r = jnp.sum(jnp.where(hot, before, 0.0), axis=-1, keepdims=True)
            packed = jnp.where(lane == r, top_idx[r].astype(F32), packed)
            packed = jnp.where(lane == TOP_K + r, rank_r, packed)
            packed = jnp.where(lane == 2 * TOP_K + r, ex[r] * inv_den, packed)
        rt_ref[rows, :] = packed
        return cnt_row + jnp.sum(sel, axis=0, keepdims=True)

    cnt_row = cnt_ref[0:1, :]
    for t in range(OUTPROJ_SUBTILES):
        cnt_row = sub_tile(t * sub, cnt_row)
    cnt_ref[...] = jnp.broadcast_to(cnt_row, cnt_ref.shape)
    cnt_out_ref[...] = cnt_ref[...]


def _out_proj(attn2, yc2, u2, x02, hyena_d, hyena_gn, w_out_bf, x2, g_ffn, router_w, router_b):
    T, D = x2.shape
    tm = OUTPROJ_TILE
    sub = tm // OUTPROJ_SUBTILES
    C = HYENA_WIDTH
    gid = jnp.arange(C) // (C // HYENA_GROUPS)
    grp = (gid[:, None] == gid[None, :]).astype(BF16)
    rw = jnp.zeros((D, ROUTER_PAD), F32).at[:, :N_EXPERTS].set(router_w)
    rwh, rwl = _split_bf16(rw)
    rb = jnp.full((1, ROUTER_PAD), -jnp.inf, F32).at[0, :N_EXPERTS].set(router_b)
    tri = (jnp.arange(sub)[:, None] > jnp.arange(sub)[None, :]).astype(BF16)
    row = lambda i: (i, 0)
    const = lambda i: (0, 0)
    return pl.pallas_call(
        _outproj_kernel,
        grid=(T // tm,),
        in_specs=[
            pl.BlockSpec((tm, ATTN_WIDTH), row),
            pl.BlockSpec((tm, C), row),
            pl.BlockSpec((tm, C), row),
            pl.BlockSpec((tm, C), row),
            pl.BlockSpec((1, C), const),
            pl.BlockSpec((1, C), const),
            pl.BlockSpec((C, C), const),
            pl.BlockSpec((D, D), const),
            pl.BlockSpec((tm, D), row),
            pl.BlockSpec((1, D), const),
            pl.BlockSpec((D, ROUTER_PAD), const),
            pl.BlockSpec((D, ROUTER_PAD), const),
            pl.BlockSpec((1, ROUTER_PAD), const),
            pl.BlockSpec((sub, sub), const),
        ],
        out_specs=[
            pl.BlockSpec((tm, D), row),
            pl.BlockSpec((tm, D), row),
            pl.BlockSpec((tm, ROUTER_PAD), row),
            pl.BlockSpec((SUBLANES, ROUTER_PAD), const),
        ],
        out_shape=[
            jax.ShapeDtypeStruct((T, D), F32),
            jax.ShapeDtypeStruct((T, D), BF16),
            jax.ShapeDtypeStruct((T, ROUTER_PAD), F32),
            jax.ShapeDtypeStruct((SUBLANES, ROUTER_PAD), F32),
        ],
        scratch_shapes=[pltpu.VMEM((SUBLANES, ROUTER_PAD), F32)],
        compiler_params=_params("arbitrary"),
        name="out_proj",
    )(attn2, yc2, u2, x02, hyena_d.reshape(1, C), hyena_gn.reshape(1, C), grp, w_out_bf, x2,
      g_ffn.reshape(1, D), rwh, rwl, rb, tri)


def _expert_kernel(be_ref, nused_ref, x_ref, wg_ref, bg_ref, wu_ref, bu_ref, wd_ref, bd_ref, *rest):
    y_ref, wg_bf, wu_bf, wd_bf = rest[-4:]
    i = pl.program_id(0)
    used = i < nused_ref[0]
    new_expert = jnp.logical_or(i == 0, be_ref[i] != be_ref[jnp.maximum(i - 1, 0)])

    @pl.when(jnp.logical_and(used, new_expert))
    def _():
        wg_bf[...] = wg_ref[0].astype(BF16)
        wu_bf[...] = wu_ref[0].astype(BF16)
        wd_bf[...] = wd_ref[0].astype(BF16)

    @pl.when(used)
    def _():
        x = x_ref[...]
        g = jnp.minimum(jnp.dot(x, wg_bf[...], preferred_element_type=F32) + bg_ref[0], SWIGLU_LIMIT)
        u = jnp.clip(jnp.dot(x, wu_bf[...], preferred_element_type=F32) + bu_ref[0],
                     -SWIGLU_LIMIT, SWIGLU_LIMIT)
        a = (u + 1.0) * (g * jax.nn.sigmoid(SWIGLU_ALPHA * g))
        y = jnp.dot(a.astype(BF16), wd_bf[...], preferred_element_type=F32) + bd_ref[0]
        y_ref[...] = y.astype(y_ref.dtype)

    @pl.when(jnp.logical_not(used))
    def _():
        y_ref[...] = jnp.zeros_like(y_ref)


def _experts(block_e, n_used, x_parts, wg, bg, wu, bu, wd, bd):
    Pp, D = x_parts[0].shape
    tm = EXPERT_TILE
    nb = Pp // tm
    P = Pp * len(x_parts)
    E, _, FF = wg.shape
    wmap = lambda i, be, nu: (be[i], 0, 0)
    y = None
    for part, xs in enumerate(x_parts):
        in_specs = [
            pl.BlockSpec((tm, D), lambda i, be, nu: (i, 0)),
            pl.BlockSpec((1, D, FF), wmap),
            pl.BlockSpec((1, 1, FF), wmap),
            pl.BlockSpec((1, D, FF), wmap),
            pl.BlockSpec((1, 1, FF), wmap),
            pl.BlockSpec((1, FF, D), wmap),
            pl.BlockSpec((1, 1, D), wmap),
        ]
        args = [block_e[part * nb:(part + 1) * nb], jnp.clip(n_used - part * nb, 0, nb), xs,
                wg, bg.reshape(E, 1, FF), wu, bu.reshape(E, 1, FF), wd, bd.reshape(E, 1, D)]
        aliases = {}
        if y is not None:
            in_specs.append(pl.BlockSpec(memory_space=pl.ANY))
            args.append(y)
            aliases = {len(args) - 1: 0}
        grid_spec = pltpu.PrefetchScalarGridSpec(
            num_scalar_prefetch=2,
            grid=(nb,),
            in_specs=in_specs,
            out_specs=pl.BlockSpec((tm, D), lambda i, be, nu, part=part: (i + part * nb, 0)),
            scratch_shapes=[pltpu.VMEM((D, FF), BF16), pltpu.VMEM((D, FF), BF16),
                            pltpu.VMEM((FF, D), BF16)],
        )
        y = pl.pallas_call(
            _expert_kernel,
            grid_spec=grid_spec,
            out_shape=jax.ShapeDtypeStruct((P, D), BF16),
            input_output_aliases=aliases,
            compiler_params=_params("arbitrary"),
            name=f"moe_experts_{part}",
        )(*args)
    return y


def _final_kernel(h_ref, yg_ref, rt_ref, p_ref, wp_ref, gp_ref, wg_ref, bg_ref, gfin_ref, *rest):
    o_ref = rest[-1]
    h = h_ref[...]
    for r in range(TOP_K):
        h = h + yg_ref[r].astype(F32) * rt_ref[:, 2 * TOP_K + r: 2 * TOP_K + r + 1]
    e = _rms(jnp.dot(p_ref[...].astype(BF16), wp_ref[...], preferred_element_type=F32),
             gp_ref[...], EPS)
    gate = jax.nn.sigmoid(jnp.dot(h.astype(BF16), wg_ref[...], preferred_element_type=F32)
                          + bg_ref[...])
    h = h + gate * e
    o_ref[...] = _rms(h, gfin_ref[...], EPS)


def _final(h1, yg_parts, route, p2, w_ple_bf, g_ple, w_gate_bf, b_gate, g_final):
    T, D = h1.shape
    tm = TOKEN_TILE
    PD = p2.shape[1]
    nb = T // tm // len(yg_parts)
    const = lambda i: (0, 0)
    out = None
    for part, yg in enumerate(yg_parts):
        row = lambda i, part=part: (i + part * nb, 0)
        in_specs = [
            pl.BlockSpec((tm, D), row),
            pl.BlockSpec((TOP_K, tm, D), lambda i: (0, i, 0)),
            pl.BlockSpec((tm, ROUTER_PAD), row),
            pl.BlockSpec((tm, PD), row),
            pl.BlockSpec((PD, D), const),
            pl.BlockSpec((1, D), const),
            pl.BlockSpec((D, D), const),
            pl.BlockSpec((1, D), const),
            pl.BlockSpec((1, D), const),
        ]
        args = [h1, yg, route, p2, w_ple_bf, g_ple.reshape(1, D), w_gate_bf, b_gate.reshape(1, D),
                g_final.reshape(1, D)]
        aliases = {}
        if out is not None:
            in_specs.append(pl.BlockSpec(memory_space=pl.ANY))
            args.append(out)
            aliases = {len(args) - 1: 0}
        out = pl.pallas_call(
            _final_kernel,
            grid=(nb,),
            in_specs=in_specs,
            out_specs=pl.BlockSpec((tm, D), row),
            out_shape=jax.ShapeDtypeStruct((T, D), F32),
            input_output_aliases=aliases,
            compiler_params=_params("parallel"),
            name=f"final_{part}",
        )(*args)
    return out


def _rope_tables(seq):
    d = DIFF_HEAD_DIM
    pos = jnp.arange(seq, dtype=F32)
    inv = ROPE_THETA ** (-jnp.arange(0, d, 2, dtype=F32) / d)
    ang = pos[:, None] * inv[None, :]
    cos, sin = jnp.cos(ang), jnp.sin(ang)
    z = jnp.zeros_like(sin)
    cos_t = jnp.tile(jnp.concatenate([cos, cos], -1), (1, LANES // d))
    s1_t = jnp.tile(jnp.concatenate([-sin, z], -1), (1, LANES // d))
    s2_t = jnp.tile(jnp.concatenate([z, sin], -1), (1, LANES // d))
    return cos_t, s1_t, s2_t


def _long_conv(u, hf_t, hb_t):
    B, L, C = u.shape
    assert 2 * L == FFT_N and B % 2 == 0
    P, R = B // 2, FFT_N1 // 2
    cst = _dft_constants()
    kf = _filter_spectrum(hf_t.reshape(C, R, FFT_N2).astype(BF16),
                          hb_t.reshape(C, R, FFT_N2).astype(BF16), cst)
    x_slabs = (u.reshape(2, P, R, FFT_N2, C).transpose(1, 4, 0, 2, 3)
               .reshape(P, C, FFT_N1, FFT_N2).astype(BF16))
    y = _fft_conv(x_slabs, kf, cst)
    return y.reshape(P, C, 2, R, FFT_N2).transpose(2, 0, 3, 4, 1).reshape(B, L, C)


def _dispatch_indices(route, cnt, T):
    tm = EXPERT_TILE
    A = T * TOP_K
    route_t = route[:, :2 * TOP_K].T
    top_e = route_t[:TOP_K].astype(jnp.int32)
    rank = route_t[TOP_K:].astype(jnp.int32)
    counts = cnt[0, :N_EXPERTS].astype(jnp.int32)
    padded = (counts + tm - 1) // tm * tm
    pad_end = jnp.cumsum(padded)
    pad_start = pad_end - padded
    start = jnp.cumsum(counts) - counts
    pos = pad_start[top_e] + rank
    n_blocks = -(-A // tm) + N_EXPERTS
    P = n_blocks * tm
    block_first = jnp.arange(n_blocks, dtype=jnp.int32) * tm
    block_e = jnp.minimum(jnp.sum(pad_end[None, :] <= block_first[:, None], axis=1),
                          N_EXPERTS - 1).astype(jnp.int32)
    n_used = (pad_end[-1] // tm).astype(jnp.int32).reshape(1)
    tok = jnp.broadcast_to(jnp.arange(T, dtype=jnp.int32)[None, :], (TOP_K, T))
    _, sorted_tok = lax.sort_key_val(pos.reshape(A), tok.reshape(A))
    in_blk = jnp.arange(tm, dtype=jnp.int32)[None, :]
    r = (block_first - pad_start[block_e])[:, None] + in_blk
    compact = jnp.clip(start[block_e][:, None] + r, 0, A - 1)
    filler = (block_first[:, None] + in_blk) % T
    slot_tok = jnp.where(r < counts[block_e][:, None], sorted_tok[compact], filler).reshape(P)
    return pos, slot_tok, block_e, n_used


def kernel(x, p, g_mix, w_in, hyena_conv_w, hyena_conv_b, flt_w1, flt_b1, flt_w_inner, flt_b_inner, flt_freq, flt_w_out, hyena_d, hyena_gn, lambda_q1, lambda_k1, lambda_q2, lambda_k2, attn_subln, w_out, g_ffn, router_w, router_b, w_gate, b_gate, w_up, b_up, w_down, b_down, w_ple, g_ple, w_ple_gate, b_ple_gate, g_final):
    B, S, D = x.shape
    T = B * S
    i = 0
    x2 = x.reshape(T, D)

    cos_t, s1_t, s2_t = _rope_tables(S)
    q, k, vt, hy = _in_proj(x2, g_mix[i], w_in[i].astype(BF16), cos_t, s1_t, s2_t, S)

    lam = (jnp.exp(jnp.sum(lambda_q1[i] * lambda_k1[i])) - jnp.exp(jnp.sum(lambda_q2[i] * lambda_k2[i]))
           + LAMBDA_INIT).reshape(1).astype(F32)
    attn = _diff_attention(lam, q.reshape(B, S, -1), k.reshape(B, S, -1), vt, attn_subln[i])

    u, hx0 = _short_conv(hy.reshape(B, S, -1), hyena_conv_w[i], hyena_conv_b[i])
    h_fwd, h_bwd = _hyena_filters(S, flt_w1[i], flt_b1[i], flt_w_inner[i], flt_b_inner[i],
                                  flt_freq[i], flt_w_out[i])
    yc = _long_conv(u, h_fwd, h_bwd)

    h1, hn, route, cnt = _out_proj(attn.reshape(T, -1), yc.reshape(T, -1), u.reshape(T, -1),
                                   hx0.reshape(T, -1), hyena_d[i], hyena_gn[i],
                                   w_out[i].astype(BF16), x2, g_ffn[i], router_w[i], router_b[i])

    pos, slot_tok, block_e, n_used = _dispatch_indices(route, cnt, T)
    n_parts = MOE_OVERLAP_PARTS
    slot_parts = jnp.split(slot_tok, n_parts)
    x_parts = [hn[s] for s in slot_parts]
    y = _experts(block_e, n_used, x_parts, w_gate[i], b_gate[i], w_up[i], b_up[i],
                 w_down[i], b_down[i])
    pos_parts = jnp.split(pos, n_parts, axis=1)
    yg_parts = [y[pp.reshape(-1)].reshape(TOP_K, T // n_parts, D) for pp in pos_parts]

    out = _final(h1, yg_parts, route, p[i].reshape(T, -1), w_ple[i].astype(BF16), g_ple[i],
                 w_ple_gate[i].astype(BF16), b_ple_gate[i], g_final)
    return out.reshape(B, S, D)
```

```python
import functools
import math

import jax
import jax.numpy as jnp
from jax import lax
from jax.experimental import pallas as pl
from jax.experimental.pallas import tpu as pltpu

F32 = jnp.float32
BF16 = jnp.bfloat16

D_MODEL = 1024
ATTN_WIDTH = 512
HYENA_WIDTH = 512
DIFF_HEADS = 4
DIFF_HEAD_DIM = 64
DIFF_V_DIM = 128
HYENA_GROUPS = 8
FILTER_EMB = 33
FILTER_INNER = 2
FAST_DECAY_PCT = 0.3
SLOW_DECAY_PCT = 1.5
DECAY_TARGET = 1e-2
ROPE_THETA = 10000.0
N_EXPERTS = 32
TOP_K = 4
SWIGLU_ALPHA = 1.702
SWIGLU_LIMIT = 7.0
EPS = 1e-6
SUBLN_EPS = 1e-5
LAMBDA_INIT = 0.8 - 0.6 * math.exp(-0.3 * 0)

V7X_VMEM_LIMIT_BYTES = 56 * 1024 * 1024
LANES = 128
SUBLANES = 8

TOKEN_TILE = 512
ATTN_STREAM_W = 512
ATTN_Q_TILE = 1024
ATTN_ONES_ROWS = 16
ATTN_UNROLL = 4
ATTN_KV_TILE = 512
EXPERT_TILE = 512
ROUTER_PAD = LANES
OUTPROJ_TILE = 512
OUTPROJ_SUBTILES = 1
MOE_OVERLAP_PARTS = 2


def _params(*sem):
    return pltpu.CompilerParams(dimension_semantics=sem, vmem_limit_bytes=V7X_VMEM_LIMIT_BYTES)


def _rms(x, g, eps):
    return x * lax.rsqrt(jnp.mean(x * x, axis=-1, keepdims=True) + eps) * g


def _inproj_kernel(x_ref, g_ref, w_ref, wvt_ref, c_ref, s1_ref, s2_ref, q_ref, k_ref, vt_ref, hy_ref):
    a = _rms(x_ref[...], g_ref[...], EPS).astype(BF16)
    cos, s1, s2 = c_ref[...], s1_ref[...], s2_ref[...]

    def rope(t):
        return t * cos + pltpu.roll(t, LANES - 32, axis=1) * s1 + pltpu.roll(t, 32, axis=1) * s2

    q_scale = (DIFF_HEAD_DIM ** -0.5) * math.log2(math.e)
    qk = jnp.dot(a, w_ref[:, : 2 * ATTN_WIDTH], preferred_element_type=F32)
    for j in range(ATTN_WIDTH // LANES):
        sl = slice(j * LANES, (j + 1) * LANES)
        q_ref[:, sl] = (rope(qk[:, sl]) * q_scale).astype(BF16)
        k_ref[:, sl] = rope(qk[:, ATTN_WIDTH + j * LANES: ATTN_WIDTH + (j + 1) * LANES]).astype(BF16)
    vt_ref[0] = lax.dot_general(wvt_ref[...], a, (((1,), (1,)), ((), ())),
                                preferred_element_type=F32).astype(BF16)
    hy_ref[...] = jnp.dot(a, w_ref[:, 3 * ATTN_WIDTH:], preferred_element_type=F32).astype(hy_ref.dtype)


def _in_proj(x2, g_mix, w_in_bf, cos_t, s1_t, s2_t, seq):
    T, D = x2.shape
    tm = TOKEN_TILE
    nseq = seq // tm
    wvt = w_in_bf[:, 2 * ATTN_WIDTH: 3 * ATTN_WIDTH].T
    row = lambda i: (i, 0)
    const = lambda i: (0, 0)
    pos = lambda i: (i % nseq, 0)
    return pl.pallas_call(
        _inproj_kernel,
        grid=(T // tm,),
        in_specs=[
            pl.BlockSpec((tm, D), row),
            pl.BlockSpec((1, D), const),
            pl.BlockSpec(w_in_bf.shape, const),
            pl.BlockSpec(wvt.shape, const),
            pl.BlockSpec((tm, LANES), pos),
            pl.BlockSpec((tm, LANES), pos),
            pl.BlockSpec((tm, LANES), pos),
        ],
        out_specs=[
            pl.BlockSpec((tm, ATTN_WIDTH), row),
            pl.BlockSpec((tm, ATTN_WIDTH), row),
            pl.BlockSpec((1, ATTN_WIDTH, tm), lambda i: (i, 0, 0)),
            pl.BlockSpec((tm, 3 * HYENA_WIDTH), row),
        ],
        out_shape=[
            jax.ShapeDtypeStruct((T, ATTN_WIDTH), BF16),
            jax.ShapeDtypeStruct((T, ATTN_WIDTH), BF16),
            jax.ShapeDtypeStruct((T // tm, ATTN_WIDTH, tm), BF16),
            jax.ShapeDtypeStruct((T, 3 * HYENA_WIDTH), BF16),
        ],
        compiler_params=_params("parallel"),
        name="in_proj",
    )(x2, g_mix.reshape(1, D), w_in_bf, wvt, cos_t, s1_t, s2_t)


def _attn_kernel(lam_ref, q_ref, k_ref, vt_ref, g_ref, o_ref, s_ref, m_ref, acc_ref, *, kc):
    w = ATTN_STREAM_W
    nc = k_ref.shape[1] // kc
    n_groups = q_ref.shape[1] // w
    streams = [(grp, c) for grp in range(n_groups) for c in range(2)]

    def stream_q(grp, c):
        q = q_ref[0, grp * w:(grp + 1) * w, :]
        lane = lax.broadcasted_iota(jnp.int32, q.shape, 1)
        keep = (lane < DIFF_HEAD_DIM) if c == 0 else (lane >= DIFF_HEAD_DIM)
        return jnp.where(keep, q, jnp.zeros_like(q))

    def fold8(t, op):
        r = t[0:SUBLANES]
        for j in range(1, kc // SUBLANES):
            r = op(r, t[j * SUBLANES:(j + 1) * SUBLANES])
        return r

    def score_chunk(i, qc, j):
        rows = pl.ds(pl.multiple_of(j * kc, kc), kc)
        s = lax.dot_general(k_ref[0, rows, :], qc, (((1,), (1,)), ((), ())),
                            preferred_element_type=F32)
        s_ref[i, rows, :] = s
        m_ref[i] = jnp.maximum(m_ref[i], fold8(s, jnp.maximum))

    ones_rows = jnp.ones((ATTN_ONES_ROWS, kc), BF16)

    def prob_chunk(i, j, m):
        rows = pl.ds(pl.multiple_of(j * kc, kc), kc)
        p = jnp.exp2(s_ref[i, rows, :] - m)
        v_aug = jnp.concatenate([vt_ref[j], ones_rows], axis=0)
        acc_ref[...] += jnp.dot(v_aug, p.astype(BF16), preferred_element_type=F32)

    m_ref[...] = jnp.full(m_ref.shape, -jnp.inf, F32)
    q0 = stream_q(*streams[0])

    def first_body(j, carry):
        score_chunk(0, q0, j)
        return carry

    lax.fori_loop(0, nc, first_body, 0, unroll=ATTN_UNROLL)
    outs = []
    for i, (grp, c) in enumerate(streams):
        m = jnp.max(m_ref[i], axis=0, keepdims=True)
        acc_ref[...] = jnp.zeros_like(acc_ref)
        if i + 1 < len(streams):
            qn = stream_q(*streams[i + 1])

            def body(j, carry, i=i, m=m, qn=qn):
                score_chunk(i + 1, qn, j)
                prob_chunk(i, j, m)
                return carry
        else:
            def body(j, carry, i=i, m=m):
                prob_chunk(i, j, m)
                return carry

        lax.fori_loop(0, nc, body, 0, unroll=ATTN_UNROLL)
        outs.append(acc_ref[:DIFF_V_DIM, :] * (1.0 / acc_ref[DIFF_V_DIM:DIFF_V_DIM + 1, :]))
        if c == 1:
            o = outs[-2] - lam_ref[0] * outs[-1]
            o = o * lax.rsqrt(jnp.mean(o * o, axis=0, keepdims=True) + SUBLN_EPS)
            o = o * (g_ref[...] * (1.0 - LAMBDA_INIT))
            o_ref[0, grp * w:(grp + 1) * w, :] = o.T.astype(o_ref.dtype)


def _diff_attention(lam, q, k, vt, g_subln):
    B, S, _ = q.shape
    tq, kc = ATTN_Q_TILE, ATTN_KV_TILE
    nc = S // kc
    n_streams = 2 * (tq // ATTN_STREAM_W)
    return pl.pallas_call(
        functools.partial(_attn_kernel, kc=kc),
        grid=(B, DIFF_HEADS, S // tq),
        in_specs=[
            pl.BlockSpec(memory_space=pltpu.SMEM),
            pl.BlockSpec((1, tq, LANES), lambda b, h, i: (b, i, h)),
            pl.BlockSpec((1, S, LANES), lambda b, h, i: (b, 0, h)),
            pl.BlockSpec((nc, DIFF_V_DIM, kc), lambda b, h, i: (b, h, 0)),
            pl.BlockSpec((DIFF_V_DIM, 1), lambda b, h, i: (0, 0)),
        ],
        out_specs=pl.BlockSpec((1, tq, LANES), lambda b, h, i: (b, i, h)),
        out_shape=jax.ShapeDtypeStruct((B, S, ATTN_WIDTH), BF16),
        scratch_shapes=[pltpu.VMEM((n_streams, S, ATTN_STREAM_W), F32),
                        pltpu.VMEM((n_streams, SUBLANES, ATTN_STREAM_W), F32),
                        pltpu.VMEM((DIFF_V_DIM + ATTN_ONES_ROWS, ATTN_STREAM_W), F32)],
        compiler_params=_params("parallel", "parallel", "parallel"),
        name="diff_attn",
    )(lam, q, k, vt, g_subln.reshape(DIFF_V_DIM, 1))


def _shortconv_kernel(hy_ref, prev_ref, next_ref, w_ref, b_ref, u_ref, x0_ref):
    i = pl.program_id(1)
    last = pl.num_programs(1) - 1
    x = hy_ref[0].astype(F32)
    ts = x.shape[0]
    prev_row = jnp.where(i == 0, 0.0, prev_ref[0, HALO_ROWS - 1:HALO_ROWS, :].astype(F32))
    next_row = jnp.where(i == last, 0.0, next_ref[0, 0:1, :].astype(F32))
    row = lax.broadcasted_iota(jnp.int32, (ts, 1), 0)
    xm = jnp.where(row == 0, prev_row, pltpu.roll(x, 1, axis=0))
    xp = jnp.where(row == ts - 1, next_row, pltpu.roll(x, ts - 1, axis=0))
    y = b_ref[...] + xm * w_ref[0:1, :] + x * w_ref[1:2, :] + xp * w_ref[2:3, :]
    C = HYENA_WIDTH
    u_ref[0] = (y[:, :C] * y[:, 2 * C:]).astype(u_ref.dtype)
    x0_ref[0] = y[:, C:2 * C].astype(x0_ref.dtype)


HALO_ROWS = 16


def _short_conv(hy, conv_w, conv_b):
    B, S, C3 = hy.shape
    ts = TOKEN_TILE
    nb = ts // HALO_ROWS
    return pl.pallas_call(
        _shortconv_kernel,
        grid=(B, S // ts),
        in_specs=[
            pl.BlockSpec((1, ts, C3), lambda b, i: (b, i, 0)),
            pl.BlockSpec((1, HALO_ROWS, C3), lambda b, i: (b, jnp.maximum(i * nb - 1, 0), 0)),
            pl.BlockSpec((1, HALO_ROWS, C3), lambda b, i: (b, jnp.minimum((i + 1) * nb, S // HALO_ROWS - 1), 0)),
            pl.BlockSpec((3, C3), lambda b, i: (0, 0)),
            pl.BlockSpec((1, C3), lambda b, i: (0, 0)),
        ],
        out_specs=[
            pl.BlockSpec((1, ts, HYENA_WIDTH), lambda b, i: (b, i, 0)),
            pl.BlockSpec((1, ts, HYENA_WIDTH), lambda b, i: (b, i, 0)),
        ],
        out_shape=[
            jax.ShapeDtypeStruct((B, S, HYENA_WIDTH), BF16),
            jax.ShapeDtypeStruct((B, S, HYENA_WIDTH), BF16),
        ],
        compiler_params=_params("parallel", "parallel"),
        name="short_conv",
    )(hy, hy, hy, conv_w, conv_b.reshape(1, C3))


FFT_N = 8192
FFT_N1 = 64
FFT_N2 = 128
FFT_CH_BLOCK = 32
FFT_GROUP = 4
FFT_UNROLL = 2


def _dft_constants():
    import numpy as np
    n1, n2, n = FFT_N1, FFT_N2, FFT_N
    k1 = np.arange(n1)[:, None]
    t1 = np.arange(n1)[None, :]
    f1 = np.exp(-2j * np.pi * k1 * t1 / n1)
    f1h = f1[:, : n1 // 2]
    w1c = np.block([[f1h.real, -f1h.imag], [f1h.imag, f1h.real]])
    w1r = np.concatenate([f1h.real, f1h.imag], axis=0)
    t2 = np.arange(n2)[:, None]
    k2 = np.arange(n2)[None, :]
    f2 = np.exp(-2j * np.pi * t2 * k2 / n2)
    w2a = np.concatenate([f2.real, f2.imag], axis=1)
    w2b = np.concatenate([-f2.imag, f2.real], axis=1)
    g2 = np.conj(f2)
    w3a = np.concatenate([g2.real, g2.imag], axis=1)
    w3b = np.concatenate([-g2.imag, g2.real], axis=1)
    h = np.conj(f1).T[: n1 // 2] / n
    w4 = np.block([[h.real, -h.imag], [h.imag, h.real]])
    tw = np.exp(-2j * np.pi * np.arange(n1)[:, None] * np.arange(n2)[None, :] / n)
    bf = lambda a: jnp.asarray(a, dtype=F32).astype(BF16)
    return dict(w1c=bf(w1c), w1r=bf(w1r), w2a=bf(w2a), w2b=bf(w2b), w3a=bf(w3a), w3b=bf(w3b),
                w4=bf(w4), tc=jnp.asarray(np.tile(tw.real, (1, FFT_GROUP)), F32),
                ts=jnp.asarray(np.tile(tw.imag, (1, FFT_GROUP)), F32))


def _dft_forward(x_ref, w1_ref, tc_ref, ts_ref, w2a_ref, w2b_ref, ar_ref, ai_ref):
    ng = x_ref.shape[0] // FFT_GROUP
    n1, n2 = FFT_N1, FFT_N2
    w1 = w1_ref[...]
    tc, ts = tc_ref[...], ts_ref[...]

    def body(g, carry):
        xg = jnp.concatenate([x_ref[g * FFT_GROUP + cl] for cl in range(FFT_GROUP)], axis=1)
        a = jnp.dot(w1, xg, preferred_element_type=F32)
        ar, ai = a[:n1], a[n1:]
        tr = (ar * tc - ai * ts).astype(BF16)
        ti = (ar * ts + ai * tc).astype(BF16)
        for cl in range(FFT_GROUP):
            rows = pl.ds(pl.multiple_of((g * FFT_GROUP + cl) * n1, n1), n1)
            ar_ref[rows, :] = tr[:, cl * n2:(cl + 1) * n2]
            ai_ref[rows, :] = ti[:, cl * n2:(cl + 1) * n2]
        return carry

    lax.fori_loop(0, ng, body, 0, unroll=FFT_UNROLL)
    return (jnp.dot(ar_ref[...], w2a_ref[...], preferred_element_type=F32)
            + jnp.dot(ai_ref[...], w2b_ref[...], preferred_element_type=F32))


def _spectrum_kernel(hf_ref, hb_ref, w1_ref, tc_ref, ts_ref, w2a_ref, w2b_ref, o_ref, ar_ref, ai_ref):
    n2 = FFT_N2
    f = _dft_forward(hf_ref, w1_ref, tc_ref, ts_ref, w2a_ref, w2b_ref, ar_ref, ai_ref)
    o_ref[...] = f.reshape(o_ref.shape)
    b = _dft_forward(hb_ref, w1_ref, tc_ref, ts_ref, w2a_ref, w2b_ref, ar_ref, ai_ref)
    b = b.reshape(o_ref.shape)
    o_ref[:, :, :n2] = o_ref[:, :, :n2] + b[:, :, :n2]
    o_ref[:, :, n2:] = o_ref[:, :, n2:] - b[:, :, n2:]


def _fftconv_kernel(x_ref, kf_ref, w1_ref, tc_ref, ts_ref, w2a_ref, w2b_ref, w3a_ref, w3b_ref,
                    w4_ref, y_ref, ar_ref, ai_ref, c_ref):
    cb = x_ref.shape[0]
    ng = cb // FFT_GROUP
    n1, n2 = FFT_N1, FFT_N2
    b = _dft_forward(x_ref, w1_ref, tc_ref, ts_ref, w2a_ref, w2b_ref, ar_ref, ai_ref)
    kf = kf_ref[...].reshape(cb * n1, 2 * n2)
    br, bi = b[:, :n2], b[:, n2:]
    kr, ki = kf[:, :n2], kf[:, n2:]
    ar_ref[...] = (br * kr - bi * ki).astype(BF16)
    ai_ref[...] = (br * ki + bi * kr).astype(BF16)
    c_ref[...] = (jnp.dot(ar_ref[...], w3a_ref[...], preferred_element_type=F32)
                  + jnp.dot(ai_ref[...], w3b_ref[...], preferred_element_type=F32))
    w4 = w4_ref[...]
    tc, ts = tc_ref[...], ts_ref[...]

    def body(g, carry):
        rows = [pl.ds(pl.multiple_of((g * FFT_GROUP + cl) * n1, n1), n1) for cl in range(FFT_GROUP)]
        cr = jnp.concatenate([c_ref[r, :n2] for r in rows], axis=1)
        ci = jnp.concatenate([c_ref[r, n2:] for r in rows], axis=1)
        dr = (cr * tc + ci * ts).astype(BF16)
        di = (ci * tc - cr * ts).astype(BF16)
        yg = (jnp.dot(w4[:, :n1], dr, preferred_element_type=F32)
              + jnp.dot(w4[:, n1:], di, preferred_element_type=F32)).astype(y_ref.dtype)
        for cl in range(FFT_GROUP):
            y_ref[g * FFT_GROUP + cl] = yg[:, cl * n2:(cl + 1) * n2]
        return carry

    lax.fori_loop(0, ng, body, 0, unroll=FFT_UNROLL)


def _const_spec(a):
    nd = a.ndim
    return pl.BlockSpec(a.shape, lambda *_: (0,) * nd)


def _filter_spectrum(hf_slabs, hb_slabs, cst):
    C = hf_slabs.shape[0]
    cb = FFT_CH_BLOCK
    consts = [cst["w1r"], cst["tc"], cst["ts"], cst["w2a"], cst["w2b"]]
    slab = pl.BlockSpec((cb, FFT_N1 // 2, FFT_N2), lambda i: (i, 0, 0))
    return pl.pallas_call(
        _spectrum_kernel,
        grid=(C // cb,),
        in_specs=[slab, slab] + [_const_spec(a) for a in consts],
        out_specs=pl.BlockSpec((cb, FFT_N1, 2 * FFT_N2), lambda i: (i, 0, 0)),
        out_shape=jax.ShapeDtypeStruct((C, FFT_N1, 2 * FFT_N2), F32),
        scratch_shapes=[pltpu.VMEM((cb * FFT_N1, FFT_N2), BF16), pltpu.VMEM((cb * FFT_N1, FFT_N2), BF16)],
        compiler_params=_params("parallel"),
        name="filter_spectrum",
    )(hf_slabs, hb_slabs, *consts)


FILTER_TIME_TILE = 512


def _filter_kernel(zt_ref, w1t_ref, b1_ref, wit_ref, bi_ref, fr_ref, wot_ref, dec_ref, hf_ref, hb_ref):
    hp = lax.Precision.HIGHEST
    h = jnp.sin(fr_ref[0] * (jnp.dot(w1t_ref[...], zt_ref[...], precision=hp,
                                     preferred_element_type=F32) + b1_ref[...]))
    for j in range(FILTER_INNER):
        h = jnp.sin(fr_ref[j + 1] * (jnp.dot(wit_ref[j], h, precision=hp,
                                             preferred_element_type=F32) + bi_ref[j]))
    o = jnp.dot(wot_ref[...], h, precision=hp, preferred_element_type=F32)
    dec = dec_ref[...]
    hf_ref[...] = o[:HYENA_WIDTH] * dec
    hb_ref[...] = o[HYENA_WIDTH:] * dec


def _hyena_filters(seq, w1, b1, w_inner, b_inner, freq, w_out):
    C = HYENA_WIDTH
    order = w1.shape[1]
    pos = jnp.arange(seq, dtype=F32)
    t = pos / (seq - 1)
    bands = (FILTER_EMB - 1) // 2
    f = jnp.linspace(1e-4, bands - 1, bands, dtype=F32)
    fw = ((2.0 * math.pi / seq) * pos)[:, None] * f[None, :]
    z = jnp.concatenate([t[:, None], jnp.cos(fw), -jnp.sin(fw)], axis=-1)
    zt = jnp.zeros((LANES, seq), F32).at[:FILTER_EMB].set(z.T)
    w1t = jnp.zeros((order, LANES), F32).at[:, :FILTER_EMB].set(w1.T)
    max_decay = math.log(DECAY_TARGET) / FAST_DECAY_PCT
    min_decay = math.log(DECAY_TARGET) / SLOW_DECAY_PCT
    deltas = jnp.abs(jnp.linspace(min_decay, max_decay, C, dtype=F32))
    dec_t = jnp.exp(-deltas[:, None] * t[None, :])
    tt = FILTER_TIME_TILE
    lane_blk = lambda r: pl.BlockSpec((r, tt), lambda i: (0, i))
    args = [zt, w1t, b1.reshape(order, 1), jnp.swapaxes(w_inner, 1, 2),
            b_inner.reshape(FILTER_INNER, order, 1), freq.reshape(FILTER_INNER + 1, order, 1),
            w_out.T, dec_t]
    return pl.pallas_call(
        _filter_kernel,
        grid=(seq // tt,),
        in_specs=[lane_blk(LANES)] + [_const_spec(a) for a in args[1:7]] + [lane_blk(C)],
        out_specs=[lane_blk(C), lane_blk(C)],
        out_shape=[jax.ShapeDtypeStruct((C, seq), F32), jax.ShapeDtypeStruct((C, seq), F32)],
        compiler_params=_params("parallel"),
        name="hyena_filters",
    )(*args)


def _fft_conv(x_slabs, kf, cst):
    P, C = x_slabs.shape[:2]
    cb = FFT_CH_BLOCK
    consts = [cst["w1c"], cst["tc"], cst["ts"], cst["w2a"], cst["w2b"], cst["w3a"], cst["w3b"], cst["w4"]]
    return pl.pallas_call(
        _fftconv_kernel,
        grid=(C // cb, P),
        in_specs=[pl.BlockSpec((None, cb, FFT_N1, FFT_N2), lambda i, p: (p, i, 0, 0)),
                  pl.BlockSpec((cb, FFT_N1, 2 * FFT_N2), lambda i, p: (i, 0, 0))]
                 + [_const_spec(a) for a in consts],
        out_specs=pl.BlockSpec((None, cb, FFT_N1, FFT_N2), lambda i, p: (p, i, 0, 0)),
        out_shape=jax.ShapeDtypeStruct((P, C, FFT_N1, FFT_N2), BF16),
        scratch_shapes=[pltpu.VMEM((cb * FFT_N1, FFT_N2), BF16), pltpu.VMEM((cb * FFT_N1, FFT_N2), BF16),
                        pltpu.VMEM((cb * FFT_N1, 2 * FFT_N2), F32)],
        compiler_params=_params("parallel", "arbitrary"),
        name="fft_conv",
    )(x_slabs, kf, *consts)


def _split_bf16(x):
    hi = x.astype(BF16)
    lo = (x - hi.astype(F32)).astype(BF16)
    return hi, lo


def _outproj_kernel(attn_ref, yc_ref, u_ref, x0_ref, d_ref, gn_ref, grp_ref, wo_ref, x_ref,
                    gf_ref, rwh_ref, rwl_ref, rb_ref, tri_ref, h_ref, hn_ref, rt_ref, cnt_out_ref,
                    cnt_ref):
    @pl.when(pl.program_id(0) == 0)
    def _():
        cnt_ref[...] = jnp.zeros_like(cnt_ref)

    tm = x_ref.shape[0]
    sub = tm // OUTPROJ_SUBTILES
    grp = grp_ref[...]
    gsz = HYENA_WIDTH // HYENA_GROUPS

    def sub_tile(r0, cnt_row):
        rows = slice(r0, r0 + sub)
        z = ((yc_ref[rows, :].astype(F32) + u_ref[rows, :].astype(F32) * d_ref[...])
             * x0_ref[rows, :].astype(F32))
        zh, zl = _split_bf16(z * z)
        ssq = (jnp.dot(zh, grp, preferred_element_type=F32)
               + jnp.dot(zl, grp, preferred_element_type=F32))
        hy_out = (z * lax.rsqrt(ssq * (1.0 / gsz) + EPS) * gn_ref[...]).astype(BF16)
        mix = (jnp.dot(attn_ref[rows, :], wo_ref[:ATTN_WIDTH, :], preferred_element_type=F32)
               + jnp.dot(hy_out, wo_ref[ATTN_WIDTH:, :], preferred_element_type=F32))
        h = x_ref[rows, :] + mix
        h_ref[rows, :] = h
        hn = _rms(h, gf_ref[...], EPS)
        hn_ref[rows, :] = hn.astype(BF16)
        nh, nl = _split_bf16(hn)
        logits = (jnp.dot(nh, rwh_ref[...], preferred_element_type=F32)
                  + jnp.dot(nl, rwh_ref[...], preferred_element_type=F32)
                  + jnp.dot(nh, rwl_ref[...], preferred_element_type=F32)) + rb_ref[...]

        lane = lax.broadcasted_iota(jnp.int32, logits.shape, 1)
        work = logits
        top_val, top_idx, top_hot = [], [], []
        for _ in range(TOP_K):
            m = jnp.max(work, axis=-1, keepdims=True)
            idx = jnp.min(jnp.where(work == m, lane, ROUTER_PAD), axis=-1, keepdims=True)
            hot = lane == idx
            top_val.append(m)
            top_idx.append(idx)
            top_hot.append(hot)
            work = jnp.where(hot, -jnp.inf, work)
        ex = [jnp.exp(v - top_val[0]) for v in top_val]
        inv_den = 1.0 / (ex[0] + ex[1] + ex[2] + ex[3])
        sel = jnp.zeros(logits.shape, F32)
        for hot in top_hot:
            sel = sel + hot.astype(F32)
        before = jnp.dot(tri_ref[...], sel.astype(BF16), preferred_element_type=F32) + cnt_row
        packed = jnp.zeros(logits.shape, F32)
        for r, hot in enumerate(top_hot):
            rank_r = jnp.sum(jnp.where(hot, before, 0.0), axis=-1, keepdims=True)
            packed = jnp.where(lane == r, top_idx[r].astype(F32), packed)
            packed = jnp.where(lane == TOP_K + r, rank_r, packed)
            packed = jnp.where(lane == 2 * TOP_K + r, ex[r] * inv_den, packed)
        rt_ref[rows, :] = packed
        return cnt_row + jnp.sum(sel, axis=0, keepdims=True)

    cnt_row = cnt_ref[0:1, :]
    for t in range(OUTPROJ_SUBTILES):
        cnt_row = sub_tile(t * sub, cnt_row)
    cnt_ref[...] = jnp.broadcast_to(cnt_row, cnt_ref.shape)
    cnt_out_ref[...] = cnt_ref[...]


def _out_proj(attn2, yc2, u2, x02, hyena_d, hyena_gn, w_out_bf, x2, g_ffn, router_w, router_b):
    T, D = x2.shape
    tm = OUTPROJ_TILE
    sub = tm // OUTPROJ_SUBTILES
    C = HYENA_WIDTH
    gid = jnp.arange(C) // (C // HYENA_GROUPS)
    grp = (gid[:, None] == gid[None, :]).astype(BF16)
    rw = jnp.zeros((D, ROUTER_PAD), F32).at[:, :N_EXPERTS].set(router_w)
    rwh, rwl = _split_bf16(rw)
    rb = jnp.full((1, ROUTER_PAD), -jnp.inf, F32).at[0, :N_EXPERTS].set(router_b)
    tri = (jnp.arange(sub)[:, None] > jnp.arange(sub)[None, :]).astype(BF16)
    row = lambda i: (i, 0)
    const = lambda i: (0, 0)
    return pl.pallas_call(
        _outproj_kernel,
        grid=(T // tm,),
        in_specs=[
            pl.BlockSpec((tm, ATTN_WIDTH), row),
            pl.BlockSpec((tm, C), row),
            pl.BlockSpec((tm, C), row),
            pl.BlockSpec((tm, C), row),
            pl.BlockSpec((1, C), const),
            pl.BlockSpec((1, C), const),
            pl.BlockSpec((C, C), const),
            pl.BlockSpec((D, D), const),
            pl.BlockSpec((tm, D), row),
            pl.BlockSpec((1, D), const),
            pl.BlockSpec((D, ROUTER_PAD), const),
            pl.BlockSpec((D, ROUTER_PAD), const),
            pl.BlockSpec((1, ROUTER_PAD), const),
            pl.BlockSpec((sub, sub), const),
        ],
        out_specs=[
            pl.BlockSpec((tm, D), row),
            pl.BlockSpec((tm, D), row),
            pl.BlockSpec((tm, ROUTER_PAD), row),
            pl.BlockSpec((SUBLANES, ROUTER_PAD), const),
        ],
        out_shape=[
            jax.ShapeDtypeStruct((T, D), F32),
            jax.ShapeDtypeStruct((T, D), BF16),
            jax.ShapeDtypeStruct((T, ROUTER_PAD), F32),
            jax.ShapeDtypeStruct((SUBLANES, ROUTER_PAD), F32),
        ],
        scratch_shapes=[pltpu.VMEM((SUBLANES, ROUTER_PAD), F32)],
        compiler_params=_params("arbitrary"),
        name="out_proj",
    )(attn2, yc2, u2, x02, hyena_d.reshape(1, C), hyena_gn.reshape(1, C), grp, w_out_bf, x2,
      g_ffn.reshape(1, D), rwh, rwl, rb, tri)


def _expert_kernel(be_ref, nused_ref, x_ref, wg_ref, bg_ref, wu_ref, bu_ref, wd_ref, bd_ref, *rest):
    y_ref, wg_bf, wu_bf, wd_bf = rest[-4:]
    i = pl.program_id(0)
    used = i < nused_ref[0]
    new_expert = jnp.logical_or(i == 0, be_ref[i] != be_ref[jnp.maximum(i - 1, 0)])

    @pl.when(jnp.logical_and(used, new_expert))
    def _():
        wg_bf[...] = wg_ref[0].astype(BF16)
        wu_bf[...] = wu_ref[0].astype(BF16)
        wd_bf[...] = wd_ref[0].astype(BF16)

    @pl.when(used)
    def _():
        x = x_ref[...]
        g = jnp.minimum(jnp.dot(x, wg_bf[...], preferred_element_type=F32) + bg_ref[0], SWIGLU_LIMIT)
        u = jnp.clip(jnp.dot(x, wu_bf[...], preferred_element_type=F32) + bu_ref[0],
                     -SWIGLU_LIMIT, SWIGLU_LIMIT)
        a = (u + 1.0) * (g * jax.nn.sigmoid(SWIGLU_ALPHA * g))
        y = jnp.dot(a.astype(BF16), wd_bf[...], preferred_element_type=F32) + bd_ref[0]
        y_ref[...] = y.astype(y_ref.dtype)

    @pl.when(jnp.logical_not(used))
    def _():
        y_ref[...] = jnp.zeros_like(y_ref)


def _experts(block_e, n_used, x_parts, wg, bg, wu, bu, wd, bd):
    Pp, D = x_parts[0].shape
    tm = EXPERT_TILE
    nb = Pp // tm
    P = Pp * len(x_parts)
    E, _, FF = wg.shape
    wmap = lambda i, be, nu: (be[i], 0, 0)
    y = None
    for part, xs in enumerate(x_parts):
        in_specs = [
            pl.BlockSpec((tm, D), lambda i, be, nu: (i, 0)),
            pl.BlockSpec((1, D, FF), wmap),
            pl.BlockSpec((1, 1, FF), wmap),
            pl.BlockSpec((1, D, FF), wmap),
            pl.BlockSpec((1, 1, FF), wmap),
            pl.BlockSpec((1, FF, D), wmap),
            pl.BlockSpec((1, 1, D), wmap),
        ]
        args = [block_e[part * nb:(part + 1) * nb], jnp.clip(n_used - part * nb, 0, nb), xs,
                wg, bg.reshape(E, 1, FF), wu, bu.reshape(E, 1, FF), wd, bd.reshape(E, 1, D)]
        aliases = {}
        if y is not None:
            in_specs.append(pl.BlockSpec(memory_space=pl.ANY))
            args.append(y)
            aliases = {len(args) - 1: 0}
        grid_spec = pltpu.PrefetchScalarGridSpec(
            num_scalar_prefetch=2,
            grid=(nb,),
            in_specs=in_specs,
            out_specs=pl.BlockSpec((tm, D), lambda i, be, nu, part=part: (i + part * nb, 0)),
            scratch_shapes=[pltpu.VMEM((D, FF), BF16), pltpu.VMEM((D, FF), BF16),
                            pltpu.VMEM((FF, D), BF16)],
        )
        y = pl.pallas_call(
            _expert_kernel,
            grid_spec=grid_spec,
            out_shape=jax.ShapeDtypeStruct((P, D), BF16),
            input_output_aliases=aliases,
            compiler_params=_params("arbitrary"),
            name=f"moe_experts_{part}",
        )(*args)
    return y


def _final_kernel(h_ref, yg_ref, rt_ref, p_ref, wp_ref, gp_ref, wg_ref, bg_ref, gfin_ref, *rest):
    o_ref = rest[-1]
    h = h_ref[...]
    for r in range(TOP_K):
        h = h + yg_ref[r].astype(F32) * rt_ref[:, 2 * TOP_K + r: 2 * TOP_K + r + 1]
    e = _rms(jnp.dot(p_ref[...].astype(BF16), wp_ref[...], preferred_element_type=F32),
             gp_ref[...], EPS)
    gate = jax.nn.sigmoid(jnp.dot(h.astype(BF16), wg_ref[...], preferred_element_type=F32)
                          + bg_ref[...])
    h = h + gate * e
    o_ref[...] = _rms(h, gfin_ref[...], EPS)


def _final(h1, yg_parts, route, p2, w_ple_bf, g_ple, w_gate_bf, b_gate, g_final):
    T, D = h1.shape
    tm = TOKEN_TILE
    PD = p2.shape[1]
    nb = T // tm // len(yg_parts)
    const = lambda i: (0, 0)
    out = None
    for part, yg in enumerate(yg_parts):
        row = lambda i, part=part: (i + part * nb, 0)
        in_specs = [
            pl.BlockSpec((tm, D), row),
            pl.BlockSpec((TOP_K, tm, D), lambda i: (0, i, 0)),
            pl.BlockSpec((tm, ROUTER_PAD), row),
            pl.BlockSpec((tm, PD), row),
            pl.BlockSpec((PD, D), const),
            pl.BlockSpec((1, D), const),
            pl.BlockSpec((D, D), const),
            pl.BlockSpec((1, D), const),
            pl.BlockSpec((1, D), const),
        ]
        args = [h1, yg, route, p2, w_ple_bf, g_ple.reshape(1, D), w_gate_bf, b_gate.reshape(1, D),
                g_final.reshape(1, D)]
        aliases = {}
        if out is not None:
            in_specs.append(pl.BlockSpec(memory_space=pl.ANY))
            args.append(out)
            aliases = {len(args) - 1: 0}
        out = pl.pallas_call(
            _final_kernel,
            grid=(nb,),
            in_specs=in_specs,
            out_specs=pl.BlockSpec((tm, D), row),
            out_shape=jax.ShapeDtypeStruct((T, D), F32),
            input_output_aliases=aliases,
            compiler_params=_params("parallel"),
            name=f"final_{part}",
        )(*args)
    return out


def _rope_tables(seq):
    d = DIFF_HEAD_DIM
    pos = jnp.arange(seq, dtype=F32)
    inv = ROPE_THETA ** (-jnp.arange(0, d, 2, dtype=F32) / d)
    ang = pos[:, None] * inv[None, :]
    cos, sin = jnp.cos(ang), jnp.sin(ang)
    z = jnp.zeros_like(sin)
    cos_t = jnp.tile(jnp.concatenate([cos, cos], -1), (1, LANES // d))
    s1_t = jnp.tile(jnp.concatenate([-sin, z], -1), (1, LANES // d))
    s2_t = jnp.tile(jnp.concatenate([z, sin], -1), (1, LANES // d))
    return cos_t, s1_t, s2_t


def _long_conv(u, hf_t, hb_t):
    B, L, C = u.shape
    assert 2 * L == FFT_N and B % 2 == 0
    P, R = B // 2, FFT_N1 // 2
    cst = _dft_constants()
    kf = _filter_spectrum(hf_t.reshape(C, R, FFT_N2).astype(BF16),
                          hb_t.reshape(C, R, FFT_N2).astype(BF16), cst)
    x_slabs = (u.reshape(2, P, R, FFT_N2, C).transpose(1, 4, 0, 2, 3)
               .reshape(P, C, FFT_N1, FFT_N2).astype(BF16))
    y = _fft_conv(x_slabs, kf, cst)
    return y.reshape(P, C, 2, R, FFT_N2).transpose(2, 0, 3, 4, 1).reshape(B, L, C)


def _dispatch_indices(route, cnt, T):
    tm = EXPERT_TILE
    A = T * TOP_K
    route_t = route[:, :2 * TOP_K].T
    top_e = route_t[:TOP_K].astype(jnp.int32)
    rank = route_t[TOP_K:].astype(jnp.int32)
    counts = cnt[0, :N_EXPERTS].astype(jnp.int32)
    padded = (counts + tm - 1) // tm * tm
    pad_end = jnp.cumsum(padded)
    pad_start = pad_end - padded
    start = jnp.cumsum(counts) - counts
    experts = jnp.arange(N_EXPERTS, dtype=jnp.int32)[:, None, None]
    pos = rank + jnp.sum(jnp.where(top_e[None] == experts, pad_start[:, None, None], 0), axis=0)
    n_blocks = -(-A // tm) + N_EXPERTS
    P = n_blocks * tm
    block_first = jnp.arange(n_blocks, dtype=jnp.int32) * tm
    block_e = jnp.minimum(jnp.sum(pad_end[None, :] <= block_first[:, None], axis=1),
                          N_EXPERTS - 1).astype(jnp.int32)
    n_used = (pad_end[-1] // tm).astype(jnp.int32).reshape(1)
    tok = jnp.broadcast_to(jnp.arange(T, dtype=jnp.int32)[None, :], (TOP_K, T))
    _, sorted_tok = lax.sort_key_val(pos.reshape(A), tok.reshape(A))
    in_blk = jnp.arange(tm, dtype=jnp.int32)[None, :]
    r = (block_first - pad_start[block_e])[:, None] + in_blk
    compact = jnp.clip(start[block_e][:, None] + r, 0, A - 1)
    filler = (block_first[:, None] + in_blk) % T
    slot_tok = jnp.where(r < counts[block_e][:, None], sorted_tok[compact], filler).reshape(P)
    return pos, slot_tok, block_e, n_used


def kernel(x, p, g_mix, w_in, hyena_conv_w, hyena_conv_b, flt_w1, flt_b1, flt_w_inner, flt_b_inner, flt_freq, flt_w_out, hyena_d, hyena_gn, lambda_q1, lambda_k1, lambda_q2, lambda_k2, attn_subln, w_out, g_ffn, router_w, router_b, w_gate, b_gate, w_up, b_up, w_down, b_down, w_ple, g_ple, w_ple_gate, b_ple_gate, g_final):
    B, S, D = x.shape
    T = B * S
    i = 0
    x2 = x.reshape(T, D)

    cos_t, s1_t, s2_t = _rope_tables(S)
    q, k, vt, hy = _in_proj(x2, g_mix[i], w_in[i].astype(BF16), cos_t, s1_t, s2_t, S)

    lam = (jnp.exp(jnp.sum(lambda_q1[i] * lambda_k1[i])) - jnp.exp(jnp.sum(lambda_q2[i] * lambda_k2[i]))
           + LAMBDA_INIT).reshape(1).astype(F32)
    attn = _diff_attention(lam, q.reshape(B, S, -1), k.reshape(B, S, -1), vt, attn_subln[i])

    u, hx0 = _short_conv(hy.reshape(B, S, -1), hyena_conv_w[i], hyena_conv_b[i])
    h_fwd, h_bwd = _hyena_filters(S, flt_w1[i], flt_b1[i], flt_w_inner[i], flt_b_inner[i],
                                  flt_freq[i], flt_w_out[i])
    yc = _long_conv(u, h_fwd, h_bwd)

    h1, hn, route, cnt = _out_proj(attn.reshape(T, -1), yc.reshape(T, -1), u.reshape(T, -1),
                                   hx0.reshape(T, -1), hyena_d[i], hyena_gn[i],
                                   w_out[i].astype(BF16), x2, g_ffn[i], router_w[i], router_b[i])

    pos, slot_tok, block_e, n_used = _dispatch_indices(route, cnt, T)
    n_parts = MOE_OVERLAP_PARTS
    slot_parts = jnp.split(slot_tok, n_parts)
    x_parts = [hn[s] for s in slot_parts]
    y = _experts(block_e, n_used, x_parts, w_gate[i], b_gate[i], w_up[i], b_up[i],
                 w_down[i], b_down[i])
    pos_parts = jnp.split(pos, n_parts, axis=1)
    yg_parts = [y[pp.reshape(-1)].reshape(TOP_K, T // n_parts, D) for pp in pos_parts]

    out = _final(h1, yg_parts, route, p[i].reshape(T, -1), w_ple[i].astype(BF16), g_ple[i],
                 w_ple_gate[i].astype(BF16), b_ple_gate[i], g_final)
    return out.reshape(B, S, D)
```

```python
import functools
import math

import jax
import jax.numpy as jnp
from jax import lax
from jax.experimental import pallas as pl
from jax.experimental.pallas import tpu as pltpu

F32 = jnp.float32
BF16 = jnp.bfloat16

D_MODEL = 1024
ATTN_WIDTH = 512
HYENA_WIDTH = 512
DIFF_HEADS = 4
DIFF_HEAD_DIM = 64
DIFF_V_DIM = 128
HYENA_GROUPS = 8
FILTER_EMB = 33
FILTER_INNER = 2
FAST_DECAY_PCT = 0.3
SLOW_DECAY_PCT = 1.5
DECAY_TARGET = 1e-2
ROPE_THETA = 10000.0
N_EXPERTS = 32
TOP_K = 4
SWIGLU_ALPHA = 1.702
SWIGLU_LIMIT = 7.0
EPS = 1e-6
SUBLN_EPS = 1e-5
LAMBDA_INIT = 0.8 - 0.6 * math.exp(-0.3 * 0)

V7X_VMEM_LIMIT_BYTES = 56 * 1024 * 1024
LANES = 128
SUBLANES = 8

TOKEN_TILE = 512
ATTN_STREAM_W = 512
ATTN_ONES_ROWS = 16
ATTN_UNROLL = 4
ATTN_KV_TILE = 512
EXPERT_TILE = 512
ROUTER_PAD = LANES
MOE_OVERLAP_PARTS = 2


def _params(*sem):
    return pltpu.CompilerParams(dimension_semantics=sem, vmem_limit_bytes=V7X_VMEM_LIMIT_BYTES)


def _rms(x, g, eps):
    return x * lax.rsqrt(jnp.mean(x * x, axis=-1, keepdims=True) + eps) * g


def _inproj_kernel(x_ref, g_ref, w_ref, wvt_ref, c_ref, s1_ref, s2_ref, q_ref, k_ref, vt_ref, hy_ref):
    a = _rms(x_ref[...], g_ref[...], EPS).astype(BF16)
    cos, s1, s2 = c_ref[...], s1_ref[...], s2_ref[...]

    half = DIFF_HEAD_DIM // 2

    def rope(t):
        return (t * cos + pltpu.roll(t, LANES - half, axis=1) * s1
                + pltpu.roll(t, half, axis=1) * s2)

    q_scale = (DIFF_HEAD_DIM ** -0.5) * math.log2(math.e)
    qk = jnp.dot(a, w_ref[:, : 2 * ATTN_WIDTH], preferred_element_type=F32)
    for j in range(ATTN_WIDTH // LANES):
        sl = slice(j * LANES, (j + 1) * LANES)
        q_ref[:, sl] = (rope(qk[:, sl]) * q_scale).astype(BF16)
        k_ref[:, sl] = rope(qk[:, ATTN_WIDTH + j * LANES: ATTN_WIDTH + (j + 1) * LANES]).astype(BF16)
    vt_ref[0] = lax.dot_general(wvt_ref[...], a, (((1,), (1,)), ((), ())),
                                preferred_element_type=F32).astype(BF16)
    hy_ref[...] = jnp.dot(a, w_ref[:, 3 * ATTN_WIDTH:], preferred_element_type=F32).astype(hy_ref.dtype)


def _in_proj(x2, g_mix, w_in_bf, cos_t, s1_t, s2_t, seq):
    T, D = x2.shape
    tm = TOKEN_TILE
    nseq = seq // tm
    wvt = w_in_bf[:, 2 * ATTN_WIDTH: 3 * ATTN_WIDTH].T
    row = lambda i: (i, 0)
    const = lambda i: (0, 0)
    pos = lambda i: (i % nseq, 0)
    return pl.pallas_call(
        _inproj_kernel,
        grid=(T // tm,),
        in_specs=[
            pl.BlockSpec((tm, D), row),
            pl.BlockSpec((1, D), const),
            pl.BlockSpec(w_in_bf.shape, const),
            pl.BlockSpec(wvt.shape, const),
            pl.BlockSpec((tm, LANES), pos),
            pl.BlockSpec((tm, LANES), pos),
            pl.BlockSpec((tm, LANES), pos),
        ],
        out_specs=[
            pl.BlockSpec((tm, ATTN_WIDTH), row),
            pl.BlockSpec((tm, ATTN_WIDTH), row),
            pl.BlockSpec((1, ATTN_WIDTH, tm), lambda i: (i, 0, 0)),
            pl.BlockSpec((tm, 3 * HYENA_WIDTH), row),
        ],
        out_shape=[
            jax.ShapeDtypeStruct((T, ATTN_WIDTH), BF16),
            jax.ShapeDtypeStruct((T, ATTN_WIDTH), BF16),
            jax.ShapeDtypeStruct((T // tm, ATTN_WIDTH, tm), BF16),
            jax.ShapeDtypeStruct((T, 3 * HYENA_WIDTH), BF16),
        ],
        compiler_params=_params("parallel"),
        name="in_proj",
    )(x2, g_mix.reshape(1, D), w_in_bf, wvt, cos_t, s1_t, s2_t)


def _attn_kernel(lam_ref, q_ref, k_ref, vt_ref, g_ref, o_ref, s_ref, m_ref, acc_ref, o0_ref, *, kc):
    w = ATTN_STREAM_W
    nc = k_ref.shape[1] // kc
    n_groups = q_ref.shape[1] // w

    def group_rows(grp):
        return pl.ds(pl.multiple_of(grp * w, w), w)

    def stream_q(grp, c):
        q = q_ref[0, group_rows(grp), :]
        lane = lax.broadcasted_iota(jnp.int32, q.shape, 1)
        keep = (lane < DIFF_HEAD_DIM) if c == 0 else (lane >= DIFF_HEAD_DIM)
        return jnp.where(keep, q, jnp.zeros_like(q))

    def fold8(t, op):
        r = t[0:SUBLANES]
        for j in range(1, kc // SUBLANES):
            r = op(r, t[j * SUBLANES:(j + 1) * SUBLANES])
        return r

    def score_chunk(i, qc, j):
        rows = pl.ds(pl.multiple_of(j * kc, kc), kc)
        s = lax.dot_general(k_ref[0, rows, :], qc, (((1,), (1,)), ((), ())),
                            preferred_element_type=F32)
        s_ref[i, rows, :] = s
        m_ref[i] = jnp.maximum(m_ref[i], fold8(s, jnp.maximum))

    ones_rows = jnp.ones((ATTN_ONES_ROWS, kc), BF16)

    def prob_chunk(i, j, m):
        rows = pl.ds(pl.multiple_of(j * kc, kc), kc)
        p = jnp.exp2(s_ref[i, rows, :] - m)
        v_aug = jnp.concatenate([vt_ref[j], ones_rows], axis=0)
        acc_ref[...] += jnp.dot(v_aug, p.astype(BF16), preferred_element_type=F32)

    def phase(read_buf, write_buf, q_next):
        if write_buf is not None:
            m_ref[write_buf] = jnp.full(m_ref.shape[1:], -jnp.inf, F32)
        if read_buf is not None:
            m = jnp.max(m_ref[read_buf], axis=0, keepdims=True)
            acc_ref[...] = jnp.zeros_like(acc_ref)

        def body(j, carry):
            if write_buf is not None:
                score_chunk(write_buf, q_next, j)
            if read_buf is not None:
                prob_chunk(read_buf, j, m)
            return carry

        lax.fori_loop(0, nc, body, 0, unroll=ATTN_UNROLL)
        if read_buf is None:
            return None
        return acc_ref[:DIFF_V_DIM, :] * (1.0 / acc_ref[DIFF_V_DIM:DIFF_V_DIM + 1, :])

    def finish_group(grp, out0, out1):
        o = out0 - lam_ref[0] * out1
        o = o * lax.rsqrt(jnp.mean(o * o, axis=0, keepdims=True) + SUBLN_EPS)
        o = o * (g_ref[...] * (1.0 - LAMBDA_INIT))
        o_ref[0, group_rows(grp), :] = o.T.astype(o_ref.dtype)

    phase(None, 0, stream_q(0, 0))

    def group_body(grp, carry):
        o0_ref[...] = phase(0, 1, stream_q(grp, 1))
        out1 = phase(1, 0, stream_q(grp + 1, 0))
        finish_group(grp, o0_ref[...], out1)
        return carry

    lax.fori_loop(0, n_groups - 1, group_body, 0)
    last = n_groups - 1
    o0_ref[...] = phase(0, 1, stream_q(last, 1))
    out1 = phase(1, None, None)
    finish_group(last, o0_ref[...], out1)


def _diff_attention(lam, q, k, vt, g_subln):
    B, S, _ = q.shape
    kc, w = ATTN_KV_TILE, ATTN_STREAM_W
    nc = S // kc
    assert vt.shape[2] == kc and S % w == 0 and S // w >= 2
    head = lambda b, h: (b, 0, h)
    return pl.pallas_call(
        functools.partial(_attn_kernel, kc=kc),
        grid=(B, DIFF_HEADS),
        in_specs=[
            pl.BlockSpec(memory_space=pltpu.SMEM),
            pl.BlockSpec((1, S, LANES), head),
            pl.BlockSpec((1, S, LANES), head),
            pl.BlockSpec((nc, DIFF_V_DIM, kc), lambda b, h: (b, h, 0)),
            pl.BlockSpec((DIFF_V_DIM, 1), lambda b, h: (0, 0)),
        ],
        out_specs=pl.BlockSpec((1, S, LANES), head),
        out_shape=jax.ShapeDtypeStruct((B, S, ATTN_WIDTH), BF16),
        scratch_shapes=[pltpu.VMEM((2, S, w), F32),
                        pltpu.VMEM((2, SUBLANES, w), F32),
                        pltpu.VMEM((DIFF_V_DIM + ATTN_ONES_ROWS, w), F32),
                        pltpu.VMEM((DIFF_V_DIM, w), F32)],
        compiler_params=_params("parallel", "parallel"),
        name="diff_attn",
    )(lam, q, k, vt, g_subln.reshape(DIFF_V_DIM, 1))


def _shortconv_kernel(hy_ref, prev_ref, next_ref, w_ref, b_ref, u_ref, x0_ref):
    i = pl.program_id(1)
    last = pl.num_programs(1) - 1
    x = hy_ref[0].astype(F32)
    ts = x.shape[0]
    prev_row = jnp.where(i == 0, 0.0, prev_ref[0, HALO_ROWS - 1:HALO_ROWS, :].astype(F32))
    next_row = jnp.where(i == last, 0.0, next_ref[0, 0:1, :].astype(F32))
    row = lax.broadcasted_iota(jnp.int32, (ts, 1), 0)
    xm = jnp.where(row == 0, prev_row, pltpu.roll(x, 1, axis=0))
    xp = jnp.where(row == ts - 1, next_row, pltpu.roll(x, ts - 1, axis=0))
    y = b_ref[...] + xm * w_ref[0:1, :] + x * w_ref[1:2, :] + xp * w_ref[2:3, :]
    C = HYENA_WIDTH
    u_ref[0] = (y[:, :C] * y[:, 2 * C:]).astype(u_ref.dtype)
    x0_ref[0] = y[:, C:2 * C].astype(x0_ref.dtype)


HALO_ROWS = 16


def _short_conv(hy, conv_w, conv_b):
    B, S, C3 = hy.shape
    ts = TOKEN_TILE
    nb = ts // HALO_ROWS
    return pl.pallas_call(
        _shortconv_kernel,
        grid=(B, S // ts),
        in_specs=[
            pl.BlockSpec((1, ts, C3), lambda b, i: (b, i, 0)),
            pl.BlockSpec((1, HALO_ROWS, C3), lambda b, i: (b, jnp.maximum(i * nb - 1, 0), 0)),
            pl.BlockSpec((1, HALO_ROWS, C3), lambda b, i: (b, jnp.minimum((i + 1) * nb, S // HALO_ROWS - 1), 0)),
            pl.BlockSpec((3, C3), lambda b, i: (0, 0)),
            pl.BlockSpec((1, C3), lambda b, i: (0, 0)),
        ],
        out_specs=[
            pl.BlockSpec((1, ts, HYENA_WIDTH), lambda b, i: (b, i, 0)),
            pl.BlockSpec((1, ts, HYENA_WIDTH), lambda b, i: (b, i, 0)),
        ],
        out_shape=[
            jax.ShapeDtypeStruct((B, S, HYENA_WIDTH), BF16),
            jax.ShapeDtypeStruct((B, S, HYENA_WIDTH), BF16),
        ],
        compiler_params=_params("parallel", "parallel"),
        name="short_conv",
    )(hy, hy, hy, conv_w, conv_b.reshape(1, C3))


FFT_N = 8192
FFT_N1 = 64
FFT_N2 = 128
FFT_CH_BLOCK = 32
FFT_GROUP = 4
FFT_UNROLL = 2


def _dft_constants():
    import numpy as np
    n1, n2, n = FFT_N1, FFT_N2, FFT_N
    k1 = np.arange(n1)[:, None]
    t1 = np.arange(n1)[None, :]
    f1 = np.exp(-2j * np.pi * k1 * t1 / n1)
    f1h = f1[:, : n1 // 2]
    w1c = np.block([[f1h.real, -f1h.imag], [f1h.imag, f1h.real]])
    w1r = np.concatenate([f1h.real, f1h.imag], axis=0)
    t2 = np.arange(n2)[:, None]
    k2 = np.arange(n2)[None, :]
    f2 = np.exp(-2j * np.pi * t2 * k2 / n2)
    w2a = np.concatenate([f2.real, f2.imag], axis=1)
    w2b = np.concatenate([-f2.imag, f2.real], axis=1)
    g2 = np.conj(f2)
    w3a = np.concatenate([g2.real, g2.imag], axis=1)
    w3b = np.concatenate([-g2.imag, g2.real], axis=1)
    h = np.conj(f1).T[: n1 // 2] / n
    w4 = np.block([[h.real, -h.imag], [h.imag, h.real]])
    tw = np.exp(-2j * np.pi * np.arange(n1)[:, None] * np.arange(n2)[None, :] / n)
    bf = lambda a: jnp.asarray(a, dtype=F32).astype(BF16)
    return dict(w1c=bf(w1c), w1r=bf(w1r), w2a=bf(w2a), w2b=bf(w2b), w3a=bf(w3a), w3b=bf(w3b),
                w4=bf(w4), tc=jnp.asarray(np.tile(tw.real, (1, FFT_GROUP)), F32),
                ts=jnp.asarray(np.tile(tw.imag, (1, FFT_GROUP)), F32))


def _dft_forward(x_ref, w1_ref, tc_ref, ts_ref, w2a_ref, w2b_ref, ar_ref, ai_ref):
    ng = x_ref.shape[0] // FFT_GROUP
    n1, n2 = FFT_N1, FFT_N2
    w1 = w1_ref[...]
    tc, ts = tc_ref[...], ts_ref[...]

    def body(g, carry):
        xg = jnp.concatenate([x_ref[g * FFT_GROUP + cl] for cl in range(FFT_GROUP)], axis=1)
        a = jnp.dot(w1, xg, preferred_element_type=F32)
        ar, ai = a[:n1], a[n1:]
        tr = (ar * tc - ai * ts).astype(BF16)
        ti = (ar * ts + ai * tc).astype(BF16)
        for cl in range(FFT_GROUP):
            rows = pl.ds(pl.multiple_of((g * FFT_GROUP + cl) * n1, n1), n1)
            ar_ref[rows, :] = tr[:, cl * n2:(cl + 1) * n2]
            ai_ref[rows, :] = ti[:, cl * n2:(cl + 1) * n2]
        return carry

    lax.fori_loop(0, ng, body, 0, unroll=FFT_UNROLL)
    return (jnp.dot(ar_ref[...], w2a_ref[...], preferred_element_type=F32)
            + jnp.dot(ai_ref[...], w2b_ref[...], preferred_element_type=F32))


def _spectrum_kernel(hf_ref, hb_ref, w1_ref, tc_ref, ts_ref, w2a_ref, w2b_ref, o_ref, ar_ref, ai_ref):
    n2 = FFT_N2
    f = _dft_forward(hf_ref, w1_ref, tc_ref, ts_ref, w2a_ref, w2b_ref, ar_ref, ai_ref)
    o_ref[...] = f.reshape(o_ref.shape)
    b = _dft_forward(hb_ref, w1_ref, tc_ref, ts_ref, w2a_ref, w2b_ref, ar_ref, ai_ref)
    b = b.reshape(o_ref.shape)
    o_ref[:, :, :n2] = o_ref[:, :, :n2] + b[:, :, :n2]
    o_ref[:, :, n2:] = o_ref[:, :, n2:] - b[:, :, n2:]


def _fftconv_kernel(x_ref, kf_ref, w1_ref, tc_ref, ts_ref, w2a_ref, w2b_ref, w3a_ref, w3b_ref,
                    w4_ref, y_ref, ar_ref, ai_ref, c_ref):
    cb = x_ref.shape[0]
    ng = cb // FFT_GROUP
    n1, n2 = FFT_N1, FFT_N2
    b = _dft_forward(x_ref, w1_ref, tc_ref, ts_ref, w2a_ref, w2b_ref, ar_ref, ai_ref)
    kf = kf_ref[...].reshape(cb * n1, 2 * n2)
    br, bi = b[:, :n2], b[:, n2:]
    kr, ki = kf[:, :n2], kf[:, n2:]
    ar_ref[...] = (br * kr - bi * ki).astype(BF16)
    ai_ref[...] = (br * ki + bi * kr).astype(BF16)
    c_ref[...] = (jnp.dot(ar_ref[...], w3a_ref[...], preferred_element_type=F32)
                  + jnp.dot(ai_ref[...], w3b_ref[...], preferred_element_type=F32))
    w4 = w4_ref[...]
    tc, ts = tc_ref[...], ts_ref[...]

    def body(g, carry):
        rows = [pl.ds(pl.multiple_of((g * FFT_GROUP + cl) * n1, n1), n1) for cl in range(FFT_GROUP)]
        cr = jnp.concatenate([c_ref[r, :n2] for r in rows], axis=1)
        ci = jnp.concatenate([c_ref[r, n2:] for r in rows], axis=1)
        dr = (cr * tc + ci * ts).astype(BF16)
        di = (ci * tc - cr * ts).astype(BF16)
        yg = (jnp.dot(w4[:, :n1], dr, preferred_element_type=F32)
              + jnp.dot(w4[:, n1:], di, preferred_element_type=F32)).astype(y_ref.dtype)
        for cl in range(FFT_GROUP):
            y_ref[g * FFT_GROUP + cl] = yg[:, cl * n2:(cl + 1) * n2]
        return carry

    lax.fori_loop(0, ng, body, 0, unroll=FFT_UNROLL)


def _const_spec(a):
    nd = a.ndim
    return pl.BlockSpec(a.shape, lambda *_: (0,) * nd)


def _filter_spectrum(hf_slabs, hb_slabs, cst):
    C = hf_slabs.shape[0]
    cb = FFT_CH_BLOCK
    consts = [cst["w1r"], cst["tc"], cst["ts"], cst["w2a"], cst["w2b"]]
    slab = pl.BlockSpec((cb, FFT_N1 // 2, FFT_N2), lambda i: (i, 0, 0))
    return pl.pallas_call(
        _spectrum_kernel,
        grid=(C // cb,),
        in_specs=[slab, slab] + [_const_spec(a) for a in consts],
        out_specs=pl.BlockSpec((cb, FFT_N1, 2 * FFT_N2), lambda i: (i, 0, 0)),
        out_shape=jax.ShapeDtypeStruct((C, FFT_N1, 2 * FFT_N2), F32),
        scratch_shapes=[pltpu.VMEM((cb * FFT_N1, FFT_N2), BF16), pltpu.VMEM((cb * FFT_N1, FFT_N2), BF16)],
        compiler_params=_params("parallel"),
        name="filter_spectrum",
    )(hf_slabs, hb_slabs, *consts)


FILTER_TIME_TILE = 512


def _filter_kernel(zt_ref, w1t_ref, b1_ref, wit_ref, bi_ref, fr_ref, wot_ref, dec_ref, hf_ref, hb_ref):
    hp = lax.Precision.HIGHEST
    h = jnp.sin(fr_ref[0] * (jnp.dot(w1t_ref[...], zt_ref[...], precision=hp,
                                     preferred_element_type=F32) + b1_ref[...]))
    for j in range(FILTER_INNER):
        h = jnp.sin(fr_ref[j + 1] * (jnp.dot(wit_ref[j], h, precision=hp,
                                             preferred_element_type=F32) + bi_ref[j]))
    o = jnp.dot(wot_ref[...], h, precision=hp, preferred_element_type=F32)
    dec = dec_ref[...]
    hf_ref[...] = o[:HYENA_WIDTH] * dec
    hb_ref[...] = o[HYENA_WIDTH:] * dec


def _hyena_filters(seq, w1, b1, w_inner, b_inner, freq, w_out):
    C = HYENA_WIDTH
    order = w1.shape[1]
    pos = jnp.arange(seq, dtype=F32)
    t = pos / (seq - 1)
    bands = (FILTER_EMB - 1) // 2
    f = jnp.linspace(1e-4, bands - 1, bands, dtype=F32)
    fw = ((2.0 * math.pi / seq) * pos)[:, None] * f[None, :]
    z = jnp.concatenate([t[:, None], jnp.cos(fw), -jnp.sin(fw)], axis=-1)
    zt = jnp.zeros((LANES, seq), F32).at[:FILTER_EMB].set(z.T)
    w1t = jnp.zeros((order, LANES), F32).at[:, :FILTER_EMB].set(w1.T)
    max_decay = math.log(DECAY_TARGET) / FAST_DECAY_PCT
    min_decay = math.log(DECAY_TARGET) / SLOW_DECAY_PCT
    deltas = jnp.abs(jnp.linspace(min_decay, max_decay, C, dtype=F32))
    dec_t = jnp.exp(-deltas[:, None] * t[None, :])
    tt = FILTER_TIME_TILE
    lane_blk = lambda r: pl.BlockSpec((r, tt), lambda i: (0, i))
    args = [zt, w1t, b1.reshape(order, 1), jnp.swapaxes(w_inner, 1, 2),
            b_inner.reshape(FILTER_INNER, order, 1), freq.reshape(FILTER_INNER + 1, order, 1),
            w_out.T, dec_t]
    return pl.pallas_call(
        _filter_kernel,
        grid=(seq // tt,),
        in_specs=[lane_blk(LANES)] + [_const_spec(a) for a in args[1:7]] + [lane_blk(C)],
        out_specs=[lane_blk(C), lane_blk(C)],
        out_shape=[jax.ShapeDtypeStruct((C, seq), F32), jax.ShapeDtypeStruct((C, seq), F32)],
        compiler_params=_params("parallel"),
        name="hyena_filters",
    )(*args)


def _fft_conv(x_slabs, kf, cst):
    P, C = x_slabs.shape[:2]
    cb = FFT_CH_BLOCK
    consts = [cst["w1c"], cst["tc"], cst["ts"], cst["w2a"], cst["w2b"], cst["w3a"], cst["w3b"], cst["w4"]]
    return pl.pallas_call(
        _fftconv_kernel,
        grid=(C // cb, P),
        in_specs=[pl.BlockSpec((None, cb, FFT_N1, FFT_N2), lambda i, p: (p, i, 0, 0)),
                  pl.BlockSpec((cb, FFT_N1, 2 * FFT_N2), lambda i, p: (i, 0, 0))]
                 + [_const_spec(a) for a in consts],
        out_specs=pl.BlockSpec((None, cb, FFT_N1, FFT_N2), lambda i, p: (p, i, 0, 0)),
        out_shape=jax.ShapeDtypeStruct((P, C, FFT_N1, FFT_N2), BF16),
        scratch_shapes=[pltpu.VMEM((cb * FFT_N1, FFT_N2), BF16), pltpu.VMEM((cb * FFT_N1, FFT_N2), BF16),
                        pltpu.VMEM((cb * FFT_N1, 2 * FFT_N2), F32)],
        compiler_params=_params("parallel", "arbitrary"),
        name="fft_conv",
    )(x_slabs, kf, *consts)


def _split_bf16(x):
    hi = x.astype(BF16)
    lo = (x - hi.astype(F32)).astype(BF16)
    return hi, lo


def _outproj_kernel(attn_ref, yc_ref, u_ref, x0_ref, d_ref, gn_ref, grp_ref, wo_ref, x_ref,
                    gf_ref, rwh_ref, rwl_ref, rb_ref, tri_ref, h_ref, hn_ref, rt_ref, cnt_out_ref,
                    cnt_ref):
    @pl.when(pl.program_id(0) == 0)
    def _():
        cnt_ref[...] = jnp.zeros_like(cnt_ref)

    grp = grp_ref[...]
    gsz = HYENA_WIDTH // HYENA_GROUPS

    def token_tile(cnt_row):
        rows = slice(None)
        z = ((yc_ref[rows, :].astype(F32) + u_ref[rows, :].astype(F32) * d_ref[...])
             * x0_ref[rows, :].astype(F32))
        zh, zl = _split_bf16(z * z)
        ssq = (jnp.dot(zh, grp, preferred_element_type=F32)
               + jnp.dot(zl, grp, preferred_element_type=F32))
        hy_out = (z * lax.rsqrt(ssq * (1.0 / gsz) + EPS) * gn_ref[...]).astype(BF16)
        mix = (jnp.dot(attn_ref[rows, :], wo_ref[:ATTN_WIDTH, :], preferred_element_type=F32)
               + jnp.dot(hy_out, wo_ref[ATTN_WIDTH:, :], preferred_element_type=F32))
        h = x_ref[rows, :] + mix
        h_ref[rows, :] = h
        hn = _rms(h, gf_ref[...], EPS)
        hn_ref[rows, :] = hn.astype(BF16)
        nh, nl = _split_bf16(hn)
        logits = (jnp.dot(nh, rwh_ref[...], preferred_element_type=F32)
                  + jnp.dot(nl, rwh_ref[...], preferred_element_type=F32)
                  + jnp.dot(nh, rwl_ref[...], preferred_element_type=F32)) + rb_ref[...]

        lane = lax.broadcasted_iota(jnp.int32, logits.shape, 1)
        work = logits
        top_val, top_idx, top_hot = [], [], []
        for _ in range(TOP_K):
            m = jnp.max(work, axis=-1, keepdims=True)
            idx = jnp.min(jnp.where(work == m, lane, ROUTER_PAD), axis=-1, keepdims=True)
            hot = lane == idx
            top_val.append(m)
            top_idx.append(idx)
            top_hot.append(hot)
            work = jnp.where(hot, -jnp.inf, work)
        ex = [jnp.exp(v - top_val[0]) for v in top_val]
        inv_den = 1.0 / (ex[0] + ex[1] + ex[2] + ex[3])
        sel = jnp.zeros(logits.shape, F32)
        for hot in top_hot:
            sel = sel + hot.astype(F32)
        before = jnp.dot(tri_ref[...], sel.astype(BF16), preferred_element_type=F32) + cnt_row
        packed = jnp.zeros(logits.shape, F32)
        for r, hot in enumerate(top_hot):
            rank_r = jnp.sum(jnp.where(hot, before, 0.0), axis=-1, keepdims=True)
            packed = jnp.where(lane == r, top_idx[r].astype(F32), packed)
            packed = jnp.where(lane == TOP_K + r, rank_r, packed)
            packed = jnp.where(lane == 2 * TOP_K + r, ex[r] * inv_den, packed)
        rt_ref[rows, :] = packed
        return cnt_row + jnp.sum(sel, axis=0, keepdims=True)

    cnt_row = token_tile(cnt_ref[0:1, :])
    cnt_ref[...] = jnp.broadcast_to(cnt_row, cnt_ref.shape)
    cnt_out_ref[...] = cnt_ref[...]


def _out_proj(attn2, yc2, u2, x02, hyena_d, hyena_gn, w_out_bf, x2, g_ffn, router_w, router_b):
    T, D = x2.shape
    tm = TOKEN_TILE
    C = HYENA_WIDTH
    gid = jnp.arange(C) // (C // HYENA_GROUPS)
    grp = (gid[:, None] == gid[None, :]).astype(BF16)
    rw = jnp.zeros((D, ROUTER_PAD), F32).at[:, :N_EXPERTS].set(router_w)
    rwh, rwl = _split_bf16(rw)
    rb = jnp.full((1, ROUTER_PAD), -jnp.inf, F32).at[0, :N_EXPERTS].set(router_b)
    tri = (jnp.arange(tm)[:, None] > jnp.arange(tm)[None, :]).astype(BF16)
    row = lambda i: (i, 0)
    const = lambda i: (0, 0)
    return pl.pallas_call(
        _outproj_kernel,
        grid=(T // tm,),
        in_specs=[
            pl.BlockSpec((tm, ATTN_WIDTH), row),
            pl.BlockSpec((tm, C), row),
            pl.BlockSpec((tm, C), row),
            pl.BlockSpec((tm, C), row),
            pl.BlockSpec((1, C), const),
            pl.BlockSpec((1, C), const),
            pl.BlockSpec((C, C), const),
            pl.BlockSpec((D, D), const),
            pl.BlockSpec((tm, D), row),
            pl.BlockSpec((1, D), const),
            pl.BlockSpec((D, ROUTER_PAD), const),
            pl.BlockSpec((D, ROUTER_PAD), const),
            pl.BlockSpec((1, ROUTER_PAD), const),
            pl.BlockSpec((tm, tm), const),
        ],
        out_specs=[
            pl.BlockSpec((tm, D), row),
            pl.BlockSpec((tm, D), row),
            pl.BlockSpec((tm, ROUTER_PAD), row),
            pl.BlockSpec((SUBLANES, ROUTER_PAD), const),
        ],
        out_shape=[
            jax.ShapeDtypeStruct((T, D), F32),
            jax.ShapeDtypeStruct((T, D), BF16),
            jax.ShapeDtypeStruct((T, ROUTER_PAD), F32),
            jax.ShapeDtypeStruct((SUBLANES, ROUTER_PAD), F32),
        ],
        scratch_shapes=[pltpu.VMEM((SUBLANES, ROUTER_PAD), F32)],
        compiler_params=_params("arbitrary"),
        name="out_proj",
    )(attn2, yc2, u2, x02, hyena_d.reshape(1, C), hyena_gn.reshape(1, C), grp, w_out_bf, x2,
      g_ffn.reshape(1, D), rwh, rwl, rb, tri)


def _expert_kernel(be_ref, nused_ref, x_ref, wg_ref, bg_ref, wu_ref, bu_ref, wd_ref, bd_ref, *rest):
    y_ref, wg_bf, wu_bf, wd_bf = rest[-4:]
    i = pl.program_id(0)
    used = i < nused_ref[0]
    new_expert = jnp.logical_or(i == 0, be_ref[i] != be_ref[jnp.maximum(i - 1, 0)])

    @pl.when(jnp.logical_and(used, new_expert))
    def _():
        wg_bf[...] = wg_ref[0].astype(BF16)
        wu_bf[...] = wu_ref[0].astype(BF16)
        wd_bf[...] = wd_ref[0].astype(BF16)

    @pl.when(used)
    def _():
        x = x_ref[...]
        g = jnp.minimum(jnp.dot(x, wg_bf[...], preferred_element_type=F32) + bg_ref[0], SWIGLU_LIMIT)
        u = jnp.clip(jnp.dot(x, wu_bf[...], preferred_element_type=F32) + bu_ref[0],
                     -SWIGLU_LIMIT, SWIGLU_LIMIT)
        a = (u + 1.0) * (g * jax.nn.sigmoid(SWIGLU_ALPHA * g))
        y = jnp.dot(a.astype(BF16), wd_bf[...], preferred_element_type=F32) + bd_ref[0]
        y_ref[...] = y.astype(y_ref.dtype)

    @pl.when(jnp.logical_not(used))
    def _():
        y_ref[...] = jnp.zeros_like(y_ref)


def _experts(block_e, n_used, x_parts, wg, bg, wu, bu, wd, bd):
    Pp, D = x_parts[0].shape
    tm = EXPERT_TILE
    nb = Pp // tm
    P = Pp * len(x_parts)
    E, _, FF = wg.shape
    wmap = lambda i, be, nu: (be[i], 0, 0)
    y = None
    for part, xs in enumerate(x_parts):
        in_specs = [
            pl.BlockSpec((tm, D), lambda i, be, nu: (i, 0)),
            pl.BlockSpec((1, D, FF), wmap),
            pl.BlockSpec((1, 1, FF), wmap),
            pl.BlockSpec((1, D, FF), wmap),
            pl.BlockSpec((1, 1, FF), wmap),
            pl.BlockSpec((1, FF, D), wmap),
            pl.BlockSpec((1, 1, D), wmap),
        ]
        args = [block_e[part * nb:(part + 1) * nb], jnp.clip(n_used - part * nb, 0, nb), xs,
                wg, bg.reshape(E, 1, FF), wu, bu.reshape(E, 1, FF), wd, bd.reshape(E, 1, D)]
        aliases = {}
        if y is not None:
            in_specs.append(pl.BlockSpec(memory_space=pl.ANY))
            args.append(y)
            aliases = {len(args) - 1: 0}
        grid_spec = pltpu.PrefetchScalarGridSpec(
            num_scalar_prefetch=2,
            grid=(nb,),
            in_specs=in_specs,
            out_specs=pl.BlockSpec((tm, D), lambda i, be, nu, part=part: (i + part * nb, 0)),
            scratch_shapes=[pltpu.VMEM((D, FF), BF16), pltpu.VMEM((D, FF), BF16),
                            pltpu.VMEM((FF, D), BF16)],
        )
        y = pl.pallas_call(
            _expert_kernel,
            grid_spec=grid_spec,
            out_shape=jax.ShapeDtypeStruct((P, D), BF16),
            input_output_aliases=aliases,
            compiler_params=_params("arbitrary"),
            name=f"moe_experts_{part}",
        )(*args)
    return y


def _final_kernel(h_ref, yg_ref, rt_ref, p_ref, wp_ref, gp_ref, wg_ref, bg_ref, gfin_ref, *rest):
    o_ref = rest[-1]
    h = h_ref[...]
    for r in range(TOP_K):
        h = h + yg_ref[r].astype(F32) * rt_ref[:, 2 * TOP_K + r: 2 * TOP_K + r + 1]
    e = _rms(jnp.dot(p_ref[...].astype(BF16), wp_ref[...], preferred_element_type=F32),
             gp_ref[...], EPS)
    gate = jax.nn.sigmoid(jnp.dot(h.astype(BF16), wg_ref[...], preferred_element_type=F32)
                          + bg_ref[...])
    h = h + gate * e
    o_ref[...] = _rms(h, gfin_ref[...], EPS)


def _final(h1, yg_parts, route, p2, w_ple_bf, g_ple, w_gate_bf, b_gate, g_final):
    T, D = h1.shape
    tm = TOKEN_TILE
    PD = p2.shape[1]
    nb = T // tm // len(yg_parts)
    const = lambda i: (0, 0)
    out = None
    for part, yg in enumerate(yg_parts):
        row = lambda i, part=part: (i + part * nb, 0)
        in_specs = [
            pl.BlockSpec((tm, D), row),
            pl.BlockSpec((TOP_K, tm, D), lambda i: (0, i, 0)),
            pl.BlockSpec((tm, ROUTER_PAD), row),
            pl.BlockSpec((tm, PD), row),
            pl.BlockSpec((PD, D), const),
            pl.BlockSpec((1, D), const),
            pl.BlockSpec((D, D), const),
            pl.BlockSpec((1, D), const),
            pl.BlockSpec((1, D), const),
        ]
        args = [h1, yg, route, p2, w_ple_bf, g_ple.reshape(1, D), w_gate_bf, b_gate.reshape(1, D),
                g_final.reshape(1, D)]
        aliases = {}
        if out is not None:
            in_specs.append(pl.BlockSpec(memory_space=pl.ANY))
            args.append(out)
            aliases = {len(args) - 1: 0}
        out = pl.pallas_call(
            _final_kernel,
            grid=(nb,),
            in_specs=in_specs,
            out_specs=pl.BlockSpec((tm, D), row),
            out_shape=jax.ShapeDtypeStruct((T, D), F32),
            input_output_aliases=aliases,
            compiler_params=_params("parallel"),
            name=f"final_{part}",
        )(*args)
    return out


def _rope_tables(seq):
    d = DIFF_HEAD_DIM
    pos = jnp.arange(seq, dtype=F32)
    inv = ROPE_THETA ** (-jnp.arange(0, d, 2, dtype=F32) / d)
    ang = pos[:, None] * inv[None, :]
    cos, sin = jnp.cos(ang), jnp.sin(ang)
    z = jnp.zeros_like(sin)
    cos_t = jnp.tile(jnp.concatenate([cos, cos], -1), (1, LANES // d))
    s1_t = jnp.tile(jnp.concatenate([-sin, z], -1), (1, LANES // d))
    s2_t = jnp.tile(jnp.concatenate([z, sin], -1), (1, LANES // d))
    return cos_t, s1_t, s2_t


def _long_conv(u, hf_t, hb_t):
    B, L, C = u.shape
    assert 2 * L == FFT_N and B % 2 == 0
    P, R = B // 2, FFT_N1 // 2
    cst = _dft_constants()
    kf = _filter_spectrum(hf_t.reshape(C, R, FFT_N2).astype(BF16),
                          hb_t.reshape(C, R, FFT_N2).astype(BF16), cst)
    x_slabs = (u.reshape(2, P, R, FFT_N2, C).transpose(1, 4, 0, 2, 3)
               .reshape(P, C, FFT_N1, FFT_N2).astype(BF16))
    y = _fft_conv(x_slabs, kf, cst)
    return y.reshape(P, C, 2, R, FFT_N2).transpose(2, 0, 3, 4, 1).reshape(B, L, C)


def _dispatch_indices(route, cnt, T):
    tm = EXPERT_TILE
    A = T * TOP_K
    route_t = route[:, :2 * TOP_K].T
    top_e = route_t[:TOP_K].astype(jnp.int32)
    rank = route_t[TOP_K:].astype(jnp.int32)
    counts = cnt[0, :N_EXPERTS].astype(jnp.int32)
    padded = (counts + tm - 1) // tm * tm
    pad_end = jnp.cumsum(padded)
    pad_start = pad_end - padded
    start = jnp.cumsum(counts) - counts
    experts = jnp.arange(N_EXPERTS, dtype=jnp.int32)[:, None, None]
    pos = rank + jnp.sum(jnp.where(top_e[None] == experts, pad_start[:, None, None], 0), axis=0)
    n_blocks = -(-A // tm) + N_EXPERTS
    P = n_blocks * tm
    block_first = jnp.arange(n_blocks, dtype=jnp.int32) * tm
    block_e = jnp.minimum(jnp.sum(pad_end[None, :] <= block_first[:, None], axis=1),
                          N_EXPERTS - 1).astype(jnp.int32)
    n_used = (pad_end[-1] // tm).astype(jnp.int32).reshape(1)
    tok = jnp.broadcast_to(jnp.arange(T, dtype=jnp.int32)[None, :], (TOP_K, T))
    _, sorted_tok = lax.sort_key_val(pos.reshape(A), tok.reshape(A))
    in_blk = jnp.arange(tm, dtype=jnp.int32)[None, :]
    r = (block_first - pad_start[block_e])[:, None] + in_blk
    compact = jnp.clip(start[block_e][:, None] + r, 0, A - 1)
    filler = (block_first[:, None] + in_blk) % T
    slot_tok = jnp.where(r < counts[block_e][:, None], sorted_tok[compact], filler).reshape(P)
    return pos, slot_tok, block_e, n_used


def kernel(x, p, g_mix, w_in, hyena_conv_w, hyena_conv_b, flt_w1, flt_b1, flt_w_inner, flt_b_inner, flt_freq, flt_w_out, hyena_d, hyena_gn, lambda_q1, lambda_k1, lambda_q2, lambda_k2, attn_subln, w_out, g_ffn, router_w, router_b, w_gate, b_gate, w_up, b_up, w_down, b_down, w_ple, g_ple, w_ple_gate, b_ple_gate, g_final):
    B, S, D = x.shape
    T = B * S
    i = 0
    x2 = x.reshape(T, D)

    cos_t, s1_t, s2_t = _rope_tables(S)
    q, k, vt, hy = _in_proj(x2, g_mix[i], w_in[i].astype(BF16), cos_t, s1_t, s2_t, S)

    lam = (jnp.exp(jnp.sum(lambda_q1[i] * lambda_k1[i])) - jnp.exp(jnp.sum(lambda_q2[i] * lambda_k2[i]))
           + LAMBDA_INIT).reshape(1).astype(F32)
    attn = _diff_attention(lam, q.reshape(B, S, -1), k.reshape(B, S, -1), vt, attn_subln[i])

    u, hx0 = _short_conv(hy.reshape(B, S, -1), hyena_conv_w[i], hyena_conv_b[i])
    h_fwd, h_bwd = _hyena_filters(S, flt_w1[i], flt_b1[i], flt_w_inner[i], flt_b_inner[i],
                                  flt_freq[i], flt_w_out[i])
    yc = _long_conv(u, h_fwd, h_bwd)

    h1, hn, route, cnt = _out_proj(attn.reshape(T, -1), yc.reshape(T, -1), u.reshape(T, -1),
                                   hx0.reshape(T, -1), hyena_d[i], hyena_gn[i],
                                   w_out[i].astype(BF16), x2, g_ffn[i], router_w[i], router_b[i])

    pos, slot_tok, block_e, n_used = _dispatch_indices(route, cnt, T)
    n_parts = MOE_OVERLAP_PARTS
    slot_parts = jnp.split(slot_tok, n_parts)
    x_parts = [hn[s] for s in slot_parts]
    y = _experts(block_e, n_used, x_parts, w_gate[i], b_gate[i], w_up[i], b_up[i],
                 w_down[i], b_down[i])
    pos_parts = jnp.split(pos, n_parts, axis=1)
    yg_parts = [y[pp.reshape(-1)].reshape(TOP_K, T // n_parts, D) for pp in pos_parts]

    out = _final(h1, yg_parts, route, p[i].reshape(T, -1), w_ple[i].astype(BF16), g_ple[i],
                 w_ple_gate[i].astype(BF16), b_ple_gate[i], g_final)
    return out.reshape(B, S, D)
```

```python
import functools
import math

import jax
import jax.numpy as jnp
from jax import lax
from jax.experimental import pallas as pl
from jax.experimental.pallas import tpu as pltpu

F32 = jnp.float32
BF16 = jnp.bfloat16

D_MODEL = 1024
ATTN_WIDTH = 512
HYENA_WIDTH = 512
DIFF_HEADS = 4
DIFF_HEAD_DIM = 64
DIFF_V_DIM = 128
HYENA_GROUPS = 8
FILTER_EMB = 33
FILTER_INNER = 2
FAST_DECAY_PCT = 0.3
SLOW_DECAY_PCT = 1.5
DECAY_TARGET = 1e-2
ROPE_THETA = 10000.0
N_EXPERTS = 32
TOP_K = 4
SWIGLU_ALPHA = 1.702
SWIGLU_LIMIT = 7.0
EPS = 1e-6
SUBLN_EPS = 1e-5
LAMBDA_INIT = 0.8 - 0.6 * math.exp(-0.3 * 0)

V7X_VMEM_LIMIT_BYTES = 56 * 1024 * 1024
LANES = 128
SUBLANES = 8

TOKEN_TILE = 512
ATTN_STREAM_W = 512
ATTN_ONES_ROWS = 16
ATTN_UNROLL = 4
ATTN_KV_TILE = 512
EXPERT_TILE = 512
ROUTER_PAD = LANES
DISPATCH_HEAD_DIV = 4
COMBINE_PARTS = 2


def _params(*sem):
    return pltpu.CompilerParams(dimension_semantics=sem, vmem_limit_bytes=V7X_VMEM_LIMIT_BYTES)


def _rms(x, g, eps):
    return x * lax.rsqrt(jnp.mean(x * x, axis=-1, keepdims=True) + eps) * g


def _inproj_kernel(x_ref, g_ref, w_ref, wvt_ref, c_ref, s1_ref, s2_ref, q_ref, k_ref, vt_ref, hy_ref):
    a = _rms(x_ref[...], g_ref[...], EPS).astype(BF16)
    cos, s1, s2 = c_ref[...], s1_ref[...], s2_ref[...]

    half = DIFF_HEAD_DIM // 2

    def rope(t):
        return (t * cos + pltpu.roll(t, LANES - half, axis=1) * s1
                + pltpu.roll(t, half, axis=1) * s2)

    q_scale = (DIFF_HEAD_DIM ** -0.5) * math.log2(math.e)
    qk = jnp.dot(a, w_ref[:, : 2 * ATTN_WIDTH], preferred_element_type=F32)
    for j in range(ATTN_WIDTH // LANES):
        sl = slice(j * LANES, (j + 1) * LANES)
        q_ref[:, sl] = (rope(qk[:, sl]) * q_scale).astype(BF16)
        k_ref[:, sl] = rope(qk[:, ATTN_WIDTH + j * LANES: ATTN_WIDTH + (j + 1) * LANES]).astype(BF16)
    vt_ref[0] = lax.dot_general(wvt_ref[...], a, (((1,), (1,)), ((), ())),
                                preferred_element_type=F32).astype(BF16)
    hy_ref[...] = jnp.dot(a, w_ref[:, 3 * ATTN_WIDTH:], preferred_element_type=F32).astype(hy_ref.dtype)


def _in_proj(x2, g_mix, w_in_bf, cos_t, s1_t, s2_t, seq):
    T, D = x2.shape
    tm = TOKEN_TILE
    nseq = seq // tm
    wvt = w_in_bf[:, 2 * ATTN_WIDTH: 3 * ATTN_WIDTH].T
    row = lambda i: (i, 0)
    const = lambda i: (0, 0)
    pos = lambda i: (i % nseq, 0)
    return pl.pallas_call(
        _inproj_kernel,
        grid=(T // tm,),
        in_specs=[
            pl.BlockSpec((tm, D), row),
            pl.BlockSpec((1, D), const),
            pl.BlockSpec(w_in_bf.shape, const),
            pl.BlockSpec(wvt.shape, const),
            pl.BlockSpec((tm, LANES), pos),
            pl.BlockSpec((tm, LANES), pos),
            pl.BlockSpec((tm, LANES), pos),
        ],
        out_specs=[
            pl.BlockSpec((tm, ATTN_WIDTH), row),
            pl.BlockSpec((tm, ATTN_WIDTH), row),
            pl.BlockSpec((1, ATTN_WIDTH, tm), lambda i: (i, 0, 0)),
            pl.BlockSpec((tm, 3 * HYENA_WIDTH), row),
        ],
        out_shape=[
            jax.ShapeDtypeStruct((T, ATTN_WIDTH), BF16),
            jax.ShapeDtypeStruct((T, ATTN_WIDTH), BF16),
            jax.ShapeDtypeStruct((T // tm, ATTN_WIDTH, tm), BF16),
            jax.ShapeDtypeStruct((T, 3 * HYENA_WIDTH), BF16),
        ],
        compiler_params=_params("parallel"),
        name="in_proj",
    )(x2, g_mix.reshape(1, D), w_in_bf, wvt, cos_t, s1_t, s2_t)


def _attn_kernel(lam_ref, q_ref, k_ref, vt_ref, g_ref, o_ref, s_ref, m_ref, acc_ref, o0_ref, *, kc):
    w = ATTN_STREAM_W
    nc = k_ref.shape[1] // kc
    n_groups = q_ref.shape[1] // w

    def group_rows(grp):
        return pl.ds(pl.multiple_of(grp * w, w), w)

    def stream_q(grp, c):
        q = q_ref[0, group_rows(grp), :]
        lane = lax.broadcasted_iota(jnp.int32, q.shape, 1)
        keep = (lane < DIFF_HEAD_DIM) if c == 0 else (lane >= DIFF_HEAD_DIM)
        return jnp.where(keep, q, jnp.zeros_like(q))

    def fold8(t, op):
        r = t[0:SUBLANES]
        for j in range(1, kc // SUBLANES):
            r = op(r, t[j * SUBLANES:(j + 1) * SUBLANES])
        return r

    def score_chunk(i, qc, j):
        rows = pl.ds(pl.multiple_of(j * kc, kc), kc)
        s = lax.dot_general(k_ref[0, rows, :], qc, (((1,), (1,)), ((), ())),
                            preferred_element_type=F32)
        s_ref[i, rows, :] = s
        m_ref[i] = jnp.maximum(m_ref[i], fold8(s, jnp.maximum))

    ones_rows = jnp.ones((ATTN_ONES_ROWS, kc), BF16)

    def prob_chunk(i, j, m):
        rows = pl.ds(pl.multiple_of(j * kc, kc), kc)
        p = jnp.exp2(s_ref[i, rows, :] - m)
        v_aug = jnp.concatenate([vt_ref[j], ones_rows], axis=0)
        acc_ref[...] += jnp.dot(v_aug, p.astype(BF16), preferred_element_type=F32)

    def phase(read_buf, write_buf, q_next):
        if write_buf is not None:
            m_ref[write_buf] = jnp.full(m_ref.shape[1:], -jnp.inf, F32)
        if read_buf is not None:
            m = jnp.max(m_ref[read_buf], axis=0, keepdims=True)
            acc_ref[...] = jnp.zeros_like(acc_ref)

        def body(j, carry):
            if write_buf is not None:
                score_chunk(write_buf, q_next, j)
            if read_buf is not None:
                prob_chunk(read_buf, j, m)
            return carry

        lax.fori_loop(0, nc, body, 0, unroll=ATTN_UNROLL)
        if read_buf is None:
            return None
        return acc_ref[:DIFF_V_DIM, :] * (1.0 / acc_ref[DIFF_V_DIM:DIFF_V_DIM + 1, :])

    def finish_group(grp, out0, out1):
        o = out0 - lam_ref[0] * out1
        o = o * lax.rsqrt(jnp.mean(o * o, axis=0, keepdims=True) + SUBLN_EPS)
        o = o * (g_ref[...] * (1.0 - LAMBDA_INIT))
        o_ref[0, group_rows(grp), :] = o.T.astype(o_ref.dtype)

    phase(None, 0, stream_q(0, 0))

    def group_body(grp, carry):
        o0_ref[...] = phase(0, 1, stream_q(grp, 1))
        out1 = phase(1, 0, stream_q(grp + 1, 0))
        finish_group(grp, o0_ref[...], out1)
        return carry

    lax.fori_loop(0, n_groups - 1, group_body, 0)
    last = n_groups - 1
    o0_ref[...] = phase(0, 1, stream_q(last, 1))
    out1 = phase(1, None, None)
    finish_group(last, o0_ref[...], out1)


def _diff_attention(lam, q, k, vt, g_subln):
    B, S, _ = q.shape
    kc, w = ATTN_KV_TILE, ATTN_STREAM_W
    nc = S // kc
    assert vt.shape[2] == kc and S % w == 0 and S // w >= 2
    head = lambda b, h: (b, 0, h)
    return pl.pallas_call(
        functools.partial(_attn_kernel, kc=kc),
        grid=(B, DIFF_HEADS),
        in_specs=[
            pl.BlockSpec(memory_space=pltpu.SMEM),
            pl.BlockSpec((1, S, LANES), head),
            pl.BlockSpec((1, S, LANES), head),
            pl.BlockSpec((nc, DIFF_V_DIM, kc), lambda b, h: (b, h, 0)),
            pl.BlockSpec((DIFF_V_DIM, 1), lambda b, h: (0, 0)),
        ],
        out_specs=pl.BlockSpec((1, S, LANES), head),
        out_shape=jax.ShapeDtypeStruct((B, S, ATTN_WIDTH), BF16),
        scratch_shapes=[pltpu.VMEM((2, S, w), F32),
                        pltpu.VMEM((2, SUBLANES, w), F32),
                        pltpu.VMEM((DIFF_V_DIM + ATTN_ONES_ROWS, w), F32),
                        pltpu.VMEM((DIFF_V_DIM, w), F32)],
        compiler_params=_params("parallel", "parallel"),
        name="diff_attn",
    )(lam, q, k, vt, g_subln.reshape(DIFF_V_DIM, 1))


def _shortconv_kernel(hy_ref, prev_ref, next_ref, w_ref, b_ref, u_ref, x0_ref):
    i = pl.program_id(1)
    last = pl.num_programs(1) - 1
    x = hy_ref[0].astype(F32)
    ts = x.shape[0]
    prev_row = jnp.where(i == 0, 0.0, prev_ref[0, HALO_ROWS - 1:HALO_ROWS, :].astype(F32))
    next_row = jnp.where(i == last, 0.0, next_ref[0, 0:1, :].astype(F32))
    row = lax.broadcasted_iota(jnp.int32, (ts, 1), 0)
    xm = jnp.where(row == 0, prev_row, pltpu.roll(x, 1, axis=0))
    xp = jnp.where(row == ts - 1, next_row, pltpu.roll(x, ts - 1, axis=0))
    y = b_ref[...] + xm * w_ref[0:1, :] + x * w_ref[1:2, :] + xp * w_ref[2:3, :]
    C = HYENA_WIDTH
    u_ref[0] = (y[:, :C] * y[:, 2 * C:]).astype(u_ref.dtype)
    x0_ref[0] = y[:, C:2 * C].astype(x0_ref.dtype)


HALO_ROWS = 16


def _short_conv(hy, conv_w, conv_b):
    B, S, C3 = hy.shape
    ts = TOKEN_TILE
    nb = ts // HALO_ROWS
    return pl.pallas_call(
        _shortconv_kernel,
        grid=(B, S // ts),
        in_specs=[
            pl.BlockSpec((1, ts, C3), lambda b, i: (b, i, 0)),
            pl.BlockSpec((1, HALO_ROWS, C3), lambda b, i: (b, jnp.maximum(i * nb - 1, 0), 0)),
            pl.BlockSpec((1, HALO_ROWS, C3), lambda b, i: (b, jnp.minimum((i + 1) * nb, S // HALO_ROWS - 1), 0)),
            pl.BlockSpec((3, C3), lambda b, i: (0, 0)),
            pl.BlockSpec((1, C3), lambda b, i: (0, 0)),
        ],
        out_specs=[
            pl.BlockSpec((1, ts, HYENA_WIDTH), lambda b, i: (b, i, 0)),
            pl.BlockSpec((1, ts, HYENA_WIDTH), lambda b, i: (b, i, 0)),
        ],
        out_shape=[
            jax.ShapeDtypeStruct((B, S, HYENA_WIDTH), BF16),
            jax.ShapeDtypeStruct((B, S, HYENA_WIDTH), BF16),
        ],
        compiler_params=_params("parallel", "parallel"),
        name="short_conv",
    )(hy, hy, hy, conv_w, conv_b.reshape(1, C3))


FFT_N = 8192
FFT_N1 = 64
FFT_N2 = 128
FFT_CH_BLOCK = 32
FFT_GROUP = 4
FFT_UNROLL = 2


def _dft_constants():
    import numpy as np
    n1, n2, n = FFT_N1, FFT_N2, FFT_N
    k1 = np.arange(n1)[:, None]
    t1 = np.arange(n1)[None, :]
    f1 = np.exp(-2j * np.pi * k1 * t1 / n1)
    f1h = f1[:, : n1 // 2]
    w1c = np.block([[f1h.real, -f1h.imag], [f1h.imag, f1h.real]])
    w1r = np.concatenate([f1h.real, f1h.imag], axis=0)
    t2 = np.arange(n2)[:, None]
    k2 = np.arange(n2)[None, :]
    f2 = np.exp(-2j * np.pi * t2 * k2 / n2)
    w2a = np.concatenate([f2.real, f2.imag], axis=1)
    w2b = np.concatenate([-f2.imag, f2.real], axis=1)
    g2 = np.conj(f2)
    w3a = np.concatenate([g2.real, g2.imag], axis=1)
    w3b = np.concatenate([-g2.imag, g2.real], axis=1)
    h = np.conj(f1).T[: n1 // 2] / n
    w4 = np.block([[h.real, -h.imag], [h.imag, h.real]])
    tw = np.exp(-2j * np.pi * np.arange(n1)[:, None] * np.arange(n2)[None, :] / n)
    bf = lambda a: jnp.asarray(a, dtype=F32).astype(BF16)
    return dict(w1c=bf(w1c), w1r=bf(w1r), w2a=bf(w2a), w2b=bf(w2b), w3a=bf(w3a), w3b=bf(w3b),
                w4=bf(w4), tc=jnp.asarray(np.tile(tw.real, (1, FFT_GROUP)), F32),
                ts=jnp.asarray(np.tile(tw.imag, (1, FFT_GROUP)), F32))


def _dft_forward(x_ref, w1_ref, tc_ref, ts_ref, w2a_ref, w2b_ref, ar_ref, ai_ref):
    ng = x_ref.shape[0] // FFT_GROUP
    n1, n2 = FFT_N1, FFT_N2
    w1 = w1_ref[...]
    tc, ts = tc_ref[...], ts_ref[...]

    def body(g, carry):
        xg = jnp.concatenate([x_ref[g * FFT_GROUP + cl] for cl in range(FFT_GROUP)], axis=1)
        a = jnp.dot(w1, xg, preferred_element_type=F32)
        ar, ai = a[:n1], a[n1:]
        tr = (ar * tc - ai * ts).astype(BF16)
        ti = (ar * ts + ai * tc).astype(BF16)
        for cl in range(FFT_GROUP):
            rows = pl.ds(pl.multiple_of((g * FFT_GROUP + cl) * n1, n1), n1)
            ar_ref[rows, :] = tr[:, cl * n2:(cl + 1) * n2]
            ai_ref[rows, :] = ti[:, cl * n2:(cl + 1) * n2]
        return carry

    lax.fori_loop(0, ng, body, 0, unroll=FFT_UNROLL)
    return (jnp.dot(ar_ref[...], w2a_ref[...], preferred_element_type=F32)
            + jnp.dot(ai_ref[...], w2b_ref[...], preferred_element_type=F32))


def _spectrum_kernel(hf_ref, hb_ref, w1_ref, tc_ref, ts_ref, w2a_ref, w2b_ref, o_ref, ar_ref, ai_ref):
    n2 = FFT_N2
    f = _dft_forward(hf_ref, w1_ref, tc_ref, ts_ref, w2a_ref, w2b_ref, ar_ref, ai_ref)
    o_ref[...] = f.reshape(o_ref.shape)
    b = _dft_forward(hb_ref, w1_ref, tc_ref, ts_ref, w2a_ref, w2b_ref, ar_ref, ai_ref)
    b = b.reshape(o_ref.shape)
    o_ref[:, :, :n2] = o_ref[:, :, :n2] + b[:, :, :n2]
    o_ref[:, :, n2:] = o_ref[:, :, n2:] - b[:, :, n2:]


def _fftconv_kernel(x_ref, kf_ref, w1_ref, tc_ref, ts_ref, w2a_ref, w2b_ref, w3a_ref, w3b_ref,
                    w4_ref, y_ref, ar_ref, ai_ref, c_ref):
    cb = x_ref.shape[0]
    ng = cb // FFT_GROUP
    n1, n2 = FFT_N1, FFT_N2
    b = _dft_forward(x_ref, w1_ref, tc_ref, ts_ref, w2a_ref, w2b_ref, ar_ref, ai_ref)
    kf = kf_ref[...].reshape(cb * n1, 2 * n2)
    br, bi = b[:, :n2], b[:, n2:]
    kr, ki = kf[:, :n2], kf[:, n2:]
    ar_ref[...] = (br * kr - bi * ki).astype(BF16)
    ai_ref[...] = (br * ki + bi * kr).astype(BF16)
    c_ref[...] = (jnp.dot(ar_ref[...], w3a_ref[...], preferred_element_type=F32)
                  + jnp.dot(ai_ref[...], w3b_ref[...], preferred_element_type=F32))
    w4 = w4_ref[...]
    tc, ts = tc_ref[...], ts_ref[...]

    def body(g, carry):
        rows = [pl.ds(pl.multiple_of((g * FFT_GROUP + cl) * n1, n1), n1) for cl in range(FFT_GROUP)]
        cr = jnp.concatenate([c_ref[r, :n2] for r in rows], axis=1)
        ci = jnp.concatenate([c_ref[r, n2:] for r in rows], axis=1)
        dr = (cr * tc + ci * ts).astype(BF16)
        di = (ci * tc - cr * ts).astype(BF16)
        yg = (jnp.dot(w4[:, :n1], dr, preferred_element_type=F32)
              + jnp.dot(w4[:, n1:], di, preferred_element_type=F32)).astype(y_ref.dtype)
        for cl in range(FFT_GROUP):
            y_ref[g * FFT_GROUP + cl] = yg[:, cl * n2:(cl + 1) * n2]
        return carry

    lax.fori_loop(0, ng, body, 0, unroll=FFT_UNROLL)


def _const_spec(a):
    nd = a.ndim
    return pl.BlockSpec(a.shape, lambda *_: (0,) * nd)


def _filter_spectrum(hf_slabs, hb_slabs, cst):
    C = hf_slabs.shape[0]
    cb = FFT_CH_BLOCK
    consts = [cst["w1r"], cst["tc"], cst["ts"], cst["w2a"], cst["w2b"]]
    slab = pl.BlockSpec((cb, FFT_N1 // 2, FFT_N2), lambda i: (i, 0, 0))
    return pl.pallas_call(
        _spectrum_kernel,
        grid=(C // cb,),
        in_specs=[slab, slab] + [_const_spec(a) for a in consts],
        out_specs=pl.BlockSpec((cb, FFT_N1, 2 * FFT_N2), lambda i: (i, 0, 0)),
        out_shape=jax.ShapeDtypeStruct((C, FFT_N1, 2 * FFT_N2), F32),
        scratch_shapes=[pltpu.VMEM((cb * FFT_N1, FFT_N2), BF16), pltpu.VMEM((cb * FFT_N1, FFT_N2), BF16)],
        compiler_params=_params("parallel"),
        name="filter_spectrum",
    )(hf_slabs, hb_slabs, *consts)


FILTER_TIME_TILE = 512


def _filter_kernel(zt_ref, w1t_ref, b1_ref, wit_ref, bi_ref, fr_ref, wot_ref, dec_ref, hf_ref, hb_ref):
    hp = lax.Precision.HIGHEST
    h = jnp.sin(fr_ref[0] * (jnp.dot(w1t_ref[...], zt_ref[...], precision=hp,
                                     preferred_element_type=F32) + b1_ref[...]))
    for j in range(FILTER_INNER):
        h = jnp.sin(fr_ref[j + 1] * (jnp.dot(wit_ref[j], h, precision=hp,
                                             preferred_element_type=F32) + bi_ref[j]))
    o = jnp.dot(wot_ref[...], h, precision=hp, preferred_element_type=F32)
    dec = dec_ref[...]
    hf_ref[...] = o[:HYENA_WIDTH] * dec
    hb_ref[...] = o[HYENA_WIDTH:] * dec


def _hyena_filters(seq, w1, b1, w_inner, b_inner, freq, w_out):
    C = HYENA_WIDTH
    order = w1.shape[1]
    pos = jnp.arange(seq, dtype=F32)
    t = pos / (seq - 1)
    bands = (FILTER_EMB - 1) // 2
    f = jnp.linspace(1e-4, bands - 1, bands, dtype=F32)
    fw = ((2.0 * math.pi / seq) * pos)[:, None] * f[None, :]
    z = jnp.concatenate([t[:, None], jnp.cos(fw), -jnp.sin(fw)], axis=-1)
    zt = jnp.zeros((LANES, seq), F32).at[:FILTER_EMB].set(z.T)
    w1t = jnp.zeros((order, LANES), F32).at[:, :FILTER_EMB].set(w1.T)
    max_decay = math.log(DECAY_TARGET) / FAST_DECAY_PCT
    min_decay = math.log(DECAY_TARGET) / SLOW_DECAY_PCT
    deltas = jnp.abs(jnp.linspace(min_decay, max_decay, C, dtype=F32))
    dec_t = jnp.exp(-deltas[:, None] * t[None, :])
    tt = FILTER_TIME_TILE
    lane_blk = lambda r: pl.BlockSpec((r, tt), lambda i: (0, i))
    args = [zt, w1t, b1.reshape(order, 1), jnp.swapaxes(w_inner, 1, 2),
            b_inner.reshape(FILTER_INNER, order, 1), freq.reshape(FILTER_INNER + 1, order, 1),
            w_out.T, dec_t]
    return pl.pallas_call(
        _filter_kernel,
        grid=(seq // tt,),
        in_specs=[lane_blk(LANES)] + [_const_spec(a) for a in args[1:7]] + [lane_blk(C)],
        out_specs=[lane_blk(C), lane_blk(C)],
        out_shape=[jax.ShapeDtypeStruct((C, seq), F32), jax.ShapeDtypeStruct((C, seq), F32)],
        compiler_params=_params("parallel"),
        name="hyena_filters",
    )(*args)


def _fft_conv(x_slabs, kf, cst):
    P, C = x_slabs.shape[:2]
    cb = FFT_CH_BLOCK
    consts = [cst["w1c"], cst["tc"], cst["ts"], cst["w2a"], cst["w2b"], cst["w3a"], cst["w3b"], cst["w4"]]
    return pl.pallas_call(
        _fftconv_kernel,
        grid=(C // cb, P),
        in_specs=[pl.BlockSpec((None, cb, FFT_N1, FFT_N2), lambda i, p: (p, i, 0, 0)),
                  pl.BlockSpec((cb, FFT_N1, 2 * FFT_N2), lambda i, p: (i, 0, 0))]
                 + [_const_spec(a) for a in consts],
        out_specs=pl.BlockSpec((None, cb, FFT_N1, FFT_N2), lambda i, p: (p, i, 0, 0)),
        out_shape=jax.ShapeDtypeStruct((P, C, FFT_N1, FFT_N2), BF16),
        scratch_shapes=[pltpu.VMEM((cb * FFT_N1, FFT_N2), BF16), pltpu.VMEM((cb * FFT_N1, FFT_N2), BF16),
                        pltpu.VMEM((cb * FFT_N1, 2 * FFT_N2), F32)],
        compiler_params=_params("parallel", "arbitrary"),
        name="fft_conv",
    )(x_slabs, kf, *consts)


def _split_bf16(x):
    hi = x.astype(BF16)
    lo = (x - hi.astype(F32)).astype(BF16)
    return hi, lo


def _outproj_kernel(attn_ref, yc_ref, u_ref, x0_ref, d_ref, gn_ref, grp_ref, wo_ref, x_ref,
                    gf_ref, rwh_ref, rwl_ref, rb_ref, tri_ref, h_ref, hn_ref, rt_ref, cnt_out_ref,
                    cnt_ref):
    @pl.when(pl.program_id(0) == 0)
    def _():
        cnt_ref[...] = jnp.zeros_like(cnt_ref)

    grp = grp_ref[...]
    gsz = HYENA_WIDTH // HYENA_GROUPS

    def token_tile(cnt_row):
        rows = slice(None)
        z = ((yc_ref[rows, :].astype(F32) + u_ref[rows, :].astype(F32) * d_ref[...])
             * x0_ref[rows, :].astype(F32))
        zh, zl = _split_bf16(z * z)
        ssq = (jnp.dot(zh, grp, preferred_element_type=F32)
               + jnp.dot(zl, grp, preferred_element_type=F32))
        hy_out = (z * lax.rsqrt(ssq * (1.0 / gsz) + EPS) * gn_ref[...]).astype(BF16)
        mix = (jnp.dot(attn_ref[rows, :], wo_ref[:ATTN_WIDTH, :], preferred_element_type=F32)
               + jnp.dot(hy_out, wo_ref[ATTN_WIDTH:, :], preferred_element_type=F32))
        h = x_ref[rows, :] + mix
        h_ref[rows, :] = h
        hn = _rms(h, gf_ref[...], EPS)
        hn_ref[rows, :] = hn.astype(BF16)
        nh, nl = _split_bf16(hn)
        logits = (jnp.dot(nh, rwh_ref[...], preferred_element_type=F32)
                  + jnp.dot(nl, rwh_ref[...], preferred_element_type=F32)
                  + jnp.dot(nh, rwl_ref[...], preferred_element_type=F32)) + rb_ref[...]

        lane = lax.broadcasted_iota(jnp.int32, logits.shape, 1)
        work = logits
        top_val, top_idx, top_hot = [], [], []
        for _ in range(TOP_K):
            m = jnp.max(work, axis=-1, keepdims=True)
            idx = jnp.min(jnp.where(work == m, lane, ROUTER_PAD), axis=-1, keepdims=True)
            hot = lane == idx
            top_val.append(m)
            top_idx.append(idx)
            top_hot.append(hot)
            work = jnp.where(hot, -jnp.inf, work)
        ex = [jnp.exp(v - top_val[0]) for v in top_val]
        inv_den = 1.0 / (ex[0] + ex[1] + ex[2] + ex[3])
        sel = jnp.zeros(logits.shape, F32)
        for hot in top_hot:
            sel = sel + hot.astype(F32)
        before = jnp.dot(tri_ref[...], sel.astype(BF16), preferred_element_type=F32) + cnt_row
        packed = jnp.zeros(logits.shape, F32)
        for r, hot in enumerate(top_hot):
            rank_r = jnp.sum(jnp.where(hot, before, 0.0), axis=-1, keepdims=True)
            packed = jnp.where(lane == r, top_idx[r].astype(F32), packed)
            packed = jnp.where(lane == TOP_K + r, rank_r, packed)
            packed = jnp.where(lane == 2 * TOP_K + r, ex[r] * inv_den, packed)
        rt_ref[rows, :] = packed
        return cnt_row + jnp.sum(sel, axis=0, keepdims=True)

    cnt_row = token_tile(cnt_ref[0:1, :])
    cnt_ref[...] = jnp.broadcast_to(cnt_row, cnt_ref.shape)
    cnt_out_ref[...] = cnt_ref[...]


def _out_proj(attn2, yc2, u2, x02, hyena_d, hyena_gn, w_out_bf, x2, g_ffn, router_w, router_b):
    T, D = x2.shape
    tm = TOKEN_TILE
    C = HYENA_WIDTH
    gid = jnp.arange(C) // (C // HYENA_GROUPS)
    grp = (gid[:, None] == gid[None, :]).astype(BF16)
    rw = jnp.zeros((D, ROUTER_PAD), F32).at[:, :N_EXPERTS].set(router_w)
    rwh, rwl = _split_bf16(rw)
    rb = jnp.full((1, ROUTER_PAD), -jnp.inf, F32).at[0, :N_EXPERTS].set(router_b)
    tri = (jnp.arange(tm)[:, None] > jnp.arange(tm)[None, :]).astype(BF16)
    row = lambda i: (i, 0)
    const = lambda i: (0, 0)
    return pl.pallas_call(
        _outproj_kernel,
        grid=(T // tm,),
        in_specs=[
            pl.BlockSpec((tm, ATTN_WIDTH), row),
            pl.BlockSpec((tm, C), row),
            pl.BlockSpec((tm, C), row),
            pl.BlockSpec((tm, C), row),
            pl.BlockSpec((1, C), const),
            pl.BlockSpec((1, C), const),
            pl.BlockSpec((C, C), const),
            pl.BlockSpec((D, D), const),
            pl.BlockSpec((tm, D), row),
            pl.BlockSpec((1, D), const),
            pl.BlockSpec((D, ROUTER_PAD), const),
            pl.BlockSpec((D, ROUTER_PAD), const),
            pl.BlockSpec((1, ROUTER_PAD), const),
            pl.BlockSpec((tm, tm), const),
        ],
        out_specs=[
            pl.BlockSpec((tm, D), row),
            pl.BlockSpec((tm, D), row),
            pl.BlockSpec((tm, ROUTER_PAD), row),
            pl.BlockSpec((SUBLANES, ROUTER_PAD), const),
        ],
        out_shape=[
            jax.ShapeDtypeStruct((T, D), F32),
            jax.ShapeDtypeStruct((T, D), BF16),
            jax.ShapeDtypeStruct((T, ROUTER_PAD), F32),
            jax.ShapeDtypeStruct((SUBLANES, ROUTER_PAD), F32),
        ],
        scratch_shapes=[pltpu.VMEM((SUBLANES, ROUTER_PAD), F32)],
        compiler_params=_params("arbitrary"),
        name="out_proj",
    )(attn2, yc2, u2, x02, hyena_d.reshape(1, C), hyena_gn.reshape(1, C), grp, w_out_bf, x2,
      g_ffn.reshape(1, D), rwh, rwl, rb, tri)


def _expert_kernel(be_ref, nused_ref, x_ref, wg_ref, bg_ref, wu_ref, bu_ref, wd_ref, bd_ref, *rest):
    y_ref, wg_bf, wu_bf, wd_bf = rest[-4:]
    i = pl.program_id(0)
    used = i < nused_ref[0]
    new_expert = jnp.logical_or(i == 0, be_ref[i] != be_ref[jnp.maximum(i - 1, 0)])

    @pl.when(jnp.logical_and(used, new_expert))
    def _():
        wg_bf[...] = wg_ref[0].astype(BF16)
        wu_bf[...] = wu_ref[0].astype(BF16)
        wd_bf[...] = wd_ref[0].astype(BF16)

    @pl.when(used)
    def _():
        x = x_ref[...]
        g = jnp.minimum(jnp.dot(x, wg_bf[...], preferred_element_type=F32) + bg_ref[0], SWIGLU_LIMIT)
        u = jnp.clip(jnp.dot(x, wu_bf[...], preferred_element_type=F32) + bu_ref[0],
                     -SWIGLU_LIMIT, SWIGLU_LIMIT)
        a = (u + 1.0) * (g * jax.nn.sigmoid(SWIGLU_ALPHA * g))
        y = jnp.dot(a.astype(BF16), wd_bf[...], preferred_element_type=F32) + bd_ref[0]
        y_ref[...] = y.astype(y_ref.dtype)

    @pl.when(jnp.logical_not(used))
    def _():
        y_ref[...] = jnp.zeros_like(y_ref)


def _experts(block_e, n_used, x_parts, wg, bg, wu, bu, wd, bd):
    D = x_parts[0].shape[1]
    tm = EXPERT_TILE
    P = sum(xs.shape[0] for xs in x_parts)
    E, _, FF = wg.shape
    wmap = lambda i, be, nu: (be[i], 0, 0)
    y = None
    first = 0
    for part, xs in enumerate(x_parts):
        nb = xs.shape[0] // tm
        in_specs = [
            pl.BlockSpec((tm, D), lambda i, be, nu: (i, 0)),
            pl.BlockSpec((1, D, FF), wmap),
            pl.BlockSpec((1, 1, FF), wmap),
            pl.BlockSpec((1, D, FF), wmap),
            pl.BlockSpec((1, 1, FF), wmap),
            pl.BlockSpec((1, FF, D), wmap),
            pl.BlockSpec((1, 1, D), wmap),
        ]
        args = [block_e[first:first + nb], jnp.clip(n_used - first, 0, nb), xs,
                wg, bg.reshape(E, 1, FF), wu, bu.reshape(E, 1, FF), wd, bd.reshape(E, 1, D)]
        aliases = {}
        if y is not None:
            in_specs.append(pl.BlockSpec(memory_space=pl.ANY))
            args.append(y)
            aliases = {len(args) - 1: 0}
        grid_spec = pltpu.PrefetchScalarGridSpec(
            num_scalar_prefetch=2,
            grid=(nb,),
            in_specs=in_specs,
            out_specs=pl.BlockSpec((tm, D), lambda i, be, nu, first=first: (i + first, 0)),
            scratch_shapes=[pltpu.VMEM((D, FF), BF16), pltpu.VMEM((D, FF), BF16),
                            pltpu.VMEM((FF, D), BF16)],
        )
        y = pl.pallas_call(
            _expert_kernel,
            grid_spec=grid_spec,
            out_shape=jax.ShapeDtypeStruct((P, D), BF16),
            input_output_aliases=aliases,
            compiler_params=_params("arbitrary"),
            name=f"moe_experts_{part}",
        )(*args)
        first += nb
    return y


def _final_kernel(h_ref, yg_ref, rt_ref, p_ref, wp_ref, gp_ref, wg_ref, bg_ref, gfin_ref, *rest):
    o_ref = rest[-1]
    h = h_ref[...]
    for r in range(TOP_K):
        h = h + yg_ref[r].astype(F32) * rt_ref[:, 2 * TOP_K + r: 2 * TOP_K + r + 1]
    e = _rms(jnp.dot(p_ref[...].astype(BF16), wp_ref[...], preferred_element_type=F32),
             gp_ref[...], EPS)
    gate = jax.nn.sigmoid(jnp.dot(h.astype(BF16), wg_ref[...], preferred_element_type=F32)
                          + bg_ref[...])
    h = h + gate * e
    o_ref[...] = _rms(h, gfin_ref[...], EPS)


def _final(h1, yg_parts, route, p2, w_ple_bf, g_ple, w_gate_bf, b_gate, g_final):
    T, D = h1.shape
    tm = TOKEN_TILE
    PD = p2.shape[1]
    nb = T // tm // len(yg_parts)
    const = lambda i: (0, 0)
    out = None
    for part, yg in enumerate(yg_parts):
        row = lambda i, part=part: (i + part * nb, 0)
        in_specs = [
            pl.BlockSpec((tm, D), row),
            pl.BlockSpec((TOP_K, tm, D), lambda i: (0, i, 0)),
            pl.BlockSpec((tm, ROUTER_PAD), row),
            pl.BlockSpec((tm, PD), row),
            pl.BlockSpec((PD, D), const),
            pl.BlockSpec((1, D), const),
            pl.BlockSpec((D, D), const),
            pl.BlockSpec((1, D), const),
            pl.BlockSpec((1, D), const),
        ]
        args = [h1, yg, route, p2, w_ple_bf, g_ple.reshape(1, D), w_gate_bf, b_gate.reshape(1, D),
                g_final.reshape(1, D)]
        aliases = {}
        if out is not None:
            in_specs.append(pl.BlockSpec(memory_space=pl.ANY))
            args.append(out)
            aliases = {len(args) - 1: 0}
        out = pl.pallas_call(
            _final_kernel,
            grid=(nb,),
            in_specs=in_specs,
            out_specs=pl.BlockSpec((tm, D), row),
            out_shape=jax.ShapeDtypeStruct((T, D), F32),
            input_output_aliases=aliases,
            compiler_params=_params("parallel"),
            name=f"final_{part}",
        )(*args)
    return out


def _rope_tables(seq):
    d = DIFF_HEAD_DIM
    pos = jnp.arange(seq, dtype=F32)
    inv = ROPE_THETA ** (-jnp.arange(0, d, 2, dtype=F32) / d)
    ang = pos[:, None] * inv[None, :]
    cos, sin = jnp.cos(ang), jnp.sin(ang)
    z = jnp.zeros_like(sin)
    cos_t = jnp.tile(jnp.concatenate([cos, cos], -1), (1, LANES // d))
    s1_t = jnp.tile(jnp.concatenate([-sin, z], -1), (1, LANES // d))
    s2_t = jnp.tile(jnp.concatenate([z, sin], -1), (1, LANES // d))
    return cos_t, s1_t, s2_t


def _long_conv(u, hf_t, hb_t):
    B, L, C = u.shape
    assert 2 * L == FFT_N and B % 2 == 0
    P, R = B // 2, FFT_N1 // 2
    cst = _dft_constants()
    kf = _filter_spectrum(hf_t.reshape(C, R, FFT_N2).astype(BF16),
                          hb_t.reshape(C, R, FFT_N2).astype(BF16), cst)
    x_slabs = (u.reshape(2, P, R, FFT_N2, C).transpose(1, 4, 0, 2, 3)
               .reshape(P, C, FFT_N1, FFT_N2).astype(BF16))
    y = _fft_conv(x_slabs, kf, cst)
    return y.reshape(P, C, 2, R, FFT_N2).transpose(2, 0, 3, 4, 1).reshape(B, L, C)


def _dispatch_indices(route, cnt, T):
    tm = EXPERT_TILE
    A = T * TOP_K
    route_t = route[:, :2 * TOP_K].T
    top_e = route_t[:TOP_K].astype(jnp.int32)
    rank = route_t[TOP_K:].astype(jnp.int32)
    counts = cnt[0, :N_EXPERTS].astype(jnp.int32)
    padded = (counts + tm - 1) // tm * tm
    pad_end = jnp.cumsum(padded)
    pad_start = pad_end - padded
    start = jnp.cumsum(counts) - counts
    experts = jnp.arange(N_EXPERTS, dtype=jnp.int32)[:, None, None]
    pos = rank + jnp.sum(jnp.where(top_e[None] == experts, pad_start[:, None, None], 0), axis=0)
    n_blocks = -(-A // tm) + N_EXPERTS
    P = n_blocks * tm
    block_first = jnp.arange(n_blocks, dtype=jnp.int32) * tm
    block_e = jnp.minimum(jnp.sum(pad_end[None, :] <= block_first[:, None], axis=1),
                          N_EXPERTS - 1).astype(jnp.int32)
    n_used = (pad_end[-1] // tm).astype(jnp.int32).reshape(1)
    tok = jnp.broadcast_to(jnp.arange(T, dtype=jnp.int32)[None, :], (TOP_K, T))
    _, sorted_tok = lax.sort_key_val(pos.reshape(A), tok.reshape(A))
    in_blk = jnp.arange(tm, dtype=jnp.int32)[None, :]
    r = (block_first - pad_start[block_e])[:, None] + in_blk
    compact = jnp.clip(start[block_e][:, None] + r, 0, A - 1)
    filler = (block_first[:, None] + in_blk) % T
    slot_tok = jnp.where(r < counts[block_e][:, None], sorted_tok[compact], filler).reshape(P)
    return pos, slot_tok, block_e, n_used


def kernel(x, p, g_mix, w_in, hyena_conv_w, hyena_conv_b, flt_w1, flt_b1, flt_w_inner, flt_b_inner, flt_freq, flt_w_out, hyena_d, hyena_gn, lambda_q1, lambda_k1, lambda_q2, lambda_k2, attn_subln, w_out, g_ffn, router_w, router_b, w_gate, b_gate, w_up, b_up, w_down, b_down, w_ple, g_ple, w_ple_gate, b_ple_gate, g_final):
    B, S, D = x.shape
    T = B * S
    i = 0
    x2 = x.reshape(T, D)

    cos_t, s1_t, s2_t = _rope_tables(S)
    q, k, vt, hy = _in_proj(x2, g_mix[i], w_in[i].astype(BF16), cos_t, s1_t, s2_t, S)

    lam = (jnp.exp(jnp.sum(lambda_q1[i] * lambda_k1[i])) - jnp.exp(jnp.sum(lambda_q2[i] * lambda_k2[i]))
           + LAMBDA_INIT).reshape(1).astype(F32)
    attn = _diff_attention(lam, q.reshape(B, S, -1), k.reshape(B, S, -1), vt, attn_subln[i])

    u, hx0 = _short_conv(hy.reshape(B, S, -1), hyena_conv_w[i], hyena_conv_b[i])
    h_fwd, h_bwd = _hyena_filters(S, flt_w1[i], flt_b1[i], flt_w_inner[i], flt_b_inner[i],
                                  flt_freq[i], flt_w_out[i])
    yc = _long_conv(u, h_fwd, h_bwd)

    h1, hn, route, cnt = _out_proj(attn.reshape(T, -1), yc.reshape(T, -1), u.reshape(T, -1),
                                   hx0.reshape(T, -1), hyena_d[i], hyena_gn[i],
                                   w_out[i].astype(BF16), x2, g_ffn[i], router_w[i], router_b[i])

    pos, slot_tok, block_e, n_used = _dispatch_indices(route, cnt, T)
    n_parts = COMBINE_PARTS
    head_rows = slot_tok.shape[0] // EXPERT_TILE // DISPATCH_HEAD_DIV * EXPERT_TILE
    x_parts = [hn[slot_tok[:head_rows]], hn[slot_tok[head_rows:]]]
    y = _experts(block_e, n_used, x_parts, w_gate[i], b_gate[i], w_up[i], b_up[i],
                 w_down[i], b_down[i])
    pos_parts = jnp.split(pos, n_parts, axis=1)
    yg_parts = [y[pp.reshape(-1)].reshape(TOP_K, T // n_parts, D) for pp in pos_parts]

    out = _final(h1, yg_parts, route, p[i].reshape(T, -1), w_ple[i].astype(BF16), g_ple[i],
                 w_ple_gate[i].astype(BF16), b_ple_gate[i], g_final)
    return out.reshape(B, S, D)
```

```python
import functools
import math

import jax
import jax.numpy as jnp
from jax import lax
from jax.experimental import pallas as pl
from jax.experimental.pallas import tpu as pltpu

F32 = jnp.float32
BF16 = jnp.bfloat16

D_MODEL = 1024
ATTN_WIDTH = 512
HYENA_WIDTH = 512
DIFF_HEADS = 4
DIFF_HEAD_DIM = 64
DIFF_V_DIM = 128
HYENA_GROUPS = 8
FILTER_EMB = 33
FILTER_INNER = 2
FAST_DECAY_PCT = 0.3
SLOW_DECAY_PCT = 1.5
DECAY_TARGET = 1e-2
ROPE_THETA = 10000.0
N_EXPERTS = 32
TOP_K = 4
SWIGLU_ALPHA = 1.702
SWIGLU_LIMIT = 7.0
EPS = 1e-6
SUBLN_EPS = 1e-5
LAMBDA_INIT = 0.8 - 0.6 * math.exp(-0.3 * 0)

V7X_VMEM_LIMIT_BYTES = 56 * 1024 * 1024
LANES = 128
SUBLANES = 8

TOKEN_TILE = 512
ATTN_STREAM_W = 1024
ATTN_ONES_ROWS = 16
ATTN_UNROLL = 4
ATTN_KV_TILE = 512
EXPERT_TILE = 512
ROUTER_PAD = LANES
DISPATCH_HEAD_DIV = 4
COMBINE_PARTS = 2


def _params(*sem):
    return pltpu.CompilerParams(dimension_semantics=sem, vmem_limit_bytes=V7X_VMEM_LIMIT_BYTES)


def _rms(x, g, eps):
    return x * lax.rsqrt(jnp.mean(x * x, axis=-1, keepdims=True) + eps) * g


def _inproj_kernel(x_ref, g_ref, w_ref, wvt_ref, c_ref, s1_ref, s2_ref, q_ref, k_ref, vt_ref, hy_ref):
    a = _rms(x_ref[...], g_ref[...], EPS).astype(BF16)
    cos, s1, s2 = c_ref[...], s1_ref[...], s2_ref[...]

    half = DIFF_HEAD_DIM // 2

    def rope(t):
        return (t * cos + pltpu.roll(t, LANES - half, axis=1) * s1
                + pltpu.roll(t, half, axis=1) * s2)

    q_scale = (DIFF_HEAD_DIM ** -0.5) * math.log2(math.e)
    qk = jnp.dot(a, w_ref[:, : 2 * ATTN_WIDTH], preferred_element_type=F32)
    for j in range(ATTN_WIDTH // LANES):
        sl = slice(j * LANES, (j + 1) * LANES)
        q_ref[:, sl] = (rope(qk[:, sl]) * q_scale).astype(BF16)
        k_ref[:, sl] = rope(qk[:, ATTN_WIDTH + j * LANES: ATTN_WIDTH + (j + 1) * LANES]).astype(BF16)
    vt_ref[0] = lax.dot_general(wvt_ref[...], a, (((1,), (1,)), ((), ())),
                                preferred_element_type=F32).astype(BF16)
    hy_ref[...] = jnp.dot(a, w_ref[:, 3 * ATTN_WIDTH:], preferred_element_type=F32).astype(hy_ref.dtype)


def _in_proj(x2, g_mix, w_in_bf, cos_t, s1_t, s2_t, seq):
    T, D = x2.shape
    tm = TOKEN_TILE
    nseq = seq // tm
    wvt = w_in_bf[:, 2 * ATTN_WIDTH: 3 * ATTN_WIDTH].T
    row = lambda i: (i, 0)
    const = lambda i: (0, 0)
    pos = lambda i: (i % nseq, 0)
    return pl.pallas_call(
        _inproj_kernel,
        grid=(T // tm,),
        in_specs=[
            pl.BlockSpec((tm, D), row),
            pl.BlockSpec((1, D), const),
            pl.BlockSpec(w_in_bf.shape, const),
            pl.BlockSpec(wvt.shape, const),
            pl.BlockSpec((tm, LANES), pos),
            pl.BlockSpec((tm, LANES), pos),
            pl.BlockSpec((tm, LANES), pos),
        ],
        out_specs=[
            pl.BlockSpec((tm, ATTN_WIDTH), row),
            pl.BlockSpec((tm, ATTN_WIDTH), row),
            pl.BlockSpec((1, ATTN_WIDTH, tm), lambda i: (i, 0, 0)),
            pl.BlockSpec((tm, 3 * HYENA_WIDTH), row),
        ],
        out_shape=[
            jax.ShapeDtypeStruct((T, ATTN_WIDTH), BF16),
            jax.ShapeDtypeStruct((T, ATTN_WIDTH), BF16),
            jax.ShapeDtypeStruct((T // tm, ATTN_WIDTH, tm), BF16),
            jax.ShapeDtypeStruct((T, 3 * HYENA_WIDTH), BF16),
        ],
        compiler_params=_params("parallel"),
        name="in_proj",
    )(x2, g_mix.reshape(1, D), w_in_bf, wvt, cos_t, s1_t, s2_t)


def _attn_kernel(lam_ref, q_ref, k_ref, vt_ref, g_ref, o_ref, s_ref, m_ref, acc_ref, o0_ref, *, kc):
    w = ATTN_STREAM_W
    nc = k_ref.shape[1] // kc
    n_groups = q_ref.shape[1] // w

    def group_rows(grp):
        return pl.ds(pl.multiple_of(grp * w, w), w)

    def stream_q(grp, c):
        q = q_ref[0, group_rows(grp), :]
        lane = lax.broadcasted_iota(jnp.int32, q.shape, 1)
        keep = (lane < DIFF_HEAD_DIM) if c == 0 else (lane >= DIFF_HEAD_DIM)
        return jnp.where(keep, q, jnp.zeros_like(q))

    def fold8(t, op):
        r = t[0:SUBLANES]
        for j in range(1, kc // SUBLANES):
            r = op(r, t[j * SUBLANES:(j + 1) * SUBLANES])
        return r

    def score_chunk(i, qc, j):
        rows = pl.ds(pl.multiple_of(j * kc, kc), kc)
        s = lax.dot_general(k_ref[0, rows, :], qc, (((1,), (1,)), ((), ())),
                            preferred_element_type=F32)
        s_ref[i, rows, :] = s
        m_ref[i] = jnp.maximum(m_ref[i], fold8(s, jnp.maximum))

    ones_rows = jnp.ones((ATTN_ONES_ROWS, kc), BF16)

    def prob_chunk(i, j, m):
        rows = pl.ds(pl.multiple_of(j * kc, kc), kc)
        p = jnp.exp2(s_ref[i, rows, :] - m)
        v_aug = jnp.concatenate([vt_ref[j], ones_rows], axis=0)
        acc_ref[...] += jnp.dot(v_aug, p.astype(BF16), preferred_element_type=F32)

    def phase(read_buf, write_buf, q_next):
        if write_buf is not None:
            m_ref[write_buf] = jnp.full(m_ref.shape[1:], -jnp.inf, F32)
        if read_buf is not None:
            m = jnp.max(m_ref[read_buf], axis=0, keepdims=True)
            acc_ref[...] = jnp.zeros_like(acc_ref)

        def body(j, carry):
            if write_buf is not None:
                score_chunk(write_buf, q_next, j)
            if read_buf is not None:
                prob_chunk(read_buf, j, m)
            return carry

        lax.fori_loop(0, nc, body, 0, unroll=ATTN_UNROLL)
        if read_buf is None:
            return None
        return acc_ref[:DIFF_V_DIM, :] * (1.0 / acc_ref[DIFF_V_DIM:DIFF_V_DIM + 1, :])

    def finish_group(grp, out0, out1):
        o = out0 - lam_ref[0] * out1
        o = o * lax.rsqrt(jnp.mean(o * o, axis=0, keepdims=True) + SUBLN_EPS)
        o = o * (g_ref[...] * (1.0 - LAMBDA_INIT))
        o_ref[0, group_rows(grp), :] = o.T.astype(o_ref.dtype)

    phase(None, 0, stream_q(0, 0))

    def group_body(grp, carry):
        o0_ref[...] = phase(0, 1, stream_q(grp, 1))
        out1 = phase(1, 0, stream_q(grp + 1, 0))
        finish_group(grp, o0_ref[...], out1)
        return carry

    lax.fori_loop(0, n_groups - 1, group_body, 0)
    last = n_groups - 1
    o0_ref[...] = phase(0, 1, stream_q(last, 1))
    out1 = phase(1, None, None)
    finish_group(last, o0_ref[...], out1)


def _diff_attention(lam, q, k, vt, g_subln):
    B, S, _ = q.shape
    kc, w = ATTN_KV_TILE, ATTN_STREAM_W
    nc = S // kc
    assert vt.shape[2] == kc and S % w == 0 and S // w >= 2
    head = lambda b, h: (b, 0, h)
    return pl.pallas_call(
        functools.partial(_attn_kernel, kc=kc),
        grid=(B, DIFF_HEADS),
        in_specs=[
            pl.BlockSpec(memory_space=pltpu.SMEM),
            pl.BlockSpec((1, S, LANES), head),
            pl.BlockSpec((1, S, LANES), head),
            pl.BlockSpec((nc, DIFF_V_DIM, kc), lambda b, h: (b, h, 0)),
            pl.BlockSpec((DIFF_V_DIM, 1), lambda b, h: (0, 0)),
        ],
        out_specs=pl.BlockSpec((1, S, LANES), head),
        out_shape=jax.ShapeDtypeStruct((B, S, ATTN_WIDTH), BF16),
        scratch_shapes=[pltpu.VMEM((2, S, w), F32),
                        pltpu.VMEM((2, SUBLANES, w), F32),
                        pltpu.VMEM((DIFF_V_DIM + ATTN_ONES_ROWS, w), F32),
                        pltpu.VMEM((DIFF_V_DIM, w), F32)],
        compiler_params=_params("parallel", "parallel"),
        name="diff_attn",
    )(lam, q, k, vt, g_subln.reshape(DIFF_V_DIM, 1))


def _shortconv_kernel(hy_ref, prev_ref, next_ref, w_ref, b_ref, u_ref, x0_ref):
    i = pl.program_id(1)
    last = pl.num_programs(1) - 1
    x = hy_ref[0].astype(F32)
    ts = x.shape[0]
    prev_row = jnp.where(i == 0, 0.0, prev_ref[0, HALO_ROWS - 1:HALO_ROWS, :].astype(F32))
    next_row = jnp.where(i == last, 0.0, next_ref[0, 0:1, :].astype(F32))
    row = lax.broadcasted_iota(jnp.int32, (ts, 1), 0)
    xm = jnp.where(row == 0, prev_row, pltpu.roll(x, 1, axis=0))
    xp = jnp.where(row == ts - 1, next_row, pltpu.roll(x, ts - 1, axis=0))
    y = b_ref[...] + xm * w_ref[0:1, :] + x * w_ref[1:2, :] + xp * w_ref[2:3, :]
    C = HYENA_WIDTH
    u_ref[0] = (y[:, :C] * y[:, 2 * C:]).astype(u_ref.dtype)
    x0_ref[0] = y[:, C:2 * C].astype(x0_ref.dtype)


HALO_ROWS = 16


def _short_conv(hy, conv_w, conv_b):
    B, S, C3 = hy.shape
    ts = TOKEN_TILE
    nb = ts // HALO_ROWS
    return pl.pallas_call(
        _shortconv_kernel,
        grid=(B, S // ts),
        in_specs=[
            pl.BlockSpec((1, ts, C3), lambda b, i: (b, i, 0)),
            pl.BlockSpec((1, HALO_ROWS, C3), lambda b, i: (b, jnp.maximum(i * nb - 1, 0), 0)),
            pl.BlockSpec((1, HALO_ROWS, C3), lambda b, i: (b, jnp.minimum((i + 1) * nb, S // HALO_ROWS - 1), 0)),
            pl.BlockSpec((3, C3), lambda b, i: (0, 0)),
            pl.BlockSpec((1, C3), lambda b, i: (0, 0)),
        ],
        out_specs=[
            pl.BlockSpec((1, ts, HYENA_WIDTH), lambda b, i: (b, i, 0)),
            pl.BlockSpec((1, ts, HYENA_WIDTH), lambda b, i: (b, i, 0)),
        ],
        out_shape=[
            jax.ShapeDtypeStruct((B, S, HYENA_WIDTH), BF16),
            jax.ShapeDtypeStruct((B, S, HYENA_WIDTH), BF16),
        ],
        compiler_params=_params("parallel", "parallel"),
        name="short_conv",
    )(hy, hy, hy, conv_w, conv_b.reshape(1, C3))


FFT_N = 8192
FFT_N1 = 64
FFT_N2 = 128
FFT_CH_BLOCK = 32
FFT_GROUP = 4
FFT_UNROLL = 2


def _dft_constants():
    import numpy as np
    n1, n2, n = FFT_N1, FFT_N2, FFT_N
    k1 = np.arange(n1)[:, None]
    t1 = np.arange(n1)[None, :]
    f1 = np.exp(-2j * np.pi * k1 * t1 / n1)
    f1h = f1[:, : n1 // 2]
    w1c = np.block([[f1h.real, -f1h.imag], [f1h.imag, f1h.real]])
    w1r = np.concatenate([f1h.real, f1h.imag], axis=0)
    t2 = np.arange(n2)[:, None]
    k2 = np.arange(n2)[None, :]
    f2 = np.exp(-2j * np.pi * t2 * k2 / n2)
    w2a = np.concatenate([f2.real, f2.imag], axis=1)
    w2b = np.concatenate([-f2.imag, f2.real], axis=1)
    g2 = np.conj(f2)
    w3a = np.concatenate([g2.real, g2.imag], axis=1)
    w3b = np.concatenate([-g2.imag, g2.real], axis=1)
    h = np.conj(f1).T[: n1 // 2] / n
    w4 = np.block([[h.real, -h.imag], [h.imag, h.real]])
    tw = np.exp(-2j * np.pi * np.arange(n1)[:, None] * np.arange(n2)[None, :] / n)
    bf = lambda a: jnp.asarray(a, dtype=F32).astype(BF16)
    return dict(w1c=bf(w1c), w1r=bf(w1r), w2a=bf(w2a), w2b=bf(w2b), w3a=bf(w3a), w3b=bf(w3b),
                w4=bf(w4), tc=jnp.asarray(np.tile(tw.real, (1, FFT_GROUP)), F32),
                ts=jnp.asarray(np.tile(tw.imag, (1, FFT_GROUP)), F32))


def _dft_forward(x_ref, w1_ref, tc_ref, ts_ref, w2a_ref, w2b_ref, ar_ref, ai_ref):
    ng = x_ref.shape[0] // FFT_GROUP
    n1, n2 = FFT_N1, FFT_N2
    w1 = w1_ref[...]
    tc, ts = tc_ref[...], ts_ref[...]

    def body(g, carry):
        xg = jnp.concatenate([x_ref[g * FFT_GROUP + cl] for cl in range(FFT_GROUP)], axis=1)
        a = jnp.dot(w1, xg, preferred_element_type=F32)
        ar, ai = a[:n1], a[n1:]
        tr = (ar * tc - ai * ts).astype(BF16)
        ti = (ar * ts + ai * tc).astype(BF16)
        for cl in range(FFT_GROUP):
            rows = pl.ds(pl.multiple_of((g * FFT_GROUP + cl) * n1, n1), n1)
            ar_ref[rows, :] = tr[:, cl * n2:(cl + 1) * n2]
            ai_ref[rows, :] = ti[:, cl * n2:(cl + 1) * n2]
        return carry

    lax.fori_loop(0, ng, body, 0, unroll=FFT_UNROLL)
    return (jnp.dot(ar_ref[...], w2a_ref[...], preferred_element_type=F32)
            + jnp.dot(ai_ref[...], w2b_ref[...], preferred_element_type=F32))


def _spectrum_kernel(hf_ref, hb_ref, w1_ref, tc_ref, ts_ref, w2a_ref, w2b_ref, o_ref, ar_ref, ai_ref):
    n2 = FFT_N2
    f = _dft_forward(hf_ref, w1_ref, tc_ref, ts_ref, w2a_ref, w2b_ref, ar_ref, ai_ref)
    o_ref[...] = f.reshape(o_ref.shape)
    b = _dft_forward(hb_ref, w1_ref, tc_ref, ts_ref, w2a_ref, w2b_ref, ar_ref, ai_ref)
    b = b.reshape(o_ref.shape)
    o_ref[:, :, :n2] = o_ref[:, :, :n2] + b[:, :, :n2]
    o_ref[:, :, n2:] = o_ref[:, :, n2:] - b[:, :, n2:]


def _fftconv_kernel(x_ref, kf_ref, w1_ref, tc_ref, ts_ref, w2a_ref, w2b_ref, w3a_ref, w3b_ref,
                    w4_ref, y_ref, ar_ref, ai_ref, c_ref):
    cb = x_ref.shape[0]
    ng = cb // FFT_GROUP
    n1, n2 = FFT_N1, FFT_N2
    b = _dft_forward(x_ref, w1_ref, tc_ref, ts_ref, w2a_ref, w2b_ref, ar_ref, ai_ref)
    kf = kf_ref[...].reshape(cb * n1, 2 * n2)
    br, bi = b[:, :n2], b[:, n2:]
    kr, ki = kf[:, :n2], kf[:, n2:]
    ar_ref[...] = (br * kr - bi * ki).astype(BF16)
    ai_ref[...] = (br * ki + bi * kr).astype(BF16)
    c_ref[...] = (jnp.dot(ar_ref[...], w3a_ref[...], preferred_element_type=F32)
                  + jnp.dot(ai_ref[...], w3b_ref[...], preferred_element_type=F32))
    w4 = w4_ref[...]
    tc, ts = tc_ref[...], ts_ref[...]

    def body(g, carry):
        rows = [pl.ds(pl.multiple_of((g * FFT_GROUP + cl) * n1, n1), n1) for cl in range(FFT_GROUP)]
        cr = jnp.concatenate([c_ref[r, :n2] for r in rows], axis=1)
        ci = jnp.concatenate([c_ref[r, n2:] for r in rows], axis=1)
        dr = (cr * tc + ci * ts).astype(BF16)
        di = (ci * tc - cr * ts).astype(BF16)
        yg = (jnp.dot(w4[:, :n1], dr, preferred_element_type=F32)
              + jnp.dot(w4[:, n1:], di, preferred_element_type=F32)).astype(y_ref.dtype)
        for cl in range(FFT_GROUP):
            y_ref[g * FFT_GROUP + cl] = yg[:, cl * n2:(cl + 1) * n2]
        return carry

    lax.fori_loop(0, ng, body, 0, unroll=FFT_UNROLL)


def _const_spec(a):
    nd = a.ndim
    return pl.BlockSpec(a.shape, lambda *_: (0,) * nd)


def _filter_spectrum(hf_slabs, hb_slabs, cst):
    C = hf_slabs.shape[0]
    cb = FFT_CH_BLOCK
    consts = [cst["w1r"], cst["tc"], cst["ts"], cst["w2a"], cst["w2b"]]
    slab = pl.BlockSpec((cb, FFT_N1 // 2, FFT_N2), lambda i: (i, 0, 0))
    return pl.pallas_call(
        _spectrum_kernel,
        grid=(C // cb,),
        in_specs=[slab, slab] + [_const_spec(a) for a in consts],
        out_specs=pl.BlockSpec((cb, FFT_N1, 2 * FFT_N2), lambda i: (i, 0, 0)),
        out_shape=jax.ShapeDtypeStruct((C, FFT_N1, 2 * FFT_N2), F32),
        scratch_shapes=[pltpu.VMEM((cb * FFT_N1, FFT_N2), BF16), pltpu.VMEM((cb * FFT_N1, FFT_N2), BF16)],
        compiler_params=_params("parallel"),
        name="filter_spectrum",
    )(hf_slabs, hb_slabs, *consts)


FILTER_TIME_TILE = 512


def _filter_kernel(zt_ref, w1t_ref, b1_ref, wit_ref, bi_ref, fr_ref, wot_ref, dec_ref, hf_ref, hb_ref):
    hp = lax.Precision.HIGHEST
    h = jnp.sin(fr_ref[0] * (jnp.dot(w1t_ref[...], zt_ref[...], precision=hp,
                                     preferred_element_type=F32) + b1_ref[...]))
    for j in range(FILTER_INNER):
        h = jnp.sin(fr_ref[j + 1] * (jnp.dot(wit_ref[j], h, precision=hp,
                                             preferred_element_type=F32) + bi_ref[j]))
    o = jnp.dot(wot_ref[...], h, precision=hp, preferred_element_type=F32)
    dec = dec_ref[...]
    hf_ref[...] = o[:HYENA_WIDTH] * dec
    hb_ref[...] = o[HYENA_WIDTH:] * dec


def _hyena_filters(seq, w1, b1, w_inner, b_inner, freq, w_out):
    C = HYENA_WIDTH
    order = w1.shape[1]
    pos = jnp.arange(seq, dtype=F32)
    t = pos / (seq - 1)
    bands = (FILTER_EMB - 1) // 2
    f = jnp.linspace(1e-4, bands - 1, bands, dtype=F32)
    fw = ((2.0 * math.pi / seq) * pos)[:, None] * f[None, :]
    z = jnp.concatenate([t[:, None], jnp.cos(fw), -jnp.sin(fw)], axis=-1)
    zt = jnp.zeros((LANES, seq), F32).at[:FILTER_EMB].set(z.T)
    w1t = jnp.zeros((order, LANES), F32).at[:, :FILTER_EMB].set(w1.T)
    max_decay = math.log(DECAY_TARGET) / FAST_DECAY_PCT
    min_decay = math.log(DECAY_TARGET) / SLOW_DECAY_PCT
    deltas = jnp.abs(jnp.linspace(min_decay, max_decay, C, dtype=F32))
    dec_t = jnp.exp(-deltas[:, None] * t[None, :])
    tt = FILTER_TIME_TILE
    lane_blk = lambda r: pl.BlockSpec((r, tt), lambda i: (0, i))
    args = [zt, w1t, b1.reshape(order, 1), jnp.swapaxes(w_inner, 1, 2),
            b_inner.reshape(FILTER_INNER, order, 1), freq.reshape(FILTER_INNER + 1, order, 1),
            w_out.T, dec_t]
    return pl.pallas_call(
        _filter_kernel,
        grid=(seq // tt,),
        in_specs=[lane_blk(LANES)] + [_const_spec(a) for a in args[1:7]] + [lane_blk(C)],
        out_specs=[lane_blk(C), lane_blk(C)],
        out_shape=[jax.ShapeDtypeStruct((C, seq), F32), jax.ShapeDtypeStruct((C, seq), F32)],
        compiler_params=_params("parallel"),
        name="hyena_filters",
    )(*args)


def _fft_conv(x_slabs, kf, cst):
    P, C = x_slabs.shape[:2]
    cb = FFT_CH_BLOCK
    consts = [cst["w1c"], cst["tc"], cst["ts"], cst["w2a"], cst["w2b"], cst["w3a"], cst["w3b"], cst["w4"]]
    return pl.pallas_call(
        _fftconv_kernel,
        grid=(C // cb, P),
        in_specs=[pl.BlockSpec((None, cb, FFT_N1, FFT_N2), lambda i, p: (p, i, 0, 0)),
                  pl.BlockSpec((cb, FFT_N1, 2 * FFT_N2), lambda i, p: (i, 0, 0))]
                 + [_const_spec(a) for a in consts],
        out_specs=pl.BlockSpec((None, cb, FFT_N1, FFT_N2), lambda i, p: (p, i, 0, 0)),
        out_shape=jax.ShapeDtypeStruct((P, C, FFT_N1, FFT_N2), BF16),
        scratch_shapes=[pltpu.VMEM((cb * FFT_N1, FFT_N2), BF16), pltpu.VMEM((cb * FFT_N1, FFT_N2), BF16),
                        pltpu.VMEM((cb * FFT_N1, 2 * FFT_N2), F32)],
        compiler_params=_params("parallel", "arbitrary"),
        name="fft_conv",
    )(x_slabs, kf, *consts)


def _split_bf16(x):
    hi = x.astype(BF16)
    lo = (x - hi.astype(F32)).astype(BF16)
    return hi, lo


def _outproj_kernel(attn_ref, yc_ref, u_ref, x0_ref, d_ref, gn_ref, grp_ref, wo_ref, x_ref,
                    gf_ref, rwh_ref, rwl_ref, rb_ref, tri_ref, h_ref, hn_ref, rt_ref, cnt_out_ref,
                    cnt_ref):
    @pl.when(pl.program_id(0) == 0)
    def _():
        cnt_ref[...] = jnp.zeros_like(cnt_ref)

    grp = grp_ref[...]
    gsz = HYENA_WIDTH // HYENA_GROUPS

    def token_tile(cnt_row):
        rows = slice(None)
        z = ((yc_ref[rows, :].astype(F32) + u_ref[rows, :].astype(F32) * d_ref[...])
             * x0_ref[rows, :].astype(F32))
        zh, zl = _split_bf16(z * z)
        ssq = (jnp.dot(zh, grp, preferred_element_type=F32)
               + jnp.dot(zl, grp, preferred_element_type=F32))
        hy_out = (z * lax.rsqrt(ssq * (1.0 / gsz) + EPS) * gn_ref[...]).astype(BF16)
        mix = (jnp.dot(attn_ref[rows, :], wo_ref[:ATTN_WIDTH, :], preferred_element_type=F32)
               + jnp.dot(hy_out, wo_ref[ATTN_WIDTH:, :], preferred_element_type=F32))
        h = x_ref[rows, :] + mix
        h_ref[rows, :] = h
        hn = _rms(h, gf_ref[...], EPS)
        hn_ref[rows, :] = hn.astype(BF16)
        nh, nl = _split_bf16(hn)
        logits = (jnp.dot(nh, rwh_ref[...], preferred_element_type=F32)
                  + jnp.dot(nl, rwh_ref[...], preferred_element_type=F32)
                  + jnp.dot(nh, rwl_ref[...], preferred_element_type=F32)) + rb_ref[...]

        lane = lax.broadcasted_iota(jnp.int32, logits.shape, 1)
        work = logits
        top_val, top_idx, top_hot = [], [], []
        for _ in range(TOP_K):
            m = jnp.max(work, axis=-1, keepdims=True)
            idx = jnp.min(jnp.where(work == m, lane, ROUTER_PAD), axis=-1, keepdims=True)
            hot = lane == idx
            top_val.append(m)
            top_idx.append(idx)
            top_hot.append(hot)
            work = jnp.where(hot, -jnp.inf, work)
        ex = [jnp.exp(v - top_val[0]) for v in top_val]
        inv_den = 1.0 / (ex[0] + ex[1] + ex[2] + ex[3])
        sel = jnp.zeros(logits.shape, F32)
        for hot in top_hot:
            sel = sel + hot.astype(F32)
        before = jnp.dot(tri_ref[...], sel.astype(BF16), preferred_element_type=F32) + cnt_row
        packed = jnp.zeros(logits.shape, F32)
        for r, hot in enumerate(top_hot):
            rank_r = jnp.sum(jnp.where(hot, before, 0.0), axis=-1, keepdims=True)
            packed = jnp.where(lane == r, top_idx[r].astype(F32), packed)
            packed = jnp.where(lane == TOP_K + r, rank_r, packed)
            packed = jnp.where(lane == 2 * TOP_K + r, ex[r] * inv_den, packed)
        rt_ref[rows, :] = packed
        return cnt_row + jnp.sum(sel, axis=0, keepdims=True)

    cnt_row = token_tile(cnt_ref[0:1, :])
    cnt_ref[...] = jnp.broadcast_to(cnt_row, cnt_ref.shape)
    cnt_out_ref[...] = cnt_ref[...]


def _out_proj(attn2, yc2, u2, x02, hyena_d, hyena_gn, w_out_bf, x2, g_ffn, router_w, router_b):
    T, D = x2.shape
    tm = TOKEN_TILE
    C = HYENA_WIDTH
    gid = jnp.arange(C) // (C // HYENA_GROUPS)
    grp = (gid[:, None] == gid[None, :]).astype(BF16)
    rw = jnp.zeros((D, ROUTER_PAD), F32).at[:, :N_EXPERTS].set(router_w)
    rwh, rwl = _split_bf16(rw)
    rb = jnp.full((1, ROUTER_PAD), -jnp.inf, F32).at[0, :N_EXPERTS].set(router_b)
    tri = (jnp.arange(tm)[:, None] > jnp.arange(tm)[None, :]).astype(BF16)
    row = lambda i: (i, 0)
    const = lambda i: (0, 0)
    return pl.pallas_call(
        _outproj_kernel,
        grid=(T // tm,),
        in_specs=[
            pl.BlockSpec((tm, ATTN_WIDTH), row),
            pl.BlockSpec((tm, C), row),
            pl.BlockSpec((tm, C), row),
            pl.BlockSpec((tm, C), row),
            pl.BlockSpec((1, C), const),
            pl.BlockSpec((1, C), const),
            pl.BlockSpec((C, C), const),
            pl.BlockSpec((D, D), const),
            pl.BlockSpec((tm, D), row),
            pl.BlockSpec((1, D), const),
            pl.BlockSpec((D, ROUTER_PAD), const),
            pl.BlockSpec((D, ROUTER_PAD), const),
            pl.BlockSpec((1, ROUTER_PAD), const),
            pl.BlockSpec((tm, tm), const),
        ],
        out_specs=[
            pl.BlockSpec((tm, D), row),
            pl.BlockSpec((tm, D), row),
            pl.BlockSpec((tm, ROUTER_PAD), row),
            pl.BlockSpec((SUBLANES, ROUTER_PAD), const),
        ],
        out_shape=[
            jax.ShapeDtypeStruct((T, D), F32),
            jax.ShapeDtypeStruct((T, D), BF16),
            jax.ShapeDtypeStruct((T, ROUTER_PAD), F32),
            jax.ShapeDtypeStruct((SUBLANES, ROUTER_PAD), F32),
        ],
        scratch_shapes=[pltpu.VMEM((SUBLANES, ROUTER_PAD), F32)],
        compiler_params=_params("arbitrary"),
        name="out_proj",
    )(attn2, yc2, u2, x02, hyena_d.reshape(1, C), hyena_gn.reshape(1, C), grp, w_out_bf, x2,
      g_ffn.reshape(1, D), rwh, rwl, rb, tri)


def _expert_kernel(be_ref, nused_ref, x_ref, wg_ref, bg_ref, wu_ref, bu_ref, wd_ref, bd_ref, *rest):
    y_ref, wg_bf, wu_bf, wd_bf = rest[-4:]
    i = pl.program_id(0)
    used = i < nused_ref[0]
    new_expert = jnp.logical_or(i == 0, be_ref[i] != be_ref[jnp.maximum(i - 1, 0)])

    @pl.when(jnp.logical_and(used, new_expert))
    def _():
        wg_bf[...] = wg_ref[0].astype(BF16)
        wu_bf[...] = wu_ref[0].astype(BF16)
        wd_bf[...] = wd_ref[0].astype(BF16)

    @pl.when(used)
    def _():
        x = x_ref[...]
        g = jnp.minimum(jnp.dot(x, wg_bf[...], preferred_element_type=F32) + bg_ref[0], SWIGLU_LIMIT)
        u = jnp.clip(jnp.dot(x, wu_bf[...], preferred_element_type=F32) + bu_ref[0],
                     -SWIGLU_LIMIT, SWIGLU_LIMIT)
        a = (u + 1.0) * (g * jax.nn.sigmoid(SWIGLU_ALPHA * g))
        y = jnp.dot(a.astype(BF16), wd_bf[...], preferred_element_type=F32) + bd_ref[0]
        y_ref[...] = y.astype(y_ref.dtype)

    @pl.when(jnp.logical_not(used))
    def _():
        y_ref[...] = jnp.zeros_like(y_ref)


def _experts(block_e, n_used, x_parts, wg, bg, wu, bu, wd, bd):
    D = x_parts[0].shape[1]
    tm = EXPERT_TILE
    P = sum(xs.shape[0] for xs in x_parts)
    E, _, FF = wg.shape
    wmap = lambda i, be, nu: (be[i], 0, 0)
    y = None
    first = 0
    for part, xs in enumerate(x_parts):
        nb = xs.shape[0] // tm
        in_specs = [
            pl.BlockSpec((tm, D), lambda i, be, nu: (i, 0)),
            pl.BlockSpec((1, D, FF), wmap),
            pl.BlockSpec((1, 1, FF), wmap),
            pl.BlockSpec((1, D, FF), wmap),
            pl.BlockSpec((1, 1, FF), wmap),
            pl.BlockSpec((1, FF, D), wmap),
            pl.BlockSpec((1, 1, D), wmap),
        ]
        args = [block_e[first:first + nb], jnp.clip(n_used - first, 0, nb), xs,
                wg, bg.reshape(E, 1, FF), wu, bu.reshape(E, 1, FF), wd, bd.reshape(E, 1, D)]
        aliases = {}
        if y is not None:
            in_specs.append(pl.BlockSpec(memory_space=pl.ANY))
            args.append(y)
            aliases = {len(args) - 1: 0}
        grid_spec = pltpu.PrefetchScalarGridSpec(
            num_scalar_prefetch=2,
            grid=(nb,),
            in_specs=in_specs,
            out_specs=pl.BlockSpec((tm, D), lambda i, be, nu, first=first: (i + first, 0)),
            scratch_shapes=[pltpu.VMEM((D, FF), BF16), pltpu.VMEM((D, FF), BF16),
                            pltpu.VMEM((FF, D), BF16)],
        )
        y = pl.pallas_call(
            _expert_kernel,
            grid_spec=grid_spec,
            out_shape=jax.ShapeDtypeStruct((P, D), BF16),
            input_output_aliases=aliases,
            compiler_params=_params("arbitrary"),
            name=f"moe_experts_{part}",
        )(*args)
        first += nb
    return y


def _final_kernel(h_ref, yg_ref, rt_ref, p_ref, wp_ref, gp_ref, wg_ref, bg_ref, gfin_ref, *rest):
    o_ref = rest[-1]
    h = h_ref[...]
    for r in range(TOP_K):
        h = h + yg_ref[r].astype(F32) * rt_ref[:, 2 * TOP_K + r: 2 * TOP_K + r + 1]
    e = _rms(jnp.dot(p_ref[...].astype(BF16), wp_ref[...], preferred_element_type=F32),
             gp_ref[...], EPS)
    gate = jax.nn.sigmoid(jnp.dot(h.astype(BF16), wg_ref[...], preferred_element_type=F32)
                          + bg_ref[...])
    h = h + gate * e
    o_ref[...] = _rms(h, gfin_ref[...], EPS)


def _final(h1, yg_parts, route, p2, w_ple_bf, g_ple, w_gate_bf, b_gate, g_final):
    T, D = h1.shape
    tm = TOKEN_TILE
    PD = p2.shape[1]
    nb = T // tm // len(yg_parts)
    const = lambda i: (0, 0)
    out = None
    for part, yg in enumerate(yg_parts):
        row = lambda i, part=part: (i + part * nb, 0)
        in_specs = [
            pl.BlockSpec((tm, D), row),
            pl.BlockSpec((TOP_K, tm, D), lambda i: (0, i, 0)),
            pl.BlockSpec((tm, ROUTER_PAD), row),
            pl.BlockSpec((tm, PD), row),
            pl.BlockSpec((PD, D), const),
            pl.BlockSpec((1, D), const),
            pl.BlockSpec((D, D), const),
            pl.BlockSpec((1, D), const),
            pl.BlockSpec((1, D), const),
        ]
        args = [h1, yg, route, p2, w_ple_bf, g_ple.reshape(1, D), w_gate_bf, b_gate.reshape(1, D),
                g_final.reshape(1, D)]
        aliases = {}
        if out is not None:
            in_specs.append(pl.BlockSpec(memory_space=pl.ANY))
            args.append(out)
            aliases = {len(args) - 1: 0}
        out = pl.pallas_call(
            _final_kernel,
            grid=(nb,),
            in_specs=in_specs,
            out_specs=pl.BlockSpec((tm, D), row),
            out_shape=jax.ShapeDtypeStruct((T, D), F32),
            input_output_aliases=aliases,
            compiler_params=_params("parallel"),
            name=f"final_{part}",
        )(*args)
    return out


def _rope_tables(seq):
    d = DIFF_HEAD_DIM
    pos = jnp.arange(seq, dtype=F32)
    inv = ROPE_THETA ** (-jnp.arange(0, d, 2, dtype=F32) / d)
    ang = pos[:, None] * inv[None, :]
    cos, sin = jnp.cos(ang), jnp.sin(ang)
    z = jnp.zeros_like(sin)
    cos_t = jnp.tile(jnp.concatenate([cos, cos], -1), (1, LANES // d))
    s1_t = jnp.tile(jnp.concatenate([-sin, z], -1), (1, LANES // d))
    s2_t = jnp.tile(jnp.concatenate([z, sin], -1), (1, LANES // d))
    return cos_t, s1_t, s2_t


def _long_conv(u, hf_t, hb_t):
    B, L, C = u.shape
    assert 2 * L == FFT_N and B % 2 == 0
    P, R = B // 2, FFT_N1 // 2
    cst = _dft_constants()
    kf = _filter_spectrum(hf_t.reshape(C, R, FFT_N2).astype(BF16),
                          hb_t.reshape(C, R, FFT_N2).astype(BF16), cst)
    x_slabs = (u.reshape(2, P, R, FFT_N2, C).transpose(1, 4, 0, 2, 3)
               .reshape(P, C, FFT_N1, FFT_N2).astype(BF16))
    y = _fft_conv(x_slabs, kf, cst)
    return y.reshape(P, C, 2, R, FFT_N2).transpose(2, 0, 3, 4, 1).reshape(B, L, C)


def _dispatch_indices(route, cnt, T):
    tm = EXPERT_TILE
    A = T * TOP_K
    route_t = route[:, :2 * TOP_K].T
    top_e = route_t[:TOP_K].astype(jnp.int32)
    rank = route_t[TOP_K:].astype(jnp.int32)
    counts = cnt[0, :N_EXPERTS].astype(jnp.int32)
    padded = (counts + tm - 1) // tm * tm
    pad_end = jnp.cumsum(padded)
    pad_start = pad_end - padded
    start = jnp.cumsum(counts) - counts
    experts = jnp.arange(N_EXPERTS, dtype=jnp.int32)[:, None, None]
    pos = rank + jnp.sum(jnp.where(top_e[None] == experts, pad_start[:, None, None], 0), axis=0)
    n_blocks = -(-A // tm) + N_EXPERTS
    P = n_blocks * tm
    block_first = jnp.arange(n_blocks, dtype=jnp.int32) * tm
    block_e = jnp.minimum(jnp.sum(pad_end[None, :] <= block_first[:, None], axis=1),
                          N_EXPERTS - 1).astype(jnp.int32)
    n_used = (pad_end[-1] // tm).astype(jnp.int32).reshape(1)
    tok = jnp.broadcast_to(jnp.arange(T, dtype=jnp.int32)[None, :], (TOP_K, T))
    _, sorted_tok = lax.sort_key_val(pos.reshape(A), tok.reshape(A))
    in_blk = jnp.arange(tm, dtype=jnp.int32)[None, :]
    r = (block_first - pad_start[block_e])[:, None] + in_blk
    compact = jnp.clip(start[block_e][:, None] + r, 0, A - 1)
    filler = (block_first[:, None] + in_blk) % T
    slot_tok = jnp.where(r < counts[block_e][:, None], sorted_tok[compact], filler).reshape(P)
    return pos, slot_tok, block_e, n_used


def kernel(x, p, g_mix, w_in, hyena_conv_w, hyena_conv_b, flt_w1, flt_b1, flt_w_inner, flt_b_inner, flt_freq, flt_w_out, hyena_d, hyena_gn, lambda_q1, lambda_k1, lambda_q2, lambda_k2, attn_subln, w_out, g_ffn, router_w, router_b, w_gate, b_gate, w_up, b_up, w_down, b_down, w_ple, g_ple, w_ple_gate, b_ple_gate, g_final):
    B, S, D = x.shape
    T = B * S
    i = 0
    x2 = x.reshape(T, D)

    cos_t, s1_t, s2_t = _rope_tables(S)
    q, k, vt, hy = _in_proj(x2, g_mix[i], w_in[i].astype(BF16), cos_t, s1_t, s2_t, S)

    lam = (jnp.exp(jnp.sum(lambda_q1[i] * lambda_k1[i])) - jnp.exp(jnp.sum(lambda_q2[i] * lambda_k2[i]))
           + LAMBDA_INIT).reshape(1).astype(F32)
    attn = _diff_attention(lam, q.reshape(B, S, -1), k.reshape(B, S, -1), vt, attn_subln[i])

    u, hx0 = _short_conv(hy.reshape(B, S, -1), hyena_conv_w[i], hyena_conv_b[i])
    h_fwd, h_bwd = _hyena_filters(S, flt_w1[i], flt_b1[i], flt_w_inner[i], flt_b_inner[i],
                                  flt_freq[i], flt_w_out[i])
    yc = _long_conv(u, h_fwd, h_bwd)

    h1, hn, route, cnt = _out_proj(attn.reshape(T, -1), yc.reshape(T, -1), u.reshape(T, -1),
                                   hx0.reshape(T, -1), hyena_d[i], hyena_gn[i],
                                   w_out[i].astype(BF16), x2, g_ffn[i], router_w[i], router_b[i])

    pos, slot_tok, block_e, n_used = _dispatch_indices(route, cnt, T)
    n_parts = COMBINE_PARTS
    head_rows = slot_tok.shape[0] // EXPERT_TILE // DISPATCH_HEAD_DIV * EXPERT_TILE
    x_parts = [hn[slot_tok[:head_rows]], hn[slot_tok[head_rows:]]]
    y = _experts(block_e, n_used, x_parts, w_gate[i], b_gate[i], w_up[i], b_up[i],
                 w_down[i], b_down[i])
    pos_parts = jnp.split(pos, n_parts, axis=1)
    yg_parts = [y[pp.reshape(-1)].reshape(TOP_K, T // n_parts, D) for pp in pos_parts]

    out = _final(h1, yg_parts, route, p[i].reshape(T, -1), w_ple[i].astype(BF16), g_ple[i],
                 w_ple_gate[i].astype(BF16), b_ple_gate[i], g_final)
    return out.reshape(B, S, D)
```

```python
import functools
import math

import jax
import jax.numpy as jnp
from jax import lax
from jax.experimental import pallas as pl
from jax.experimental.pallas import tpu as pltpu

F32 = jnp.float32
BF16 = jnp.bfloat16

D_MODEL = 1024
ATTN_WIDTH = 512
HYENA_WIDTH = 512
DIFF_HEADS = 4
DIFF_HEAD_DIM = 64
DIFF_V_DIM = 128
HYENA_GROUPS = 8
FILTER_EMB = 33
FILTER_INNER = 2
FAST_DECAY_PCT = 0.3
SLOW_DECAY_PCT = 1.5
DECAY_TARGET = 1e-2
ROPE_THETA = 10000.0
N_EXPERTS = 32
TOP_K = 4
SWIGLU_ALPHA = 1.702
SWIGLU_LIMIT = 7.0
EPS = 1e-6
SUBLN_EPS = 1e-5
LAMBDA_INIT = 0.8 - 0.6 * math.exp(-0.3 * 0)

V7X_VMEM_LIMIT_BYTES = 56 * 1024 * 1024
LANES = 128
SUBLANES = 8

TOKEN_TILE = 512
ATTN_STREAM_W = 1024
ATTN_ONES_ROWS = 16
ATTN_UNROLL = 4
ATTN_KV_TILE = 512
EXPERT_TILE = 512
ROUTE_ROWS = 16
DISPATCH_HEAD_DIV = 4
COMBINE_PARTS = 2


def _params(*sem):
    return pltpu.CompilerParams(dimension_semantics=sem, vmem_limit_bytes=V7X_VMEM_LIMIT_BYTES)


def _rms(x, g, eps):
    return x * lax.rsqrt(jnp.mean(x * x, axis=-1, keepdims=True) + eps) * g


def _inproj_kernel(x_ref, g_ref, w_ref, wvt_ref, c_ref, s1_ref, s2_ref, q_ref, k_ref, vt_ref, hy_ref):
    a = _rms(x_ref[...], g_ref[...], EPS).astype(BF16)
    cos, s1, s2 = c_ref[...], s1_ref[...], s2_ref[...]

    half = DIFF_HEAD_DIM // 2

    def rope(t):
        return (t * cos + pltpu.roll(t, LANES - half, axis=1) * s1
                + pltpu.roll(t, half, axis=1) * s2)

    q_scale = (DIFF_HEAD_DIM ** -0.5) * math.log2(math.e)
    qk = jnp.dot(a, w_ref[:, : 2 * ATTN_WIDTH], preferred_element_type=F32)
    for j in range(ATTN_WIDTH // LANES):
        sl = slice(j * LANES, (j + 1) * LANES)
        q_ref[:, sl] = (rope(qk[:, sl]) * q_scale).astype(BF16)
        k_ref[:, sl] = rope(qk[:, ATTN_WIDTH + j * LANES: ATTN_WIDTH + (j + 1) * LANES]).astype(BF16)
    vt_ref[0] = lax.dot_general(wvt_ref[...], a, (((1,), (1,)), ((), ())),
                                preferred_element_type=F32).astype(BF16)
    hy_ref[...] = jnp.dot(a, w_ref[:, 3 * ATTN_WIDTH:], preferred_element_type=F32).astype(hy_ref.dtype)


def _in_proj(x2, g_mix, w_in_bf, cos_t, s1_t, s2_t, seq):
    T, D = x2.shape
    tm = TOKEN_TILE
    nseq = seq // tm
    wvt = w_in_bf[:, 2 * ATTN_WIDTH: 3 * ATTN_WIDTH].T
    row = lambda i: (i, 0)
    const = lambda i: (0, 0)
    pos = lambda i: (i % nseq, 0)
    return pl.pallas_call(
        _inproj_kernel,
        grid=(T // tm,),
        in_specs=[
            pl.BlockSpec((tm, D), row),
            pl.BlockSpec((1, D), const),
            pl.BlockSpec(w_in_bf.shape, const),
            pl.BlockSpec(wvt.shape, const),
            pl.BlockSpec((tm, LANES), pos),
            pl.BlockSpec((tm, LANES), pos),
            pl.BlockSpec((tm, LANES), pos),
        ],
        out_specs=[
            pl.BlockSpec((tm, ATTN_WIDTH), row),
            pl.BlockSpec((tm, ATTN_WIDTH), row),
            pl.BlockSpec((1, ATTN_WIDTH, tm), lambda i: (i, 0, 0)),
            pl.BlockSpec((tm, 3 * HYENA_WIDTH), row),
        ],
        out_shape=[
            jax.ShapeDtypeStruct((T, ATTN_WIDTH), BF16),
            jax.ShapeDtypeStruct((T, ATTN_WIDTH), BF16),
            jax.ShapeDtypeStruct((T // tm, ATTN_WIDTH, tm), BF16),
            jax.ShapeDtypeStruct((T, 3 * HYENA_WIDTH), BF16),
        ],
        compiler_params=_params("parallel"),
        name="in_proj",
    )(x2, g_mix.reshape(1, D), w_in_bf, wvt, cos_t, s1_t, s2_t)


def _attn_kernel(lam_ref, q_ref, k_ref, vt_ref, g_ref, o_ref, s_ref, m_ref, acc_ref, o0_ref, *, kc):
    w = ATTN_STREAM_W
    nc = k_ref.shape[1] // kc
    n_groups = q_ref.shape[1] // w

    def group_rows(grp):
        return pl.ds(pl.multiple_of(grp * w, w), w)

    def stream_q(grp, c):
        q = q_ref[0, group_rows(grp), :]
        lane = lax.broadcasted_iota(jnp.int32, q.shape, 1)
        keep = (lane < DIFF_HEAD_DIM) if c == 0 else (lane >= DIFF_HEAD_DIM)
        return jnp.where(keep, q, jnp.zeros_like(q))

    def fold8(t, op):
        r = t[0:SUBLANES]
        for j in range(1, kc // SUBLANES):
            r = op(r, t[j * SUBLANES:(j + 1) * SUBLANES])
        return r

    def score_chunk(i, qc, j):
        rows = pl.ds(pl.multiple_of(j * kc, kc), kc)
        s = lax.dot_general(k_ref[0, rows, :], qc, (((1,), (1,)), ((), ())),
                            preferred_element_type=F32)
        s_ref[i, rows, :] = s
        m_ref[i] = jnp.maximum(m_ref[i], fold8(s, jnp.maximum))

    ones_rows = jnp.ones((ATTN_ONES_ROWS, kc), BF16)

    def prob_chunk(i, j, m):
        rows = pl.ds(pl.multiple_of(j * kc, kc), kc)
        p = jnp.exp2(s_ref[i, rows, :] - m)
        v_aug = jnp.concatenate([vt_ref[j], ones_rows], axis=0)
        acc_ref[...] += jnp.dot(v_aug, p.astype(BF16), preferred_element_type=F32)

    def phase(read_buf, write_buf, q_next):
        if write_buf is not None:
            m_ref[write_buf] = jnp.full(m_ref.shape[1:], -jnp.inf, F32)
        if read_buf is not None:
            m = jnp.max(m_ref[read_buf], axis=0, keepdims=True)
            acc_ref[...] = jnp.zeros_like(acc_ref)

        def body(j, carry):
            if write_buf is not None:
                score_chunk(write_buf, q_next, j)
            if read_buf is not None:
                prob_chunk(read_buf, j, m)
            return carry

        lax.fori_loop(0, nc, body, 0, unroll=ATTN_UNROLL)
        if read_buf is None:
            return None
        return acc_ref[:DIFF_V_DIM, :] * (1.0 / acc_ref[DIFF_V_DIM:DIFF_V_DIM + 1, :])

    def finish_group(grp, out0, out1):
        o = out0 - lam_ref[0] * out1
        o = o * lax.rsqrt(jnp.mean(o * o, axis=0, keepdims=True) + SUBLN_EPS)
        o = o * (g_ref[...] * (1.0 - LAMBDA_INIT))
        o_ref[0, group_rows(grp), :] = o.T.astype(o_ref.dtype)

    phase(None, 0, stream_q(0, 0))

    def group_body(grp, carry):
        o0_ref[...] = phase(0, 1, stream_q(grp, 1))
        out1 = phase(1, 0, stream_q(grp + 1, 0))
        finish_group(grp, o0_ref[...], out1)
        return carry

    lax.fori_loop(0, n_groups - 1, group_body, 0)
    last = n_groups - 1
    o0_ref[...] = phase(0, 1, stream_q(last, 1))
    out1 = phase(1, None, None)
    finish_group(last, o0_ref[...], out1)


def _diff_attention(lam, q, k, vt, g_subln):
    B, S, _ = q.shape
    kc, w = ATTN_KV_TILE, ATTN_STREAM_W
    nc = S // kc
    assert vt.shape[2] == kc and S % w == 0 and S // w >= 2
    head = lambda b, h: (b, 0, h)
    return pl.pallas_call(
        functools.partial(_attn_kernel, kc=kc),
        grid=(B, DIFF_HEADS),
        in_specs=[
            pl.BlockSpec(memory_space=pltpu.SMEM),
            pl.BlockSpec((1, S, LANES), head),
            pl.BlockSpec((1, S, LANES), head),
            pl.BlockSpec((nc, DIFF_V_DIM, kc), lambda b, h: (b, h, 0)),
            pl.BlockSpec((DIFF_V_DIM, 1), lambda b, h: (0, 0)),
        ],
        out_specs=pl.BlockSpec((1, S, LANES), head),
        out_shape=jax.ShapeDtypeStruct((B, S, ATTN_WIDTH), BF16),
        scratch_shapes=[pltpu.VMEM((2, S, w), F32),
                        pltpu.VMEM((2, SUBLANES, w), F32),
                        pltpu.VMEM((DIFF_V_DIM + ATTN_ONES_ROWS, w), F32),
                        pltpu.VMEM((DIFF_V_DIM, w), F32)],
        compiler_params=_params("parallel", "parallel"),
        name="diff_attn",
    )(lam, q, k, vt, g_subln.reshape(DIFF_V_DIM, 1))


def _shortconv_kernel(hy_ref, prev_ref, next_ref, w_ref, b_ref, u_ref, x0_ref):
    i = pl.program_id(1)
    last = pl.num_programs(1) - 1
    x = hy_ref[0].astype(F32)
    ts = x.shape[0]
    prev_row = jnp.where(i == 0, 0.0, prev_ref[0, HALO_ROWS - 1:HALO_ROWS, :].astype(F32))
    next_row = jnp.where(i == last, 0.0, next_ref[0, 0:1, :].astype(F32))
    row = lax.broadcasted_iota(jnp.int32, (ts, 1), 0)
    xm = jnp.where(row == 0, prev_row, pltpu.roll(x, 1, axis=0))
    xp = jnp.where(row == ts - 1, next_row, pltpu.roll(x, ts - 1, axis=0))
    y = b_ref[...] + xm * w_ref[0:1, :] + x * w_ref[1:2, :] + xp * w_ref[2:3, :]
    C = HYENA_WIDTH
    u_ref[0] = (y[:, :C] * y[:, 2 * C:]).astype(u_ref.dtype)
    x0_ref[0] = y[:, C:2 * C].astype(x0_ref.dtype)


HALO_ROWS = 16


def _short_conv(hy, conv_w, conv_b):
    B, S, C3 = hy.shape
    ts = TOKEN_TILE
    nb = ts // HALO_ROWS
    return pl.pallas_call(
        _shortconv_kernel,
        grid=(B, S // ts),
        in_specs=[
            pl.BlockSpec((1, ts, C3), lambda b, i: (b, i, 0)),
            pl.BlockSpec((1, HALO_ROWS, C3), lambda b, i: (b, jnp.maximum(i * nb - 1, 0), 0)),
            pl.BlockSpec((1, HALO_ROWS, C3), lambda b, i: (b, jnp.minimum((i + 1) * nb, S // HALO_ROWS - 1), 0)),
            pl.BlockSpec((3, C3), lambda b, i: (0, 0)),
            pl.BlockSpec((1, C3), lambda b, i: (0, 0)),
        ],
        out_specs=[
            pl.BlockSpec((1, ts, HYENA_WIDTH), lambda b, i: (b, i, 0)),
            pl.BlockSpec((1, ts, HYENA_WIDTH), lambda b, i: (b, i, 0)),
        ],
        out_shape=[
            jax.ShapeDtypeStruct((B, S, HYENA_WIDTH), BF16),
            jax.ShapeDtypeStruct((B, S, HYENA_WIDTH), BF16),
        ],
        compiler_params=_params("parallel", "parallel"),
        name="short_conv",
    )(hy, hy, hy, conv_w, conv_b.reshape(1, C3))


FFT_N = 8192
FFT_N1 = 64
FFT_N2 = 128
FFT_CH_BLOCK = 32
FFT_GROUP = 4
FFT_UNROLL = 2


def _dft_constants():
    import numpy as np
    n1, n2, n = FFT_N1, FFT_N2, FFT_N
    k1 = np.arange(n1)[:, None]
    t1 = np.arange(n1)[None, :]
    f1 = np.exp(-2j * np.pi * k1 * t1 / n1)
    f1h = f1[:, : n1 // 2]
    w1c = np.block([[f1h.real, -f1h.imag], [f1h.imag, f1h.real]])
    w1r = np.concatenate([f1h.real, f1h.imag], axis=0)
    t2 = np.arange(n2)[:, None]
    k2 = np.arange(n2)[None, :]
    f2 = np.exp(-2j * np.pi * t2 * k2 / n2)
    w2a = np.concatenate([f2.real, f2.imag], axis=1)
    w2b = np.concatenate([-f2.imag, f2.real], axis=1)
    g2 = np.conj(f2)
    w3a = np.concatenate([g2.real, g2.imag], axis=1)
    w3b = np.concatenate([-g2.imag, g2.real], axis=1)
    h = np.conj(f1).T[: n1 // 2] / n
    w4 = np.block([[h.real, -h.imag], [h.imag, h.real]])
    tw = np.exp(-2j * np.pi * np.arange(n1)[:, None] * np.arange(n2)[None, :] / n)
    bf = lambda a: jnp.asarray(a, dtype=F32).astype(BF16)
    return dict(w1c=bf(w1c), w1r=bf(w1r), w2a=bf(w2a), w2b=bf(w2b), w3a=bf(w3a), w3b=bf(w3b),
                w4=bf(w4), tc=jnp.asarray(np.tile(tw.real, (1, FFT_GROUP)), F32),
                ts=jnp.asarray(np.tile(tw.imag, (1, FFT_GROUP)), F32))


def _dft_forward(x_ref, w1_ref, tc_ref, ts_ref, w2a_ref, w2b_ref, ar_ref, ai_ref):
    ng = x_ref.shape[0] // FFT_GROUP
    n1, n2 = FFT_N1, FFT_N2
    w1 = w1_ref[...]
    tc, ts = tc_ref[...], ts_ref[...]

    def body(g, carry):
        xg = jnp.concatenate([x_ref[g * FFT_GROUP + cl] for cl in range(FFT_GROUP)], axis=1)
        a = jnp.dot(w1, xg, preferred_element_type=F32)
        ar, ai = a[:n1], a[n1:]
        tr = (ar * tc - ai * ts).astype(BF16)
        ti = (ar * ts + ai * tc).astype(BF16)
        for cl in range(FFT_GROUP):
            rows = pl.ds(pl.multiple_of((g * FFT_GROUP + cl) * n1, n1), n1)
            ar_ref[rows, :] = tr[:, cl * n2:(cl + 1) * n2]
            ai_ref[rows, :] = ti[:, cl * n2:(cl + 1) * n2]
        return carry

    lax.fori_loop(0, ng, body, 0, unroll=FFT_UNROLL)
    return (jnp.dot(ar_ref[...], w2a_ref[...], preferred_element_type=F32)
            + jnp.dot(ai_ref[...], w2b_ref[...], preferred_element_type=F32))


def _spectrum_kernel(hf_ref, hb_ref, w1_ref, tc_ref, ts_ref, w2a_ref, w2b_ref, o_ref, ar_ref, ai_ref):
    n2 = FFT_N2
    f = _dft_forward(hf_ref, w1_ref, tc_ref, ts_ref, w2a_ref, w2b_ref, ar_ref, ai_ref)
    o_ref[...] = f.reshape(o_ref.shape)
    b = _dft_forward(hb_ref, w1_ref, tc_ref, ts_ref, w2a_ref, w2b_ref, ar_ref, ai_ref)
    b = b.reshape(o_ref.shape)
    o_ref[:, :, :n2] = o_ref[:, :, :n2] + b[:, :, :n2]
    o_ref[:, :, n2:] = o_ref[:, :, n2:] - b[:, :, n2:]


def _fftconv_kernel(x_ref, kf_ref, w1_ref, tc_ref, ts_ref, w2a_ref, w2b_ref, w3a_ref, w3b_ref,
                    w4_ref, y_ref, ar_ref, ai_ref, c_ref):
    cb = x_ref.shape[0]
    ng = cb // FFT_GROUP
    n1, n2 = FFT_N1, FFT_N2
    b = _dft_forward(x_ref, w1_ref, tc_ref, ts_ref, w2a_ref, w2b_ref, ar_ref, ai_ref)
    kf = kf_ref[...].reshape(cb * n1, 2 * n2)
    br, bi = b[:, :n2], b[:, n2:]
    kr, ki = kf[:, :n2], kf[:, n2:]
    ar_ref[...] = (br * kr - bi * ki).astype(BF16)
    ai_ref[...] = (br * ki + bi * kr).astype(BF16)
    c_ref[...] = (jnp.dot(ar_ref[...], w3a_ref[...], preferred_element_type=F32)
                  + jnp.dot(ai_ref[...], w3b_ref[...], preferred_element_type=F32))
    w4 = w4_ref[...]
    tc, ts = tc_ref[...], ts_ref[...]

    def body(g, carry):
        rows = [pl.ds(pl.multiple_of((g * FFT_GROUP + cl) * n1, n1), n1) for cl in range(FFT_GROUP)]
        cr = jnp.concatenate([c_ref[r, :n2] for r in rows], axis=1)
        ci = jnp.concatenate([c_ref[r, n2:] for r in rows], axis=1)
        dr = (cr * tc + ci * ts).astype(BF16)
        di = (ci * tc - cr * ts).astype(BF16)
        yg = (jnp.dot(w4[:, :n1], dr, preferred_element_type=F32)
              + jnp.dot(w4[:, n1:], di, preferred_element_type=F32)).astype(y_ref.dtype)
        for cl in range(FFT_GROUP):
            y_ref[g * FFT_GROUP + cl] = yg[:, cl * n2:(cl + 1) * n2]
        return carry

    lax.fori_loop(0, ng, body, 0, unroll=FFT_UNROLL)


def _const_spec(a):
    nd = a.ndim
    return pl.BlockSpec(a.shape, lambda *_: (0,) * nd)


def _filter_spectrum(hf_slabs, hb_slabs, cst):
    C = hf_slabs.shape[0]
    cb = FFT_CH_BLOCK
    consts = [cst["w1r"], cst["tc"], cst["ts"], cst["w2a"], cst["w2b"]]
    slab = pl.BlockSpec((cb, FFT_N1 // 2, FFT_N2), lambda i: (i, 0, 0))
    return pl.pallas_call(
        _spectrum_kernel,
        grid=(C // cb,),
        in_specs=[slab, slab] + [_const_spec(a) for a in consts],
        out_specs=pl.BlockSpec((cb, FFT_N1, 2 * FFT_N2), lambda i: (i, 0, 0)),
        out_shape=jax.ShapeDtypeStruct((C, FFT_N1, 2 * FFT_N2), F32),
        scratch_shapes=[pltpu.VMEM((cb * FFT_N1, FFT_N2), BF16), pltpu.VMEM((cb * FFT_N1, FFT_N2), BF16)],
        compiler_params=_params("parallel"),
        name="filter_spectrum",
    )(hf_slabs, hb_slabs, *consts)


FILTER_TIME_TILE = 512


def _filter_kernel(zt_ref, w1t_ref, b1_ref, wit_ref, bi_ref, fr_ref, wot_ref, dec_ref, hf_ref, hb_ref):
    hp = lax.Precision.HIGHEST
    h = jnp.sin(fr_ref[0] * (jnp.dot(w1t_ref[...], zt_ref[...], precision=hp,
                                     preferred_element_type=F32) + b1_ref[...]))
    for j in range(FILTER_INNER):
        h = jnp.sin(fr_ref[j + 1] * (jnp.dot(wit_ref[j], h, precision=hp,
                                             preferred_element_type=F32) + bi_ref[j]))
    o = jnp.dot(wot_ref[...], h, precision=hp, preferred_element_type=F32)
    dec = dec_ref[...]
    hf_ref[...] = o[:HYENA_WIDTH] * dec
    hb_ref[...] = o[HYENA_WIDTH:] * dec


def _hyena_filters(seq, w1, b1, w_inner, b_inner, freq, w_out):
    C = HYENA_WIDTH
    order = w1.shape[1]
    pos = jnp.arange(seq, dtype=F32)
    t = pos / (seq - 1)
    bands = (FILTER_EMB - 1) // 2
    f = jnp.linspace(1e-4, bands - 1, bands, dtype=F32)
    fw = ((2.0 * math.pi / seq) * pos)[:, None] * f[None, :]
    z = jnp.concatenate([t[:, None], jnp.cos(fw), -jnp.sin(fw)], axis=-1)
    zt = jnp.zeros((LANES, seq), F32).at[:FILTER_EMB].set(z.T)
    w1t = jnp.zeros((order, LANES), F32).at[:, :FILTER_EMB].set(w1.T)
    max_decay = math.log(DECAY_TARGET) / FAST_DECAY_PCT
    min_decay = math.log(DECAY_TARGET) / SLOW_DECAY_PCT
    deltas = jnp.abs(jnp.linspace(min_decay, max_decay, C, dtype=F32))
    dec_t = jnp.exp(-deltas[:, None] * t[None, :])
    tt = FILTER_TIME_TILE
    lane_blk = lambda r: pl.BlockSpec((r, tt), lambda i: (0, i))
    args = [zt, w1t, b1.reshape(order, 1), jnp.swapaxes(w_inner, 1, 2),
            b_inner.reshape(FILTER_INNER, order, 1), freq.reshape(FILTER_INNER + 1, order, 1),
            w_out.T, dec_t]
    return pl.pallas_call(
        _filter_kernel,
        grid=(seq // tt,),
        in_specs=[lane_blk(LANES)] + [_const_spec(a) for a in args[1:7]] + [lane_blk(C)],
        out_specs=[lane_blk(C), lane_blk(C)],
        out_shape=[jax.ShapeDtypeStruct((C, seq), F32), jax.ShapeDtypeStruct((C, seq), F32)],
        compiler_params=_params("parallel"),
        name="hyena_filters",
    )(*args)


def _fft_conv(x_slabs, kf, cst):
    P, C = x_slabs.shape[:2]
    cb = FFT_CH_BLOCK
    consts = [cst["w1c"], cst["tc"], cst["ts"], cst["w2a"], cst["w2b"], cst["w3a"], cst["w3b"], cst["w4"]]
    return pl.pallas_call(
        _fftconv_kernel,
        grid=(C // cb, P),
        in_specs=[pl.BlockSpec((None, cb, FFT_N1, FFT_N2), lambda i, p: (p, i, 0, 0)),
                  pl.BlockSpec((cb, FFT_N1, 2 * FFT_N2), lambda i, p: (i, 0, 0))]
                 + [_const_spec(a) for a in consts],
        out_specs=pl.BlockSpec((None, cb, FFT_N1, FFT_N2), lambda i, p: (p, i, 0, 0)),
        out_shape=jax.ShapeDtypeStruct((P, C, FFT_N1, FFT_N2), BF16),
        scratch_shapes=[pltpu.VMEM((cb * FFT_N1, FFT_N2), BF16), pltpu.VMEM((cb * FFT_N1, FFT_N2), BF16),
                        pltpu.VMEM((cb * FFT_N1, 2 * FFT_N2), F32)],
        compiler_params=_params("parallel", "arbitrary"),
        name="fft_conv",
    )(x_slabs, kf, *consts)


def _split_bf16(x):
    hi = x.astype(BF16)
    lo = (x - hi.astype(F32)).astype(BF16)
    return hi, lo


def _outproj_kernel(attn_ref, yc_ref, u_ref, x0_ref, d_ref, gn_ref, grp_ref, wo_ref, x_ref,
                    gf_ref, rwh_ref, rwl_ref, rb_ref, tri_ref, h_ref, hn_ref, rt_ref, wt_ref,
                    cnt_out_ref, cnt_ref):
    @pl.when(pl.program_id(0) == 0)
    def _():
        cnt_ref[...] = jnp.zeros_like(cnt_ref)

    grp = grp_ref[...]
    gsz = HYENA_WIDTH // HYENA_GROUPS

    def token_tile(cnt_col):
        rows = slice(None)
        z = ((yc_ref[rows, :].astype(F32) + u_ref[rows, :].astype(F32) * d_ref[...])
             * x0_ref[rows, :].astype(F32))
        zh, zl = _split_bf16(z * z)
        ssq = (jnp.dot(zh, grp, preferred_element_type=F32)
               + jnp.dot(zl, grp, preferred_element_type=F32))
        hy_out = (z * lax.rsqrt(ssq * (1.0 / gsz) + EPS) * gn_ref[...]).astype(BF16)
        mix = (jnp.dot(attn_ref[rows, :], wo_ref[:ATTN_WIDTH, :], preferred_element_type=F32)
               + jnp.dot(hy_out, wo_ref[ATTN_WIDTH:, :], preferred_element_type=F32))
        h = x_ref[rows, :] + mix
        h_ref[rows, :] = h
        hn = _rms(h, gf_ref[...], EPS)
        hn_ref[rows, :] = hn.astype(BF16)
        nh, nl = _split_bf16(hn)
        nt = (((1,), (1,)), ((), ()))
        logits = (lax.dot_general(rwh_ref[...], nh, nt, preferred_element_type=F32)
                  + lax.dot_general(rwh_ref[...], nl, nt, preferred_element_type=F32)
                  + lax.dot_general(rwl_ref[...], nh, nt, preferred_element_type=F32)) + rb_ref[...]

        erow = lax.broadcasted_iota(jnp.int32, logits.shape, 0)
        work = logits
        top_val, top_idx, top_hot = [], [], []
        for _ in range(TOP_K):
            m = jnp.max(work, axis=0, keepdims=True)
            idx = jnp.min(jnp.where(work == m, erow, N_EXPERTS), axis=0, keepdims=True)
            hot = erow == idx
            top_val.append(m)
            top_idx.append(idx)
            top_hot.append(hot)
            work = jnp.where(hot, -jnp.inf, work)
        ex = [jnp.exp(v - top_val[0]) for v in top_val]
        inv_den = 1.0 / (ex[0] + ex[1] + ex[2] + ex[3])
        sel = jnp.zeros(logits.shape, F32)
        for hot in top_hot:
            sel = sel + hot.astype(F32)
        before = jnp.dot(sel.astype(BF16), tri_ref[...], preferred_element_type=F32) + cnt_col
        tm = logits.shape[1]
        row_rt = lax.broadcasted_iota(jnp.int32, (ROUTE_ROWS, tm), 0)
        row_w = lax.broadcasted_iota(jnp.int32, (LANES, tm), 0)
        packed = jnp.zeros((ROUTE_ROWS, tm), F32)
        w_rows = jnp.zeros((LANES, tm), F32)
        for r, hot in enumerate(top_hot):
            rank_r = jnp.sum(jnp.where(hot, before, 0.0), axis=0, keepdims=True)
            weight_r = ex[r] * inv_den
            packed = jnp.where(row_rt == r, top_idx[r].astype(F32), packed)
            packed = jnp.where(row_rt == TOP_K + r, rank_r, packed)
            packed = jnp.where(row_rt == 2 * TOP_K + r, weight_r, packed)
            w_rows = jnp.where(row_w == r, weight_r, w_rows)
        rt_ref[...] = packed
        wt_ref[...] = w_rows.T
        return cnt_col + jnp.sum(sel, axis=1, keepdims=True)

    cnt_col = token_tile(cnt_ref[:, 0:1])
    cnt_ref[...] = jnp.broadcast_to(cnt_col, cnt_ref.shape)
    cnt_out_ref[...] = cnt_ref[...]


def _out_proj(attn2, yc2, u2, x02, hyena_d, hyena_gn, w_out_bf, x2, g_ffn, router_w, router_b):
    T, D = x2.shape
    tm = TOKEN_TILE
    C = HYENA_WIDTH
    gid = jnp.arange(C) // (C // HYENA_GROUPS)
    grp = (gid[:, None] == gid[None, :]).astype(BF16)
    rwh, rwl = _split_bf16(router_w.T)
    rb = router_b.reshape(N_EXPERTS, 1)
    tri = (jnp.arange(tm)[:, None] < jnp.arange(tm)[None, :]).astype(BF16)
    row = lambda i: (i, 0)
    const = lambda i: (0, 0)
    return pl.pallas_call(
        _outproj_kernel,
        grid=(T // tm,),
        in_specs=[
            pl.BlockSpec((tm, ATTN_WIDTH), row),
            pl.BlockSpec((tm, C), row),
            pl.BlockSpec((tm, C), row),
            pl.BlockSpec((tm, C), row),
            pl.BlockSpec((1, C), const),
            pl.BlockSpec((1, C), const),
            pl.BlockSpec((C, C), const),
            pl.BlockSpec((D, D), const),
            pl.BlockSpec((tm, D), row),
            pl.BlockSpec((1, D), const),
            pl.BlockSpec((N_EXPERTS, D), const),
            pl.BlockSpec((N_EXPERTS, D), const),
            pl.BlockSpec((N_EXPERTS, 1), const),
            pl.BlockSpec((tm, tm), const),
        ],
        out_specs=[
            pl.BlockSpec((tm, D), row),
            pl.BlockSpec((tm, D), row),
            pl.BlockSpec((ROUTE_ROWS, tm), lambda i: (0, i)),
            pl.BlockSpec((tm, LANES), row),
            pl.BlockSpec((N_EXPERTS, LANES), const),
        ],
        out_shape=[
            jax.ShapeDtypeStruct((T, D), F32),
            jax.ShapeDtypeStruct((T, D), BF16),
            jax.ShapeDtypeStruct((ROUTE_ROWS, T), F32),
            jax.ShapeDtypeStruct((T, LANES), F32),
            jax.ShapeDtypeStruct((N_EXPERTS, LANES), F32),
        ],
        scratch_shapes=[pltpu.VMEM((N_EXPERTS, LANES), F32)],
        compiler_params=_params("arbitrary"),
        name="out_proj",
    )(attn2, yc2, u2, x02, hyena_d.reshape(1, C), hyena_gn.reshape(1, C), grp, w_out_bf, x2,
      g_ffn.reshape(1, D), rwh, rwl, rb, tri)


def _expert_kernel(be_ref, nused_ref, x_ref, wg_ref, bg_ref, wu_ref, bu_ref, wd_ref, bd_ref, *rest):
    y_ref, wg_bf, wu_bf, wd_bf = rest[-4:]
    i = pl.program_id(0)
    used = i < nused_ref[0]
    new_expert = jnp.logical_or(i == 0, be_ref[i] != be_ref[jnp.maximum(i - 1, 0)])

    @pl.when(jnp.logical_and(used, new_expert))
    def _():
        wg_bf[...] = wg_ref[0].astype(BF16)
        wu_bf[...] = wu_ref[0].astype(BF16)
        wd_bf[...] = wd_ref[0].astype(BF16)

    @pl.when(used)
    def _():
        x = x_ref[...]
        g = jnp.minimum(jnp.dot(x, wg_bf[...], preferred_element_type=F32) + bg_ref[0], SWIGLU_LIMIT)
        u = jnp.clip(jnp.dot(x, wu_bf[...], preferred_element_type=F32) + bu_ref[0],
                     -SWIGLU_LIMIT, SWIGLU_LIMIT)
        a = (u + 1.0) * (g * jax.nn.sigmoid(SWIGLU_ALPHA * g))
        y = jnp.dot(a.astype(BF16), wd_bf[...], preferred_element_type=F32) + bd_ref[0]
        y_ref[...] = y.astype(y_ref.dtype)

    @pl.when(jnp.logical_not(used))
    def _():
        y_ref[...] = jnp.zeros_like(y_ref)


def _experts(block_e, n_used, x_parts, wg, bg, wu, bu, wd, bd):
    D = x_parts[0].shape[1]
    tm = EXPERT_TILE
    P = sum(xs.shape[0] for xs in x_parts)
    E, _, FF = wg.shape
    wmap = lambda i, be, nu: (be[i], 0, 0)
    y = None
    first = 0
    for part, xs in enumerate(x_parts):
        nb = xs.shape[0] // tm
        in_specs = [
            pl.BlockSpec((tm, D), lambda i, be, nu: (i, 0)),
            pl.BlockSpec((1, D, FF), wmap),
            pl.BlockSpec((1, 1, FF), wmap),
            pl.BlockSpec((1, D, FF), wmap),
            pl.BlockSpec((1, 1, FF), wmap),
            pl.BlockSpec((1, FF, D), wmap),
            pl.BlockSpec((1, 1, D), wmap),
        ]
        args = [block_e[first:first + nb], jnp.clip(n_used - first, 0, nb), xs,
                wg, bg.reshape(E, 1, FF), wu, bu.reshape(E, 1, FF), wd, bd.reshape(E, 1, D)]
        aliases = {}
        if y is not None:
            in_specs.append(pl.BlockSpec(memory_space=pl.ANY))
            args.append(y)
            aliases = {len(args) - 1: 0}
        grid_spec = pltpu.PrefetchScalarGridSpec(
            num_scalar_prefetch=2,
            grid=(nb,),
            in_specs=in_specs,
            out_specs=pl.BlockSpec((tm, D), lambda i, be, nu, first=first: (i + first, 0)),
            scratch_shapes=[pltpu.VMEM((D, FF), BF16), pltpu.VMEM((D, FF), BF16),
                            pltpu.VMEM((FF, D), BF16)],
        )
        y = pl.pallas_call(
            _expert_kernel,
            grid_spec=grid_spec,
            out_shape=jax.ShapeDtypeStruct((P, D), BF16),
            input_output_aliases=aliases,
            compiler_params=_params("arbitrary"),
            name=f"moe_experts_{part}",
        )(*args)
        first += nb
    return y


def _final_kernel(h_ref, yg_ref, rt_ref, p_ref, wp_ref, gp_ref, wg_ref, bg_ref, gfin_ref, *rest):
    o_ref = rest[-1]
    h = h_ref[...]
    for r in range(TOP_K):
        h = h + yg_ref[r].astype(F32) * rt_ref[:, r:r + 1]
    e = _rms(jnp.dot(p_ref[...].astype(BF16), wp_ref[...], preferred_element_type=F32),
             gp_ref[...], EPS)
    gate = jax.nn.sigmoid(jnp.dot(h.astype(BF16), wg_ref[...], preferred_element_type=F32)
                          + bg_ref[...])
    h = h + gate * e
    o_ref[...] = _rms(h, gfin_ref[...], EPS)


def _final(h1, yg_parts, route, p2, w_ple_bf, g_ple, w_gate_bf, b_gate, g_final):
    T, D = h1.shape
    tm = TOKEN_TILE
    PD = p2.shape[1]
    nb = T // tm // len(yg_parts)
    const = lambda i: (0, 0)
    out = None
    for part, yg in enumerate(yg_parts):
        row = lambda i, part=part: (i + part * nb, 0)
        in_specs = [
            pl.BlockSpec((tm, D), row),
            pl.BlockSpec((TOP_K, tm, D), lambda i: (0, i, 0)),
            pl.BlockSpec((tm, LANES), row),
            pl.BlockSpec((tm, PD), row),
            pl.BlockSpec((PD, D), const),
            pl.BlockSpec((1, D), const),
            pl.BlockSpec((D, D), const),
            pl.BlockSpec((1, D), const),
            pl.BlockSpec((1, D), const),
        ]
        args = [h1, yg, route, p2, w_ple_bf, g_ple.reshape(1, D), w_gate_bf, b_gate.reshape(1, D),
                g_final.reshape(1, D)]
        aliases = {}
        if out is not None:
            in_specs.append(pl.BlockSpec(memory_space=pl.ANY))
            args.append(out)
            aliases = {len(args) - 1: 0}
        out = pl.pallas_call(
            _final_kernel,
            grid=(nb,),
            in_specs=in_specs,
            out_specs=pl.BlockSpec((tm, D), row),
            out_shape=jax.ShapeDtypeStruct((T, D), F32),
            input_output_aliases=aliases,
            compiler_params=_params("parallel"),
            name=f"final_{part}",
        )(*args)
    return out


def _rope_tables(seq):
    d = DIFF_HEAD_DIM
    pos = jnp.arange(seq, dtype=F32)
    inv = ROPE_THETA ** (-jnp.arange(0, d, 2, dtype=F32) / d)
    ang = pos[:, None] * inv[None, :]
    cos, sin = jnp.cos(ang), jnp.sin(ang)
    z = jnp.zeros_like(sin)
    cos_t = jnp.tile(jnp.concatenate([cos, cos], -1), (1, LANES // d))
    s1_t = jnp.tile(jnp.concatenate([-sin, z], -1), (1, LANES // d))
    s2_t = jnp.tile(jnp.concatenate([z, sin], -1), (1, LANES // d))
    return cos_t, s1_t, s2_t


def _long_conv(u, hf_t, hb_t):
    B, L, C = u.shape
    assert 2 * L == FFT_N and B % 2 == 0
    P, R = B // 2, FFT_N1 // 2
    cst = _dft_constants()
    kf = _filter_spectrum(hf_t.reshape(C, R, FFT_N2).astype(BF16),
                          hb_t.reshape(C, R, FFT_N2).astype(BF16), cst)
    x_slabs = (u.reshape(2, P, R, FFT_N2, C).transpose(1, 4, 0, 2, 3)
               .reshape(P, C, FFT_N1, FFT_N2).astype(BF16))
    y = _fft_conv(x_slabs, kf, cst)
    return y.reshape(P, C, 2, R, FFT_N2).transpose(2, 0, 3, 4, 1).reshape(B, L, C)


def _dispatch_indices(route_t, cnt, T):
    tm = EXPERT_TILE
    A = T * TOP_K
    top_e = route_t[:TOP_K].astype(jnp.int32)
    rank = route_t[TOP_K:2 * TOP_K].astype(jnp.int32)
    counts = cnt[:, 0].astype(jnp.int32)
    padded = (counts + tm - 1) // tm * tm
    pad_end = jnp.cumsum(padded)
    pad_start = pad_end - padded
    start = jnp.cumsum(counts) - counts
    experts = jnp.arange(N_EXPERTS, dtype=jnp.int32)[:, None, None]
    pos = rank + jnp.sum(jnp.where(top_e[None] == experts, pad_start[:, None, None], 0), axis=0)
    n_blocks = -(-A // tm) + N_EXPERTS
    P = n_blocks * tm
    block_first = jnp.arange(n_blocks, dtype=jnp.int32) * tm
    block_e = jnp.minimum(jnp.sum(pad_end[None, :] <= block_first[:, None], axis=1),
                          N_EXPERTS - 1).astype(jnp.int32)
    n_used = (pad_end[-1] // tm).astype(jnp.int32).reshape(1)
    tok = jnp.broadcast_to(jnp.arange(T, dtype=jnp.int32)[None, :], (TOP_K, T))
    _, sorted_tok = lax.sort_key_val(pos.reshape(A), tok.reshape(A))
    in_blk = jnp.arange(tm, dtype=jnp.int32)[None, :]
    r = (block_first - pad_start[block_e])[:, None] + in_blk
    compact = jnp.clip(start[block_e][:, None] + r, 0, A - 1)
    filler = (block_first[:, None] + in_blk) % T
    slot_tok = jnp.where(r < counts[block_e][:, None], sorted_tok[compact], filler).reshape(P)
    return pos, slot_tok, block_e, n_used


def kernel(x, p, g_mix, w_in, hyena_conv_w, hyena_conv_b, flt_w1, flt_b1, flt_w_inner, flt_b_inner, flt_freq, flt_w_out, hyena_d, hyena_gn, lambda_q1, lambda_k1, lambda_q2, lambda_k2, attn_subln, w_out, g_ffn, router_w, router_b, w_gate, b_gate, w_up, b_up, w_down, b_down, w_ple, g_ple, w_ple_gate, b_ple_gate, g_final):
    B, S, D = x.shape
    T = B * S
    i = 0
    x2 = x.reshape(T, D)

    cos_t, s1_t, s2_t = _rope_tables(S)
    q, k, vt, hy = _in_proj(x2, g_mix[i], w_in[i].astype(BF16), cos_t, s1_t, s2_t, S)

    lam = (jnp.exp(jnp.sum(lambda_q1[i] * lambda_k1[i])) - jnp.exp(jnp.sum(lambda_q2[i] * lambda_k2[i]))
           + LAMBDA_INIT).reshape(1).astype(F32)
    attn = _diff_attention(lam, q.reshape(B, S, -1), k.reshape(B, S, -1), vt, attn_subln[i])

    u, hx0 = _short_conv(hy.reshape(B, S, -1), hyena_conv_w[i], hyena_conv_b[i])
    h_fwd, h_bwd = _hyena_filters(S, flt_w1[i], flt_b1[i], flt_w_inner[i], flt_b_inner[i],
                                  flt_freq[i], flt_w_out[i])
    yc = _long_conv(u, h_fwd, h_bwd)

    h1, hn, route_t, top_w, cnt = _out_proj(
        attn.reshape(T, -1), yc.reshape(T, -1), u.reshape(T, -1), hx0.reshape(T, -1), hyena_d[i],
        hyena_gn[i], w_out[i].astype(BF16), x2, g_ffn[i], router_w[i], router_b[i])

    pos, slot_tok, block_e, n_used = _dispatch_indices(route_t, cnt, T)
    n_parts = COMBINE_PARTS
    head_rows = slot_tok.shape[0] // EXPERT_TILE // DISPATCH_HEAD_DIV * EXPERT_TILE
    x_parts = [hn[slot_tok[:head_rows]], hn[slot_tok[head_rows:]]]
    y = _experts(block_e, n_used, x_parts, w_gate[i], b_gate[i], w_up[i], b_up[i],
                 w_down[i], b_down[i])
    pos_parts = jnp.split(pos, n_parts, axis=1)
    yg_parts = [y[pp.reshape(-1)].reshape(TOP_K, T // n_parts, D) for pp in pos_parts]

    out = _final(h1, yg_parts, top_w, p[i].reshape(T, -1), w_ple[i].astype(BF16), g_ple[i],
                 w_ple_gate[i].astype(BF16), b_ple_gate[i], g_final)
    return out.reshape(B, S, D)
```

```python
import functools
import math

import jax
import jax.numpy as jnp
from jax import lax
from jax.experimental import pallas as pl
from jax.experimental.pallas import tpu as pltpu

F32 = jnp.float32
BF16 = jnp.bfloat16

D_MODEL = 1024
ATTN_WIDTH = 512
HYENA_WIDTH = 512
DIFF_HEADS = 4
DIFF_HEAD_DIM = 64
DIFF_V_DIM = 128
HYENA_GROUPS = 8
FILTER_EMB = 33
FILTER_INNER = 2
FAST_DECAY_PCT = 0.3
SLOW_DECAY_PCT = 1.5
DECAY_TARGET = 1e-2
ROPE_THETA = 10000.0
N_EXPERTS = 32
TOP_K = 4
SWIGLU_ALPHA = 1.702
SWIGLU_LIMIT = 7.0
EPS = 1e-6
SUBLN_EPS = 1e-5
LAMBDA_INIT = 0.8 - 0.6 * math.exp(-0.3 * 0)

V7X_VMEM_LIMIT_BYTES = 56 * 1024 * 1024
LANES = 128
SUBLANES = 8

TOKEN_TILE = 512
ATTN_STREAM_W = 1024
ATTN_ONES_ROWS = 16
ATTN_UNROLL = 4
ATTN_KV_TILE = 512
EXPERT_TILE = 512
ROUTE_ROWS = 16
DISPATCH_HEAD_DIV = 6
COMBINE_PARTS = 2


def _params(*sem):
    return pltpu.CompilerParams(dimension_semantics=sem, vmem_limit_bytes=V7X_VMEM_LIMIT_BYTES)


def _rms(x, g, eps):
    return x * lax.rsqrt(jnp.mean(x * x, axis=-1, keepdims=True) + eps) * g


def _inproj_kernel(x_ref, g_ref, w_ref, wvt_ref, c_ref, s1_ref, s2_ref, q_ref, k_ref, vt_ref, hy_ref):
    a = _rms(x_ref[...], g_ref[...], EPS).astype(BF16)
    cos, s1, s2 = c_ref[...], s1_ref[...], s2_ref[...]

    half = DIFF_HEAD_DIM // 2

    def rope(t):
        return (t * cos + pltpu.roll(t, LANES - half, axis=1) * s1
                + pltpu.roll(t, half, axis=1) * s2)

    q_scale = (DIFF_HEAD_DIM ** -0.5) * math.log2(math.e)
    qk = jnp.dot(a, w_ref[:, : 2 * ATTN_WIDTH], preferred_element_type=F32)
    for j in range(ATTN_WIDTH // LANES):
        sl = slice(j * LANES, (j + 1) * LANES)
        q_ref[:, sl] = (rope(qk[:, sl]) * q_scale).astype(BF16)
        k_ref[:, sl] = rope(qk[:, ATTN_WIDTH + j * LANES: ATTN_WIDTH + (j + 1) * LANES]).astype(BF16)
    vt_ref[0] = lax.dot_general(wvt_ref[...], a, (((1,), (1,)), ((), ())),
                                preferred_element_type=F32).astype(BF16)
    hy_ref[...] = jnp.dot(a, w_ref[:, 3 * ATTN_WIDTH:], preferred_element_type=F32).astype(hy_ref.dtype)


def _in_proj(x2, g_mix, w_in_bf, cos_t, s1_t, s2_t, seq):
    T, D = x2.shape
    tm = TOKEN_TILE
    nseq = seq // tm
    wvt = w_in_bf[:, 2 * ATTN_WIDTH: 3 * ATTN_WIDTH].T
    row = lambda i: (i, 0)
    const = lambda i: (0, 0)
    pos = lambda i: (i % nseq, 0)
    return pl.pallas_call(
        _inproj_kernel,
        grid=(T // tm,),
        in_specs=[
            pl.BlockSpec((tm, D), row),
            pl.BlockSpec((1, D), const),
            pl.BlockSpec(w_in_bf.shape, const),
            pl.BlockSpec(wvt.shape, const),
            pl.BlockSpec((tm, LANES), pos),
            pl.BlockSpec((tm, LANES), pos),
            pl.BlockSpec((tm, LANES), pos),
        ],
        out_specs=[
            pl.BlockSpec((tm, ATTN_WIDTH), row),
            pl.BlockSpec((tm, ATTN_WIDTH), row),
            pl.BlockSpec((1, ATTN_WIDTH, tm), lambda i: (i, 0, 0)),
            pl.BlockSpec((tm, 3 * HYENA_WIDTH), row),
        ],
        out_shape=[
            jax.ShapeDtypeStruct((T, ATTN_WIDTH), BF16),
            jax.ShapeDtypeStruct((T, ATTN_WIDTH), BF16),
            jax.ShapeDtypeStruct((T // tm, ATTN_WIDTH, tm), BF16),
            jax.ShapeDtypeStruct((T, 3 * HYENA_WIDTH), BF16),
        ],
        compiler_params=_params("parallel"),
        name="in_proj",
    )(x2, g_mix.reshape(1, D), w_in_bf, wvt, cos_t, s1_t, s2_t)


def _attn_kernel(lam_ref, q_ref, k_ref, vt_ref, g_ref, o_ref, s_ref, m_ref, acc_ref, o0_ref, *, kc):
    w = ATTN_STREAM_W
    nc = k_ref.shape[1] // kc
    n_groups = q_ref.shape[1] // w

    def group_rows(grp):
        return pl.ds(pl.multiple_of(grp * w, w), w)

    def stream_q(grp, c):
        q = q_ref[0, group_rows(grp), :]
        lane = lax.broadcasted_iota(jnp.int32, q.shape, 1)
        keep = (lane < DIFF_HEAD_DIM) if c == 0 else (lane >= DIFF_HEAD_DIM)
        return jnp.where(keep, q, jnp.zeros_like(q))

    def fold8(t, op):
        r = t[0:SUBLANES]
        for j in range(1, kc // SUBLANES):
            r = op(r, t[j * SUBLANES:(j + 1) * SUBLANES])
        return r

    def score_chunk(i, qc, j):
        rows = pl.ds(pl.multiple_of(j * kc, kc), kc)
        s = lax.dot_general(k_ref[0, rows, :], qc, (((1,), (1,)), ((), ())),
                            preferred_element_type=F32)
        s_ref[i, rows, :] = s
        m_ref[i] = jnp.maximum(m_ref[i], fold8(s, jnp.maximum))

    ones_rows = jnp.ones((ATTN_ONES_ROWS, kc), BF16)

    def prob_chunk(i, j, m):
        rows = pl.ds(pl.multiple_of(j * kc, kc), kc)
        p = jnp.exp2(s_ref[i, rows, :] - m)
        v_aug = jnp.concatenate([vt_ref[j], ones_rows], axis=0)
        acc_ref[...] += jnp.dot(v_aug, p.astype(BF16), preferred_element_type=F32)

    def phase(read_buf, write_buf, q_next):
        if write_buf is not None:
            m_ref[write_buf] = jnp.full(m_ref.shape[1:], -jnp.inf, F32)
        if read_buf is not None:
            m = jnp.max(m_ref[read_buf], axis=0, keepdims=True)
            acc_ref[...] = jnp.zeros_like(acc_ref)

        def body(j, carry):
            if write_buf is not None:
                score_chunk(write_buf, q_next, j)
            if read_buf is not None:
                prob_chunk(read_buf, j, m)
            return carry

        lax.fori_loop(0, nc, body, 0, unroll=ATTN_UNROLL)
        if read_buf is None:
            return None
        return acc_ref[:DIFF_V_DIM, :] * (1.0 / acc_ref[DIFF_V_DIM:DIFF_V_DIM + 1, :])

    def finish_group(grp, out0, out1):
        o = out0 - lam_ref[0] * out1
        o = o * lax.rsqrt(jnp.mean(o * o, axis=0, keepdims=True) + SUBLN_EPS)
        o = o * (g_ref[...] * (1.0 - LAMBDA_INIT))
        o_ref[0, group_rows(grp), :] = o.T.astype(o_ref.dtype)

    phase(None, 0, stream_q(0, 0))

    def group_body(grp, carry):
        o0_ref[...] = phase(0, 1, stream_q(grp, 1))
        out1 = phase(1, 0, stream_q(grp + 1, 0))
        finish_group(grp, o0_ref[...], out1)
        return carry

    lax.fori_loop(0, n_groups - 1, group_body, 0)
    last = n_groups - 1
    o0_ref[...] = phase(0, 1, stream_q(last, 1))
    out1 = phase(1, None, None)
    finish_group(last, o0_ref[...], out1)


def _diff_attention(lam, q, k, vt, g_subln):
    B, S, _ = q.shape
    kc, w = ATTN_KV_TILE, ATTN_STREAM_W
    nc = S // kc
    assert vt.shape[2] == kc and S % w == 0 and S // w >= 2
    head = lambda b, h: (b, 0, h)
    return pl.pallas_call(
        functools.partial(_attn_kernel, kc=kc),
        grid=(B, DIFF_HEADS),
        in_specs=[
            pl.BlockSpec(memory_space=pltpu.SMEM),
            pl.BlockSpec((1, S, LANES), head),
            pl.BlockSpec((1, S, LANES), head),
            pl.BlockSpec((nc, DIFF_V_DIM, kc), lambda b, h: (b, h, 0)),
            pl.BlockSpec((DIFF_V_DIM, 1), lambda b, h: (0, 0)),
        ],
        out_specs=pl.BlockSpec((1, S, LANES), head),
        out_shape=jax.ShapeDtypeStruct((B, S, ATTN_WIDTH), BF16),
        scratch_shapes=[pltpu.VMEM((2, S, w), F32),
                        pltpu.VMEM((2, SUBLANES, w), F32),
                        pltpu.VMEM((DIFF_V_DIM + ATTN_ONES_ROWS, w), F32),
                        pltpu.VMEM((DIFF_V_DIM, w), F32)],
        compiler_params=_params("parallel", "parallel"),
        name="diff_attn",
    )(lam, q, k, vt, g_subln.reshape(DIFF_V_DIM, 1))


def _shortconv_kernel(hy_ref, prev_ref, next_ref, w_ref, b_ref, u_ref, x0_ref):
    i = pl.program_id(1)
    last = pl.num_programs(1) - 1
    x = hy_ref[0].astype(F32)
    ts = x.shape[0]
    prev_row = jnp.where(i == 0, 0.0, prev_ref[0, HALO_ROWS - 1:HALO_ROWS, :].astype(F32))
    next_row = jnp.where(i == last, 0.0, next_ref[0, 0:1, :].astype(F32))
    row = lax.broadcasted_iota(jnp.int32, (ts, 1), 0)
    xm = jnp.where(row == 0, prev_row, pltpu.roll(x, 1, axis=0))
    xp = jnp.where(row == ts - 1, next_row, pltpu.roll(x, ts - 1, axis=0))
    y = b_ref[...] + xm * w_ref[0:1, :] + x * w_ref[1:2, :] + xp * w_ref[2:3, :]
    C = HYENA_WIDTH
    u_ref[0] = (y[:, :C] * y[:, 2 * C:]).astype(u_ref.dtype)
    x0_ref[0] = y[:, C:2 * C].astype(x0_ref.dtype)


HALO_ROWS = 16


def _short_conv(hy, conv_w, conv_b):
    B, S, C3 = hy.shape
    ts = TOKEN_TILE
    nb = ts // HALO_ROWS
    return pl.pallas_call(
        _shortconv_kernel,
        grid=(B, S // ts),
        in_specs=[
            pl.BlockSpec((1, ts, C3), lambda b, i: (b, i, 0)),
            pl.BlockSpec((1, HALO_ROWS, C3), lambda b, i: (b, jnp.maximum(i * nb - 1, 0), 0)),
            pl.BlockSpec((1, HALO_ROWS, C3), lambda b, i: (b, jnp.minimum((i + 1) * nb, S // HALO_ROWS - 1), 0)),
            pl.BlockSpec((3, C3), lambda b, i: (0, 0)),
            pl.BlockSpec((1, C3), lambda b, i: (0, 0)),
        ],
        out_specs=[
            pl.BlockSpec((1, ts, HYENA_WIDTH), lambda b, i: (b, i, 0)),
            pl.BlockSpec((1, ts, HYENA_WIDTH), lambda b, i: (b, i, 0)),
        ],
        out_shape=[
            jax.ShapeDtypeStruct((B, S, HYENA_WIDTH), BF16),
            jax.ShapeDtypeStruct((B, S, HYENA_WIDTH), BF16),
        ],
        compiler_params=_params("parallel", "parallel"),
        name="short_conv",
    )(hy, hy, hy, conv_w, conv_b.reshape(1, C3))


FFT_N = 8192
FFT_N1 = 64
FFT_N2 = 128
FFT_CH_BLOCK = 32
FFT_GROUP = 4
FFT_UNROLL = 2


def _dft_constants():
    import numpy as np
    n1, n2, n = FFT_N1, FFT_N2, FFT_N
    k1 = np.arange(n1)[:, None]
    t1 = np.arange(n1)[None, :]
    f1 = np.exp(-2j * np.pi * k1 * t1 / n1)
    f1h = f1[:, : n1 // 2]
    w1c = np.block([[f1h.real, -f1h.imag], [f1h.imag, f1h.real]])
    w1r = np.concatenate([f1h.real, f1h.imag], axis=0)
    t2 = np.arange(n2)[:, None]
    k2 = np.arange(n2)[None, :]
    f2 = np.exp(-2j * np.pi * t2 * k2 / n2)
    w2a = np.concatenate([f2.real, f2.imag], axis=1)
    w2b = np.concatenate([-f2.imag, f2.real], axis=1)
    g2 = np.conj(f2)
    w3a = np.concatenate([g2.real, g2.imag], axis=1)
    w3b = np.concatenate([-g2.imag, g2.real], axis=1)
    h = np.conj(f1).T[: n1 // 2] / n
    w4 = np.block([[h.real, -h.imag], [h.imag, h.real]])
    tw = np.exp(-2j * np.pi * np.arange(n1)[:, None] * np.arange(n2)[None, :] / n)
    bf = lambda a: jnp.asarray(a, dtype=F32).astype(BF16)
    return dict(w1c=bf(w1c), w1r=bf(w1r), w2a=bf(w2a), w2b=bf(w2b), w3a=bf(w3a), w3b=bf(w3b),
                w4=bf(w4), tc=jnp.asarray(np.tile(tw.real, (1, FFT_GROUP)), F32),
                ts=jnp.asarray(np.tile(tw.imag, (1, FFT_GROUP)), F32))


def _dft_forward(x_ref, w1_ref, tc_ref, ts_ref, w2a_ref, w2b_ref, ar_ref, ai_ref):
    ng = x_ref.shape[0] // FFT_GROUP
    n1, n2 = FFT_N1, FFT_N2
    w1 = w1_ref[...]
    tc, ts = tc_ref[...], ts_ref[...]

    def body(g, carry):
        xg = jnp.concatenate([x_ref[g * FFT_GROUP + cl] for cl in range(FFT_GROUP)], axis=1)
        a = jnp.dot(w1, xg, preferred_element_type=F32)
        ar, ai = a[:n1], a[n1:]
        tr = (ar * tc - ai * ts).astype(BF16)
        ti = (ar * ts + ai * tc).astype(BF16)
        for cl in range(FFT_GROUP):
            rows = pl.ds(pl.multiple_of((g * FFT_GROUP + cl) * n1, n1), n1)
            ar_ref[rows, :] = tr[:, cl * n2:(cl + 1) * n2]
            ai_ref[rows, :] = ti[:, cl * n2:(cl + 1) * n2]
        return carry

    lax.fori_loop(0, ng, body, 0, unroll=FFT_UNROLL)
    return (jnp.dot(ar_ref[...], w2a_ref[...], preferred_element_type=F32)
            + jnp.dot(ai_ref[...], w2b_ref[...], preferred_element_type=F32))


def _spectrum_kernel(hf_ref, hb_ref, w1_ref, tc_ref, ts_ref, w2a_ref, w2b_ref, o_ref, ar_ref, ai_ref):
    n2 = FFT_N2
    f = _dft_forward(hf_ref, w1_ref, tc_ref, ts_ref, w2a_ref, w2b_ref, ar_ref, ai_ref)
    o_ref[...] = f.reshape(o_ref.shape)
    b = _dft_forward(hb_ref, w1_ref, tc_ref, ts_ref, w2a_ref, w2b_ref, ar_ref, ai_ref)
    b = b.reshape(o_ref.shape)
    o_ref[:, :, :n2] = o_ref[:, :, :n2] + b[:, :, :n2]
    o_ref[:, :, n2:] = o_ref[:, :, n2:] - b[:, :, n2:]


def _fftconv_kernel(x_ref, kf_ref, w1_ref, tc_ref, ts_ref, w2a_ref, w2b_ref, w3a_ref, w3b_ref,
                    w4_ref, y_ref, ar_ref, ai_ref, c_ref):
    cb = x_ref.shape[0]
    ng = cb // FFT_GROUP
    n1, n2 = FFT_N1, FFT_N2
    b = _dft_forward(x_ref, w1_ref, tc_ref, ts_ref, w2a_ref, w2b_ref, ar_ref, ai_ref)
    kf = kf_ref[...].reshape(cb * n1, 2 * n2)
    br, bi = b[:, :n2], b[:, n2:]
    kr, ki = kf[:, :n2], kf[:, n2:]
    ar_ref[...] = (br * kr - bi * ki).astype(BF16)
    ai_ref[...] = (br * ki + bi * kr).astype(BF16)
    c_ref[...] = (jnp.dot(ar_ref[...], w3a_ref[...], preferred_element_type=F32)
                  + jnp.dot(ai_ref[...], w3b_ref[...], preferred_element_type=F32))
    w4 = w4_ref[...]
    tc, ts = tc_ref[...], ts_ref[...]

    def body(g, carry):
        rows = [pl.ds(pl.multiple_of((g * FFT_GROUP + cl) * n1, n1), n1) for cl in range(FFT_GROUP)]
        cr = jnp.concatenate([c_ref[r, :n2] for r in rows], axis=1)
        ci = jnp.concatenate([c_ref[r, n2:] for r in rows], axis=1)
        dr = (cr * tc + ci * ts).astype(BF16)
        di = (ci * tc - cr * ts).astype(BF16)
        yg = (jnp.dot(w4[:, :n1], dr, preferred_element_type=F32)
              + jnp.dot(w4[:, n1:], di, preferred_element_type=F32)).astype(y_ref.dtype)
        for cl in range(FFT_GROUP):
            y_ref[g * FFT_GROUP + cl] = yg[:, cl * n2:(cl + 1) * n2]
        return carry

    lax.fori_loop(0, ng, body, 0, unroll=FFT_UNROLL)


def _const_spec(a):
    nd = a.ndim
    return pl.BlockSpec(a.shape, lambda *_: (0,) * nd)


def _filter_spectrum(hf_slabs, hb_slabs, cst):
    C = hf_slabs.shape[0]
    cb = FFT_CH_BLOCK
    consts = [cst["w1r"], cst["tc"], cst["ts"], cst["w2a"], cst["w2b"]]
    slab = pl.BlockSpec((cb, FFT_N1 // 2, FFT_N2), lambda i: (i, 0, 0))
    return pl.pallas_call(
        _spectrum_kernel,
        grid=(C // cb,),
        in_specs=[slab, slab] + [_const_spec(a) for a in consts],
        out_specs=pl.BlockSpec((cb, FFT_N1, 2 * FFT_N2), lambda i: (i, 0, 0)),
        out_shape=jax.ShapeDtypeStruct((C, FFT_N1, 2 * FFT_N2), F32),
        scratch_shapes=[pltpu.VMEM((cb * FFT_N1, FFT_N2), BF16), pltpu.VMEM((cb * FFT_N1, FFT_N2), BF16)],
        compiler_params=_params("parallel"),
        name="filter_spectrum",
    )(hf_slabs, hb_slabs, *consts)


FILTER_TIME_TILE = 512


def _filter_kernel(zt_ref, w1t_ref, b1_ref, wit_ref, bi_ref, fr_ref, wot_ref, dec_ref, hf_ref, hb_ref):
    hp = lax.Precision.HIGHEST
    h = jnp.sin(fr_ref[0] * (jnp.dot(w1t_ref[...], zt_ref[...], precision=hp,
                                     preferred_element_type=F32) + b1_ref[...]))
    for j in range(FILTER_INNER):
        h = jnp.sin(fr_ref[j + 1] * (jnp.dot(wit_ref[j], h, precision=hp,
                                             preferred_element_type=F32) + bi_ref[j]))
    o = jnp.dot(wot_ref[...], h, precision=hp, preferred_element_type=F32)
    dec = dec_ref[...]
    hf_ref[...] = o[:HYENA_WIDTH] * dec
    hb_ref[...] = o[HYENA_WIDTH:] * dec


def _hyena_filters(seq, w1, b1, w_inner, b_inner, freq, w_out):
    C = HYENA_WIDTH
    order = w1.shape[1]
    pos = jnp.arange(seq, dtype=F32)
    t = pos / (seq - 1)
    bands = (FILTER_EMB - 1) // 2
    f = jnp.linspace(1e-4, bands - 1, bands, dtype=F32)
    fw = ((2.0 * math.pi / seq) * pos)[:, None] * f[None, :]
    z = jnp.concatenate([t[:, None], jnp.cos(fw), -jnp.sin(fw)], axis=-1)
    zt = jnp.zeros((LANES, seq), F32).at[:FILTER_EMB].set(z.T)
    w1t = jnp.zeros((order, LANES), F32).at[:, :FILTER_EMB].set(w1.T)
    max_decay = math.log(DECAY_TARGET) / FAST_DECAY_PCT
    min_decay = math.log(DECAY_TARGET) / SLOW_DECAY_PCT
    deltas = jnp.abs(jnp.linspace(min_decay, max_decay, C, dtype=F32))
    dec_t = jnp.exp(-deltas[:, None] * t[None, :])
    tt = FILTER_TIME_TILE
    lane_blk = lambda r: pl.BlockSpec((r, tt), lambda i: (0, i))
    args = [zt, w1t, b1.reshape(order, 1), jnp.swapaxes(w_inner, 1, 2),
            b_inner.reshape(FILTER_INNER, order, 1), freq.reshape(FILTER_INNER + 1, order, 1),
            w_out.T, dec_t]
    return pl.pallas_call(
        _filter_kernel,
        grid=(seq // tt,),
        in_specs=[lane_blk(LANES)] + [_const_spec(a) for a in args[1:7]] + [lane_blk(C)],
        out_specs=[lane_blk(C), lane_blk(C)],
        out_shape=[jax.ShapeDtypeStruct((C, seq), F32), jax.ShapeDtypeStruct((C, seq), F32)],
        compiler_params=_params("parallel"),
        name="hyena_filters",
    )(*args)


def _fft_conv(x_slabs, kf, cst):
    P, C = x_slabs.shape[:2]
    cb = FFT_CH_BLOCK
    consts = [cst["w1c"], cst["tc"], cst["ts"], cst["w2a"], cst["w2b"], cst["w3a"], cst["w3b"], cst["w4"]]
    return pl.pallas_call(
        _fftconv_kernel,
        grid=(C // cb, P),
        in_specs=[pl.BlockSpec((None, cb, FFT_N1, FFT_N2), lambda i, p: (p, i, 0, 0)),
                  pl.BlockSpec((cb, FFT_N1, 2 * FFT_N2), lambda i, p: (i, 0, 0))]
                 + [_const_spec(a) for a in consts],
        out_specs=pl.BlockSpec((None, cb, FFT_N1, FFT_N2), lambda i, p: (p, i, 0, 0)),
        out_shape=jax.ShapeDtypeStruct((P, C, FFT_N1, FFT_N2), BF16),
        scratch_shapes=[pltpu.VMEM((cb * FFT_N1, FFT_N2), BF16), pltpu.VMEM((cb * FFT_N1, FFT_N2), BF16),
                        pltpu.VMEM((cb * FFT_N1, 2 * FFT_N2), F32)],
        compiler_params=_params("parallel", "arbitrary"),
        name="fft_conv",
    )(x_slabs, kf, *consts)


def _split_bf16(x):
    hi = x.astype(BF16)
    lo = (x - hi.astype(F32)).astype(BF16)
    return hi, lo


def _outproj_kernel(attn_ref, yc_ref, u_ref, x0_ref, d_ref, gn_ref, grp_ref, wo_ref, x_ref,
                    gf_ref, rwh_ref, rwl_ref, rb_ref, tri_ref, h_ref, hn_ref, rt_ref, wt_ref,
                    cnt_out_ref, cnt_ref):
    @pl.when(pl.program_id(0) == 0)
    def _():
        cnt_ref[...] = jnp.zeros_like(cnt_ref)

    grp = grp_ref[...]
    gsz = HYENA_WIDTH // HYENA_GROUPS

    def token_tile(cnt_col):
        rows = slice(None)
        z = ((yc_ref[rows, :].astype(F32) + u_ref[rows, :].astype(F32) * d_ref[...])
             * x0_ref[rows, :].astype(F32))
        zh, zl = _split_bf16(z * z)
        ssq = (jnp.dot(zh, grp, preferred_element_type=F32)
               + jnp.dot(zl, grp, preferred_element_type=F32))
        hy_out = (z * lax.rsqrt(ssq * (1.0 / gsz) + EPS) * gn_ref[...]).astype(BF16)
        mix = (jnp.dot(attn_ref[rows, :], wo_ref[:ATTN_WIDTH, :], preferred_element_type=F32)
               + jnp.dot(hy_out, wo_ref[ATTN_WIDTH:, :], preferred_element_type=F32))
        h = x_ref[rows, :] + mix
        h_ref[rows, :] = h
        hn = _rms(h, gf_ref[...], EPS)
        hn_ref[rows, :] = hn.astype(BF16)
        nh, nl = _split_bf16(hn)
        nt = (((1,), (1,)), ((), ()))
        logits = (lax.dot_general(rwh_ref[...], nh, nt, preferred_element_type=F32)
                  + lax.dot_general(rwh_ref[...], nl, nt, preferred_element_type=F32)
                  + lax.dot_general(rwl_ref[...], nh, nt, preferred_element_type=F32)) + rb_ref[...]

        erow = lax.broadcasted_iota(jnp.int32, logits.shape, 0)
        work = logits
        top_val, top_idx, top_hot = [], [], []
        for _ in range(TOP_K):
            m = jnp.max(work, axis=0, keepdims=True)
            idx = jnp.min(jnp.where(work == m, erow, N_EXPERTS), axis=0, keepdims=True)
            hot = erow == idx
            top_val.append(m)
            top_idx.append(idx)
            top_hot.append(hot)
            work = jnp.where(hot, -jnp.inf, work)
        ex = [jnp.exp(v - top_val[0]) for v in top_val]
        inv_den = 1.0 / (ex[0] + ex[1] + ex[2] + ex[3])
        sel = jnp.zeros(logits.shape, F32)
        for hot in top_hot:
            sel = sel + hot.astype(F32)
        before = jnp.dot(sel.astype(BF16), tri_ref[...], preferred_element_type=F32) + cnt_col
        tm = logits.shape[1]
        row_rt = lax.broadcasted_iota(jnp.int32, (ROUTE_ROWS, tm), 0)
        row_w = lax.broadcasted_iota(jnp.int32, (LANES, tm), 0)
        packed = jnp.zeros((ROUTE_ROWS, tm), F32)
        w_rows = jnp.zeros((LANES, tm), F32)
        for r, hot in enumerate(top_hot):
            rank_r = jnp.sum(jnp.where(hot, before, 0.0), axis=0, keepdims=True)
            weight_r = ex[r] * inv_den
            packed = jnp.where(row_rt == r, top_idx[r].astype(F32), packed)
            packed = jnp.where(row_rt == TOP_K + r, rank_r, packed)
            packed = jnp.where(row_rt == 2 * TOP_K + r, weight_r, packed)
            w_rows = jnp.where(row_w == r, weight_r, w_rows)
        rt_ref[...] = packed
        wt_ref[...] = w_rows.T
        return cnt_col + jnp.sum(sel, axis=1, keepdims=True)

    cnt_col = token_tile(cnt_ref[:, 0:1])
    cnt_ref[...] = jnp.broadcast_to(cnt_col, cnt_ref.shape)
    cnt_out_ref[...] = cnt_ref[...]


def _out_proj(attn2, yc2, u2, x02, hyena_d, hyena_gn, w_out_bf, x2, g_ffn, router_w, router_b):
    T, D = x2.shape
    tm = TOKEN_TILE
    C = HYENA_WIDTH
    gid = jnp.arange(C) // (C // HYENA_GROUPS)
    grp = (gid[:, None] == gid[None, :]).astype(BF16)
    rwh, rwl = _split_bf16(router_w.T)
    rb = router_b.reshape(N_EXPERTS, 1)
    tri = (jnp.arange(tm)[:, None] < jnp.arange(tm)[None, :]).astype(BF16)
    row = lambda i: (i, 0)
    const = lambda i: (0, 0)
    return pl.pallas_call(
        _outproj_kernel,
        grid=(T // tm,),
        in_specs=[
            pl.BlockSpec((tm, ATTN_WIDTH), row),
            pl.BlockSpec((tm, C), row),
            pl.BlockSpec((tm, C), row),
            pl.BlockSpec((tm, C), row),
            pl.BlockSpec((1, C), const),
            pl.BlockSpec((1, C), const),
            pl.BlockSpec((C, C), const),
            pl.BlockSpec((D, D), const),
            pl.BlockSpec((tm, D), row),
            pl.BlockSpec((1, D), const),
            pl.BlockSpec((N_EXPERTS, D), const),
            pl.BlockSpec((N_EXPERTS, D), const),
            pl.BlockSpec((N_EXPERTS, 1), const),
            pl.BlockSpec((tm, tm), const),
        ],
        out_specs=[
            pl.BlockSpec((tm, D), row),
            pl.BlockSpec((tm, D), row),
            pl.BlockSpec((ROUTE_ROWS, tm), lambda i: (0, i)),
            pl.BlockSpec((tm, LANES), row),
            pl.BlockSpec((N_EXPERTS, LANES), const),
        ],
        out_shape=[
            jax.ShapeDtypeStruct((T, D), F32),
            jax.ShapeDtypeStruct((T, D), BF16),
            jax.ShapeDtypeStruct((ROUTE_ROWS, T), F32),
            jax.ShapeDtypeStruct((T, LANES), F32),
            jax.ShapeDtypeStruct((N_EXPERTS, LANES), F32),
        ],
        scratch_shapes=[pltpu.VMEM((N_EXPERTS, LANES), F32)],
        compiler_params=_params("arbitrary"),
        name="out_proj",
    )(attn2, yc2, u2, x02, hyena_d.reshape(1, C), hyena_gn.reshape(1, C), grp, w_out_bf, x2,
      g_ffn.reshape(1, D), rwh, rwl, rb, tri)


def _expert_kernel(be_ref, nused_ref, x_ref, wg_ref, bg_ref, wu_ref, bu_ref, wd_ref, bd_ref, *rest):
    y_ref, wg_bf, wu_bf, wd_bf = rest[-4:]
    i = pl.program_id(0)
    used = i < nused_ref[0]
    new_expert = jnp.logical_or(i == 0, be_ref[i] != be_ref[jnp.maximum(i - 1, 0)])

    @pl.when(jnp.logical_and(used, new_expert))
    def _():
        wg_bf[...] = wg_ref[0].astype(BF16)
        wu_bf[...] = wu_ref[0].astype(BF16)
        wd_bf[...] = wd_ref[0].astype(BF16)

    @pl.when(used)
    def _():
        x = x_ref[...]
        g = jnp.minimum(jnp.dot(x, wg_bf[...], preferred_element_type=F32) + bg_ref[0], SWIGLU_LIMIT)
        u = jnp.clip(jnp.dot(x, wu_bf[...], preferred_element_type=F32) + bu_ref[0],
                     -SWIGLU_LIMIT, SWIGLU_LIMIT)
        a = (u + 1.0) * (g * jax.nn.sigmoid(SWIGLU_ALPHA * g))
        y = jnp.dot(a.astype(BF16), wd_bf[...], preferred_element_type=F32) + bd_ref[0]
        y_ref[...] = y.astype(y_ref.dtype)

    @pl.when(jnp.logical_not(used))
    def _():
        y_ref[...] = jnp.zeros_like(y_ref)


def _experts(block_e, n_used, x_parts, wg, bg, wu, bu, wd, bd):
    D = x_parts[0].shape[1]
    tm = EXPERT_TILE
    P = sum(xs.shape[0] for xs in x_parts)
    E, _, FF = wg.shape
    wmap = lambda i, be, nu: (be[i], 0, 0)
    y = None
    first = 0
    for part, xs in enumerate(x_parts):
        nb = xs.shape[0] // tm
        in_specs = [
            pl.BlockSpec((tm, D), lambda i, be, nu: (i, 0)),
            pl.BlockSpec((1, D, FF), wmap),
            pl.BlockSpec((1, 1, FF), wmap),
            pl.BlockSpec((1, D, FF), wmap),
            pl.BlockSpec((1, 1, FF), wmap),
            pl.BlockSpec((1, FF, D), wmap),
            pl.BlockSpec((1, 1, D), wmap),
        ]
        args = [block_e[first:first + nb], jnp.clip(n_used - first, 0, nb), xs,
                wg, bg.reshape(E, 1, FF), wu, bu.reshape(E, 1, FF), wd, bd.reshape(E, 1, D)]
        aliases = {}
        if y is not None:
            in_specs.append(pl.BlockSpec(memory_space=pl.ANY))
            args.append(y)
            aliases = {len(args) - 1: 0}
        grid_spec = pltpu.PrefetchScalarGridSpec(
            num_scalar_prefetch=2,
            grid=(nb,),
            in_specs=in_specs,
            out_specs=pl.BlockSpec((tm, D), lambda i, be, nu, first=first: (i + first, 0)),
            scratch_shapes=[pltpu.VMEM((D, FF), BF16), pltpu.VMEM((D, FF), BF16),
                            pltpu.VMEM((FF, D), BF16)],
        )
        y = pl.pallas_call(
            _expert_kernel,
            grid_spec=grid_spec,
            out_shape=jax.ShapeDtypeStruct((P, D), BF16),
            input_output_aliases=aliases,
            compiler_params=_params("arbitrary"),
            name=f"moe_experts_{part}",
        )(*args)
        first += nb
    return y


def _final_kernel(h_ref, yg_ref, rt_ref, p_ref, wp_ref, gp_ref, wg_ref, bg_ref, gfin_ref, *rest):
    o_ref = rest[-1]
    h = h_ref[...]
    for r in range(TOP_K):
        h = h + yg_ref[r].astype(F32) * rt_ref[:, r:r + 1]
    e = _rms(jnp.dot(p_ref[...].astype(BF16), wp_ref[...], preferred_element_type=F32),
             gp_ref[...], EPS)
    gate = jax.nn.sigmoid(jnp.dot(h.astype(BF16), wg_ref[...], preferred_element_type=F32)
                          + bg_ref[...])
    h = h + gate * e
    o_ref[...] = _rms(h, gfin_ref[...], EPS)


def _final(h1, yg_parts, route, p2, w_ple_bf, g_ple, w_gate_bf, b_gate, g_final):
    T, D = h1.shape
    tm = TOKEN_TILE
    PD = p2.shape[1]
    nb = T // tm // len(yg_parts)
    const = lambda i: (0, 0)
    out = None
    for part, yg in enumerate(yg_parts):
        row = lambda i, part=part: (i + part * nb, 0)
        in_specs = [
            pl.BlockSpec((tm, D), row),
            pl.BlockSpec((TOP_K, tm, D), lambda i: (0, i, 0)),
            pl.BlockSpec((tm, LANES), row),
            pl.BlockSpec((tm, PD), row),
            pl.BlockSpec((PD, D), const),
            pl.BlockSpec((1, D), const),
            pl.BlockSpec((D, D), const),
            pl.BlockSpec((1, D), const),
            pl.BlockSpec((1, D), const),
        ]
        args = [h1, yg, route, p2, w_ple_bf, g_ple.reshape(1, D), w_gate_bf, b_gate.reshape(1, D),
                g_final.reshape(1, D)]
        aliases = {}
        if out is not None:
            in_specs.append(pl.BlockSpec(memory_space=pl.ANY))
            args.append(out)
            aliases = {len(args) - 1: 0}
        out = pl.pallas_call(
            _final_kernel,
            grid=(nb,),
            in_specs=in_specs,
            out_specs=pl.BlockSpec((tm, D), row),
            out_shape=jax.ShapeDtypeStruct((T, D), F32),
            input_output_aliases=aliases,
            compiler_params=_params("parallel"),
            name=f"final_{part}",
        )(*args)
    return out


def _rope_tables(seq):
    d = DIFF_HEAD_DIM
    pos = jnp.arange(seq, dtype=F32)
    inv = ROPE_THETA ** (-jnp.arange(0, d, 2, dtype=F32) / d)
    ang = pos[:, None] * inv[None, :]
    cos, sin = jnp.cos(ang), jnp.sin(ang)
    z = jnp.zeros_like(sin)
    cos_t = jnp.tile(jnp.concatenate([cos, cos], -1), (1, LANES // d))
    s1_t = jnp.tile(jnp.concatenate([-sin, z], -1), (1, LANES // d))
    s2_t = jnp.tile(jnp.concatenate([z, sin], -1), (1, LANES // d))
    return cos_t, s1_t, s2_t


def _long_conv(u, hf_t, hb_t):
    B, L, C = u.shape
    assert 2 * L == FFT_N and B % 2 == 0
    P, R = B // 2, FFT_N1 // 2
    cst = _dft_constants()
    kf = _filter_spectrum(hf_t.reshape(C, R, FFT_N2).astype(BF16),
                          hb_t.reshape(C, R, FFT_N2).astype(BF16), cst)
    x_slabs = (u.reshape(2, P, R, FFT_N2, C).transpose(1, 4, 0, 2, 3)
               .reshape(P, C, FFT_N1, FFT_N2).astype(BF16))
    y = _fft_conv(x_slabs, kf, cst)
    return y.reshape(P, C, 2, R, FFT_N2).transpose(2, 0, 3, 4, 1).reshape(B, L, C)


def _dispatch_indices(route_t, cnt, T):
    tm = EXPERT_TILE
    A = T * TOP_K
    top_e = route_t[:TOP_K].astype(jnp.int32)
    rank = route_t[TOP_K:2 * TOP_K].astype(jnp.int32)
    counts = cnt[:, 0].astype(jnp.int32)
    padded = (counts + tm - 1) // tm * tm
    pad_end = jnp.cumsum(padded)
    pad_start = pad_end - padded
    start = jnp.cumsum(counts) - counts
    experts = jnp.arange(N_EXPERTS, dtype=jnp.int32)[:, None, None]
    pos = rank + jnp.sum(jnp.where(top_e[None] == experts, pad_start[:, None, None], 0), axis=0)
    n_blocks = -(-A // tm) + N_EXPERTS
    P = n_blocks * tm
    block_first = jnp.arange(n_blocks, dtype=jnp.int32) * tm
    block_e = jnp.minimum(jnp.sum(pad_end[None, :] <= block_first[:, None], axis=1),
                          N_EXPERTS - 1).astype(jnp.int32)
    n_used = (pad_end[-1] // tm).astype(jnp.int32).reshape(1)
    tok = jnp.broadcast_to(jnp.arange(T, dtype=jnp.int32)[None, :], (TOP_K, T))
    _, sorted_tok = lax.sort_key_val(pos.reshape(A), tok.reshape(A))
    in_blk = jnp.arange(tm, dtype=jnp.int32)[None, :]
    r = (block_first - pad_start[block_e])[:, None] + in_blk
    compact = jnp.clip(start[block_e][:, None] + r, 0, A - 1)
    filler = (block_first[:, None] + in_blk) % T
    slot_tok = jnp.where(r < counts[block_e][:, None], sorted_tok[compact], filler).reshape(P)
    return pos, slot_tok, block_e, n_used


def kernel(x, p, g_mix, w_in, hyena_conv_w, hyena_conv_b, flt_w1, flt_b1, flt_w_inner, flt_b_inner, flt_freq, flt_w_out, hyena_d, hyena_gn, lambda_q1, lambda_k1, lambda_q2, lambda_k2, attn_subln, w_out, g_ffn, router_w, router_b, w_gate, b_gate, w_up, b_up, w_down, b_down, w_ple, g_ple, w_ple_gate, b_ple_gate, g_final):
    B, S, D = x.shape
    T = B * S
    i = 0
    x2 = x.reshape(T, D)

    cos_t, s1_t, s2_t = _rope_tables(S)
    q, k, vt, hy = _in_proj(x2, g_mix[i], w_in[i].astype(BF16), cos_t, s1_t, s2_t, S)

    lam = (jnp.exp(jnp.sum(lambda_q1[i] * lambda_k1[i])) - jnp.exp(jnp.sum(lambda_q2[i] * lambda_k2[i]))
           + LAMBDA_INIT).reshape(1).astype(F32)
    attn = _diff_attention(lam, q.reshape(B, S, -1), k.reshape(B, S, -1), vt, attn_subln[i])

    u, hx0 = _short_conv(hy.reshape(B, S, -1), hyena_conv_w[i], hyena_conv_b[i])
    h_fwd, h_bwd = _hyena_filters(S, flt_w1[i], flt_b1[i], flt_w_inner[i], flt_b_inner[i],
                                  flt_freq[i], flt_w_out[i])
    yc = _long_conv(u, h_fwd, h_bwd)

    h1, hn, route_t, top_w, cnt = _out_proj(
        attn.reshape(T, -1), yc.reshape(T, -1), u.reshape(T, -1), hx0.reshape(T, -1), hyena_d[i],
        hyena_gn[i], w_out[i].astype(BF16), x2, g_ffn[i], router_w[i], router_b[i])

    pos, slot_tok, block_e, n_used = _dispatch_indices(route_t, cnt, T)
    n_parts = COMBINE_PARTS
    head_rows = slot_tok.shape[0] // EXPERT_TILE // DISPATCH_HEAD_DIV * EXPERT_TILE
    x_parts = [hn[slot_tok[:head_rows]], hn[slot_tok[head_rows:]]]
    y = _experts(block_e, n_used, x_parts, w_gate[i], b_gate[i], w_up[i], b_up[i],
                 w_down[i], b_down[i])
    pos_parts = jnp.split(pos, n_parts, axis=1)
    yg_parts = [y[pp.reshape(-1)].reshape(TOP_K, T // n_parts, D) for pp in pos_parts]

    out = _final(h1, yg_parts, top_w, p[i].reshape(T, -1), w_ple[i].astype(BF16), g_ple[i],
                 w_ple_gate[i].astype(BF16), b_ple_gate[i], g_final)
    return out.reshape(B, S, D)
```

```python
import functools
import math

import jax
import jax.numpy as jnp
from jax import lax
from jax.experimental import pallas as pl
from jax.experimental.pallas import tpu as pltpu

F32 = jnp.float32
BF16 = jnp.bfloat16

D_MODEL = 1024
ATTN_WIDTH = 512
HYENA_WIDTH = 512
DIFF_HEADS = 4
DIFF_HEAD_DIM = 64
DIFF_V_DIM = 128
HYENA_GROUPS = 8
FILTER_EMB = 33
FILTER_INNER = 2
FAST_DECAY_PCT = 0.3
SLOW_DECAY_PCT = 1.5
DECAY_TARGET = 1e-2
ROPE_THETA = 10000.0
N_EXPERTS = 32
TOP_K = 4
SWIGLU_ALPHA = 1.702
SWIGLU_LIMIT = 7.0
EPS = 1e-6
SUBLN_EPS = 1e-5
LAMBDA_INIT = 0.8 - 0.6 * math.exp(-0.3 * 0)

V7X_VMEM_LIMIT_BYTES = 56 * 1024 * 1024
LANES = 128
SUBLANES = 8

TOKEN_TILE = 512
ATTN_STREAM_W = 1024
ATTN_ONES_ROWS = 16
ATTN_UNROLL = 4
ATTN_KV_TILE = 512
EXPERT_TILE = 512
ROUTE_ROWS = 16
DISPATCH_HEAD_DIV = 4
COMBINE_PARTS = 2


def _params(*sem):
    return pltpu.CompilerParams(dimension_semantics=sem, vmem_limit_bytes=V7X_VMEM_LIMIT_BYTES)


def _rms(x, g, eps):
    return x * lax.rsqrt(jnp.mean(x * x, axis=-1, keepdims=True) + eps) * g


def _inproj_kernel(x_ref, g_ref, w_ref, wvt_ref, c_ref, s1_ref, s2_ref, q_ref, k_ref, vt_ref, hy_ref):
    a = _rms(x_ref[...], g_ref[...], EPS).astype(BF16)
    cos, s1, s2 = c_ref[...], s1_ref[...], s2_ref[...]

    half = DIFF_HEAD_DIM // 2

    def rope(t):
        return (t * cos + pltpu.roll(t, LANES - half, axis=1) * s1
                + pltpu.roll(t, half, axis=1) * s2)

    q_scale = (DIFF_HEAD_DIM ** -0.5) * math.log2(math.e)
    qk = jnp.dot(a, w_ref[:, : 2 * ATTN_WIDTH], preferred_element_type=F32)
    for j in range(ATTN_WIDTH // LANES):
        sl = slice(j * LANES, (j + 1) * LANES)
        q_ref[:, sl] = (rope(qk[:, sl]) * q_scale).astype(BF16)
        k_ref[:, sl] = rope(qk[:, ATTN_WIDTH + j * LANES: ATTN_WIDTH + (j + 1) * LANES]).astype(BF16)
    vt_ref[0] = lax.dot_general(wvt_ref[...], a, (((1,), (1,)), ((), ())),
                                preferred_element_type=F32).astype(BF16)
    hy_ref[...] = jnp.dot(a, w_ref[:, 3 * ATTN_WIDTH:], preferred_element_type=F32).astype(hy_ref.dtype)


def _in_proj(x2, g_mix, w_in_bf, cos_t, s1_t, s2_t, seq):
    T, D = x2.shape
    tm = TOKEN_TILE
    nseq = seq // tm
    wvt = w_in_bf[:, 2 * ATTN_WIDTH: 3 * ATTN_WIDTH].T
    row = lambda i: (i, 0)
    const = lambda i: (0, 0)
    pos = lambda i: (i % nseq, 0)
    return pl.pallas_call(
        _inproj_kernel,
        grid=(T // tm,),
        in_specs=[
            pl.BlockSpec((tm, D), row),
            pl.BlockSpec((1, D), const),
            pl.BlockSpec(w_in_bf.shape, const),
            pl.BlockSpec(wvt.shape, const),
            pl.BlockSpec((tm, LANES), pos),
            pl.BlockSpec((tm, LANES), pos),
            pl.BlockSpec((tm, LANES), pos),
        ],
        out_specs=[
            pl.BlockSpec((tm, ATTN_WIDTH), row),
            pl.BlockSpec((tm, ATTN_WIDTH), row),
            pl.BlockSpec((1, ATTN_WIDTH, tm), lambda i: (i, 0, 0)),
            pl.BlockSpec((tm, 3 * HYENA_WIDTH), row),
        ],
        out_shape=[
            jax.ShapeDtypeStruct((T, ATTN_WIDTH), BF16),
            jax.ShapeDtypeStruct((T, ATTN_WIDTH), BF16),
            jax.ShapeDtypeStruct((T // tm, ATTN_WIDTH, tm), BF16),
            jax.ShapeDtypeStruct((T, 3 * HYENA_WIDTH), BF16),
        ],
        compiler_params=_params("parallel"),
        name="in_proj",
    )(x2, g_mix.reshape(1, D), w_in_bf, wvt, cos_t, s1_t, s2_t)


def _attn_kernel(lam_ref, q_ref, k_ref, vt_ref, g_ref, o_ref, s_ref, m_ref, acc_ref, o0_ref, *, kc):
    w = ATTN_STREAM_W
    nc = k_ref.shape[1] // kc
    n_groups = q_ref.shape[1] // w

    def group_rows(grp):
        return pl.ds(pl.multiple_of(grp * w, w), w)

    def stream_q(grp, c):
        q = q_ref[0, group_rows(grp), :]
        lane = lax.broadcasted_iota(jnp.int32, q.shape, 1)
        keep = (lane < DIFF_HEAD_DIM) if c == 0 else (lane >= DIFF_HEAD_DIM)
        return jnp.where(keep, q, jnp.zeros_like(q))

    def fold8(t, op):
        r = t[0:SUBLANES]
        for j in range(1, kc // SUBLANES):
            r = op(r, t[j * SUBLANES:(j + 1) * SUBLANES])
        return r

    def score_chunk(i, qc, j):
        rows = pl.ds(pl.multiple_of(j * kc, kc), kc)
        s = lax.dot_general(k_ref[0, rows, :], qc, (((1,), (1,)), ((), ())),
                            preferred_element_type=F32)
        s_ref[i, rows, :] = s
        m_ref[i] = jnp.maximum(m_ref[i], fold8(s, jnp.maximum))

    ones_rows = jnp.ones((ATTN_ONES_ROWS, kc), BF16)

    def prob_chunk(i, j, m):
        rows = pl.ds(pl.multiple_of(j * kc, kc), kc)
        p = jnp.exp2(s_ref[i, rows, :] - m)
        v_aug = jnp.concatenate([vt_ref[j], ones_rows], axis=0)
        acc_ref[...] += jnp.dot(v_aug, p.astype(BF16), preferred_element_type=F32)

    def phase(read_buf, write_buf, q_next):
        if write_buf is not None:
            m_ref[write_buf] = jnp.full(m_ref.shape[1:], -jnp.inf, F32)
        if read_buf is not None:
            m = jnp.max(m_ref[read_buf], axis=0, keepdims=True)
            acc_ref[...] = jnp.zeros_like(acc_ref)

        def body(j, carry):
            if write_buf is not None:
                score_chunk(write_buf, q_next, j)
            if read_buf is not None:
                prob_chunk(read_buf, j, m)
            return carry

        lax.fori_loop(0, nc, body, 0, unroll=ATTN_UNROLL)
        if read_buf is None:
            return None
        return acc_ref[:DIFF_V_DIM, :] * (1.0 / acc_ref[DIFF_V_DIM:DIFF_V_DIM + 1, :])

    def finish_group(grp, out0, out1):
        o = out0 - lam_ref[0] * out1
        o = o * lax.rsqrt(jnp.mean(o * o, axis=0, keepdims=True) + SUBLN_EPS)
        o = o * (g_ref[...] * (1.0 - LAMBDA_INIT))
        o_ref[0, group_rows(grp), :] = o.T.astype(o_ref.dtype)

    phase(None, 0, stream_q(0, 0))

    def group_body(grp, carry):
        o0_ref[...] = phase(0, 1, stream_q(grp, 1))
        out1 = phase(1, 0, stream_q(grp + 1, 0))
        finish_group(grp, o0_ref[...], out1)
        return carry

    lax.fori_loop(0, n_groups - 1, group_body, 0)
    last = n_groups - 1
    o0_ref[...] = phase(0, 1, stream_q(last, 1))
    out1 = phase(1, None, None)
    finish_group(last, o0_ref[...], out1)


def _diff_attention(lam, q, k, vt, g_subln):
    B, S, _ = q.shape
    kc, w = ATTN_KV_TILE, ATTN_STREAM_W
    nc = S // kc
    assert vt.shape[2] == kc and S % w == 0 and S // w >= 2
    head = lambda b, h: (b, 0, h)
    return pl.pallas_call(
        functools.partial(_attn_kernel, kc=kc),
        grid=(B, DIFF_HEADS),
        in_specs=[
            pl.BlockSpec(memory_space=pltpu.SMEM),
            pl.BlockSpec((1, S, LANES), head),
            pl.BlockSpec((1, S, LANES), head),
            pl.BlockSpec((nc, DIFF_V_DIM, kc), lambda b, h: (b, h, 0)),
            pl.BlockSpec((DIFF_V_DIM, 1), lambda b, h: (0, 0)),
        ],
        out_specs=pl.BlockSpec((1, S, LANES), head),
        out_shape=jax.ShapeDtypeStruct((B, S, ATTN_WIDTH), BF16),
        scratch_shapes=[pltpu.VMEM((2, S, w), F32),
                        pltpu.VMEM((2, SUBLANES, w), F32),
                        pltpu.VMEM((DIFF_V_DIM + ATTN_ONES_ROWS, w), F32),
                        pltpu.VMEM((DIFF_V_DIM, w), F32)],
        compiler_params=_params("parallel", "parallel"),
        name="diff_attn",
    )(lam, q, k, vt, g_subln.reshape(DIFF_V_DIM, 1))


def _shortconv_kernel(hy_ref, prev_ref, next_ref, w_ref, b_ref, u_ref, x0_ref):
    i = pl.program_id(1)
    last = pl.num_programs(1) - 1
    x = hy_ref[0].astype(F32)
    ts = x.shape[0]
    prev_row = jnp.where(i == 0, 0.0, prev_ref[0, HALO_ROWS - 1:HALO_ROWS, :].astype(F32))
    next_row = jnp.where(i == last, 0.0, next_ref[0, 0:1, :].astype(F32))
    row = lax.broadcasted_iota(jnp.int32, (ts, 1), 0)
    xm = jnp.where(row == 0, prev_row, pltpu.roll(x, 1, axis=0))
    xp = jnp.where(row == ts - 1, next_row, pltpu.roll(x, ts - 1, axis=0))
    y = b_ref[...] + xm * w_ref[0:1, :] + x * w_ref[1:2, :] + xp * w_ref[2:3, :]
    C = HYENA_WIDTH
    u_ref[0] = (y[:, :C] * y[:, 2 * C:]).astype(u_ref.dtype)
    x0_ref[0] = y[:, C:2 * C].astype(x0_ref.dtype)


HALO_ROWS = 16


def _short_conv(hy, conv_w, conv_b):
    B, S, C3 = hy.shape
    ts = TOKEN_TILE
    nb = ts // HALO_ROWS
    return pl.pallas_call(
        _shortconv_kernel,
        grid=(B, S // ts),
        in_specs=[
            pl.BlockSpec((1, ts, C3), lambda b, i: (b, i, 0)),
            pl.BlockSpec((1, HALO_ROWS, C3), lambda b, i: (b, jnp.maximum(i * nb - 1, 0), 0)),
            pl.BlockSpec((1, HALO_ROWS, C3), lambda b, i: (b, jnp.minimum((i + 1) * nb, S // HALO_ROWS - 1), 0)),
            pl.BlockSpec((3, C3), lambda b, i: (0, 0)),
            pl.BlockSpec((1, C3), lambda b, i: (0, 0)),
        ],
        out_specs=[
            pl.BlockSpec((1, ts, HYENA_WIDTH), lambda b, i: (b, i, 0)),
            pl.BlockSpec((1, ts, HYENA_WIDTH), lambda b, i: (b, i, 0)),
        ],
        out_shape=[
            jax.ShapeDtypeStruct((B, S, HYENA_WIDTH), BF16),
            jax.ShapeDtypeStruct((B, S, HYENA_WIDTH), BF16),
        ],
        compiler_params=_params("parallel", "parallel"),
        name="short_conv",
    )(hy, hy, hy, conv_w, conv_b.reshape(1, C3))


FFT_N = 8192
FFT_N1 = 64
FFT_N2 = 128
FFT_CH_BLOCK = 32
FFT_GROUP = 4
FFT_UNROLL = 2


def _dft_constants():
    import numpy as np
    n1, n2, n = FFT_N1, FFT_N2, FFT_N
    k1 = np.arange(n1)[:, None]
    t1 = np.arange(n1)[None, :]
    f1 = np.exp(-2j * np.pi * k1 * t1 / n1)
    f1h = f1[:, : n1 // 2]
    w1c = np.block([[f1h.real, -f1h.imag], [f1h.imag, f1h.real]])
    w1r = np.concatenate([f1h.real, f1h.imag], axis=0)
    t2 = np.arange(n2)[:, None]
    k2 = np.arange(n2)[None, :]
    f2 = np.exp(-2j * np.pi * t2 * k2 / n2)
    w2a = np.concatenate([f2.real, f2.imag], axis=1)
    w2b = np.concatenate([-f2.imag, f2.real], axis=1)
    g2 = np.conj(f2)
    w3a = np.concatenate([g2.real, g2.imag], axis=1)
    w3b = np.concatenate([-g2.imag, g2.real], axis=1)
    h = np.conj(f1).T[: n1 // 2] / n
    w4 = np.block([[h.real, -h.imag], [h.imag, h.real]])
    tw = np.exp(-2j * np.pi * np.arange(n1)[:, None] * np.arange(n2)[None, :] / n)
    bf = lambda a: jnp.asarray(a, dtype=F32).astype(BF16)
    return dict(w1c=bf(w1c), w1r=bf(w1r), w2a=bf(w2a), w2b=bf(w2b), w3a=bf(w3a), w3b=bf(w3b),
                w4=bf(w4), tc=jnp.asarray(np.tile(tw.real, (1, FFT_GROUP)), F32),
                ts=jnp.asarray(np.tile(tw.imag, (1, FFT_GROUP)), F32))


def _dft_forward(x_ref, w1_ref, tc_ref, ts_ref, w2a_ref, w2b_ref, ar_ref, ai_ref):
    ng = x_ref.shape[0] // FFT_GROUP
    n1, n2 = FFT_N1, FFT_N2
    w1 = w1_ref[...]
    tc, ts = tc_ref[...], ts_ref[...]

    def body(g, carry):
        xg = jnp.concatenate([x_ref[g * FFT_GROUP + cl] for cl in range(FFT_GROUP)], axis=1)
        a = jnp.dot(w1, xg, preferred_element_type=F32)
        ar, ai = a[:n1], a[n1:]
        tr = (ar * tc - ai * ts).astype(BF16)
        ti = (ar * ts + ai * tc).astype(BF16)
        for cl in range(FFT_GROUP):
            rows = pl.ds(pl.multiple_of((g * FFT_GROUP + cl) * n1, n1), n1)
            ar_ref[rows, :] = tr[:, cl * n2:(cl + 1) * n2]
            ai_ref[rows, :] = ti[:, cl * n2:(cl + 1) * n2]
        return carry

    lax.fori_loop(0, ng, body, 0, unroll=FFT_UNROLL)
    return (jnp.dot(ar_ref[...], w2a_ref[...], preferred_element_type=F32)
            + jnp.dot(ai_ref[...], w2b_ref[...], preferred_element_type=F32))


def _spectrum_kernel(hf_ref, hb_ref, w1_ref, tc_ref, ts_ref, w2a_ref, w2b_ref, o_ref, ar_ref, ai_ref):
    n2 = FFT_N2
    f = _dft_forward(hf_ref, w1_ref, tc_ref, ts_ref, w2a_ref, w2b_ref, ar_ref, ai_ref)
    o_ref[...] = f.reshape(o_ref.shape)
    b = _dft_forward(hb_ref, w1_ref, tc_ref, ts_ref, w2a_ref, w2b_ref, ar_ref, ai_ref)
    b = b.reshape(o_ref.shape)
    o_ref[:, :, :n2] = o_ref[:, :, :n2] + b[:, :, :n2]
    o_ref[:, :, n2:] = o_ref[:, :, n2:] - b[:, :, n2:]


def _fftconv_kernel(x_ref, kf_ref, w1_ref, tc_ref, ts_ref, w2a_ref, w2b_ref, w3a_ref, w3b_ref,
                    w4_ref, y_ref, ar_ref, ai_ref, c_ref):
    cb = x_ref.shape[0]
    ng = cb // FFT_GROUP
    n1, n2 = FFT_N1, FFT_N2
    b = _dft_forward(x_ref, w1_ref, tc_ref, ts_ref, w2a_ref, w2b_ref, ar_ref, ai_ref)
    kf = kf_ref[...].reshape(cb * n1, 2 * n2)
    br, bi = b[:, :n2], b[:, n2:]
    kr, ki = kf[:, :n2], kf[:, n2:]
    ar_ref[...] = (br * kr - bi * ki).astype(BF16)
    ai_ref[...] = (br * ki + bi * kr).astype(BF16)
    c_ref[...] = (jnp.dot(ar_ref[...], w3a_ref[...], preferred_element_type=F32)
                  + jnp.dot(ai_ref[...], w3b_ref[...], preferred_element_type=F32))
    w4 = w4_ref[...]
    tc, ts = tc_ref[...], ts_ref[...]

    def body(g, carry):
        rows = [pl.ds(pl.multiple_of((g * FFT_GROUP + cl) * n1, n1), n1) for cl in range(FFT_GROUP)]
        cr = jnp.concatenate([c_ref[r, :n2] for r in rows], axis=1)
        ci = jnp.concatenate([c_ref[r, n2:] for r in rows], axis=1)
        dr = (cr * tc + ci * ts).astype(BF16)
        di = (ci * tc - cr * ts).astype(BF16)
        yg = (jnp.dot(w4[:, :n1], dr, preferred_element_type=F32)
              + jnp.dot(w4[:, n1:], di, preferred_element_type=F32)).astype(y_ref.dtype)
        for cl in range(FFT_GROUP):
            y_ref[g * FFT_GROUP + cl] = yg[:, cl * n2:(cl + 1) * n2]
        return carry

    lax.fori_loop(0, ng, body, 0, unroll=FFT_UNROLL)


def _const_spec(a):
    nd = a.ndim
    return pl.BlockSpec(a.shape, lambda *_: (0,) * nd)


def _filter_spectrum(hf_slabs, hb_slabs, cst):
    C = hf_slabs.shape[0]
    cb = FFT_CH_BLOCK
    consts = [cst["w1r"], cst["tc"], cst["ts"], cst["w2a"], cst["w2b"]]
    slab = pl.BlockSpec((cb, FFT_N1 // 2, FFT_N2), lambda i: (i, 0, 0))
    return pl.pallas_call(
        _spectrum_kernel,
        grid=(C // cb,),
        in_specs=[slab, slab] + [_const_spec(a) for a in consts],
        out_specs=pl.BlockSpec((cb, FFT_N1, 2 * FFT_N2), lambda i: (i, 0, 0)),
        out_shape=jax.ShapeDtypeStruct((C, FFT_N1, 2 * FFT_N2), F32),
        scratch_shapes=[pltpu.VMEM((cb * FFT_N1, FFT_N2), BF16), pltpu.VMEM((cb * FFT_N1, FFT_N2), BF16)],
        compiler_params=_params("parallel"),
        name="filter_spectrum",
    )(hf_slabs, hb_slabs, *consts)


FILTER_TIME_TILE = 512


def _filter_kernel(zt_ref, w1t_ref, b1_ref, wit_ref, bi_ref, fr_ref, wot_ref, dec_ref, hf_ref, hb_ref):
    hp = lax.Precision.HIGHEST
    h = jnp.sin(fr_ref[0] * (jnp.dot(w1t_ref[...], zt_ref[...], precision=hp,
                                     preferred_element_type=F32) + b1_ref[...]))
    for j in range(FILTER_INNER):
        h = jnp.sin(fr_ref[j + 1] * (jnp.dot(wit_ref[j], h, precision=hp,
                                             preferred_element_type=F32) + bi_ref[j]))
    o = jnp.dot(wot_ref[...], h, precision=hp, preferred_element_type=F32)
    dec = dec_ref[...]
    hf_ref[...] = o[:HYENA_WIDTH] * dec
    hb_ref[...] = o[HYENA_WIDTH:] * dec


def _hyena_filters(seq, w1, b1, w_inner, b_inner, freq, w_out):
    C = HYENA_WIDTH
    order = w1.shape[1]
    pos = jnp.arange(seq, dtype=F32)
    t = pos / (seq - 1)
    bands = (FILTER_EMB - 1) // 2
    f = jnp.linspace(1e-4, bands - 1, bands, dtype=F32)
    fw = ((2.0 * math.pi / seq) * pos)[:, None] * f[None, :]
    z = jnp.concatenate([t[:, None], jnp.cos(fw), -jnp.sin(fw)], axis=-1)
    zt = jnp.zeros((LANES, seq), F32).at[:FILTER_EMB].set(z.T)
    w1t = jnp.zeros((order, LANES), F32).at[:, :FILTER_EMB].set(w1.T)
    max_decay = math.log(DECAY_TARGET) / FAST_DECAY_PCT
    min_decay = math.log(DECAY_TARGET) / SLOW_DECAY_PCT
    deltas = jnp.abs(jnp.linspace(min_decay, max_decay, C, dtype=F32))
    dec_t = jnp.exp(-deltas[:, None] * t[None, :])
    tt = FILTER_TIME_TILE
    lane_blk = lambda r: pl.BlockSpec((r, tt), lambda i: (0, i))
    args = [zt, w1t, b1.reshape(order, 1), jnp.swapaxes(w_inner, 1, 2),
            b_inner.reshape(FILTER_INNER, order, 1), freq.reshape(FILTER_INNER + 1, order, 1),
            w_out.T, dec_t]
    return pl.pallas_call(
        _filter_kernel,
        grid=(seq // tt,),
        in_specs=[lane_blk(LANES)] + [_const_spec(a) for a in args[1:7]] + [lane_blk(C)],
        out_specs=[lane_blk(C), lane_blk(C)],
        out_shape=[jax.ShapeDtypeStruct((C, seq), F32), jax.ShapeDtypeStruct((C, seq), F32)],
        compiler_params=_params("parallel"),
        name="hyena_filters",
    )(*args)


def _fft_conv(x_slabs, kf, cst):
    P, C = x_slabs.shape[:2]
    cb = FFT_CH_BLOCK
    consts = [cst["w1c"], cst["tc"], cst["ts"], cst["w2a"], cst["w2b"], cst["w3a"], cst["w3b"], cst["w4"]]
    return pl.pallas_call(
        _fftconv_kernel,
        grid=(C // cb, P),
        in_specs=[pl.BlockSpec((None, cb, FFT_N1, FFT_N2), lambda i, p: (p, i, 0, 0)),
                  pl.BlockSpec((cb, FFT_N1, 2 * FFT_N2), lambda i, p: (i, 0, 0))]
                 + [_const_spec(a) for a in consts],
        out_specs=pl.BlockSpec((None, cb, FFT_N1, FFT_N2), lambda i, p: (p, i, 0, 0)),
        out_shape=jax.ShapeDtypeStruct((P, C, FFT_N1, FFT_N2), BF16),
        scratch_shapes=[pltpu.VMEM((cb * FFT_N1, FFT_N2), BF16), pltpu.VMEM((cb * FFT_N1, FFT_N2), BF16),
                        pltpu.VMEM((cb * FFT_N1, 2 * FFT_N2), F32)],
        compiler_params=_params("parallel", "arbitrary"),
        name="fft_conv",
    )(x_slabs, kf, *consts)


def _split_bf16(x):
    hi = x.astype(BF16)
    lo = (x - hi.astype(F32)).astype(BF16)
    return hi, lo


def _outproj_kernel(attn_ref, yc_ref, u_ref, x0_ref, d_ref, gn_ref, grp_ref, wo_ref, x_ref,
                    gf_ref, rwh_ref, rwl_ref, rb_ref, tri_ref, h_ref, hn_ref, rt_ref, wt_ref,
                    cnt_out_ref, cnt_ref):
    @pl.when(pl.program_id(0) == 0)
    def _():
        cnt_ref[...] = jnp.zeros_like(cnt_ref)

    grp = grp_ref[...]
    gsz = HYENA_WIDTH // HYENA_GROUPS

    def token_tile(cnt_col):
        rows = slice(None)
        z = ((yc_ref[rows, :].astype(F32) + u_ref[rows, :].astype(F32) * d_ref[...])
             * x0_ref[rows, :].astype(F32))
        zh, zl = _split_bf16(z * z)
        ssq = (jnp.dot(zh, grp, preferred_element_type=F32)
               + jnp.dot(zl, grp, preferred_element_type=F32))
        hy_out = (z * lax.rsqrt(ssq * (1.0 / gsz) + EPS) * gn_ref[...]).astype(BF16)
        mix = (jnp.dot(attn_ref[rows, :], wo_ref[:ATTN_WIDTH, :], preferred_element_type=F32)
               + jnp.dot(hy_out, wo_ref[ATTN_WIDTH:, :], preferred_element_type=F32))
        h = x_ref[rows, :] + mix
        h_ref[rows, :] = h
        hn = _rms(h, gf_ref[...], EPS)
        hn_ref[rows, :] = hn.astype(BF16)
        nh, nl = _split_bf16(hn)
        nt = (((1,), (1,)), ((), ()))
        logits = (lax.dot_general(rwh_ref[...], nh, nt, preferred_element_type=F32)
                  + lax.dot_general(rwh_ref[...], nl, nt, preferred_element_type=F32)
                  + lax.dot_general(rwl_ref[...], nh, nt, preferred_element_type=F32)) + rb_ref[...]

        erow = lax.broadcasted_iota(jnp.int32, logits.shape, 0)
        work = logits
        top_val, top_idx, top_hot = [], [], []
        for _ in range(TOP_K):
            m = jnp.max(work, axis=0, keepdims=True)
            idx = jnp.min(jnp.where(work == m, erow, N_EXPERTS), axis=0, keepdims=True)
            hot = erow == idx
            top_val.append(m)
            top_idx.append(idx)
            top_hot.append(hot)
            work = jnp.where(hot, -jnp.inf, work)
        ex = [jnp.exp(v - top_val[0]) for v in top_val]
        inv_den = 1.0 / (ex[0] + ex[1] + ex[2] + ex[3])
        sel = jnp.zeros(logits.shape, F32)
        for hot in top_hot:
            sel = sel + hot.astype(F32)
        before = jnp.dot(sel.astype(BF16), tri_ref[...], preferred_element_type=F32) + cnt_col
        tm = logits.shape[1]
        row_rt = lax.broadcasted_iota(jnp.int32, (ROUTE_ROWS, tm), 0)
        row_w = lax.broadcasted_iota(jnp.int32, (LANES, tm), 0)
        packed = jnp.zeros((ROUTE_ROWS, tm), F32)
        w_rows = jnp.zeros((LANES, tm), F32)
        for r, hot in enumerate(top_hot):
            rank_r = jnp.sum(jnp.where(hot, before, 0.0), axis=0, keepdims=True)
            weight_r = ex[r] * inv_den
            packed = jnp.where(row_rt == r, top_idx[r].astype(F32), packed)
            packed = jnp.where(row_rt == TOP_K + r, rank_r, packed)
            packed = jnp.where(row_rt == 2 * TOP_K + r, weight_r, packed)
            w_rows = jnp.where(row_w == r, weight_r, w_rows)
        rt_ref[...] = packed
        wt_ref[...] = w_rows.T
        return cnt_col + jnp.sum(sel, axis=1, keepdims=True)

    cnt_col = token_tile(cnt_ref[:, 0:1])
    cnt_ref[...] = jnp.broadcast_to(cnt_col, cnt_ref.shape)
    cnt_out_ref[...] = cnt_ref[...]


def _out_proj(attn2, yc2, u2, x02, hyena_d, hyena_gn, w_out_bf, x2, g_ffn, router_w, router_b):
    T, D = x2.shape
    tm = TOKEN_TILE
    C = HYENA_WIDTH
    gid = jnp.arange(C) // (C // HYENA_GROUPS)
    grp = (gid[:, None] == gid[None, :]).astype(BF16)
    rwh, rwl = _split_bf16(router_w.T)
    rb = router_b.reshape(N_EXPERTS, 1)
    tri = (jnp.arange(tm)[:, None] < jnp.arange(tm)[None, :]).astype(BF16)
    row = lambda i: (i, 0)
    const = lambda i: (0, 0)
    return pl.pallas_call(
        _outproj_kernel,
        grid=(T // tm,),
        in_specs=[
            pl.BlockSpec((tm, ATTN_WIDTH), row),
            pl.BlockSpec((tm, C), row),
            pl.BlockSpec((tm, C), row),
            pl.BlockSpec((tm, C), row),
            pl.BlockSpec((1, C), const),
            pl.BlockSpec((1, C), const),
            pl.BlockSpec((C, C), const),
            pl.BlockSpec((D, D), const),
            pl.BlockSpec((tm, D), row),
            pl.BlockSpec((1, D), const),
            pl.BlockSpec((N_EXPERTS, D), const),
            pl.BlockSpec((N_EXPERTS, D), const),
            pl.BlockSpec((N_EXPERTS, 1), const),
            pl.BlockSpec((tm, tm), const),
        ],
        out_specs=[
            pl.BlockSpec((tm, D), row),
            pl.BlockSpec((tm, D), row),
            pl.BlockSpec((ROUTE_ROWS, tm), lambda i: (0, i)),
            pl.BlockSpec((tm, LANES), row),
            pl.BlockSpec((N_EXPERTS, LANES), const),
        ],
        out_shape=[
            jax.ShapeDtypeStruct((T, D), F32),
            jax.ShapeDtypeStruct((T, D), BF16),
            jax.ShapeDtypeStruct((ROUTE_ROWS, T), F32),
            jax.ShapeDtypeStruct((T, LANES), F32),
            jax.ShapeDtypeStruct((N_EXPERTS, LANES), F32),
        ],
        scratch_shapes=[pltpu.VMEM((N_EXPERTS, LANES), F32)],
        compiler_params=_params("arbitrary"),
        name="out_proj",
    )(attn2, yc2, u2, x02, hyena_d.reshape(1, C), hyena_gn.reshape(1, C), grp, w_out_bf, x2,
      g_ffn.reshape(1, D), rwh, rwl, rb, tri)


def _expert_kernel(be_ref, nused_ref, x_ref, wg_ref, bg_ref, wu_ref, bu_ref, wd_ref, bd_ref, *rest):
    y_ref, wg_bf, wu_bf, wd_bf = rest[-4:]
    i = pl.program_id(0)
    used = i < nused_ref[0]
    new_expert = jnp.logical_or(i == 0, be_ref[i] != be_ref[jnp.maximum(i - 1, 0)])

    @pl.when(jnp.logical_and(used, new_expert))
    def _():
        wg_bf[...] = wg_ref[0].astype(BF16)
        wu_bf[...] = wu_ref[0].astype(BF16)
        wd_bf[...] = wd_ref[0].astype(BF16)

    @pl.when(used)
    def _():
        x = x_ref[...]
        g = jnp.minimum(jnp.dot(x, wg_bf[...], preferred_element_type=F32) + bg_ref[0], SWIGLU_LIMIT)
        u = jnp.clip(jnp.dot(x, wu_bf[...], preferred_element_type=F32) + bu_ref[0],
                     -SWIGLU_LIMIT, SWIGLU_LIMIT)
        a = (u + 1.0) * (g * jax.nn.sigmoid(SWIGLU_ALPHA * g))
        y = jnp.dot(a.astype(BF16), wd_bf[...], preferred_element_type=F32) + bd_ref[0]
        y_ref[...] = y.astype(y_ref.dtype)

    @pl.when(jnp.logical_not(used))
    def _():
        y_ref[...] = jnp.zeros_like(y_ref)


def _experts(block_e, n_used, x_parts, wg, bg, wu, bu, wd, bd):
    D = x_parts[0].shape[1]
    tm = EXPERT_TILE
    P = sum(xs.shape[0] for xs in x_parts)
    E, _, FF = wg.shape
    wmap = lambda i, be, nu: (be[i], 0, 0)
    y = None
    first = 0
    for part, xs in enumerate(x_parts):
        nb = xs.shape[0] // tm
        in_specs = [
            pl.BlockSpec((tm, D), lambda i, be, nu: (i, 0)),
            pl.BlockSpec((1, D, FF), wmap),
            pl.BlockSpec((1, 1, FF), wmap),
            pl.BlockSpec((1, D, FF), wmap),
            pl.BlockSpec((1, 1, FF), wmap),
            pl.BlockSpec((1, FF, D), wmap),
            pl.BlockSpec((1, 1, D), wmap),
        ]
        args = [block_e[first:first + nb], jnp.clip(n_used - first, 0, nb), xs,
                wg, bg.reshape(E, 1, FF), wu, bu.reshape(E, 1, FF), wd, bd.reshape(E, 1, D)]
        aliases = {}
        if y is not None:
            in_specs.append(pl.BlockSpec(memory_space=pl.ANY))
            args.append(y)
            aliases = {len(args) - 1: 0}
        grid_spec = pltpu.PrefetchScalarGridSpec(
            num_scalar_prefetch=2,
            grid=(nb,),
            in_specs=in_specs,
            out_specs=pl.BlockSpec((tm, D), lambda i, be, nu, first=first: (i + first, 0)),
            scratch_shapes=[pltpu.VMEM((D, FF), BF16), pltpu.VMEM((D, FF), BF16),
                            pltpu.VMEM((FF, D), BF16)],
        )
        y = pl.pallas_call(
            _expert_kernel,
            grid_spec=grid_spec,
            out_shape=jax.ShapeDtypeStruct((P, D), BF16),
            input_output_aliases=aliases,
            compiler_params=_params("arbitrary"),
            name=f"moe_experts_{part}",
        )(*args)
        first += nb
    return y


def _ple_kernel(p_ref, wp_ref, gp_ref, e_ref):
    e = _rms(jnp.dot(p_ref[...].astype(BF16), wp_ref[...], preferred_element_type=F32),
             gp_ref[...], EPS)
    e_ref[...] = e.astype(e_ref.dtype)


def _ple_embed(p2, w_ple_bf, g_ple):
    T, PD = p2.shape
    D = w_ple_bf.shape[1]
    tm = TOKEN_TILE
    return pl.pallas_call(
        _ple_kernel,
        grid=(T // tm,),
        in_specs=[pl.BlockSpec((tm, PD), lambda i: (i, 0)),
                  pl.BlockSpec((PD, D), lambda i: (0, 0)),
                  pl.BlockSpec((1, D), lambda i: (0, 0))],
        out_specs=pl.BlockSpec((tm, D), lambda i: (i, 0)),
        out_shape=jax.ShapeDtypeStruct((T, D), BF16),
        compiler_params=_params("parallel"),
        name="ple_embed",
    )(p2, w_ple_bf, g_ple.reshape(1, D))


def _final_kernel(h_ref, yg_ref, rt_ref, e_ref, wg_ref, bg_ref, gfin_ref, *rest):
    o_ref = rest[-1]
    h = h_ref[...]
    for r in range(TOP_K):
        h = h + yg_ref[r].astype(F32) * rt_ref[:, r:r + 1]
    gate = jax.nn.sigmoid(jnp.dot(h.astype(BF16), wg_ref[...], preferred_element_type=F32)
                          + bg_ref[...])
    h = h + gate * e_ref[...].astype(F32)
    o_ref[...] = _rms(h, gfin_ref[...], EPS)


def _final(h1, yg_parts, route, e_ple, w_gate_bf, b_gate, g_final):
    T, D = h1.shape
    tm = TOKEN_TILE
    nb = T // tm // len(yg_parts)
    const = lambda i: (0, 0)
    out = None
    for part, yg in enumerate(yg_parts):
        row = lambda i, part=part: (i + part * nb, 0)
        in_specs = [
            pl.BlockSpec((tm, D), row),
            pl.BlockSpec((TOP_K, tm, D), lambda i: (0, i, 0)),
            pl.BlockSpec((tm, LANES), row),
            pl.BlockSpec((tm, D), row),
            pl.BlockSpec((D, D), const),
            pl.BlockSpec((1, D), const),
            pl.BlockSpec((1, D), const),
        ]
        args = [h1, yg, route, e_ple, w_gate_bf, b_gate.reshape(1, D), g_final.reshape(1, D)]
        aliases = {}
        if out is not None:
            in_specs.append(pl.BlockSpec(memory_space=pl.ANY))
            args.append(out)
            aliases = {len(args) - 1: 0}
        out = pl.pallas_call(
            _final_kernel,
            grid=(nb,),
            in_specs=in_specs,
            out_specs=pl.BlockSpec((tm, D), row),
            out_shape=jax.ShapeDtypeStruct((T, D), F32),
            input_output_aliases=aliases,
            compiler_params=_params("parallel"),
            name=f"final_{part}",
        )(*args)
    return out


def _rope_tables(seq):
    d = DIFF_HEAD_DIM
    pos = jnp.arange(seq, dtype=F32)
    inv = ROPE_THETA ** (-jnp.arange(0, d, 2, dtype=F32) / d)
    ang = pos[:, None] * inv[None, :]
    cos, sin = jnp.cos(ang), jnp.sin(ang)
    z = jnp.zeros_like(sin)
    cos_t = jnp.tile(jnp.concatenate([cos, cos], -1), (1, LANES // d))
    s1_t = jnp.tile(jnp.concatenate([-sin, z], -1), (1, LANES // d))
    s2_t = jnp.tile(jnp.concatenate([z, sin], -1), (1, LANES // d))
    return cos_t, s1_t, s2_t


def _long_conv(u, hf_t, hb_t):
    B, L, C = u.shape
    assert 2 * L == FFT_N and B % 2 == 0
    P, R = B // 2, FFT_N1 // 2
    cst = _dft_constants()
    kf = _filter_spectrum(hf_t.reshape(C, R, FFT_N2).astype(BF16),
                          hb_t.reshape(C, R, FFT_N2).astype(BF16), cst)
    x_slabs = (u.reshape(2, P, R, FFT_N2, C).transpose(1, 4, 0, 2, 3)
               .reshape(P, C, FFT_N1, FFT_N2).astype(BF16))
    y = _fft_conv(x_slabs, kf, cst)
    return y.reshape(P, C, 2, R, FFT_N2).transpose(2, 0, 3, 4, 1).reshape(B, L, C)


def _dispatch_indices(route_t, cnt, T):
    tm = EXPERT_TILE
    A = T * TOP_K
    top_e = route_t[:TOP_K].astype(jnp.int32)
    rank = route_t[TOP_K:2 * TOP_K].astype(jnp.int32)
    counts = cnt[:, 0].astype(jnp.int32)
    padded = (counts + tm - 1) // tm * tm
    pad_end = jnp.cumsum(padded)
    pad_start = pad_end - padded
    start = jnp.cumsum(counts) - counts
    experts = jnp.arange(N_EXPERTS, dtype=jnp.int32)[:, None, None]
    pos = rank + jnp.sum(jnp.where(top_e[None] == experts, pad_start[:, None, None], 0), axis=0)
    n_blocks = -(-A // tm) + N_EXPERTS
    P = n_blocks * tm
    block_first = jnp.arange(n_blocks, dtype=jnp.int32) * tm
    block_e = jnp.minimum(jnp.sum(pad_end[None, :] <= block_first[:, None], axis=1),
                          N_EXPERTS - 1).astype(jnp.int32)
    n_used = (pad_end[-1] // tm).astype(jnp.int32).reshape(1)
    tok = jnp.broadcast_to(jnp.arange(T, dtype=jnp.int32)[None, :], (TOP_K, T))
    _, sorted_tok = lax.sort_key_val(pos.reshape(A), tok.reshape(A))
    in_blk = jnp.arange(tm, dtype=jnp.int32)[None, :]
    r = (block_first - pad_start[block_e])[:, None] + in_blk
    compact = jnp.clip(start[block_e][:, None] + r, 0, A - 1)
    filler = (block_first[:, None] + in_blk) % T
    slot_tok = jnp.where(r < counts[block_e][:, None], sorted_tok[compact], filler).reshape(P)
    return pos, slot_tok, block_e, n_used


def kernel(x, p, g_mix, w_in, hyena_conv_w, hyena_conv_b, flt_w1, flt_b1, flt_w_inner, flt_b_inner, flt_freq, flt_w_out, hyena_d, hyena_gn, lambda_q1, lambda_k1, lambda_q2, lambda_k2, attn_subln, w_out, g_ffn, router_w, router_b, w_gate, b_gate, w_up, b_up, w_down, b_down, w_ple, g_ple, w_ple_gate, b_ple_gate, g_final):
    B, S, D = x.shape
    T = B * S
    i = 0
    x2 = x.reshape(T, D)

    cos_t, s1_t, s2_t = _rope_tables(S)
    q, k, vt, hy = _in_proj(x2, g_mix[i], w_in[i].astype(BF16), cos_t, s1_t, s2_t, S)

    lam = (jnp.exp(jnp.sum(lambda_q1[i] * lambda_k1[i])) - jnp.exp(jnp.sum(lambda_q2[i] * lambda_k2[i]))
           + LAMBDA_INIT).reshape(1).astype(F32)
    attn = _diff_attention(lam, q.reshape(B, S, -1), k.reshape(B, S, -1), vt, attn_subln[i])

    u, hx0 = _short_conv(hy.reshape(B, S, -1), hyena_conv_w[i], hyena_conv_b[i])
    h_fwd, h_bwd = _hyena_filters(S, flt_w1[i], flt_b1[i], flt_w_inner[i], flt_b_inner[i],
                                  flt_freq[i], flt_w_out[i])
    yc = _long_conv(u, h_fwd, h_bwd)

    h1, hn, route_t, top_w, cnt = _out_proj(
        attn.reshape(T, -1), yc.reshape(T, -1), u.reshape(T, -1), hx0.reshape(T, -1), hyena_d[i],
        hyena_gn[i], w_out[i].astype(BF16), x2, g_ffn[i], router_w[i], router_b[i])

    e_ple = _ple_embed(p[i].reshape(T, -1), w_ple[i].astype(BF16), g_ple[i])
    pos, slot_tok, block_e, n_used = _dispatch_indices(route_t, cnt, T)
    n_parts = COMBINE_PARTS
    head_rows = slot_tok.shape[0] // EXPERT_TILE // DISPATCH_HEAD_DIV * EXPERT_TILE
    x_parts = [hn[slot_tok[:head_rows]], hn[slot_tok[head_rows:]]]
    y = _experts(block_e, n_used, x_parts, w_gate[i], b_gate[i], w_up[i], b_up[i],
                 w_down[i], b_down[i])
    pos_parts = jnp.split(pos, n_parts, axis=1)
    yg_parts = [y[pp.reshape(-1)].reshape(TOP_K, T // n_parts, D) for pp in pos_parts]

    out = _final(h1, yg_parts, top_w, e_ple, w_ple_gate[i].astype(BF16), b_ple_gate[i], g_final)
    return out.reshape(B, S, D)
```

```python
import functools
import math

import jax
import jax.numpy as jnp
from jax import lax
from jax.experimental import pallas as pl
from jax.experimental.pallas import tpu as pltpu

F32 = jnp.float32
BF16 = jnp.bfloat16

ATTN_WIDTH = 512
HYENA_WIDTH = 512
DIFF_HEADS = 4
DIFF_HEAD_DIM = 64
DIFF_V_DIM = 128
HYENA_GROUPS = 8
FILTER_EMB = 33
FILTER_INNER = 2
FAST_DECAY_PCT = 0.3
SLOW_DECAY_PCT = 1.5
DECAY_TARGET = 1e-2
ROPE_THETA = 10000.0
N_EXPERTS = 32
TOP_K = 4
SWIGLU_ALPHA = 1.702
SWIGLU_LIMIT = 7.0
EPS = 1e-6
SUBLN_EPS = 1e-5
LAMBDA_INIT = 0.8 - 0.6 * math.exp(-0.3 * 0)

V7X_VMEM_LIMIT_BYTES = 56 * 1024 * 1024
LANES = 128
SUBLANES = 8

TOKEN_TILE = 512
HALO_ROWS = 16
ATTN_STREAM_W = 1024
ATTN_ONES_ROWS = 16
ATTN_UNROLL = 4
ATTN_KV_TILE = 512
EXPERT_TILE = 512
ROUTE_ROWS = 16
DISPATCH_HEAD_DIV = 4
COMBINE_PARTS = 2


def _params(*sem):
    return pltpu.CompilerParams(dimension_semantics=sem, vmem_limit_bytes=V7X_VMEM_LIMIT_BYTES)


def _rms(x, g, eps):
    return x * lax.rsqrt(jnp.mean(x * x, axis=-1, keepdims=True) + eps) * g


def _inproj_kernel(x_ref, g_ref, w_ref, wvt_ref, c_ref, s1_ref, s2_ref, q_ref, k_ref, vt_ref, hy_ref):
    a = _rms(x_ref[...], g_ref[...], EPS).astype(BF16)
    cos, s1, s2 = c_ref[...], s1_ref[...], s2_ref[...]

    half = DIFF_HEAD_DIM // 2

    def rope(t):
        return (t * cos + pltpu.roll(t, LANES - half, axis=1) * s1
                + pltpu.roll(t, half, axis=1) * s2)

    q_scale = (DIFF_HEAD_DIM ** -0.5) * math.log2(math.e)
    qk = jnp.dot(a, w_ref[:, : 2 * ATTN_WIDTH], preferred_element_type=F32)
    for j in range(ATTN_WIDTH // LANES):
        sl = slice(j * LANES, (j + 1) * LANES)
        q_ref[:, sl] = (rope(qk[:, sl]) * q_scale).astype(BF16)
        k_ref[:, sl] = rope(qk[:, ATTN_WIDTH + j * LANES: ATTN_WIDTH + (j + 1) * LANES]).astype(BF16)
    vt_ref[0] = lax.dot_general(wvt_ref[...], a, (((1,), (1,)), ((), ())),
                                preferred_element_type=F32).astype(BF16)
    hy_ref[...] = jnp.dot(a, w_ref[:, 3 * ATTN_WIDTH:], preferred_element_type=F32).astype(hy_ref.dtype)


def _in_proj(x2, g_mix, w_in_bf, cos_t, s1_t, s2_t, seq):
    T, D = x2.shape
    tm = TOKEN_TILE
    nseq = seq // tm
    wvt = w_in_bf[:, 2 * ATTN_WIDTH: 3 * ATTN_WIDTH].T
    row = lambda i: (i, 0)
    const = lambda i: (0, 0)
    pos = lambda i: (i % nseq, 0)
    return pl.pallas_call(
        _inproj_kernel,
        grid=(T // tm,),
        in_specs=[
            pl.BlockSpec((tm, D), row),
            pl.BlockSpec((1, D), const),
            pl.BlockSpec(w_in_bf.shape, const),
            pl.BlockSpec(wvt.shape, const),
            pl.BlockSpec((tm, LANES), pos),
            pl.BlockSpec((tm, LANES), pos),
            pl.BlockSpec((tm, LANES), pos),
        ],
        out_specs=[
            pl.BlockSpec((tm, ATTN_WIDTH), row),
            pl.BlockSpec((tm, ATTN_WIDTH), row),
            pl.BlockSpec((1, ATTN_WIDTH, tm), lambda i: (i, 0, 0)),
            pl.BlockSpec((tm, 3 * HYENA_WIDTH), row),
        ],
        out_shape=[
            jax.ShapeDtypeStruct((T, ATTN_WIDTH), BF16),
            jax.ShapeDtypeStruct((T, ATTN_WIDTH), BF16),
            jax.ShapeDtypeStruct((T // tm, ATTN_WIDTH, tm), BF16),
            jax.ShapeDtypeStruct((T, 3 * HYENA_WIDTH), BF16),
        ],
        compiler_params=_params("parallel"),
        name="in_proj",
    )(x2, g_mix.reshape(1, D), w_in_bf, wvt, cos_t, s1_t, s2_t)


def _attn_kernel(lam_ref, q_ref, k_ref, vt_ref, g_ref, o_ref, s_ref, m_ref, acc_ref, o0_ref, *, kc):
    w = ATTN_STREAM_W
    nc = k_ref.shape[1] // kc
    n_groups = q_ref.shape[1] // w

    def group_rows(grp):
        return pl.ds(pl.multiple_of(grp * w, w), w)

    def stream_q(grp, c):
        q = q_ref[0, group_rows(grp), :]
        lane = lax.broadcasted_iota(jnp.int32, q.shape, 1)
        keep = (lane < DIFF_HEAD_DIM) if c == 0 else (lane >= DIFF_HEAD_DIM)
        return jnp.where(keep, q, jnp.zeros_like(q))

    def fold8(t, op):
        r = t[0:SUBLANES]
        for j in range(1, kc // SUBLANES):
            r = op(r, t[j * SUBLANES:(j + 1) * SUBLANES])
        return r

    def score_chunk(i, qc, j):
        rows = pl.ds(pl.multiple_of(j * kc, kc), kc)
        s = lax.dot_general(k_ref[0, rows, :], qc, (((1,), (1,)), ((), ())),
                            preferred_element_type=F32)
        s_ref[i, rows, :] = s
        m_ref[i] = jnp.maximum(m_ref[i], fold8(s, jnp.maximum))

    ones_rows = jnp.ones((ATTN_ONES_ROWS, kc), BF16)

    def prob_chunk(i, j, m):
        rows = pl.ds(pl.multiple_of(j * kc, kc), kc)
        p = jnp.exp2(s_ref[i, rows, :] - m)
        v_aug = jnp.concatenate([vt_ref[j], ones_rows], axis=0)
        acc_ref[...] += jnp.dot(v_aug, p.astype(BF16), preferred_element_type=F32)

    def phase(read_buf, write_buf, q_next):
        if write_buf is not None:
            m_ref[write_buf] = jnp.full(m_ref.shape[1:], -jnp.inf, F32)
        if read_buf is not None:
            m = jnp.max(m_ref[read_buf], axis=0, keepdims=True)
            acc_ref[...] = jnp.zeros_like(acc_ref)

        def body(j, carry):
            if write_buf is not None:
                score_chunk(write_buf, q_next, j)
            if read_buf is not None:
                prob_chunk(read_buf, j, m)
            return carry

        lax.fori_loop(0, nc, body, 0, unroll=ATTN_UNROLL)
        if read_buf is None:
            return None
        return acc_ref[:DIFF_V_DIM, :] * (1.0 / acc_ref[DIFF_V_DIM:DIFF_V_DIM + 1, :])

    def finish_group(grp, out0, out1):
        o = out0 - lam_ref[0] * out1
        o = o * lax.rsqrt(jnp.mean(o * o, axis=0, keepdims=True) + SUBLN_EPS)
        o = o * (g_ref[...] * (1.0 - LAMBDA_INIT))
        o_ref[0, group_rows(grp), :] = o.T.astype(o_ref.dtype)

    phase(None, 0, stream_q(0, 0))

    def group_body(grp, carry):
        o0_ref[...] = phase(0, 1, stream_q(grp, 1))
        out1 = phase(1, 0, stream_q(grp + 1, 0))
        finish_group(grp, o0_ref[...], out1)
        return carry

    lax.fori_loop(0, n_groups - 1, group_body, 0)
    last = n_groups - 1
    o0_ref[...] = phase(0, 1, stream_q(last, 1))
    out1 = phase(1, None, None)
    finish_group(last, o0_ref[...], out1)


def _diff_attention(lam, q, k, vt, g_subln):
    B, S, _ = q.shape
    kc, w = ATTN_KV_TILE, ATTN_STREAM_W
    nc = S // kc
    assert vt.shape[2] == kc and S % w == 0 and S // w >= 2
    head = lambda b, h: (b, 0, h)
    return pl.pallas_call(
        functools.partial(_attn_kernel, kc=kc),
        grid=(B, DIFF_HEADS),
        in_specs=[
            pl.BlockSpec(memory_space=pltpu.SMEM),
            pl.BlockSpec((1, S, LANES), head),
            pl.BlockSpec((1, S, LANES), head),
            pl.BlockSpec((nc, DIFF_V_DIM, kc), lambda b, h: (b, h, 0)),
            pl.BlockSpec((DIFF_V_DIM, 1), lambda b, h: (0, 0)),
        ],
        out_specs=pl.BlockSpec((1, S, LANES), head),
        out_shape=jax.ShapeDtypeStruct((B, S, ATTN_WIDTH), BF16),
        scratch_shapes=[pltpu.VMEM((2, S, w), F32),
                        pltpu.VMEM((2, SUBLANES, w), F32),
                        pltpu.VMEM((DIFF_V_DIM + ATTN_ONES_ROWS, w), F32),
                        pltpu.VMEM((DIFF_V_DIM, w), F32)],
        compiler_params=_params("parallel", "parallel"),
        name="diff_attn",
    )(lam, q, k, vt, g_subln.reshape(DIFF_V_DIM, 1))


def _shortconv_kernel(hy_ref, prev_ref, next_ref, w_ref, b_ref, u_ref, x0_ref):
    i = pl.program_id(1)
    last = pl.num_programs(1) - 1
    x = hy_ref[0].astype(F32)
    ts = x.shape[0]
    prev_row = jnp.where(i == 0, 0.0, prev_ref[0, HALO_ROWS - 1:HALO_ROWS, :].astype(F32))
    next_row = jnp.where(i == last, 0.0, next_ref[0, 0:1, :].astype(F32))
    row = lax.broadcasted_iota(jnp.int32, (ts, 1), 0)
    xm = jnp.where(row == 0, prev_row, pltpu.roll(x, 1, axis=0))
    xp = jnp.where(row == ts - 1, next_row, pltpu.roll(x, ts - 1, axis=0))
    y = b_ref[...] + xm * w_ref[0:1, :] + x * w_ref[1:2, :] + xp * w_ref[2:3, :]
    C = HYENA_WIDTH
    u_ref[0] = (y[:, :C] * y[:, 2 * C:]).astype(u_ref.dtype)
    x0_ref[0] = y[:, C:2 * C].astype(x0_ref.dtype)


def _short_conv(hy, conv_w, conv_b):
    B, S, C3 = hy.shape
    ts = TOKEN_TILE
    nb = ts // HALO_ROWS
    return pl.pallas_call(
        _shortconv_kernel,
        grid=(B, S // ts),
        in_specs=[
            pl.BlockSpec((1, ts, C3), lambda b, i: (b, i, 0)),
            pl.BlockSpec((1, HALO_ROWS, C3), lambda b, i: (b, jnp.maximum(i * nb - 1, 0), 0)),
            pl.BlockSpec((1, HALO_ROWS, C3), lambda b, i: (b, jnp.minimum((i + 1) * nb, S // HALO_ROWS - 1), 0)),
            pl.BlockSpec((3, C3), lambda b, i: (0, 0)),
            pl.BlockSpec((1, C3), lambda b, i: (0, 0)),
        ],
        out_specs=[
            pl.BlockSpec((1, ts, HYENA_WIDTH), lambda b, i: (b, i, 0)),
            pl.BlockSpec((1, ts, HYENA_WIDTH), lambda b, i: (b, i, 0)),
        ],
        out_shape=[
            jax.ShapeDtypeStruct((B, S, HYENA_WIDTH), BF16),
            jax.ShapeDtypeStruct((B, S, HYENA_WIDTH), BF16),
        ],
        compiler_params=_params("parallel", "parallel"),
        name="short_conv",
    )(hy, hy, hy, conv_w, conv_b.reshape(1, C3))


FFT_N = 8192
FFT_N1 = 64
FFT_N2 = 128
FFT_CH_BLOCK = 32
FFT_GROUP = 4
FFT_UNROLL = 2


def _dft_constants():
    import numpy as np
    n1, n2, n = FFT_N1, FFT_N2, FFT_N
    k1 = np.arange(n1)[:, None]
    t1 = np.arange(n1)[None, :]
    f1 = np.exp(-2j * np.pi * k1 * t1 / n1)
    f1h = f1[:, : n1 // 2]
    w1c = np.block([[f1h.real, -f1h.imag], [f1h.imag, f1h.real]])
    w1r = np.concatenate([f1h.real, f1h.imag], axis=0)
    t2 = np.arange(n2)[:, None]
    k2 = np.arange(n2)[None, :]
    f2 = np.exp(-2j * np.pi * t2 * k2 / n2)
    w2a = np.concatenate([f2.real, f2.imag], axis=1)
    w2b = np.concatenate([-f2.imag, f2.real], axis=1)
    g2 = np.conj(f2)
    w3a = np.concatenate([g2.real, g2.imag], axis=1)
    w3b = np.concatenate([-g2.imag, g2.real], axis=1)
    h = np.conj(f1).T[: n1 // 2] / n
    w4 = np.block([[h.real, -h.imag], [h.imag, h.real]])
    tw = np.exp(-2j * np.pi * np.arange(n1)[:, None] * np.arange(n2)[None, :] / n)
    bf = lambda a: jnp.asarray(a, dtype=F32).astype(BF16)
    return dict(w1c=bf(w1c), w1r=bf(w1r), w2a=bf(w2a), w2b=bf(w2b), w3a=bf(w3a), w3b=bf(w3b),
                w4=bf(w4), tc=jnp.asarray(np.tile(tw.real, (1, FFT_GROUP)), F32),
                ts=jnp.asarray(np.tile(tw.imag, (1, FFT_GROUP)), F32))


def _dft_forward(x_ref, w1_ref, tc_ref, ts_ref, w2a_ref, w2b_ref, ar_ref, ai_ref):
    ng = x_ref.shape[0] // FFT_GROUP
    n1, n2 = FFT_N1, FFT_N2
    w1 = w1_ref[...]
    tc, ts = tc_ref[...], ts_ref[...]

    def body(g, carry):
        xg = jnp.concatenate([x_ref[g * FFT_GROUP + cl] for cl in range(FFT_GROUP)], axis=1)
        a = jnp.dot(w1, xg, preferred_element_type=F32)
        ar, ai = a[:n1], a[n1:]
        tr = (ar * tc - ai * ts).astype(BF16)
        ti = (ar * ts + ai * tc).astype(BF16)
        for cl in range(FFT_GROUP):
            rows = pl.ds(pl.multiple_of((g * FFT_GROUP + cl) * n1, n1), n1)
            ar_ref[rows, :] = tr[:, cl * n2:(cl + 1) * n2]
            ai_ref[rows, :] = ti[:, cl * n2:(cl + 1) * n2]
        return carry

    lax.fori_loop(0, ng, body, 0, unroll=FFT_UNROLL)
    return (jnp.dot(ar_ref[...], w2a_ref[...], preferred_element_type=F32)
            + jnp.dot(ai_ref[...], w2b_ref[...], preferred_element_type=F32))


def _spectrum_kernel(hf_ref, hb_ref, w1_ref, tc_ref, ts_ref, w2a_ref, w2b_ref, o_ref, ar_ref, ai_ref):
    n2 = FFT_N2
    f = _dft_forward(hf_ref, w1_ref, tc_ref, ts_ref, w2a_ref, w2b_ref, ar_ref, ai_ref)
    o_ref[...] = f.reshape(o_ref.shape)
    b = _dft_forward(hb_ref, w1_ref, tc_ref, ts_ref, w2a_ref, w2b_ref, ar_ref, ai_ref)
    b = b.reshape(o_ref.shape)
    o_ref[:, :, :n2] = o_ref[:, :, :n2] + b[:, :, :n2]
    o_ref[:, :, n2:] = o_ref[:, :, n2:] - b[:, :, n2:]


def _fftconv_kernel(x_ref, kf_ref, w1_ref, tc_ref, ts_ref, w2a_ref, w2b_ref, w3a_ref, w3b_ref,
                    w4_ref, y_ref, ar_ref, ai_ref, c_ref):
    cb = x_ref.shape[0]
    ng = cb // FFT_GROUP
    n1, n2 = FFT_N1, FFT_N2
    b = _dft_forward(x_ref, w1_ref, tc_ref, ts_ref, w2a_ref, w2b_ref, ar_ref, ai_ref)
    kf = kf_ref[...].reshape(cb * n1, 2 * n2)
    br, bi = b[:, :n2], b[:, n2:]
    kr, ki = kf[:, :n2], kf[:, n2:]
    ar_ref[...] = (br * kr - bi * ki).astype(BF16)
    ai_ref[...] = (br * ki + bi * kr).astype(BF16)
    c_ref[...] = (jnp.dot(ar_ref[...], w3a_ref[...], preferred_element_type=F32)
                  + jnp.dot(ai_ref[...], w3b_ref[...], preferred_element_type=F32))
    w4 = w4_ref[...]
    tc, ts = tc_ref[...], ts_ref[...]

    def body(g, carry):
        rows = [pl.ds(pl.multiple_of((g * FFT_GROUP + cl) * n1, n1), n1) for cl in range(FFT_GROUP)]
        cr = jnp.concatenate([c_ref[r, :n2] for r in rows], axis=1)
        ci = jnp.concatenate([c_ref[r, n2:] for r in rows], axis=1)
        dr = (cr * tc + ci * ts).astype(BF16)
        di = (ci * tc - cr * ts).astype(BF16)
        yg = (jnp.dot(w4[:, :n1], dr, preferred_element_type=F32)
              + jnp.dot(w4[:, n1:], di, preferred_element_type=F32)).astype(y_ref.dtype)
        for cl in range(FFT_GROUP):
            y_ref[g * FFT_GROUP + cl] = yg[:, cl * n2:(cl + 1) * n2]
        return carry

    lax.fori_loop(0, ng, body, 0, unroll=FFT_UNROLL)


def _const_spec(a):
    nd = a.ndim
    return pl.BlockSpec(a.shape, lambda *_: (0,) * nd)


def _filter_spectrum(hf_slabs, hb_slabs, cst):
    C = hf_slabs.shape[0]
    cb = FFT_CH_BLOCK
    consts = [cst["w1r"], cst["tc"], cst["ts"], cst["w2a"], cst["w2b"]]
    slab = pl.BlockSpec((cb, FFT_N1 // 2, FFT_N2), lambda i: (i, 0, 0))
    return pl.pallas_call(
        _spectrum_kernel,
        grid=(C // cb,),
        in_specs=[slab, slab] + [_const_spec(a) for a in consts],
        out_specs=pl.BlockSpec((cb, FFT_N1, 2 * FFT_N2), lambda i: (i, 0, 0)),
        out_shape=jax.ShapeDtypeStruct((C, FFT_N1, 2 * FFT_N2), F32),
        scratch_shapes=[pltpu.VMEM((cb * FFT_N1, FFT_N2), BF16), pltpu.VMEM((cb * FFT_N1, FFT_N2), BF16)],
        compiler_params=_params("parallel"),
        name="filter_spectrum",
    )(hf_slabs, hb_slabs, *consts)


FILTER_TIME_TILE = 512


def _filter_kernel(zt_ref, w1t_ref, b1_ref, wit_ref, bi_ref, fr_ref, wot_ref, dec_ref, hf_ref, hb_ref):
    hp = lax.Precision.HIGHEST
    h = jnp.sin(fr_ref[0] * (jnp.dot(w1t_ref[...], zt_ref[...], precision=hp,
                                     preferred_element_type=F32) + b1_ref[...]))
    for j in range(FILTER_INNER):
        h = jnp.sin(fr_ref[j + 1] * (jnp.dot(wit_ref[j], h, precision=hp,
                                             preferred_element_type=F32) + bi_ref[j]))
    o = jnp.dot(wot_ref[...], h, precision=hp, preferred_element_type=F32)
    dec = dec_ref[...]
    hf_ref[...] = o[:HYENA_WIDTH] * dec
    hb_ref[...] = o[HYENA_WIDTH:] * dec


def _hyena_filters(seq, w1, b1, w_inner, b_inner, freq, w_out):
    C = HYENA_WIDTH
    order = w1.shape[1]
    pos = jnp.arange(seq, dtype=F32)
    t = pos / (seq - 1)
    bands = (FILTER_EMB - 1) // 2
    f = jnp.linspace(1e-4, bands - 1, bands, dtype=F32)
    fw = ((2.0 * math.pi / seq) * pos)[:, None] * f[None, :]
    z = jnp.concatenate([t[:, None], jnp.cos(fw), -jnp.sin(fw)], axis=-1)
    zt = jnp.zeros((LANES, seq), F32).at[:FILTER_EMB].set(z.T)
    w1t = jnp.zeros((order, LANES), F32).at[:, :FILTER_EMB].set(w1.T)
    max_decay = math.log(DECAY_TARGET) / FAST_DECAY_PCT
    min_decay = math.log(DECAY_TARGET) / SLOW_DECAY_PCT
    deltas = jnp.abs(jnp.linspace(min_decay, max_decay, C, dtype=F32))
    dec_t = jnp.exp(-deltas[:, None] * t[None, :])
    tt = FILTER_TIME_TILE
    lane_blk = lambda r: pl.BlockSpec((r, tt), lambda i: (0, i))
    args = [zt, w1t, b1.reshape(order, 1), jnp.swapaxes(w_inner, 1, 2),
            b_inner.reshape(FILTER_INNER, order, 1), freq.reshape(FILTER_INNER + 1, order, 1),
            w_out.T, dec_t]
    return pl.pallas_call(
        _filter_kernel,
        grid=(seq // tt,),
        in_specs=[lane_blk(LANES)] + [_const_spec(a) for a in args[1:7]] + [lane_blk(C)],
        out_specs=[lane_blk(C), lane_blk(C)],
        out_shape=[jax.ShapeDtypeStruct((C, seq), F32), jax.ShapeDtypeStruct((C, seq), F32)],
        compiler_params=_params("parallel"),
        name="hyena_filters",
    )(*args)


def _fft_conv(x_slabs, kf, cst):
    P, C = x_slabs.shape[:2]
    cb = FFT_CH_BLOCK
    consts = [cst["w1c"], cst["tc"], cst["ts"], cst["w2a"], cst["w2b"], cst["w3a"], cst["w3b"], cst["w4"]]
    return pl.pallas_call(
        _fftconv_kernel,
        grid=(C // cb, P),
        in_specs=[pl.BlockSpec((None, cb, FFT_N1, FFT_N2), lambda i, p: (p, i, 0, 0)),
                  pl.BlockSpec((cb, FFT_N1, 2 * FFT_N2), lambda i, p: (i, 0, 0))]
                 + [_const_spec(a) for a in consts],
        out_specs=pl.BlockSpec((None, cb, FFT_N1, FFT_N2), lambda i, p: (p, i, 0, 0)),
        out_shape=jax.ShapeDtypeStruct((P, C, FFT_N1, FFT_N2), BF16),
        scratch_shapes=[pltpu.VMEM((cb * FFT_N1, FFT_N2), BF16), pltpu.VMEM((cb * FFT_N1, FFT_N2), BF16),
                        pltpu.VMEM((cb * FFT_N1, 2 * FFT_N2), F32)],
        compiler_params=_params("parallel", "arbitrary"),
        name="fft_conv",
    )(x_slabs, kf, *consts)


def _split_bf16(x):
    hi = x.astype(BF16)
    lo = (x - hi.astype(F32)).astype(BF16)
    return hi, lo


def _outproj_kernel(attn_ref, yc_ref, u_ref, x0_ref, d_ref, gn_ref, grp_ref, wo_ref, x_ref,
                    gf_ref, rwh_ref, rwl_ref, rb_ref, tri_ref, h_ref, hn_ref, rt_ref, wt_ref,
                    cnt_out_ref, cnt_ref):
    @pl.when(pl.program_id(0) == 0)
    def _():
        cnt_ref[...] = jnp.zeros_like(cnt_ref)

    grp = grp_ref[...]
    gsz = HYENA_WIDTH // HYENA_GROUPS

    def token_tile(cnt_col):
        rows = slice(None)
        z = ((yc_ref[rows, :].astype(F32) + u_ref[rows, :].astype(F32) * d_ref[...])
             * x0_ref[rows, :].astype(F32))
        zh, zl = _split_bf16(z * z)
        ssq = (jnp.dot(zh, grp, preferred_element_type=F32)
               + jnp.dot(zl, grp, preferred_element_type=F32))
        hy_out = (z * lax.rsqrt(ssq * (1.0 / gsz) + EPS) * gn_ref[...]).astype(BF16)
        mix = (jnp.dot(attn_ref[rows, :], wo_ref[:ATTN_WIDTH, :], preferred_element_type=F32)
               + jnp.dot(hy_out, wo_ref[ATTN_WIDTH:, :], preferred_element_type=F32))
        h = x_ref[rows, :] + mix
        h_ref[rows, :] = h
        hn = _rms(h, gf_ref[...], EPS)
        hn_ref[rows, :] = hn.astype(BF16)
        nh, nl = _split_bf16(hn)
        nt = (((1,), (1,)), ((), ()))
        logits = (lax.dot_general(rwh_ref[...], nh, nt, preferred_element_type=F32)
                  + lax.dot_general(rwh_ref[...], nl, nt, preferred_element_type=F32)
                  + lax.dot_general(rwl_ref[...], nh, nt, preferred_element_type=F32)) + rb_ref[...]

        erow = lax.broadcasted_iota(jnp.int32, logits.shape, 0)
        work = logits
        top_val, top_idx, top_hot = [], [], []
        for _ in range(TOP_K):
            m = jnp.max(work, axis=0, keepdims=True)
            idx = jnp.min(jnp.where(work == m, erow, N_EXPERTS), axis=0, keepdims=True)
            hot = erow == idx
            top_val.append(m)
            top_idx.append(idx)
            top_hot.append(hot)
            work = jnp.where(hot, -jnp.inf, work)
        ex = [jnp.exp(v - top_val[0]) for v in top_val]
        inv_den = 1.0 / (ex[0] + ex[1] + ex[2] + ex[3])
        sel = jnp.zeros(logits.shape, F32)
        for hot in top_hot:
            sel = sel + hot.astype(F32)
        before = jnp.dot(sel.astype(BF16), tri_ref[...], preferred_element_type=F32) + cnt_col
        tm = logits.shape[1]
        row_rt = lax.broadcasted_iota(jnp.int32, (ROUTE_ROWS, tm), 0)
        row_w = lax.broadcasted_iota(jnp.int32, (LANES, tm), 0)
        packed = jnp.zeros((ROUTE_ROWS, tm), F32)
        w_rows = jnp.zeros((LANES, tm), F32)
        for r, hot in enumerate(top_hot):
            rank_r = jnp.sum(jnp.where(hot, before, 0.0), axis=0, keepdims=True)
            weight_r = ex[r] * inv_den
            packed = jnp.where(row_rt == r, top_idx[r].astype(F32), packed)
            packed = jnp.where(row_rt == TOP_K + r, rank_r, packed)
            packed = jnp.where(row_rt == 2 * TOP_K + r, weight_r, packed)
            w_rows = jnp.where(row_w == r, weight_r, w_rows)
        rt_ref[...] = packed
        wt_ref[...] = w_rows.T
        return cnt_col + jnp.sum(sel, axis=1, keepdims=True)

    cnt_col = token_tile(cnt_ref[:, 0:1])
    cnt_ref[...] = jnp.broadcast_to(cnt_col, cnt_ref.shape)
    cnt_out_ref[...] = cnt_ref[...]


def _out_proj(attn2, yc2, u2, x02, hyena_d, hyena_gn, w_out_bf, x2, g_ffn, router_w, router_b):
    T, D = x2.shape
    tm = TOKEN_TILE
    C = HYENA_WIDTH
    gid = jnp.arange(C) // (C // HYENA_GROUPS)
    grp = (gid[:, None] == gid[None, :]).astype(BF16)
    rwh, rwl = _split_bf16(router_w.T)
    rb = router_b.reshape(N_EXPERTS, 1)
    tri = (jnp.arange(tm)[:, None] < jnp.arange(tm)[None, :]).astype(BF16)
    row = lambda i: (i, 0)
    const = lambda i: (0, 0)
    return pl.pallas_call(
        _outproj_kernel,
        grid=(T // tm,),
        in_specs=[
            pl.BlockSpec((tm, ATTN_WIDTH), row),
            pl.BlockSpec((tm, C), row),
            pl.BlockSpec((tm, C), row),
            pl.BlockSpec((tm, C), row),
            pl.BlockSpec((1, C), const),
            pl.BlockSpec((1, C), const),
            pl.BlockSpec((C, C), const),
            pl.BlockSpec((D, D), const),
            pl.BlockSpec((tm, D), row),
            pl.BlockSpec((1, D), const),
            pl.BlockSpec((N_EXPERTS, D), const),
            pl.BlockSpec((N_EXPERTS, D), const),
            pl.BlockSpec((N_EXPERTS, 1), const),
            pl.BlockSpec((tm, tm), const),
        ],
        out_specs=[
            pl.BlockSpec((tm, D), row),
            pl.BlockSpec((tm, D), row),
            pl.BlockSpec((ROUTE_ROWS, tm), lambda i: (0, i)),
            pl.BlockSpec((tm, LANES), row),
            pl.BlockSpec((N_EXPERTS, LANES), const),
        ],
        out_shape=[
            jax.ShapeDtypeStruct((T, D), F32),
            jax.ShapeDtypeStruct((T, D), BF16),
            jax.ShapeDtypeStruct((ROUTE_ROWS, T), F32),
            jax.ShapeDtypeStruct((T, LANES), F32),
            jax.ShapeDtypeStruct((N_EXPERTS, LANES), F32),
        ],
        scratch_shapes=[pltpu.VMEM((N_EXPERTS, LANES), F32)],
        compiler_params=_params("arbitrary"),
        name="out_proj",
    )(attn2, yc2, u2, x02, hyena_d.reshape(1, C), hyena_gn.reshape(1, C), grp, w_out_bf, x2,
      g_ffn.reshape(1, D), rwh, rwl, rb, tri)


def _expert_kernel(be_ref, nused_ref, x_ref, wg_ref, bg_ref, wu_ref, bu_ref, wd_ref, bd_ref, *rest):
    y_ref, wg_bf, wu_bf, wd_bf = rest[-4:]
    i = pl.program_id(0)
    used = i < nused_ref[0]
    new_expert = jnp.logical_or(i == 0, be_ref[i] != be_ref[jnp.maximum(i - 1, 0)])

    @pl.when(jnp.logical_and(used, new_expert))
    def _():
        wg_bf[...] = wg_ref[0].astype(BF16)
        wu_bf[...] = wu_ref[0].astype(BF16)
        wd_bf[...] = wd_ref[0].astype(BF16)

    @pl.when(used)
    def _():
        x = x_ref[...]
        g = jnp.minimum(jnp.dot(x, wg_bf[...], preferred_element_type=F32) + bg_ref[0], SWIGLU_LIMIT)
        u = jnp.clip(jnp.dot(x, wu_bf[...], preferred_element_type=F32) + bu_ref[0],
                     -SWIGLU_LIMIT, SWIGLU_LIMIT)
        a = (u + 1.0) * (g * jax.nn.sigmoid(SWIGLU_ALPHA * g))
        y = jnp.dot(a.astype(BF16), wd_bf[...], preferred_element_type=F32) + bd_ref[0]
        y_ref[...] = y.astype(y_ref.dtype)

    @pl.when(jnp.logical_not(used))
    def _():
        y_ref[...] = jnp.zeros_like(y_ref)


def _experts(block_e, n_used, x_parts, wg, bg, wu, bu, wd, bd):
    D = x_parts[0].shape[1]
    tm = EXPERT_TILE
    P = sum(xs.shape[0] for xs in x_parts)
    E, _, FF = wg.shape
    wmap = lambda i, be, nu: (be[i], 0, 0)
    y = None
    first = 0
    for part, xs in enumerate(x_parts):
        nb = xs.shape[0] // tm
        in_specs = [
            pl.BlockSpec((tm, D), lambda i, be, nu: (i, 0)),
            pl.BlockSpec((1, D, FF), wmap),
            pl.BlockSpec((1, 1, FF), wmap),
            pl.BlockSpec((1, D, FF), wmap),
            pl.BlockSpec((1, 1, FF), wmap),
            pl.BlockSpec((1, FF, D), wmap),
            pl.BlockSpec((1, 1, D), wmap),
        ]
        args = [block_e[first:first + nb], jnp.clip(n_used - first, 0, nb), xs,
                wg, bg.reshape(E, 1, FF), wu, bu.reshape(E, 1, FF), wd, bd.reshape(E, 1, D)]
        aliases = {}
        if y is not None:
            in_specs.append(pl.BlockSpec(memory_space=pl.ANY))
            args.append(y)
            aliases = {len(args) - 1: 0}
        grid_spec = pltpu.PrefetchScalarGridSpec(
            num_scalar_prefetch=2,
            grid=(nb,),
            in_specs=in_specs,
            out_specs=pl.BlockSpec((tm, D), lambda i, be, nu, first=first: (i + first, 0)),
            scratch_shapes=[pltpu.VMEM((D, FF), BF16), pltpu.VMEM((D, FF), BF16),
                            pltpu.VMEM((FF, D), BF16)],
        )
        y = pl.pallas_call(
            _expert_kernel,
            grid_spec=grid_spec,
            out_shape=jax.ShapeDtypeStruct((P, D), BF16),
            input_output_aliases=aliases,
            compiler_params=_params("arbitrary"),
            name=f"moe_experts_{part}",
        )(*args)
        first += nb
    return y


def _final_kernel(h_ref, yg_ref, rt_ref, p_ref, wp_ref, gp_ref, wg_ref, bg_ref, gfin_ref, *rest):
    o_ref = rest[-1]
    h = h_ref[...]
    for r in range(TOP_K):
        h = h + yg_ref[r].astype(F32) * rt_ref[:, r:r + 1]
    e = _rms(jnp.dot(p_ref[...].astype(BF16), wp_ref[...], preferred_element_type=F32),
             gp_ref[...], EPS)
    gate = jax.nn.sigmoid(jnp.dot(h.astype(BF16), wg_ref[...], preferred_element_type=F32)
                          + bg_ref[...])
    h = h + gate * e
    o_ref[...] = _rms(h, gfin_ref[...], EPS)


def _final(h1, yg_parts, route, p2, w_ple_bf, g_ple, w_gate_bf, b_gate, g_final):
    T, D = h1.shape
    tm = TOKEN_TILE
    PD = p2.shape[1]
    nb = T // tm // len(yg_parts)
    const = lambda i: (0, 0)
    out = None
    for part, yg in enumerate(yg_parts):
        row = lambda i, part=part: (i + part * nb, 0)
        in_specs = [
            pl.BlockSpec((tm, D), row),
            pl.BlockSpec((TOP_K, tm, D), lambda i: (0, i, 0)),
            pl.BlockSpec((tm, LANES), row),
            pl.BlockSpec((tm, PD), row),
            pl.BlockSpec((PD, D), const),
            pl.BlockSpec((1, D), const),
            pl.BlockSpec((D, D), const),
            pl.BlockSpec((1, D), const),
            pl.BlockSpec((1, D), const),
        ]
        args = [h1, yg, route, p2, w_ple_bf, g_ple.reshape(1, D), w_gate_bf, b_gate.reshape(1, D),
                g_final.reshape(1, D)]
        aliases = {}
        if out is not None:
            in_specs.append(pl.BlockSpec(memory_space=pl.ANY))
            args.append(out)
            aliases = {len(args) - 1: 0}
        out = pl.pallas_call(
            _final_kernel,
            grid=(nb,),
            in_specs=in_specs,
            out_specs=pl.BlockSpec((tm, D), row),
            out_shape=jax.ShapeDtypeStruct((T, D), F32),
            input_output_aliases=aliases,
            compiler_params=_params("parallel"),
            name=f"final_{part}",
        )(*args)
    return out


def _rope_tables(seq):
    d = DIFF_HEAD_DIM
    pos = jnp.arange(seq, dtype=F32)
    inv = ROPE_THETA ** (-jnp.arange(0, d, 2, dtype=F32) / d)
    ang = pos[:, None] * inv[None, :]
    cos, sin = jnp.cos(ang), jnp.sin(ang)
    z = jnp.zeros_like(sin)
    cos_t = jnp.tile(jnp.concatenate([cos, cos], -1), (1, LANES // d))
    s1_t = jnp.tile(jnp.concatenate([-sin, z], -1), (1, LANES // d))
    s2_t = jnp.tile(jnp.concatenate([z, sin], -1), (1, LANES // d))
    return cos_t, s1_t, s2_t


def _long_conv(u, hf_t, hb_t):
    B, L, C = u.shape
    assert 2 * L == FFT_N and B % 2 == 0
    P, R = B // 2, FFT_N1 // 2
    cst = _dft_constants()
    kf = _filter_spectrum(hf_t.reshape(C, R, FFT_N2).astype(BF16),
                          hb_t.reshape(C, R, FFT_N2).astype(BF16), cst)
    x_slabs = (u.reshape(2, P, R, FFT_N2, C).transpose(1, 4, 0, 2, 3)
               .reshape(P, C, FFT_N1, FFT_N2).astype(BF16))
    y = _fft_conv(x_slabs, kf, cst)
    return y.reshape(P, C, 2, R, FFT_N2).transpose(2, 0, 3, 4, 1).reshape(B, L, C)


def _dispatch_indices(route_t, cnt, T):
    tm = EXPERT_TILE
    A = T * TOP_K
    top_e = route_t[:TOP_K].astype(jnp.int32)
    rank = route_t[TOP_K:2 * TOP_K].astype(jnp.int32)
    counts = cnt[:, 0].astype(jnp.int32)
    padded = (counts + tm - 1) // tm * tm
    pad_end = jnp.cumsum(padded)
    pad_start = pad_end - padded
    start = jnp.cumsum(counts) - counts
    experts = jnp.arange(N_EXPERTS, dtype=jnp.int32)[:, None, None]
    pos = rank + jnp.sum(jnp.where(top_e[None] == experts, pad_start[:, None, None], 0), axis=0)
    n_blocks = -(-A // tm) + N_EXPERTS
    P = n_blocks * tm
    block_first = jnp.arange(n_blocks, dtype=jnp.int32) * tm
    block_e = jnp.minimum(jnp.sum(pad_end[None, :] <= block_first[:, None], axis=1),
                          N_EXPERTS - 1).astype(jnp.int32)
    n_used = (pad_end[-1] // tm).astype(jnp.int32).reshape(1)
    tok = jnp.broadcast_to(jnp.arange(T, dtype=jnp.int32)[None, :], (TOP_K, T))
    _, sorted_tok = lax.sort_key_val(pos.reshape(A), tok.reshape(A))
    in_blk = jnp.arange(tm, dtype=jnp.int32)[None, :]
    r = (block_first - pad_start[block_e])[:, None] + in_blk
    compact = jnp.clip(start[block_e][:, None] + r, 0, A - 1)
    filler = (block_first[:, None] + in_blk) % T
    slot_tok = jnp.where(r < counts[block_e][:, None], sorted_tok[compact], filler).reshape(P)
    return pos, slot_tok, block_e, n_used


def kernel(x, p, g_mix, w_in, hyena_conv_w, hyena_conv_b, flt_w1, flt_b1, flt_w_inner, flt_b_inner, flt_freq, flt_w_out, hyena_d, hyena_gn, lambda_q1, lambda_k1, lambda_q2, lambda_k2, attn_subln, w_out, g_ffn, router_w, router_b, w_gate, b_gate, w_up, b_up, w_down, b_down, w_ple, g_ple, w_ple_gate, b_ple_gate, g_final):
    B, S, D = x.shape
    T = B * S
    i = 0
    x2 = x.reshape(T, D)

    cos_t, s1_t, s2_t = _rope_tables(S)
    q, k, vt, hy = _in_proj(x2, g_mix[i], w_in[i].astype(BF16), cos_t, s1_t, s2_t, S)

    lam = (jnp.exp(jnp.sum(lambda_q1[i] * lambda_k1[i])) - jnp.exp(jnp.sum(lambda_q2[i] * lambda_k2[i]))
           + LAMBDA_INIT).reshape(1).astype(F32)
    attn = _diff_attention(lam, q.reshape(B, S, -1), k.reshape(B, S, -1), vt, attn_subln[i])

    u, hx0 = _short_conv(hy.reshape(B, S, -1), hyena_conv_w[i], hyena_conv_b[i])
    h_fwd, h_bwd = _hyena_filters(S, flt_w1[i], flt_b1[i], flt_w_inner[i], flt_b_inner[i],
                                  flt_freq[i], flt_w_out[i])
    yc = _long_conv(u, h_fwd, h_bwd)

    h1, hn, route_t, top_w, cnt = _out_proj(
        attn.reshape(T, -1), yc.reshape(T, -1), u.reshape(T, -1), hx0.reshape(T, -1), hyena_d[i],
        hyena_gn[i], w_out[i].astype(BF16), x2, g_ffn[i], router_w[i], router_b[i])

    pos, slot_tok, block_e, n_used = _dispatch_indices(route_t, cnt, T)
    n_parts = COMBINE_PARTS
    head_rows = slot_tok.shape[0] // EXPERT_TILE // DISPATCH_HEAD_DIV * EXPERT_TILE
    x_parts = [hn[slot_tok[:head_rows]], hn[slot_tok[head_rows:]]]
    y = _experts(block_e, n_used, x_parts, w_gate[i], b_gate[i], w_up[i], b_up[i],
                 w_down[i], b_down[i])
    pos_parts = jnp.split(pos, n_parts, axis=1)
    yg_parts = [y[pp.reshape(-1)].reshape(TOP_K, T // n_parts, D) for pp in pos_parts]

    out = _final(h1, yg_parts, top_w, p[i].reshape(T, -1), w_ple[i].astype(BF16), g_ple[i],
                 w_ple_gate[i].astype(BF16), b_ple_gate[i], g_final)
    return out.reshape(B, S, D)
```

```python
import functools
import math

import jax
import jax.numpy as jnp
from jax import lax
from jax.experimental import pallas as pl
from jax.experimental.pallas import tpu as pltpu

F32 = jnp.float32
BF16 = jnp.bfloat16

ATTN_WIDTH = 512
HYENA_WIDTH = 512
DIFF_HEADS = 4
DIFF_HEAD_DIM = 64
DIFF_V_DIM = 128
HYENA_GROUPS = 8
FILTER_EMB = 33
FILTER_INNER = 2
FAST_DECAY_PCT = 0.3
SLOW_DECAY_PCT = 1.5
DECAY_TARGET = 1e-2
ROPE_THETA = 10000.0
N_EXPERTS = 32
TOP_K = 4
SWIGLU_ALPHA = 1.702
SWIGLU_LIMIT = 7.0
EPS = 1e-6
SUBLN_EPS = 1e-5
LAMBDA_INIT = 0.8 - 0.6 * math.exp(-0.3 * 0)

V7X_VMEM_LIMIT_BYTES = 56 * 1024 * 1024
LANES = 128
SUBLANES = 8

TOKEN_TILE = 512
HALO_ROWS = 16
ATTN_STREAM_W = 1024
ATTN_ONES_ROWS = 16
ATTN_UNROLL = 4
ATTN_KV_TILE = 512
EXPERT_TILE = 512
ROUTE_ROWS = 16
DISPATCH_HEAD_DIV = 4
COMBINE_PARTS = 2


def _params(*sem):
    return pltpu.CompilerParams(dimension_semantics=sem, vmem_limit_bytes=V7X_VMEM_LIMIT_BYTES)


def _rms(x, g, eps):
    return x * lax.rsqrt(jnp.mean(x * x, axis=-1, keepdims=True) + eps) * g


def _inproj_kernel(x_ref, g_ref, w_ref, wvt_ref, c_ref, s1_ref, s2_ref, q_ref, k_ref, vt_ref, hy_ref):
    a = _rms(x_ref[...], g_ref[...], EPS).astype(BF16)
    cos, s1, s2 = c_ref[...], s1_ref[...], s2_ref[...]

    half = DIFF_HEAD_DIM // 2

    def rope(t):
        return (t * cos + pltpu.roll(t, LANES - half, axis=1) * s1
                + pltpu.roll(t, half, axis=1) * s2)

    q_scale = (DIFF_HEAD_DIM ** -0.5) * math.log2(math.e)
    qk = jnp.dot(a, w_ref[:, : 2 * ATTN_WIDTH], preferred_element_type=F32)
    for j in range(ATTN_WIDTH // LANES):
        sl = slice(j * LANES, (j + 1) * LANES)
        q_ref[:, sl] = (rope(qk[:, sl]) * q_scale).astype(BF16)
        k_ref[:, sl] = rope(qk[:, ATTN_WIDTH + j * LANES: ATTN_WIDTH + (j + 1) * LANES]).astype(BF16)
    vt_ref[0] = lax.dot_general(wvt_ref[...], a, (((1,), (1,)), ((), ())),
                                preferred_element_type=F32).astype(BF16)
    hy_ref[...] = jnp.dot(a, w_ref[:, 3 * ATTN_WIDTH:], preferred_element_type=F32).astype(hy_ref.dtype)


def _in_proj(x2, g_mix, w_in_bf, cos_t, s1_t, s2_t, seq):
    T, D = x2.shape
    tm = TOKEN_TILE
    nseq = seq // tm
    wvt = w_in_bf[:, 2 * ATTN_WIDTH: 3 * ATTN_WIDTH].T
    row = lambda i: (i, 0)
    const = lambda i: (0, 0)
    pos = lambda i: (i % nseq, 0)
    return pl.pallas_call(
        _inproj_kernel,
        grid=(T // tm,),
        in_specs=[
            pl.BlockSpec((tm, D), row),
            pl.BlockSpec((1, D), const),
            pl.BlockSpec(w_in_bf.shape, const),
            pl.BlockSpec(wvt.shape, const),
            pl.BlockSpec((tm, LANES), pos),
            pl.BlockSpec((tm, LANES), pos),
            pl.BlockSpec((tm, LANES), pos),
        ],
        out_specs=[
            pl.BlockSpec((tm, ATTN_WIDTH), row),
            pl.BlockSpec((tm, ATTN_WIDTH), row),
            pl.BlockSpec((1, ATTN_WIDTH, tm), lambda i: (i, 0, 0)),
            pl.BlockSpec((tm, 3 * HYENA_WIDTH), row),
        ],
        out_shape=[
            jax.ShapeDtypeStruct((T, ATTN_WIDTH), BF16),
            jax.ShapeDtypeStruct((T, ATTN_WIDTH), BF16),
            jax.ShapeDtypeStruct((T // tm, ATTN_WIDTH, tm), BF16),
            jax.ShapeDtypeStruct((T, 3 * HYENA_WIDTH), BF16),
        ],
        compiler_params=_params("parallel"),
        name="in_proj",
    )(x2, g_mix.reshape(1, D), w_in_bf, wvt, cos_t, s1_t, s2_t)


def _attn_kernel(lam_ref, q_ref, k_ref, vt_ref, g_ref, o_ref, s_ref, m_ref, acc_ref, o0_ref, *, kc):
    w = ATTN_STREAM_W
    nc = k_ref.shape[1] // kc
    n_groups = q_ref.shape[1] // w

    def group_rows(grp):
        return pl.ds(pl.multiple_of(grp * w, w), w)

    def stream_q(grp, c):
        q = q_ref[0, group_rows(grp), :]
        lane = lax.broadcasted_iota(jnp.int32, q.shape, 1)
        keep = (lane < DIFF_HEAD_DIM) if c == 0 else (lane >= DIFF_HEAD_DIM)
        return jnp.where(keep, q, jnp.zeros_like(q))

    def fold8(t, op):
        r = t[0:SUBLANES]
        for j in range(1, kc // SUBLANES):
            r = op(r, t[j * SUBLANES:(j + 1) * SUBLANES])
        return r

    def score_chunk(i, qc, j):
        rows = pl.ds(pl.multiple_of(j * kc, kc), kc)
        s = lax.dot_general(k_ref[0, rows, :], qc, (((1,), (1,)), ((), ())),
                            preferred_element_type=F32)
        s_ref[i, rows, :] = s
        m_ref[i] = jnp.maximum(m_ref[i], fold8(s, jnp.maximum))

    ones_rows = jnp.ones((ATTN_ONES_ROWS, kc), BF16)

    def prob_chunk(i, j, m):
        rows = pl.ds(pl.multiple_of(j * kc, kc), kc)
        p = jnp.exp2(s_ref[i, rows, :] - m)
        v_aug = jnp.concatenate([vt_ref[j], ones_rows], axis=0)
        acc_ref[...] += jnp.dot(v_aug, p.astype(BF16), preferred_element_type=F32)

    def phase(read_buf, write_buf, q_next):
        if write_buf is not None:
            m_ref[write_buf] = jnp.full(m_ref.shape[1:], -jnp.inf, F32)
        if read_buf is not None:
            m = jnp.max(m_ref[read_buf], axis=0, keepdims=True)
            acc_ref[...] = jnp.zeros_like(acc_ref)

        def body(j, carry):
            if write_buf is not None:
                score_chunk(write_buf, q_next, j)
            if read_buf is not None:
                prob_chunk(read_buf, j, m)
            return carry

        lax.fori_loop(0, nc, body, 0, unroll=ATTN_UNROLL)
        if read_buf is None:
            return None
        return acc_ref[:DIFF_V_DIM, :] * (1.0 / acc_ref[DIFF_V_DIM:DIFF_V_DIM + 1, :])

    def finish_group(grp, out0, out1):
        o = out0 - lam_ref[0] * out1
        o = o * lax.rsqrt(jnp.mean(o * o, axis=0, keepdims=True) + SUBLN_EPS)
        o = o * (g_ref[...] * (1.0 - LAMBDA_INIT))
        o_ref[0, group_rows(grp), :] = o.T.astype(o_ref.dtype)

    phase(None, 0, stream_q(0, 0))

    def group_body(grp, carry):
        o0_ref[...] = phase(0, 1, stream_q(grp, 1))
        out1 = phase(1, 0, stream_q(grp + 1, 0))
        finish_group(grp, o0_ref[...], out1)
        return carry

    lax.fori_loop(0, n_groups - 1, group_body, 0)
    last = n_groups - 1
    o0_ref[...] = phase(0, 1, stream_q(last, 1))
    out1 = phase(1, None, None)
    finish_group(last, o0_ref[...], out1)


def _diff_attention(lam, q, k, vt, g_subln):
    B, S, _ = q.shape
    kc, w = ATTN_KV_TILE, ATTN_STREAM_W
    nc = S // kc
    assert vt.shape[2] == kc and S % w == 0 and S // w >= 2
    head = lambda b, h: (b, 0, h)
    return pl.pallas_call(
        functools.partial(_attn_kernel, kc=kc),
        grid=(B, DIFF_HEADS),
        in_specs=[
            pl.BlockSpec(memory_space=pltpu.SMEM),
            pl.BlockSpec((1, S, LANES), head),
            pl.BlockSpec((1, S, LANES), head),
            pl.BlockSpec((nc, DIFF_V_DIM, kc), lambda b, h: (b, h, 0)),
            pl.BlockSpec((DIFF_V_DIM, 1), lambda b, h: (0, 0)),
        ],
        out_specs=pl.BlockSpec((1, S, LANES), head),
        out_shape=jax.ShapeDtypeStruct((B, S, ATTN_WIDTH), BF16),
        scratch_shapes=[pltpu.VMEM((2, S, w), F32),
                        pltpu.VMEM((2, SUBLANES, w), F32),
                        pltpu.VMEM((DIFF_V_DIM + ATTN_ONES_ROWS, w), F32),
                        pltpu.VMEM((DIFF_V_DIM, w), F32)],
        compiler_params=_params("parallel", "parallel"),
        name="diff_attn",
    )(lam, q, k, vt, g_subln.reshape(DIFF_V_DIM, 1))


def _shortconv_kernel(hy_ref, prev_ref, next_ref, w_ref, b_ref, u_ref, x0_ref):
    i = pl.program_id(1)
    last = pl.num_programs(1) - 1
    x = hy_ref[0].astype(F32)
    ts = x.shape[0]
    prev_row = jnp.where(i == 0, 0.0, prev_ref[0, HALO_ROWS - 1:HALO_ROWS, :].astype(F32))
    next_row = jnp.where(i == last, 0.0, next_ref[0, 0:1, :].astype(F32))
    row = lax.broadcasted_iota(jnp.int32, (ts, 1), 0)
    xm = jnp.where(row == 0, prev_row, pltpu.roll(x, 1, axis=0))
    xp = jnp.where(row == ts - 1, next_row, pltpu.roll(x, ts - 1, axis=0))
    y = b_ref[...] + xm * w_ref[0:1, :] + x * w_ref[1:2, :] + xp * w_ref[2:3, :]
    C = HYENA_WIDTH
    u_ref[0] = (y[:, :C] * y[:, 2 * C:]).astype(u_ref.dtype)
    x0_ref[0] = y[:, C:2 * C].astype(x0_ref.dtype)


def _short_conv(hy, conv_w, conv_b):
    B, S, C3 = hy.shape
    ts = TOKEN_TILE
    nb = ts // HALO_ROWS
    return pl.pallas_call(
        _shortconv_kernel,
        grid=(B, S // ts),
        in_specs=[
            pl.BlockSpec((1, ts, C3), lambda b, i: (b, i, 0)),
            pl.BlockSpec((1, HALO_ROWS, C3), lambda b, i: (b, jnp.maximum(i * nb - 1, 0), 0)),
            pl.BlockSpec((1, HALO_ROWS, C3), lambda b, i: (b, jnp.minimum((i + 1) * nb, S // HALO_ROWS - 1), 0)),
            pl.BlockSpec((3, C3), lambda b, i: (0, 0)),
            pl.BlockSpec((1, C3), lambda b, i: (0, 0)),
        ],
        out_specs=[
            pl.BlockSpec((1, ts, HYENA_WIDTH), lambda b, i: (b, i, 0)),
            pl.BlockSpec((1, ts, HYENA_WIDTH), lambda b, i: (b, i, 0)),
        ],
        out_shape=[
            jax.ShapeDtypeStruct((B, S, HYENA_WIDTH), BF16),
            jax.ShapeDtypeStruct((B, S, HYENA_WIDTH), BF16),
        ],
        compiler_params=_params("parallel", "parallel"),
        name="short_conv",
    )(hy, hy, hy, conv_w, conv_b.reshape(1, C3))


FFT_N = 8192
FFT_N1 = 64
FFT_N2 = 128
FFT_CH_BLOCK = 32
FFT_GROUP = 4
FFT_UNROLL = 8


def _dft_constants():
    import numpy as np
    n1, n2, n = FFT_N1, FFT_N2, FFT_N
    k1 = np.arange(n1)[:, None]
    t1 = np.arange(n1)[None, :]
    f1 = np.exp(-2j * np.pi * k1 * t1 / n1)
    f1h = f1[:, : n1 // 2]
    w1c = np.block([[f1h.real, -f1h.imag], [f1h.imag, f1h.real]])
    w1r = np.concatenate([f1h.real, f1h.imag], axis=0)
    t2 = np.arange(n2)[:, None]
    k2 = np.arange(n2)[None, :]
    f2 = np.exp(-2j * np.pi * t2 * k2 / n2)
    w2a = np.concatenate([f2.real, f2.imag], axis=1)
    w2b = np.concatenate([-f2.imag, f2.real], axis=1)
    g2 = np.conj(f2)
    w3a = np.concatenate([g2.real, g2.imag], axis=1)
    w3b = np.concatenate([-g2.imag, g2.real], axis=1)
    h = np.conj(f1).T[: n1 // 2] / n
    w4 = np.block([[h.real, -h.imag], [h.imag, h.real]])
    tw = np.exp(-2j * np.pi * np.arange(n1)[:, None] * np.arange(n2)[None, :] / n)
    bf = lambda a: jnp.asarray(a, dtype=F32).astype(BF16)
    return dict(w1c=bf(w1c), w1r=bf(w1r), w2a=bf(w2a), w2b=bf(w2b), w3a=bf(w3a), w3b=bf(w3b),
                w4=bf(w4), tc=jnp.asarray(np.tile(tw.real, (1, FFT_GROUP)), F32),
                ts=jnp.asarray(np.tile(tw.imag, (1, FFT_GROUP)), F32))


def _dft_forward(x_ref, w1_ref, tc_ref, ts_ref, w2a_ref, w2b_ref, ar_ref, ai_ref):
    ng = x_ref.shape[0] // FFT_GROUP
    n1, n2 = FFT_N1, FFT_N2
    w1 = w1_ref[...]
    tc, ts = tc_ref[...], ts_ref[...]

    def body(g, carry):
        xg = jnp.concatenate([x_ref[g * FFT_GROUP + cl] for cl in range(FFT_GROUP)], axis=1)
        a = jnp.dot(w1, xg, preferred_element_type=F32)
        ar, ai = a[:n1], a[n1:]
        tr = (ar * tc - ai * ts).astype(BF16)
        ti = (ar * ts + ai * tc).astype(BF16)
        for cl in range(FFT_GROUP):
            rows = pl.ds(pl.multiple_of((g * FFT_GROUP + cl) * n1, n1), n1)
            ar_ref[rows, :] = tr[:, cl * n2:(cl + 1) * n2]
            ai_ref[rows, :] = ti[:, cl * n2:(cl + 1) * n2]
        return carry

    lax.fori_loop(0, ng, body, 0, unroll=FFT_UNROLL)
    return (jnp.dot(ar_ref[...], w2a_ref[...], preferred_element_type=F32)
            + jnp.dot(ai_ref[...], w2b_ref[...], preferred_element_type=F32))


def _spectrum_kernel(hf_ref, hb_ref, w1_ref, tc_ref, ts_ref, w2a_ref, w2b_ref, o_ref, ar_ref, ai_ref):
    n2 = FFT_N2
    f = _dft_forward(hf_ref, w1_ref, tc_ref, ts_ref, w2a_ref, w2b_ref, ar_ref, ai_ref)
    o_ref[...] = f.reshape(o_ref.shape)
    b = _dft_forward(hb_ref, w1_ref, tc_ref, ts_ref, w2a_ref, w2b_ref, ar_ref, ai_ref)
    b = b.reshape(o_ref.shape)
    o_ref[:, :, :n2] = o_ref[:, :, :n2] + b[:, :, :n2]
    o_ref[:, :, n2:] = o_ref[:, :, n2:] - b[:, :, n2:]


def _fftconv_kernel(x_ref, kf_ref, w1_ref, tc_ref, ts_ref, w2a_ref, w2b_ref, w3a_ref, w3b_ref,
                    w4_ref, y_ref, ar_ref, ai_ref, c_ref):
    cb = x_ref.shape[0]
    ng = cb // FFT_GROUP
    n1, n2 = FFT_N1, FFT_N2
    b = _dft_forward(x_ref, w1_ref, tc_ref, ts_ref, w2a_ref, w2b_ref, ar_ref, ai_ref)
    kf = kf_ref[...].reshape(cb * n1, 2 * n2)
    br, bi = b[:, :n2], b[:, n2:]
    kr, ki = kf[:, :n2], kf[:, n2:]
    ar_ref[...] = (br * kr - bi * ki).astype(BF16)
    ai_ref[...] = (br * ki + bi * kr).astype(BF16)
    c_ref[...] = (jnp.dot(ar_ref[...], w3a_ref[...], preferred_element_type=F32)
                  + jnp.dot(ai_ref[...], w3b_ref[...], preferred_element_type=F32))
    w4 = w4_ref[...]
    tc, ts = tc_ref[...], ts_ref[...]

    def body(g, carry):
        rows = [pl.ds(pl.multiple_of((g * FFT_GROUP + cl) * n1, n1), n1) for cl in range(FFT_GROUP)]
        cr = jnp.concatenate([c_ref[r, :n2] for r in rows], axis=1)
        ci = jnp.concatenate([c_ref[r, n2:] for r in rows], axis=1)
        dr = (cr * tc + ci * ts).astype(BF16)
        di = (ci * tc - cr * ts).astype(BF16)
        yg = (jnp.dot(w4[:, :n1], dr, preferred_element_type=F32)
              + jnp.dot(w4[:, n1:], di, preferred_element_type=F32)).astype(y_ref.dtype)
        for cl in range(FFT_GROUP):
            y_ref[g * FFT_GROUP + cl] = yg[:, cl * n2:(cl + 1) * n2]
        return carry

    lax.fori_loop(0, ng, body, 0, unroll=FFT_UNROLL)


def _const_spec(a):
    nd = a.ndim
    return pl.BlockSpec(a.shape, lambda *_: (0,) * nd)


def _filter_spectrum(hf_slabs, hb_slabs, cst):
    C = hf_slabs.shape[0]
    cb = FFT_CH_BLOCK
    consts = [cst["w1r"], cst["tc"], cst["ts"], cst["w2a"], cst["w2b"]]
    slab = pl.BlockSpec((cb, FFT_N1 // 2, FFT_N2), lambda i: (i, 0, 0))
    return pl.pallas_call(
        _spectrum_kernel,
        grid=(C // cb,),
        in_specs=[slab, slab] + [_const_spec(a) for a in consts],
        out_specs=pl.BlockSpec((cb, FFT_N1, 2 * FFT_N2), lambda i: (i, 0, 0)),
        out_shape=jax.ShapeDtypeStruct((C, FFT_N1, 2 * FFT_N2), F32),
        scratch_shapes=[pltpu.VMEM((cb * FFT_N1, FFT_N2), BF16), pltpu.VMEM((cb * FFT_N1, FFT_N2), BF16)],
        compiler_params=_params("parallel"),
        name="filter_spectrum",
    )(hf_slabs, hb_slabs, *consts)


FILTER_TIME_TILE = 512


def _filter_kernel(zt_ref, w1t_ref, b1_ref, wit_ref, bi_ref, fr_ref, wot_ref, dec_ref, hf_ref, hb_ref):
    hp = lax.Precision.HIGHEST
    h = jnp.sin(fr_ref[0] * (jnp.dot(w1t_ref[...], zt_ref[...], precision=hp,
                                     preferred_element_type=F32) + b1_ref[...]))
    for j in range(FILTER_INNER):
        h = jnp.sin(fr_ref[j + 1] * (jnp.dot(wit_ref[j], h, precision=hp,
                                             preferred_element_type=F32) + bi_ref[j]))
    o = jnp.dot(wot_ref[...], h, precision=hp, preferred_element_type=F32)
    dec = dec_ref[...]
    hf_ref[...] = o[:HYENA_WIDTH] * dec
    hb_ref[...] = o[HYENA_WIDTH:] * dec


def _hyena_filters(seq, w1, b1, w_inner, b_inner, freq, w_out):
    C = HYENA_WIDTH
    order = w1.shape[1]
    pos = jnp.arange(seq, dtype=F32)
    t = pos / (seq - 1)
    bands = (FILTER_EMB - 1) // 2
    f = jnp.linspace(1e-4, bands - 1, bands, dtype=F32)
    fw = ((2.0 * math.pi / seq) * pos)[:, None] * f[None, :]
    z = jnp.concatenate([t[:, None], jnp.cos(fw), -jnp.sin(fw)], axis=-1)
    zt = jnp.zeros((LANES, seq), F32).at[:FILTER_EMB].set(z.T)
    w1t = jnp.zeros((order, LANES), F32).at[:, :FILTER_EMB].set(w1.T)
    max_decay = math.log(DECAY_TARGET) / FAST_DECAY_PCT
    min_decay = math.log(DECAY_TARGET) / SLOW_DECAY_PCT
    deltas = jnp.abs(jnp.linspace(min_decay, max_decay, C, dtype=F32))
    dec_t = jnp.exp(-deltas[:, None] * t[None, :])
    tt = FILTER_TIME_TILE
    lane_blk = lambda r: pl.BlockSpec((r, tt), lambda i: (0, i))
    args = [zt, w1t, b1.reshape(order, 1), jnp.swapaxes(w_inner, 1, 2),
            b_inner.reshape(FILTER_INNER, order, 1), freq.reshape(FILTER_INNER + 1, order, 1),
            w_out.T, dec_t]
    return pl.pallas_call(
        _filter_kernel,
        grid=(seq // tt,),
        in_specs=[lane_blk(LANES)] + [_const_spec(a) for a in args[1:7]] + [lane_blk(C)],
        out_specs=[lane_blk(C), lane_blk(C)],
        out_shape=[jax.ShapeDtypeStruct((C, seq), F32), jax.ShapeDtypeStruct((C, seq), F32)],
        compiler_params=_params("parallel"),
        name="hyena_filters",
    )(*args)


def _fft_conv(x_slabs, kf, cst):
    P, C = x_slabs.shape[:2]
    cb = FFT_CH_BLOCK
    consts = [cst["w1c"], cst["tc"], cst["ts"], cst["w2a"], cst["w2b"], cst["w3a"], cst["w3b"], cst["w4"]]
    return pl.pallas_call(
        _fftconv_kernel,
        grid=(C // cb, P),
        in_specs=[pl.BlockSpec((None, cb, FFT_N1, FFT_N2), lambda i, p: (p, i, 0, 0)),
                  pl.BlockSpec((cb, FFT_N1, 2 * FFT_N2), lambda i, p: (i, 0, 0))]
                 + [_const_spec(a) for a in consts],
        out_specs=pl.BlockSpec((None, cb, FFT_N1, FFT_N2), lambda i, p: (p, i, 0, 0)),
        out_shape=jax.ShapeDtypeStruct((P, C, FFT_N1, FFT_N2), BF16),
        scratch_shapes=[pltpu.VMEM((cb * FFT_N1, FFT_N2), BF16), pltpu.VMEM((cb * FFT_N1, FFT_N2), BF16),
                        pltpu.VMEM((cb * FFT_N1, 2 * FFT_N2), F32)],
        compiler_params=_params("parallel", "arbitrary"),
        name="fft_conv",
    )(x_slabs, kf, *consts)


def _split_bf16(x):
    hi = x.astype(BF16)
    lo = (x - hi.astype(F32)).astype(BF16)
    return hi, lo


def _outproj_kernel(attn_ref, yc_ref, u_ref, x0_ref, d_ref, gn_ref, grp_ref, wo_ref, x_ref,
                    gf_ref, rwh_ref, rwl_ref, rb_ref, tri_ref, h_ref, hn_ref, rt_ref, wt_ref,
                    cnt_out_ref, cnt_ref):
    @pl.when(pl.program_id(0) == 0)
    def _():
        cnt_ref[...] = jnp.zeros_like(cnt_ref)

    grp = grp_ref[...]
    gsz = HYENA_WIDTH // HYENA_GROUPS

    def token_tile(cnt_col):
        rows = slice(None)
        z = ((yc_ref[rows, :].astype(F32) + u_ref[rows, :].astype(F32) * d_ref[...])
             * x0_ref[rows, :].astype(F32))
        zh, zl = _split_bf16(z * z)
        ssq = (jnp.dot(zh, grp, preferred_element_type=F32)
               + jnp.dot(zl, grp, preferred_element_type=F32))
        hy_out = (z * lax.rsqrt(ssq * (1.0 / gsz) + EPS) * gn_ref[...]).astype(BF16)
        mix = (jnp.dot(attn_ref[rows, :], wo_ref[:ATTN_WIDTH, :], preferred_element_type=F32)
               + jnp.dot(hy_out, wo_ref[ATTN_WIDTH:, :], preferred_element_type=F32))
        h = x_ref[rows, :] + mix
        h_ref[rows, :] = h
        hn = _rms(h, gf_ref[...], EPS)
        hn_ref[rows, :] = hn.astype(BF16)
        nh, nl = _split_bf16(hn)
        nt = (((1,), (1,)), ((), ()))
        logits = (lax.dot_general(rwh_ref[...], nh, nt, preferred_element_type=F32)
                  + lax.dot_general(rwh_ref[...], nl, nt, preferred_element_type=F32)
                  + lax.dot_general(rwl_ref[...], nh, nt, preferred_element_type=F32)) + rb_ref[...]

        erow = lax.broadcasted_iota(jnp.int32, logits.shape, 0)
        work = logits
        top_val, top_idx, top_hot = [], [], []
        for _ in range(TOP_K):
            m = jnp.max(work, axis=0, keepdims=True)
            idx = jnp.min(jnp.where(work == m, erow, N_EXPERTS), axis=0, keepdims=True)
            hot = erow == idx
            top_val.append(m)
            top_idx.append(idx)
            top_hot.append(hot)
            work = jnp.where(hot, -jnp.inf, work)
        ex = [jnp.exp(v - top_val[0]) for v in top_val]
        inv_den = 1.0 / (ex[0] + ex[1] + ex[2] + ex[3])
        sel = jnp.zeros(logits.shape, F32)
        for hot in top_hot:
            sel = sel + hot.astype(F32)
        before = jnp.dot(sel.astype(BF16), tri_ref[...], preferred_element_type=F32) + cnt_col
        tm = logits.shape[1]
        row_rt = lax.broadcasted_iota(jnp.int32, (ROUTE_ROWS, tm), 0)
        row_w = lax.broadcasted_iota(jnp.int32, (LANES, tm), 0)
        packed = jnp.zeros((ROUTE_ROWS, tm), F32)
        w_rows = jnp.zeros((LANES, tm), F32)
        for r, hot in enumerate(top_hot):
            rank_r = jnp.sum(jnp.where(hot, before, 0.0), axis=0, keepdims=True)
            weight_r = ex[r] * inv_den
            packed = jnp.where(row_rt == r, top_idx[r].astype(F32), packed)
            packed = jnp.where(row_rt == TOP_K + r, rank_r, packed)
            packed = jnp.where(row_rt == 2 * TOP_K + r, weight_r, packed)
            w_rows = jnp.where(row_w == r, weight_r, w_rows)
        rt_ref[...] = packed
        wt_ref[...] = w_rows.T
        return cnt_col + jnp.sum(sel, axis=1, keepdims=True)

    cnt_col = token_tile(cnt_ref[:, 0:1])
    cnt_ref[...] = jnp.broadcast_to(cnt_col, cnt_ref.shape)
    cnt_out_ref[...] = cnt_ref[...]


def _out_proj(attn2, yc2, u2, x02, hyena_d, hyena_gn, w_out_bf, x2, g_ffn, router_w, router_b):
    T, D = x2.shape
    tm = TOKEN_TILE
    C = HYENA_WIDTH
    gid = jnp.arange(C) // (C // HYENA_GROUPS)
    grp = (gid[:, None] == gid[None, :]).astype(BF16)
    rwh, rwl = _split_bf16(router_w.T)
    rb = router_b.reshape(N_EXPERTS, 1)
    tri = (jnp.arange(tm)[:, None] < jnp.arange(tm)[None, :]).astype(BF16)
    row = lambda i: (i, 0)
    const = lambda i: (0, 0)
    return pl.pallas_call(
        _outproj_kernel,
        grid=(T // tm,),
        in_specs=[
            pl.BlockSpec((tm, ATTN_WIDTH), row),
            pl.BlockSpec((tm, C), row),
            pl.BlockSpec((tm, C), row),
            pl.BlockSpec((tm, C), row),
            pl.BlockSpec((1, C), const),
            pl.BlockSpec((1, C), const),
            pl.BlockSpec((C, C), const),
            pl.BlockSpec((D, D), const),
            pl.BlockSpec((tm, D), row),
            pl.BlockSpec((1, D), const),
            pl.BlockSpec((N_EXPERTS, D), const),
            pl.BlockSpec((N_EXPERTS, D), const),
            pl.BlockSpec((N_EXPERTS, 1), const),
            pl.BlockSpec((tm, tm), const),
        ],
        out_specs=[
            pl.BlockSpec((tm, D), row),
            pl.BlockSpec((tm, D), row),
            pl.BlockSpec((ROUTE_ROWS, tm), lambda i: (0, i)),
            pl.BlockSpec((tm, LANES), row),
            pl.BlockSpec((N_EXPERTS, LANES), const),
        ],
        out_shape=[
            jax.ShapeDtypeStruct((T, D), F32),
            jax.ShapeDtypeStruct((T, D), BF16),
            jax.ShapeDtypeStruct((ROUTE_ROWS, T), F32),
            jax.ShapeDtypeStruct((T, LANES), F32),
            jax.ShapeDtypeStruct((N_EXPERTS, LANES), F32),
        ],
        scratch_shapes=[pltpu.VMEM((N_EXPERTS, LANES), F32)],
        compiler_params=_params("arbitrary"),
        name="out_proj",
    )(attn2, yc2, u2, x02, hyena_d.reshape(1, C), hyena_gn.reshape(1, C), grp, w_out_bf, x2,
      g_ffn.reshape(1, D), rwh, rwl, rb, tri)


def _expert_kernel(be_ref, nused_ref, x_ref, wg_ref, bg_ref, wu_ref, bu_ref, wd_ref, bd_ref, *rest):
    y_ref, wg_bf, wu_bf, wd_bf = rest[-4:]
    i = pl.program_id(0)
    used = i < nused_ref[0]
    new_expert = jnp.logical_or(i == 0, be_ref[i] != be_ref[jnp.maximum(i - 1, 0)])

    @pl.when(jnp.logical_and(used, new_expert))
    def _():
        wg_bf[...] = wg_ref[0].astype(BF16)
        wu_bf[...] = wu_ref[0].astype(BF16)
        wd_bf[...] = wd_ref[0].astype(BF16)

    @pl.when(used)
    def _():
        x = x_ref[...]
        g = jnp.minimum(jnp.dot(x, wg_bf[...], preferred_element_type=F32) + bg_ref[0], SWIGLU_LIMIT)
        u = jnp.clip(jnp.dot(x, wu_bf[...], preferred_element_type=F32) + bu_ref[0],
                     -SWIGLU_LIMIT, SWIGLU_LIMIT)
        a = (u + 1.0) * (g * jax.nn.sigmoid(SWIGLU_ALPHA * g))
        y = jnp.dot(a.astype(BF16), wd_bf[...], preferred_element_type=F32) + bd_ref[0]
        y_ref[...] = y.astype(y_ref.dtype)

    @pl.when(jnp.logical_not(used))
    def _():
        y_ref[...] = jnp.zeros_like(y_ref)


def _experts(block_e, n_used, x_parts, wg, bg, wu, bu, wd, bd):
    D = x_parts[0].shape[1]
    tm = EXPERT_TILE
    P = sum(xs.shape[0] for xs in x_parts)
    E, _, FF = wg.shape
    wmap = lambda i, be, nu: (be[i], 0, 0)
    y = None
    first = 0
    for part, xs in enumerate(x_parts):
        nb = xs.shape[0] // tm
        in_specs = [
            pl.BlockSpec((tm, D), lambda i, be, nu: (i, 0)),
            pl.BlockSpec((1, D, FF), wmap),
            pl.BlockSpec((1, 1, FF), wmap),
            pl.BlockSpec((1, D, FF), wmap),
            pl.BlockSpec((1, 1, FF), wmap),
            pl.BlockSpec((1, FF, D), wmap),
            pl.BlockSpec((1, 1, D), wmap),
        ]
        args = [block_e[first:first + nb], jnp.clip(n_used - first, 0, nb), xs,
                wg, bg.reshape(E, 1, FF), wu, bu.reshape(E, 1, FF), wd, bd.reshape(E, 1, D)]
        aliases = {}
        if y is not None:
            in_specs.append(pl.BlockSpec(memory_space=pl.ANY))
            args.append(y)
            aliases = {len(args) - 1: 0}
        grid_spec = pltpu.PrefetchScalarGridSpec(
            num_scalar_prefetch=2,
            grid=(nb,),
            in_specs=in_specs,
            out_specs=pl.BlockSpec((tm, D), lambda i, be, nu, first=first: (i + first, 0)),
            scratch_shapes=[pltpu.VMEM((D, FF), BF16), pltpu.VMEM((D, FF), BF16),
                            pltpu.VMEM((FF, D), BF16)],
        )
        y = pl.pallas_call(
            _expert_kernel,
            grid_spec=grid_spec,
            out_shape=jax.ShapeDtypeStruct((P, D), BF16),
            input_output_aliases=aliases,
            compiler_params=_params("arbitrary"),
            name=f"moe_experts_{part}",
        )(*args)
        first += nb
    return y


def _final_kernel(h_ref, yg_ref, rt_ref, p_ref, wp_ref, gp_ref, wg_ref, bg_ref, gfin_ref, *rest):
    o_ref = rest[-1]
    h = h_ref[...]
    for r in range(TOP_K):
        h = h + yg_ref[r].astype(F32) * rt_ref[:, r:r + 1]
    e = _rms(jnp.dot(p_ref[...].astype(BF16), wp_ref[...], preferred_element_type=F32),
             gp_ref[...], EPS)
    gate = jax.nn.sigmoid(jnp.dot(h.astype(BF16), wg_ref[...], preferred_element_type=F32)
                          + bg_ref[...])
    h = h + gate * e
    o_ref[...] = _rms(h, gfin_ref[...], EPS)


def _final(h1, yg_parts, route, p2, w_ple_bf, g_ple, w_gate_bf, b_gate, g_final):
    T, D = h1.shape
    tm = TOKEN_TILE
    PD = p2.shape[1]
    nb = T // tm // len(yg_parts)
    const = lambda i: (0, 0)
    out = None
    for part, yg in enumerate(yg_parts):
        row = lambda i, part=part: (i + part * nb, 0)
        in_specs = [
            pl.BlockSpec((tm, D), row),
            pl.BlockSpec((TOP_K, tm, D), lambda i: (0, i, 0)),
            pl.BlockSpec((tm, LANES), row),
            pl.BlockSpec((tm, PD), row),
            pl.BlockSpec((PD, D), const),
            pl.BlockSpec((1, D), const),
            pl.BlockSpec((D, D), const),
            pl.BlockSpec((1, D), const),
            pl.BlockSpec((1, D), const),
        ]
        args = [h1, yg, route, p2, w_ple_bf, g_ple.reshape(1, D), w_gate_bf, b_gate.reshape(1, D),
                g_final.reshape(1, D)]
        aliases = {}
        if out is not None:
            in_specs.append(pl.BlockSpec(memory_space=pl.ANY))
            args.append(out)
            aliases = {len(args) - 1: 0}
        out = pl.pallas_call(
            _final_kernel,
            grid=(nb,),
            in_specs=in_specs,
            out_specs=pl.BlockSpec((tm, D), row),
            out_shape=jax.ShapeDtypeStruct((T, D), F32),
            input_output_aliases=aliases,
            compiler_params=_params("parallel"),
            name=f"final_{part}",
        )(*args)
    return out


def _rope_tables(seq):
    d = DIFF_HEAD_DIM
    pos = jnp.arange(seq, dtype=F32)
    inv = ROPE_THETA ** (-jnp.arange(0, d, 2, dtype=F32) / d)
    ang = pos[:, None] * inv[None, :]
    cos, sin = jnp.cos(ang), jnp.sin(ang)
    z = jnp.zeros_like(sin)
    cos_t = jnp.tile(jnp.concatenate([cos, cos], -1), (1, LANES // d))
    s1_t = jnp.tile(jnp.concatenate([-sin, z], -1), (1, LANES // d))
    s2_t = jnp.tile(jnp.concatenate([z, sin], -1), (1, LANES // d))
    return cos_t, s1_t, s2_t


def _long_conv(u, hf_t, hb_t):
    B, L, C = u.shape
    assert 2 * L == FFT_N and B % 2 == 0
    P, R = B // 2, FFT_N1 // 2
    cst = _dft_constants()
    kf = _filter_spectrum(hf_t.reshape(C, R, FFT_N2).astype(BF16),
                          hb_t.reshape(C, R, FFT_N2).astype(BF16), cst)
    x_slabs = (u.reshape(2, P, R, FFT_N2, C).transpose(1, 4, 0, 2, 3)
               .reshape(P, C, FFT_N1, FFT_N2).astype(BF16))
    y = _fft_conv(x_slabs, kf, cst)
    return y.reshape(P, C, 2, R, FFT_N2).transpose(2, 0, 3, 4, 1).reshape(B, L, C)


def _dispatch_indices(route_t, cnt, T):
    tm = EXPERT_TILE
    A = T * TOP_K
    top_e = route_t[:TOP_K].astype(jnp.int32)
    rank = route_t[TOP_K:2 * TOP_K].astype(jnp.int32)
    counts = cnt[:, 0].astype(jnp.int32)
    padded = (counts + tm - 1) // tm * tm
    pad_end = jnp.cumsum(padded)
    pad_start = pad_end - padded
    start = jnp.cumsum(counts) - counts
    experts = jnp.arange(N_EXPERTS, dtype=jnp.int32)[:, None, None]
    pos = rank + jnp.sum(jnp.where(top_e[None] == experts, pad_start[:, None, None], 0), axis=0)
    n_blocks = -(-A // tm) + N_EXPERTS
    P = n_blocks * tm
    block_first = jnp.arange(n_blocks, dtype=jnp.int32) * tm
    block_e = jnp.minimum(jnp.sum(pad_end[None, :] <= block_first[:, None], axis=1),
                          N_EXPERTS - 1).astype(jnp.int32)
    n_used = (pad_end[-1] // tm).astype(jnp.int32).reshape(1)
    tok = jnp.broadcast_to(jnp.arange(T, dtype=jnp.int32)[None, :], (TOP_K, T))
    _, sorted_tok = lax.sort_key_val(pos.reshape(A), tok.reshape(A))
    in_blk = jnp.arange(tm, dtype=jnp.int32)[None, :]
    r = (block_first - pad_start[block_e])[:, None] + in_blk
    compact = jnp.clip(start[block_e][:, None] + r, 0, A - 1)
    filler = (block_first[:, None] + in_blk) % T
    slot_tok = jnp.where(r < counts[block_e][:, None], sorted_tok[compact], filler).reshape(P)
    return pos, slot_tok, block_e, n_used


def kernel(x, p, g_mix, w_in, hyena_conv_w, hyena_conv_b, flt_w1, flt_b1, flt_w_inner, flt_b_inner, flt_freq, flt_w_out, hyena_d, hyena_gn, lambda_q1, lambda_k1, lambda_q2, lambda_k2, attn_subln, w_out, g_ffn, router_w, router_b, w_gate, b_gate, w_up, b_up, w_down, b_down, w_ple, g_ple, w_ple_gate, b_ple_gate, g_final):
    B, S, D = x.shape
    T = B * S
    i = 0
    x2 = x.reshape(T, D)

    cos_t, s1_t, s2_t = _rope_tables(S)
    q, k, vt, hy = _in_proj(x2, g_mix[i], w_in[i].astype(BF16), cos_t, s1_t, s2_t, S)

    lam = (jnp.exp(jnp.sum(lambda_q1[i] * lambda_k1[i])) - jnp.exp(jnp.sum(lambda_q2[i] * lambda_k2[i]))
           + LAMBDA_INIT).reshape(1).astype(F32)
    attn = _diff_attention(lam, q.reshape(B, S, -1), k.reshape(B, S, -1), vt, attn_subln[i])

    u, hx0 = _short_conv(hy.reshape(B, S, -1), hyena_conv_w[i], hyena_conv_b[i])
    h_fwd, h_bwd = _hyena_filters(S, flt_w1[i], flt_b1[i], flt_w_inner[i], flt_b_inner[i],
                                  flt_freq[i], flt_w_out[i])
    yc = _long_conv(u, h_fwd, h_bwd)

    h1, hn, route_t, top_w, cnt = _out_proj(
        attn.reshape(T, -1), yc.reshape(T, -1), u.reshape(T, -1), hx0.reshape(T, -1), hyena_d[i],
        hyena_gn[i], w_out[i].astype(BF16), x2, g_ffn[i], router_w[i], router_b[i])

    pos, slot_tok, block_e, n_used = _dispatch_indices(route_t, cnt, T)
    n_parts = COMBINE_PARTS
    head_rows = slot_tok.shape[0] // EXPERT_TILE // DISPATCH_HEAD_DIV * EXPERT_TILE
    x_parts = [hn[slot_tok[:head_rows]], hn[slot_tok[head_rows:]]]
    y = _experts(block_e, n_used, x_parts, w_gate[i], b_gate[i], w_up[i], b_up[i],
                 w_down[i], b_down[i])
    pos_parts = jnp.split(pos, n_parts, axis=1)
    yg_parts = [y[pp.reshape(-1)].reshape(TOP_K, T // n_parts, D) for pp in pos_parts]

    out = _final(h1, yg_parts, top_w, p[i].reshape(T, -1), w_ple[i].astype(BF16), g_ple[i],
                 w_ple_gate[i].astype(BF16), b_ple_gate[i], g_final)
    return out.reshape(B, S, D)
```

```python
import functools
import math

import jax
import jax.numpy as jnp
from jax import lax
from jax.experimental import pallas as pl
from jax.experimental.pallas import tpu as pltpu

F32 = jnp.float32
BF16 = jnp.bfloat16

ATTN_WIDTH = 512
HYENA_WIDTH = 512
DIFF_HEADS = 4
DIFF_HEAD_DIM = 64
DIFF_V_DIM = 128
HYENA_GROUPS = 8
FILTER_EMB = 33
FILTER_INNER = 2
FAST_DECAY_PCT = 0.3
SLOW_DECAY_PCT = 1.5
DECAY_TARGET = 1e-2
ROPE_THETA = 10000.0
N_EXPERTS = 32
TOP_K = 4
SWIGLU_ALPHA = 1.702
SWIGLU_LIMIT = 7.0
EPS = 1e-6
SUBLN_EPS = 1e-5
LAMBDA_INIT = 0.8 - 0.6 * math.exp(-0.3 * 0)

V7X_VMEM_LIMIT_BYTES = 56 * 1024 * 1024
LANES = 128
SUBLANES = 8

TOKEN_TILE = 512
HALO_ROWS = 16
ATTN_STREAM_W = 1024
ATTN_ONES_ROWS = 16
ATTN_UNROLL = 4
ATTN_KV_TILE = 512
EXPERT_TILE = 512
ROUTE_ROWS = 16
DISPATCH_HEAD_DIV = 4
COMBINE_HEAD_NUM, COMBINE_HEAD_DEN = 3, 8


def _params(*sem):
    return pltpu.CompilerParams(dimension_semantics=sem, vmem_limit_bytes=V7X_VMEM_LIMIT_BYTES)


def _rms(x, g, eps):
    return x * lax.rsqrt(jnp.mean(x * x, axis=-1, keepdims=True) + eps) * g


def _inproj_kernel(x_ref, g_ref, w_ref, wvt_ref, c_ref, s1_ref, s2_ref, q_ref, k_ref, vt_ref, hy_ref):
    a = _rms(x_ref[...], g_ref[...], EPS).astype(BF16)
    cos, s1, s2 = c_ref[...], s1_ref[...], s2_ref[...]

    half = DIFF_HEAD_DIM // 2

    def rope(t):
        return (t * cos + pltpu.roll(t, LANES - half, axis=1) * s1
                + pltpu.roll(t, half, axis=1) * s2)

    q_scale = (DIFF_HEAD_DIM ** -0.5) * math.log2(math.e)
    qk = jnp.dot(a, w_ref[:, : 2 * ATTN_WIDTH], preferred_element_type=F32)
    for j in range(ATTN_WIDTH // LANES):
        sl = slice(j * LANES, (j + 1) * LANES)
        q_ref[:, sl] = (rope(qk[:, sl]) * q_scale).astype(BF16)
        k_ref[:, sl] = rope(qk[:, ATTN_WIDTH + j * LANES: ATTN_WIDTH + (j + 1) * LANES]).astype(BF16)
    vt_ref[0] = lax.dot_general(wvt_ref[...], a, (((1,), (1,)), ((), ())),
                                preferred_element_type=F32).astype(BF16)
    hy_ref[...] = jnp.dot(a, w_ref[:, 3 * ATTN_WIDTH:], preferred_element_type=F32).astype(hy_ref.dtype)


def _in_proj(x2, g_mix, w_in_bf, cos_t, s1_t, s2_t, seq):
    T, D = x2.shape
    tm = TOKEN_TILE
    nseq = seq // tm
    wvt = w_in_bf[:, 2 * ATTN_WIDTH: 3 * ATTN_WIDTH].T
    row = lambda i: (i, 0)
    const = lambda i: (0, 0)
    pos = lambda i: (i % nseq, 0)
    return pl.pallas_call(
        _inproj_kernel,
        grid=(T // tm,),
        in_specs=[
            pl.BlockSpec((tm, D), row),
            pl.BlockSpec((1, D), const),
            pl.BlockSpec(w_in_bf.shape, const),
            pl.BlockSpec(wvt.shape, const),
            pl.BlockSpec((tm, LANES), pos),
            pl.BlockSpec((tm, LANES), pos),
            pl.BlockSpec((tm, LANES), pos),
        ],
        out_specs=[
            pl.BlockSpec((tm, ATTN_WIDTH), row),
            pl.BlockSpec((tm, ATTN_WIDTH), row),
            pl.BlockSpec((1, ATTN_WIDTH, tm), lambda i: (i, 0, 0)),
            pl.BlockSpec((tm, 3 * HYENA_WIDTH), row),
        ],
        out_shape=[
            jax.ShapeDtypeStruct((T, ATTN_WIDTH), BF16),
            jax.ShapeDtypeStruct((T, ATTN_WIDTH), BF16),
            jax.ShapeDtypeStruct((T // tm, ATTN_WIDTH, tm), BF16),
            jax.ShapeDtypeStruct((T, 3 * HYENA_WIDTH), BF16),
        ],
        compiler_params=_params("parallel"),
        name="in_proj",
    )(x2, g_mix.reshape(1, D), w_in_bf, wvt, cos_t, s1_t, s2_t)


def _attn_kernel(lam_ref, q_ref, k_ref, vt_ref, g_ref, o_ref, s_ref, m_ref, acc_ref, o0_ref, *, kc):
    w = ATTN_STREAM_W
    nc = k_ref.shape[1] // kc
    n_groups = q_ref.shape[1] // w

    def group_rows(grp):
        return pl.ds(pl.multiple_of(grp * w, w), w)

    def stream_q(grp, c):
        q = q_ref[0, group_rows(grp), :]
        lane = lax.broadcasted_iota(jnp.int32, q.shape, 1)
        keep = (lane < DIFF_HEAD_DIM) if c == 0 else (lane >= DIFF_HEAD_DIM)
        return jnp.where(keep, q, jnp.zeros_like(q))

    def fold8(t, op):
        r = t[0:SUBLANES]
        for j in range(1, kc // SUBLANES):
            r = op(r, t[j * SUBLANES:(j + 1) * SUBLANES])
        return r

    def score_chunk(i, qc, j):
        rows = pl.ds(pl.multiple_of(j * kc, kc), kc)
        s = lax.dot_general(k_ref[0, rows, :], qc, (((1,), (1,)), ((), ())),
                            preferred_element_type=F32)
        s_ref[i, rows, :] = s
        m_ref[i] = jnp.maximum(m_ref[i], fold8(s, jnp.maximum))

    ones_rows = jnp.ones((ATTN_ONES_ROWS, kc), BF16)

    def prob_chunk(i, j, m):
        rows = pl.ds(pl.multiple_of(j * kc, kc), kc)
        p = jnp.exp2(s_ref[i, rows, :] - m)
        v_aug = jnp.concatenate([vt_ref[j], ones_rows], axis=0)
        acc_ref[...] += jnp.dot(v_aug, p.astype(BF16), preferred_element_type=F32)

    def phase(read_buf, write_buf, q_next):
        if write_buf is not None:
            m_ref[write_buf] = jnp.full(m_ref.shape[1:], -jnp.inf, F32)
        if read_buf is not None:
            m = jnp.max(m_ref[read_buf], axis=0, keepdims=True)
            acc_ref[...] = jnp.zeros_like(acc_ref)

        def body(j, carry):
            if write_buf is not None:
                score_chunk(write_buf, q_next, j)
            if read_buf is not None:
                prob_chunk(read_buf, j, m)
            return carry

        lax.fori_loop(0, nc, body, 0, unroll=ATTN_UNROLL)
        if read_buf is None:
            return None
        return acc_ref[:DIFF_V_DIM, :] * (1.0 / acc_ref[DIFF_V_DIM:DIFF_V_DIM + 1, :])

    def finish_group(grp, out0, out1):
        o = out0 - lam_ref[0] * out1
        o = o * lax.rsqrt(jnp.mean(o * o, axis=0, keepdims=True) + SUBLN_EPS)
        o = o * (g_ref[...] * (1.0 - LAMBDA_INIT))
        o_ref[0, group_rows(grp), :] = o.T.astype(o_ref.dtype)

    phase(None, 0, stream_q(0, 0))

    def group_body(grp, carry):
        o0_ref[...] = phase(0, 1, stream_q(grp, 1))
        out1 = phase(1, 0, stream_q(grp + 1, 0))
        finish_group(grp, o0_ref[...], out1)
        return carry

    lax.fori_loop(0, n_groups - 1, group_body, 0)
    last = n_groups - 1
    o0_ref[...] = phase(0, 1, stream_q(last, 1))
    out1 = phase(1, None, None)
    finish_group(last, o0_ref[...], out1)


def _diff_attention(lam, q, k, vt, g_subln):
    B, S, _ = q.shape
    kc, w = ATTN_KV_TILE, ATTN_STREAM_W
    nc = S // kc
    assert vt.shape[2] == kc and S % w == 0 and S // w >= 2
    head = lambda b, h: (b, 0, h)
    return pl.pallas_call(
        functools.partial(_attn_kernel, kc=kc),
        grid=(B, DIFF_HEADS),
        in_specs=[
            pl.BlockSpec(memory_space=pltpu.SMEM),
            pl.BlockSpec((1, S, LANES), head),
            pl.BlockSpec((1, S, LANES), head),
            pl.BlockSpec((nc, DIFF_V_DIM, kc), lambda b, h: (b, h, 0)),
            pl.BlockSpec((DIFF_V_DIM, 1), lambda b, h: (0, 0)),
        ],
        out_specs=pl.BlockSpec((1, S, LANES), head),
        out_shape=jax.ShapeDtypeStruct((B, S, ATTN_WIDTH), BF16),
        scratch_shapes=[pltpu.VMEM((2, S, w), F32),
                        pltpu.VMEM((2, SUBLANES, w), F32),
                        pltpu.VMEM((DIFF_V_DIM + ATTN_ONES_ROWS, w), F32),
                        pltpu.VMEM((DIFF_V_DIM, w), F32)],
        compiler_params=_params("parallel", "parallel"),
        name="diff_attn",
    )(lam, q, k, vt, g_subln.reshape(DIFF_V_DIM, 1))


def _shortconv_kernel(hy_ref, prev_ref, next_ref, w_ref, b_ref, u_ref, x0_ref):
    i = pl.program_id(1)
    last = pl.num_programs(1) - 1
    x = hy_ref[0].astype(F32)
    ts = x.shape[0]
    prev_row = jnp.where(i == 0, 0.0, prev_ref[0, HALO_ROWS - 1:HALO_ROWS, :].astype(F32))
    next_row = jnp.where(i == last, 0.0, next_ref[0, 0:1, :].astype(F32))
    row = lax.broadcasted_iota(jnp.int32, (ts, 1), 0)
    xm = jnp.where(row == 0, prev_row, pltpu.roll(x, 1, axis=0))
    xp = jnp.where(row == ts - 1, next_row, pltpu.roll(x, ts - 1, axis=0))
    y = b_ref[...] + xm * w_ref[0:1, :] + x * w_ref[1:2, :] + xp * w_ref[2:3, :]
    C = HYENA_WIDTH
    u_ref[0] = (y[:, :C] * y[:, 2 * C:]).astype(u_ref.dtype)
    x0_ref[0] = y[:, C:2 * C].astype(x0_ref.dtype)


def _short_conv(hy, conv_w, conv_b):
    B, S, C3 = hy.shape
    ts = TOKEN_TILE
    nb = ts // HALO_ROWS
    return pl.pallas_call(
        _shortconv_kernel,
        grid=(B, S // ts),
        in_specs=[
            pl.BlockSpec((1, ts, C3), lambda b, i: (b, i, 0)),
            pl.BlockSpec((1, HALO_ROWS, C3), lambda b, i: (b, jnp.maximum(i * nb - 1, 0), 0)),
            pl.BlockSpec((1, HALO_ROWS, C3), lambda b, i: (b, jnp.minimum((i + 1) * nb, S // HALO_ROWS - 1), 0)),
            pl.BlockSpec((3, C3), lambda b, i: (0, 0)),
            pl.BlockSpec((1, C3), lambda b, i: (0, 0)),
        ],
        out_specs=[
            pl.BlockSpec((1, ts, HYENA_WIDTH), lambda b, i: (b, i, 0)),
            pl.BlockSpec((1, ts, HYENA_WIDTH), lambda b, i: (b, i, 0)),
        ],
        out_shape=[
            jax.ShapeDtypeStruct((B, S, HYENA_WIDTH), BF16),
            jax.ShapeDtypeStruct((B, S, HYENA_WIDTH), BF16),
        ],
        compiler_params=_params("parallel", "parallel"),
        name="short_conv",
    )(hy, hy, hy, conv_w, conv_b.reshape(1, C3))


FFT_N = 8192
FFT_N1 = 64
FFT_N2 = 128
FFT_CH_BLOCK = 32
FFT_GROUP = 4
FFT_UNROLL = 8


def _dft_constants():
    import numpy as np
    n1, n2, n = FFT_N1, FFT_N2, FFT_N
    k1 = np.arange(n1)[:, None]
    t1 = np.arange(n1)[None, :]
    f1 = np.exp(-2j * np.pi * k1 * t1 / n1)
    f1h = f1[:, : n1 // 2]
    w1c = np.block([[f1h.real, -f1h.imag], [f1h.imag, f1h.real]])
    w1r = np.concatenate([f1h.real, f1h.imag], axis=0)
    t2 = np.arange(n2)[:, None]
    k2 = np.arange(n2)[None, :]
    f2 = np.exp(-2j * np.pi * t2 * k2 / n2)
    w2a = np.concatenate([f2.real, f2.imag], axis=1)
    w2b = np.concatenate([-f2.imag, f2.real], axis=1)
    g2 = np.conj(f2)
    w3a = np.concatenate([g2.real, g2.imag], axis=1)
    w3b = np.concatenate([-g2.imag, g2.real], axis=1)
    h = np.conj(f1).T[: n1 // 2] / n
    w4 = np.block([[h.real, -h.imag], [h.imag, h.real]])
    tw = np.exp(-2j * np.pi * np.arange(n1)[:, None] * np.arange(n2)[None, :] / n)
    bf = lambda a: jnp.asarray(a, dtype=F32).astype(BF16)
    return dict(w1c=bf(w1c), w1r=bf(w1r), w2a=bf(w2a), w2b=bf(w2b), w3a=bf(w3a), w3b=bf(w3b),
                w4=bf(w4), tc=jnp.asarray(np.tile(tw.real, (1, FFT_GROUP)), F32),
                ts=jnp.asarray(np.tile(tw.imag, (1, FFT_GROUP)), F32))


def _dft_forward(x_ref, w1_ref, tc_ref, ts_ref, w2a_ref, w2b_ref, ar_ref, ai_ref):
    ng = x_ref.shape[0] // FFT_GROUP
    n1, n2 = FFT_N1, FFT_N2
    w1 = w1_ref[...]
    tc, ts = tc_ref[...], ts_ref[...]

    def body(g, carry):
        xg = jnp.concatenate([x_ref[g * FFT_GROUP + cl] for cl in range(FFT_GROUP)], axis=1)
        a = jnp.dot(w1, xg, preferred_element_type=F32)
        ar, ai = a[:n1], a[n1:]
        tr = (ar * tc - ai * ts).astype(BF16)
        ti = (ar * ts + ai * tc).astype(BF16)
        for cl in range(FFT_GROUP):
            rows = pl.ds(pl.multiple_of((g * FFT_GROUP + cl) * n1, n1), n1)
            ar_ref[rows, :] = tr[:, cl * n2:(cl + 1) * n2]
            ai_ref[rows, :] = ti[:, cl * n2:(cl + 1) * n2]
        return carry

    lax.fori_loop(0, ng, body, 0, unroll=FFT_UNROLL)
    return (jnp.dot(ar_ref[...], w2a_ref[...], preferred_element_type=F32)
            + jnp.dot(ai_ref[...], w2b_ref[...], preferred_element_type=F32))


def _spectrum_kernel(hf_ref, hb_ref, w1_ref, tc_ref, ts_ref, w2a_ref, w2b_ref, o_ref, ar_ref, ai_ref):
    n2 = FFT_N2
    f = _dft_forward(hf_ref, w1_ref, tc_ref, ts_ref, w2a_ref, w2b_ref, ar_ref, ai_ref)
    o_ref[...] = f.reshape(o_ref.shape)
    b = _dft_forward(hb_ref, w1_ref, tc_ref, ts_ref, w2a_ref, w2b_ref, ar_ref, ai_ref)
    b = b.reshape(o_ref.shape)
    o_ref[:, :, :n2] = o_ref[:, :, :n2] + b[:, :, :n2]
    o_ref[:, :, n2:] = o_ref[:, :, n2:] - b[:, :, n2:]


def _fftconv_kernel(x_ref, kf_ref, w1_ref, tc_ref, ts_ref, w2a_ref, w2b_ref, w3a_ref, w3b_ref,
                    w4_ref, y_ref, ar_ref, ai_ref, c_ref):
    cb = x_ref.shape[0]
    ng = cb // FFT_GROUP
    n1, n2 = FFT_N1, FFT_N2
    b = _dft_forward(x_ref, w1_ref, tc_ref, ts_ref, w2a_ref, w2b_ref, ar_ref, ai_ref)
    kf = kf_ref[...].reshape(cb * n1, 2 * n2)
    br, bi = b[:, :n2], b[:, n2:]
    kr, ki = kf[:, :n2], kf[:, n2:]
    ar_ref[...] = (br * kr - bi * ki).astype(BF16)
    ai_ref[...] = (br * ki + bi * kr).astype(BF16)
    c_ref[...] = (jnp.dot(ar_ref[...], w3a_ref[...], preferred_element_type=F32)
                  + jnp.dot(ai_ref[...], w3b_ref[...], preferred_element_type=F32))
    w4 = w4_ref[...]
    tc, ts = tc_ref[...], ts_ref[...]

    def body(g, carry):
        rows = [pl.ds(pl.multiple_of((g * FFT_GROUP + cl) * n1, n1), n1) for cl in range(FFT_GROUP)]
        cr = jnp.concatenate([c_ref[r, :n2] for r in rows], axis=1)
        ci = jnp.concatenate([c_ref[r, n2:] for r in rows], axis=1)
        dr = (cr * tc + ci * ts).astype(BF16)
        di = (ci * tc - cr * ts).astype(BF16)
        yg = (jnp.dot(w4[:, :n1], dr, preferred_element_type=F32)
              + jnp.dot(w4[:, n1:], di, preferred_element_type=F32)).astype(y_ref.dtype)
        for cl in range(FFT_GROUP):
            y_ref[g * FFT_GROUP + cl] = yg[:, cl * n2:(cl + 1) * n2]
        return carry

    lax.fori_loop(0, ng, body, 0, unroll=FFT_UNROLL)


def _const_spec(a):
    nd = a.ndim
    return pl.BlockSpec(a.shape, lambda *_: (0,) * nd)


def _filter_spectrum(hf_slabs, hb_slabs, cst):
    C = hf_slabs.shape[0]
    cb = FFT_CH_BLOCK
    consts = [cst["w1r"], cst["tc"], cst["ts"], cst["w2a"], cst["w2b"]]
    slab = pl.BlockSpec((cb, FFT_N1 // 2, FFT_N2), lambda i: (i, 0, 0))
    return pl.pallas_call(
        _spectrum_kernel,
        grid=(C // cb,),
        in_specs=[slab, slab] + [_const_spec(a) for a in consts],
        out_specs=pl.BlockSpec((cb, FFT_N1, 2 * FFT_N2), lambda i: (i, 0, 0)),
        out_shape=jax.ShapeDtypeStruct((C, FFT_N1, 2 * FFT_N2), F32),
        scratch_shapes=[pltpu.VMEM((cb * FFT_N1, FFT_N2), BF16), pltpu.VMEM((cb * FFT_N1, FFT_N2), BF16)],
        compiler_params=_params("parallel"),
        name="filter_spectrum",
    )(hf_slabs, hb_slabs, *consts)


FILTER_TIME_TILE = 512


def _filter_kernel(zt_ref, w1t_ref, b1_ref, wit_ref, bi_ref, fr_ref, wot_ref, dec_ref, hf_ref, hb_ref):
    hp = lax.Precision.HIGHEST
    h = jnp.sin(fr_ref[0] * (jnp.dot(w1t_ref[...], zt_ref[...], precision=hp,
                                     preferred_element_type=F32) + b1_ref[...]))
    for j in range(FILTER_INNER):
        h = jnp.sin(fr_ref[j + 1] * (jnp.dot(wit_ref[j], h, precision=hp,
                                             preferred_element_type=F32) + bi_ref[j]))
    o = jnp.dot(wot_ref[...], h, precision=hp, preferred_element_type=F32)
    dec = dec_ref[...]
    hf_ref[...] = o[:HYENA_WIDTH] * dec
    hb_ref[...] = o[HYENA_WIDTH:] * dec


def _hyena_filters(seq, w1, b1, w_inner, b_inner, freq, w_out):
    C = HYENA_WIDTH
    order = w1.shape[1]
    pos = jnp.arange(seq, dtype=F32)
    t = pos / (seq - 1)
    bands = (FILTER_EMB - 1) // 2
    f = jnp.linspace(1e-4, bands - 1, bands, dtype=F32)
    fw = ((2.0 * math.pi / seq) * pos)[:, None] * f[None, :]
    z = jnp.concatenate([t[:, None], jnp.cos(fw), -jnp.sin(fw)], axis=-1)
    zt = jnp.zeros((LANES, seq), F32).at[:FILTER_EMB].set(z.T)
    w1t = jnp.zeros((order, LANES), F32).at[:, :FILTER_EMB].set(w1.T)
    max_decay = math.log(DECAY_TARGET) / FAST_DECAY_PCT
    min_decay = math.log(DECAY_TARGET) / SLOW_DECAY_PCT
    deltas = jnp.abs(jnp.linspace(min_decay, max_decay, C, dtype=F32))
    dec_t = jnp.exp(-deltas[:, None] * t[None, :])
    tt = FILTER_TIME_TILE
    lane_blk = lambda r: pl.BlockSpec((r, tt), lambda i: (0, i))
    args = [zt, w1t, b1.reshape(order, 1), jnp.swapaxes(w_inner, 1, 2),
            b_inner.reshape(FILTER_INNER, order, 1), freq.reshape(FILTER_INNER + 1, order, 1),
            w_out.T, dec_t]
    return pl.pallas_call(
        _filter_kernel,
        grid=(seq // tt,),
        in_specs=[lane_blk(LANES)] + [_const_spec(a) for a in args[1:7]] + [lane_blk(C)],
        out_specs=[lane_blk(C), lane_blk(C)],
        out_shape=[jax.ShapeDtypeStruct((C, seq), F32), jax.ShapeDtypeStruct((C, seq), F32)],
        compiler_params=_params("parallel"),
        name="hyena_filters",
    )(*args)


def _fft_conv(x_slabs, kf, cst):
    P, C = x_slabs.shape[:2]
    cb = FFT_CH_BLOCK
    consts = [cst["w1c"], cst["tc"], cst["ts"], cst["w2a"], cst["w2b"], cst["w3a"], cst["w3b"], cst["w4"]]
    return pl.pallas_call(
        _fftconv_kernel,
        grid=(C // cb, P),
        in_specs=[pl.BlockSpec((None, cb, FFT_N1, FFT_N2), lambda i, p: (p, i, 0, 0)),
                  pl.BlockSpec((cb, FFT_N1, 2 * FFT_N2), lambda i, p: (i, 0, 0))]
                 + [_const_spec(a) for a in consts],
        out_specs=pl.BlockSpec((None, cb, FFT_N1, FFT_N2), lambda i, p: (p, i, 0, 0)),
        out_shape=jax.ShapeDtypeStruct((P, C, FFT_N1, FFT_N2), BF16),
        scratch_shapes=[pltpu.VMEM((cb * FFT_N1, FFT_N2), BF16), pltpu.VMEM((cb * FFT_N1, FFT_N2), BF16),
                        pltpu.VMEM((cb * FFT_N1, 2 * FFT_N2), F32)],
        compiler_params=_params("parallel", "arbitrary"),
        name="fft_conv",
    )(x_slabs, kf, *consts)


def _split_bf16(x):
    hi = x.astype(BF16)
    lo = (x - hi.astype(F32)).astype(BF16)
    return hi, lo


def _outproj_kernel(attn_ref, yc_ref, u_ref, x0_ref, d_ref, gn_ref, grp_ref, wo_ref, x_ref,
                    gf_ref, rwh_ref, rwl_ref, rb_ref, tri_ref, h_ref, hn_ref, rt_ref, wt_ref,
                    cnt_out_ref, cnt_ref):
    @pl.when(pl.program_id(0) == 0)
    def _():
        cnt_ref[...] = jnp.zeros_like(cnt_ref)

    grp = grp_ref[...]
    gsz = HYENA_WIDTH // HYENA_GROUPS

    def token_tile(cnt_col):
        rows = slice(None)
        z = ((yc_ref[rows, :].astype(F32) + u_ref[rows, :].astype(F32) * d_ref[...])
             * x0_ref[rows, :].astype(F32))
        zh, zl = _split_bf16(z * z)
        ssq = (jnp.dot(zh, grp, preferred_element_type=F32)
               + jnp.dot(zl, grp, preferred_element_type=F32))
        hy_out = (z * lax.rsqrt(ssq * (1.0 / gsz) + EPS) * gn_ref[...]).astype(BF16)
        mix = (jnp.dot(attn_ref[rows, :], wo_ref[:ATTN_WIDTH, :], preferred_element_type=F32)
               + jnp.dot(hy_out, wo_ref[ATTN_WIDTH:, :], preferred_element_type=F32))
        h = x_ref[rows, :] + mix
        h_ref[rows, :] = h
        hn = _rms(h, gf_ref[...], EPS)
        hn_ref[rows, :] = hn.astype(BF16)
        nh, nl = _split_bf16(hn)
        nt = (((1,), (1,)), ((), ()))
        logits = (lax.dot_general(rwh_ref[...], nh, nt, preferred_element_type=F32)
                  + lax.dot_general(rwh_ref[...], nl, nt, preferred_element_type=F32)
                  + lax.dot_general(rwl_ref[...], nh, nt, preferred_element_type=F32)) + rb_ref[...]

        erow = lax.broadcasted_iota(jnp.int32, logits.shape, 0)
        work = logits
        top_val, top_idx, top_hot = [], [], []
        for _ in range(TOP_K):
            m = jnp.max(work, axis=0, keepdims=True)
            idx = jnp.min(jnp.where(work == m, erow, N_EXPERTS), axis=0, keepdims=True)
            hot = erow == idx
            top_val.append(m)
            top_idx.append(idx)
            top_hot.append(hot)
            work = jnp.where(hot, -jnp.inf, work)
        ex = [jnp.exp(v - top_val[0]) for v in top_val]
        inv_den = 1.0 / (ex[0] + ex[1] + ex[2] + ex[3])
        sel = jnp.zeros(logits.shape, F32)
        for hot in top_hot:
            sel = sel + hot.astype(F32)
        before = jnp.dot(sel.astype(BF16), tri_ref[...], preferred_element_type=F32) + cnt_col
        tm = logits.shape[1]
        row_rt = lax.broadcasted_iota(jnp.int32, (ROUTE_ROWS, tm), 0)
        row_w = lax.broadcasted_iota(jnp.int32, (LANES, tm), 0)
        packed = jnp.zeros((ROUTE_ROWS, tm), F32)
        w_rows = jnp.zeros((LANES, tm), F32)
        for r, hot in enumerate(top_hot):
            rank_r = jnp.sum(jnp.where(hot, before, 0.0), axis=0, keepdims=True)
            weight_r = ex[r] * inv_den
            packed = jnp.where(row_rt == r, top_idx[r].astype(F32), packed)
            packed = jnp.where(row_rt == TOP_K + r, rank_r, packed)
            packed = jnp.where(row_rt == 2 * TOP_K + r, weight_r, packed)
            w_rows = jnp.where(row_w == r, weight_r, w_rows)
        rt_ref[...] = packed
        wt_ref[...] = w_rows.T
        return cnt_col + jnp.sum(sel, axis=1, keepdims=True)

    cnt_col = token_tile(cnt_ref[:, 0:1])
    cnt_ref[...] = jnp.broadcast_to(cnt_col, cnt_ref.shape)
    cnt_out_ref[...] = cnt_ref[...]


def _out_proj(attn2, yc2, u2, x02, hyena_d, hyena_gn, w_out_bf, x2, g_ffn, router_w, router_b):
    T, D = x2.shape
    tm = TOKEN_TILE
    C = HYENA_WIDTH
    gid = jnp.arange(C) // (C // HYENA_GROUPS)
    grp = (gid[:, None] == gid[None, :]).astype(BF16)
    rwh, rwl = _split_bf16(router_w.T)
    rb = router_b.reshape(N_EXPERTS, 1)
    tri = (jnp.arange(tm)[:, None] < jnp.arange(tm)[None, :]).astype(BF16)
    row = lambda i: (i, 0)
    const = lambda i: (0, 0)
    return pl.pallas_call(
        _outproj_kernel,
        grid=(T // tm,),
        in_specs=[
            pl.BlockSpec((tm, ATTN_WIDTH), row),
            pl.BlockSpec((tm, C), row),
            pl.BlockSpec((tm, C), row),
            pl.BlockSpec((tm, C), row),
            pl.BlockSpec((1, C), const),
            pl.BlockSpec((1, C), const),
            pl.BlockSpec((C, C), const),
            pl.BlockSpec((D, D), const),
            pl.BlockSpec((tm, D), row),
            pl.BlockSpec((1, D), const),
            pl.BlockSpec((N_EXPERTS, D), const),
            pl.BlockSpec((N_EXPERTS, D), const),
            pl.BlockSpec((N_EXPERTS, 1), const),
            pl.BlockSpec((tm, tm), const),
        ],
        out_specs=[
            pl.BlockSpec((tm, D), row),
            pl.BlockSpec((tm, D), row),
            pl.BlockSpec((ROUTE_ROWS, tm), lambda i: (0, i)),
            pl.BlockSpec((tm, LANES), row),
            pl.BlockSpec((N_EXPERTS, LANES), const),
        ],
        out_shape=[
            jax.ShapeDtypeStruct((T, D), F32),
            jax.ShapeDtypeStruct((T, D), BF16),
            jax.ShapeDtypeStruct((ROUTE_ROWS, T), F32),
            jax.ShapeDtypeStruct((T, LANES), F32),
            jax.ShapeDtypeStruct((N_EXPERTS, LANES), F32),
        ],
        scratch_shapes=[pltpu.VMEM((N_EXPERTS, LANES), F32)],
        compiler_params=_params("arbitrary"),
        name="out_proj",
    )(attn2, yc2, u2, x02, hyena_d.reshape(1, C), hyena_gn.reshape(1, C), grp, w_out_bf, x2,
      g_ffn.reshape(1, D), rwh, rwl, rb, tri)


def _expert_kernel(be_ref, nused_ref, x_ref, wg_ref, bg_ref, wu_ref, bu_ref, wd_ref, bd_ref, *rest):
    y_ref, wg_bf, wu_bf, wd_bf = rest[-4:]
    i = pl.program_id(0)
    used = i < nused_ref[0]
    new_expert = jnp.logical_or(i == 0, be_ref[i] != be_ref[jnp.maximum(i - 1, 0)])

    @pl.when(jnp.logical_and(used, new_expert))
    def _():
        wg_bf[...] = wg_ref[0].astype(BF16)
        wu_bf[...] = wu_ref[0].astype(BF16)
        wd_bf[...] = wd_ref[0].astype(BF16)

    @pl.when(used)
    def _():
        x = x_ref[...]
        g = jnp.minimum(jnp.dot(x, wg_bf[...], preferred_element_type=F32) + bg_ref[0], SWIGLU_LIMIT)
        u = jnp.clip(jnp.dot(x, wu_bf[...], preferred_element_type=F32) + bu_ref[0],
                     -SWIGLU_LIMIT, SWIGLU_LIMIT)
        a = (u + 1.0) * (g * jax.nn.sigmoid(SWIGLU_ALPHA * g))
        y = jnp.dot(a.astype(BF16), wd_bf[...], preferred_element_type=F32) + bd_ref[0]
        y_ref[...] = y.astype(y_ref.dtype)

    @pl.when(jnp.logical_not(used))
    def _():
        y_ref[...] = jnp.zeros_like(y_ref)


def _experts(block_e, n_used, x_parts, wg, bg, wu, bu, wd, bd):
    D = x_parts[0].shape[1]
    tm = EXPERT_TILE
    P = sum(xs.shape[0] for xs in x_parts)
    E, _, FF = wg.shape
    wmap = lambda i, be, nu: (be[i], 0, 0)
    y = None
    first = 0
    for part, xs in enumerate(x_parts):
        nb = xs.shape[0] // tm
        in_specs = [
            pl.BlockSpec((tm, D), lambda i, be, nu: (i, 0)),
            pl.BlockSpec((1, D, FF), wmap),
            pl.BlockSpec((1, 1, FF), wmap),
            pl.BlockSpec((1, D, FF), wmap),
            pl.BlockSpec((1, 1, FF), wmap),
            pl.BlockSpec((1, FF, D), wmap),
            pl.BlockSpec((1, 1, D), wmap),
        ]
        args = [block_e[first:first + nb], jnp.clip(n_used - first, 0, nb), xs,
                wg, bg.reshape(E, 1, FF), wu, bu.reshape(E, 1, FF), wd, bd.reshape(E, 1, D)]
        aliases = {}
        if y is not None:
            in_specs.append(pl.BlockSpec(memory_space=pl.ANY))
            args.append(y)
            aliases = {len(args) - 1: 0}
        grid_spec = pltpu.PrefetchScalarGridSpec(
            num_scalar_prefetch=2,
            grid=(nb,),
            in_specs=in_specs,
            out_specs=pl.BlockSpec((tm, D), lambda i, be, nu, first=first: (i + first, 0)),
            scratch_shapes=[pltpu.VMEM((D, FF), BF16), pltpu.VMEM((D, FF), BF16),
                            pltpu.VMEM((FF, D), BF16)],
        )
        y = pl.pallas_call(
            _expert_kernel,
            grid_spec=grid_spec,
            out_shape=jax.ShapeDtypeStruct((P, D), BF16),
            input_output_aliases=aliases,
            compiler_params=_params("arbitrary"),
            name=f"moe_experts_{part}",
        )(*args)
        first += nb
    return y


def _final_kernel(h_ref, yg_ref, rt_ref, p_ref, wp_ref, gp_ref, wg_ref, bg_ref, gfin_ref, *rest):
    o_ref = rest[-1]
    h = h_ref[...]
    for r in range(TOP_K):
        h = h + yg_ref[r].astype(F32) * rt_ref[:, r:r + 1]
    e = _rms(jnp.dot(p_ref[...].astype(BF16), wp_ref[...], preferred_element_type=F32),
             gp_ref[...], EPS)
    gate = jax.nn.sigmoid(jnp.dot(h.astype(BF16), wg_ref[...], preferred_element_type=F32)
                          + bg_ref[...])
    h = h + gate * e
    o_ref[...] = _rms(h, gfin_ref[...], EPS)


def _final(h1, yg_parts, route, p2, w_ple_bf, g_ple, w_gate_bf, b_gate, g_final):
    T, D = h1.shape
    tm = TOKEN_TILE
    PD = p2.shape[1]
    const = lambda i: (0, 0)
    out = None
    first = 0
    for part, yg in enumerate(yg_parts):
        nb = yg.shape[1] // tm
        row = lambda i, first=first: (i + first, 0)
        in_specs = [
            pl.BlockSpec((tm, D), row),
            pl.BlockSpec((TOP_K, tm, D), lambda i: (0, i, 0)),
            pl.BlockSpec((tm, LANES), row),
            pl.BlockSpec((tm, PD), row),
            pl.BlockSpec((PD, D), const),
            pl.BlockSpec((1, D), const),
            pl.BlockSpec((D, D), const),
            pl.BlockSpec((1, D), const),
            pl.BlockSpec((1, D), const),
        ]
        args = [h1, yg, route, p2, w_ple_bf, g_ple.reshape(1, D), w_gate_bf, b_gate.reshape(1, D),
                g_final.reshape(1, D)]
        aliases = {}
        if out is not None:
            in_specs.append(pl.BlockSpec(memory_space=pl.ANY))
            args.append(out)
            aliases = {len(args) - 1: 0}
        out = pl.pallas_call(
            _final_kernel,
            grid=(nb,),
            in_specs=in_specs,
            out_specs=pl.BlockSpec((tm, D), row),
            out_shape=jax.ShapeDtypeStruct((T, D), F32),
            input_output_aliases=aliases,
            compiler_params=_params("parallel"),
            name=f"final_{part}",
        )(*args)
        first += nb
    return out


def _rope_tables(seq):
    d = DIFF_HEAD_DIM
    pos = jnp.arange(seq, dtype=F32)
    inv = ROPE_THETA ** (-jnp.arange(0, d, 2, dtype=F32) / d)
    ang = pos[:, None] * inv[None, :]
    cos, sin = jnp.cos(ang), jnp.sin(ang)
    z = jnp.zeros_like(sin)
    cos_t = jnp.tile(jnp.concatenate([cos, cos], -1), (1, LANES // d))
    s1_t = jnp.tile(jnp.concatenate([-sin, z], -1), (1, LANES // d))
    s2_t = jnp.tile(jnp.concatenate([z, sin], -1), (1, LANES // d))
    return cos_t, s1_t, s2_t


def _long_conv(u, hf_t, hb_t):
    B, L, C = u.shape
    assert 2 * L == FFT_N and B % 2 == 0
    P, R = B // 2, FFT_N1 // 2
    cst = _dft_constants()
    kf = _filter_spectrum(hf_t.reshape(C, R, FFT_N2).astype(BF16),
                          hb_t.reshape(C, R, FFT_N2).astype(BF16), cst)
    x_slabs = (u.reshape(2, P, R, FFT_N2, C).transpose(1, 4, 0, 2, 3)
               .reshape(P, C, FFT_N1, FFT_N2).astype(BF16))
    y = _fft_conv(x_slabs, kf, cst)
    return y.reshape(P, C, 2, R, FFT_N2).transpose(2, 0, 3, 4, 1).reshape(B, L, C)


def _dispatch_indices(route_t, cnt, T):
    tm = EXPERT_TILE
    A = T * TOP_K
    top_e = route_t[:TOP_K].astype(jnp.int32)
    rank = route_t[TOP_K:2 * TOP_K].astype(jnp.int32)
    counts = cnt[:, 0].astype(jnp.int32)
    padded = (counts + tm - 1) // tm * tm
    pad_end = jnp.cumsum(padded)
    pad_start = pad_end - padded
    start = jnp.cumsum(counts) - counts
    experts = jnp.arange(N_EXPERTS, dtype=jnp.int32)[:, None, None]
    pos = rank + jnp.sum(jnp.where(top_e[None] == experts, pad_start[:, None, None], 0), axis=0)
    n_blocks = -(-A // tm) + N_EXPERTS
    P = n_blocks * tm
    block_first = jnp.arange(n_blocks, dtype=jnp.int32) * tm
    block_e = jnp.minimum(jnp.sum(pad_end[None, :] <= block_first[:, None], axis=1),
                          N_EXPERTS - 1).astype(jnp.int32)
    n_used = (pad_end[-1] // tm).astype(jnp.int32).reshape(1)
    tok = jnp.broadcast_to(jnp.arange(T, dtype=jnp.int32)[None, :], (TOP_K, T))
    _, sorted_tok = lax.sort_key_val(pos.reshape(A), tok.reshape(A))
    in_blk = jnp.arange(tm, dtype=jnp.int32)[None, :]
    r = (block_first - pad_start[block_e])[:, None] + in_blk
    compact = jnp.clip(start[block_e][:, None] + r, 0, A - 1)
    filler = (block_first[:, None] + in_blk) % T
    slot_tok = jnp.where(r < counts[block_e][:, None], sorted_tok[compact], filler).reshape(P)
    return pos, slot_tok, block_e, n_used


def kernel(x, p, g_mix, w_in, hyena_conv_w, hyena_conv_b, flt_w1, flt_b1, flt_w_inner, flt_b_inner, flt_freq, flt_w_out, hyena_d, hyena_gn, lambda_q1, lambda_k1, lambda_q2, lambda_k2, attn_subln, w_out, g_ffn, router_w, router_b, w_gate, b_gate, w_up, b_up, w_down, b_down, w_ple, g_ple, w_ple_gate, b_ple_gate, g_final):
    B, S, D = x.shape
    T = B * S
    i = 0
    x2 = x.reshape(T, D)

    cos_t, s1_t, s2_t = _rope_tables(S)
    q, k, vt, hy = _in_proj(x2, g_mix[i], w_in[i].astype(BF16), cos_t, s1_t, s2_t, S)

    lam = (jnp.exp(jnp.sum(lambda_q1[i] * lambda_k1[i])) - jnp.exp(jnp.sum(lambda_q2[i] * lambda_k2[i]))
           + LAMBDA_INIT).reshape(1).astype(F32)
    attn = _diff_attention(lam, q.reshape(B, S, -1), k.reshape(B, S, -1), vt, attn_subln[i])

    u, hx0 = _short_conv(hy.reshape(B, S, -1), hyena_conv_w[i], hyena_conv_b[i])
    h_fwd, h_bwd = _hyena_filters(S, flt_w1[i], flt_b1[i], flt_w_inner[i], flt_b_inner[i],
                                  flt_freq[i], flt_w_out[i])
    yc = _long_conv(u, h_fwd, h_bwd)

    h1, hn, route_t, top_w, cnt = _out_proj(
        attn.reshape(T, -1), yc.reshape(T, -1), u.reshape(T, -1), hx0.reshape(T, -1), hyena_d[i],
        hyena_gn[i], w_out[i].astype(BF16), x2, g_ffn[i], router_w[i], router_b[i])

    pos, slot_tok, block_e, n_used = _dispatch_indices(route_t, cnt, T)
    head_rows = slot_tok.shape[0] // EXPERT_TILE // DISPATCH_HEAD_DIV * EXPERT_TILE
    x_parts = [hn[slot_tok[:head_rows]], hn[slot_tok[head_rows:]]]
    y = _experts(block_e, n_used, x_parts, w_gate[i], b_gate[i], w_up[i], b_up[i],
                 w_down[i], b_down[i])
    head_tok = T // TOKEN_TILE * COMBINE_HEAD_NUM // COMBINE_HEAD_DEN * TOKEN_TILE
    pos_parts = [pos[:, :head_tok], pos[:, head_tok:]]
    yg_parts = [y[pp.reshape(-1)].reshape(TOP_K, pp.shape[1], D) for pp in pos_parts]

    out = _final(h1, yg_parts, top_w, p[i].reshape(T, -1), w_ple[i].astype(BF16), g_ple[i],
                 w_ple_gate[i].astype(BF16), b_ple_gate[i], g_final)
    return out.reshape(B, S, D)
```
